```python
import math
import jax
import jax.numpy as jnp
from jax import lax
import numpy as np

D_MODEL = 1024
BATCH = 2
SEQ = 16384
DEPTH = 1

N_MEM = 256
ATTN_HEADS = 8
ATTN_HEAD_DIM = D_MODEL // 16
DILATED_PATTERNS = ((128, 1), (512, 4), (2048, 16))
RET_HEADS = 4
RET_QK_DIM = D_MODEL // 16
RET_V_DIM = D_MODEL // 8
RET_CHUNK = 128
RET_BASE_EXP = 5
ATTN_WIDTH = ATTN_HEADS * ATTN_HEAD_DIM
RET_QK_WIDTH = RET_HEADS * RET_QK_DIM
RET_V_WIDTH = RET_HEADS * RET_V_DIM
MIX_WIDTH = ATTN_WIDTH + RET_V_WIDTH
IN_PROJ_COLS = 3 * ATTN_WIDTH + 2 * RET_QK_WIDTH + 2 * RET_V_WIDTH
MEM_HEADS = 4
MEM_HEAD_DIM = D_MODEL // MEM_HEADS
N_GROUPS = 4
EXPERTS_PER_GROUP = 8
N_EXPERTS = N_GROUPS * EXPERTS_PER_GROUP
EXPERT_TOP_K = 2
EXPERT_FF = D_MODEL // 2
MOE_BLOCK = 128
NORM_EPS = 1e-6
GN_EPS = 1e-5
NEG_INF = -1e30

kernel_name = 'hybrid_dilated_retention_hmoe_encoder'


def rms_norm(x, w, eps=NORM_EPS):
    xf = x.astype(jnp.float32)
    y = xf * lax.rsqrt(jnp.mean(xf * xf, axis=-1, keepdims=True) + eps)
    return (y * w.astype(jnp.float32)).astype(x.dtype)


def split_heads(t, n_heads, head_dim):
    b, s, _ = t.shape
    return t.reshape(b, s, n_heads, head_dim).transpose(0, 2, 1, 3)


def merge_heads(t):
    b, h, s, d = t.shape
    return t.transpose(0, 2, 1, 3).reshape(b, s, h * d)


def alibi_slopes(n_heads):
    exps = jnp.arange(1, n_heads + 1, dtype=jnp.float32)
    return jnp.exp2(-8.0 * exps / n_heads)


def dilated_window_attention(q, k, v, slopes, window, dilation):
    B, H, S, hd = q.shape
    half = window // (2 * dilation)
    L = S // dilation
    nb = -(-L // half)
    Lp = nb * half

    def to_sub(t):
        return t.reshape(B, H, L, dilation, hd).transpose(0, 1, 3, 2, 4)

    qs, ks, vs = to_sub(q), to_sub(k), to_sub(v)
    qb = jnp.pad(qs, ((0, 0), (0, 0), (0, 0), (0, Lp - L), (0, 0))).reshape(B, H, dilation, nb, half, hd)
    kv_pad = ((0, 0), (0, 0), (0, 0), (half, Lp - L + half), (0, 0))
    kp = jnp.pad(ks, kv_pad).reshape(B, H, dilation, nb + 2, half, hd)
    vp = jnp.pad(vs, kv_pad).reshape(B, H, dilation, nb + 2, half, hd)

    def band(t):
        return jnp.concatenate([t[:, :, :, 0:nb], t[:, :, :, 1:nb + 1], t[:, :, :, 2:nb + 2]], axis=4)

    kb, vb = band(kp), band(vp)
    s = jnp.einsum('bhrnqd,bhrnkd->bhrnqk', qb, kb, preferred_element_type=jnp.float32)
    qi = jnp.arange(half)
    kc = jnp.arange(3 * half)
    delta = kc[None, :] - half - qi[:, None]
    key_pos = jnp.arange(nb)[:, None] * half - half + kc[None, :]
    allowed = (jnp.abs(delta) <= half)[None] & ((key_pos >= 0) & (key_pos < L))[:, None, :]
    bias = -slopes[:, None, None] * (dilation * jnp.abs(delta)).astype(jnp.float32)
    s = s + bias[None, :, None, None]
    s = jnp.where(allowed[None, None, None], s, NEG_INF)
    m = jnp.max(s, axis=-1, keepdims=True)
    p = jnp.exp(s - m)
    denom = jnp.sum(p, axis=-1, keepdims=True)
    o = jnp.einsum('bhrnqk,bhrnkd->bhrnqd', p.astype(v.dtype), vb, preferred_element_type=jnp.float32) / denom
    lse = (m + jnp.log(denom))[..., 0]

    def from_sub(t):
        t = t.reshape((B, H, dilation, Lp) + t.shape[5:])[:, :, :, :L]
        t = jnp.moveaxis(t, 2, 3)
        return t.reshape((B, H, S) + t.shape[4:])

    return from_sub(o), from_sub(lse)


def retention_chunkwise(q, k, v, log_gamma, inclusive):
    B, H, S, dk = q.shape
    dv = v.shape[-1]
    C = RET_CHUNK
    nC = S // C
    qc = q.reshape(B, H, nC, C, dk)
    kc = k.reshape(B, H, nC, C, dk)
    vc = v.reshape(B, H, nC, C, dv)
    i = jnp.arange(C)
    lg = log_gamma[:, None]
    diff = i[:, None] - i[None, :]
    mask = diff >= 0 if inclusive else diff > 0
    decay_mat = jnp.where(mask[None], jnp.exp(lg[:, :, None] * jnp.maximum(diff, 0)[None]), 0.0)
    scores = jnp.einsum('bhnid,bhnjd->bhnij', qc, kc) * decay_mat[None, :, None]
    y_intra = jnp.einsum('bhnij,bhnje->bhnie', scores, vc)
    k_dec = jnp.exp(lg * (C - 1 - i)[None])
    chunk_kv = jnp.einsum('bhnjd,hj,bhnje->bhnde', kc, k_dec, vc)
    chunk_decay = jnp.exp(log_gamma * C)[None, :, None, None]

    def step(state, kv_n):
        return state * chunk_decay + kv_n, state

    _, prev = lax.scan(step, jnp.zeros((B, H, dk, dv), q.dtype), jnp.moveaxis(chunk_kv, 2, 0))
    prev = jnp.moveaxis(prev, 0, 2)
    q_dec = jnp.exp(lg * (i + 1)[None])
    y_cross = jnp.einsum('bhnid,hi,bhnde->bhnie', qc, q_dec, prev)
    return (y_intra + y_cross).reshape(B, H, S, dv)


def head_group_norm(y, w, eps=GN_EPS):
    mu = jnp.mean(y, axis=-1, keepdims=True)
    var = jnp.mean(jnp.square(y - mu), axis=-1, keepdims=True)
    return merge_heads((y - mu) * lax.rsqrt(var + eps)) * w.astype(jnp.float32)


def parallel_mixer(h, w_in, q_norm_w, k_norm_w, decay_f, decay_b, gn_w, w_out):
    f32 = jnp.float32
    proj = h @ w_in
    a3 = 3 * ATTN_WIDTH
    cuts = [ATTN_WIDTH, 2 * ATTN_WIDTH, a3, a3 + RET_QK_WIDTH, a3 + 2 * RET_QK_WIDTH,
            a3 + 2 * RET_QK_WIDTH + RET_V_WIDTH]
    aq, ak, av, rq, rk, rv, rg = jnp.split(proj, cuts, axis=-1)
    q = rms_norm(split_heads(aq, ATTN_HEADS, ATTN_HEAD_DIM), q_norm_w) * (ATTN_HEAD_DIM ** -0.5)
    k = rms_norm(split_heads(ak, ATTN_HEADS, ATTN_HEAD_DIM), k_norm_w)
    v = split_heads(av, ATTN_HEADS, ATTN_HEAD_DIM)
    slopes = alibi_slopes(ATTN_HEADS)
    outs, lses = [], []
    for window, dilation in DILATED_PATTERNS:
        o, lse = dilated_window_attention(q, k, v, slopes, window, dilation)
        outs.append(o)
        lses.append(lse)
    mix_w = jax.nn.softmax(jnp.stack(lses), axis=0)
    attn = jnp.einsum('pbhs,pbhsd->bhsd', mix_w, jnp.stack(outs))
    attn = merge_heads(attn).astype(h.dtype)
    qr = split_heads(rq, RET_HEADS, RET_QK_DIM).astype(f32)
    kr = split_heads(rk, RET_HEADS, RET_QK_DIM).astype(f32) * (RET_QK_DIM ** -0.5)
    vr = split_heads(rv, RET_HEADS, RET_V_DIM).astype(f32)
    lg_f = jax.nn.log_sigmoid(decay_f.astype(f32))
    lg_b = jax.nn.log_sigmoid(decay_b.astype(f32))
    y_f = retention_chunkwise(qr, kr, vr, lg_f, inclusive=True)
    flip = lambda t: jnp.flip(t, axis=2)
    y_b = flip(retention_chunkwise(flip(qr), flip(kr), flip(vr), lg_b, inclusive=False))
    ret = head_group_norm(y_f + y_b, gn_w)
    ret = (jax.nn.silu(rg.astype(f32)) * ret).astype(h.dtype)
    return jnp.concatenate([attn, ret], axis=-1) @ w_out


def memory_cross_attention(h, mem_h, w_mq, w_mkv, q_norm_w, k_norm_w, w_mo):
    q = rms_norm(split_heads(h @ w_mq, MEM_HEADS, MEM_HEAD_DIM), q_norm_w) * (MEM_HEAD_DIM ** -0.5)
    mk, mv = jnp.split(mem_h @ w_mkv, 2, axis=-1)
    k = rms_norm(split_heads(mk, MEM_HEADS, MEM_HEAD_DIM), k_norm_w)
    v = split_heads(mv, MEM_HEADS, MEM_HEAD_DIM)
    s = jnp.einsum('bhsd,bhmd->bhsm', q, k, preferred_element_type=jnp.float32)
    p = jax.nn.softmax(s, axis=-1)
    o = jnp.einsum('bhsm,bhmd->bhsd', p.astype(v.dtype), v)
    return merge_heads(o) @ w_mo


def hierarchical_moe(h, w_group, w_router, w1, w3, w2):
    B, S, D = h.shape
    N = B * S
    t = h.reshape(N, D)
    g_logits = jnp.einsum('nd,dg->ng', t, w_group, preferred_element_type=jnp.float32)
    g_prob = jax.nn.softmax(g_logits, axis=-1)
    grp = jnp.argmax(g_logits, axis=-1)
    g_gate = jnp.take_along_axis(g_prob, grp[:, None], axis=-1)
    e_logits = jnp.einsum('nd,gde->nge', t, w_router, preferred_element_type=jnp.float32)
    e_logits = jnp.take_along_axis(e_logits, grp[:, None, None], axis=1)[:, 0]
    top_vals, top_idx = lax.top_k(e_logits, EXPERT_TOP_K)
    gates = g_gate * jax.nn.softmax(top_vals, axis=-1)
    expert_id = grp[:, None] * EXPERTS_PER_GROUP + top_idx
    A = N * EXPERT_TOP_K
    e_flat = expert_id.reshape(A)
    tok_flat = jnp.repeat(jnp.arange(N, dtype=jnp.int32), EXPERT_TOP_K)
    gate_flat = gates.reshape(A)
    order = jnp.argsort(e_flat)
    e_sorted = e_flat[order]
    counts = jnp.bincount(e_flat, length=N_EXPERTS)
    starts = jnp.cumsum(counts) - counts
    pcounts = (counts + MOE_BLOCK - 1) // MOE_BLOCK * MOE_BLOCK
    pends = jnp.cumsum(pcounts)
    pstarts = pends - pcounts
    dest = pstarts[e_sorted] + jnp.arange(A) - starts[e_sorted]
    n_blocks = -(-(A + N_EXPERTS * MOE_BLOCK) // MOE_BLOCK)
    n_slots = n_blocks * MOE_BLOCK
    slot_tok = jnp.full((n_slots,), N, dtype=jnp.int32).at[dest].set(tok_flat[order])
    slot_gate = jnp.zeros((n_slots,), jnp.float32).at[dest].set(gate_flat[order])
    block_expert = jnp.clip(jnp.searchsorted(pends, jnp.arange(n_blocks) * MOE_BLOCK, side='right'),
                            0, N_EXPERTS - 1)
    t_pad = jnp.concatenate([t, jnp.zeros((1, D), t.dtype)], axis=0)

    def expert_block(args):
        toks, e = args
        xb = t_pad[toks]
        return (jax.nn.silu(xb @ w1[e]) * (xb @ w3[e])) @ w2[e]

    y_slots = lax.map(expert_block, (slot_tok.reshape(n_blocks, MOE_BLOCK), block_expert))
    y_slots = y_slots.reshape(n_slots, D) * slot_gate[:, None]
    y = jax.ops.segment_sum(y_slots, slot_tok, num_segments=N + 1)[:N]
    return y.astype(h.dtype).reshape(B, S, D)


def setup_inputs(seed: int = 0) -> dict:
    key = jax.random.key(seed)
    ks = jax.random.split(key, 24)
    f32 = jnp.float32

    def normal(k, shape, scale):
        return jax.random.normal(k, shape, f32) * scale

    def gain(k, shape):
        return 1.0 + 0.05 * jax.random.normal(k, shape, f32)

    base_logit = jnp.log(jnp.exp2(RET_BASE_EXP + jnp.arange(RET_HEADS, dtype=f32)) - 1.0)
    return {
        'x': normal(ks[0], (BATCH, SEQ, D_MODEL), 1.0),
        'mem': normal(ks[1], (BATCH, N_MEM, D_MODEL), 1.0),
        'norm_mix_w': gain(ks[2], (DEPTH, D_MODEL)),
        'w_in': normal(ks[3], (DEPTH, D_MODEL, IN_PROJ_COLS), D_MODEL ** -0.5),
        'attn_q_norm_w': gain(ks[4], (DEPTH, ATTN_HEAD_DIM)),
        'attn_k_norm_w': gain(ks[5], (DEPTH, ATTN_HEAD_DIM)),
        'ret_decay_f': base_logit + 0.1 * jax.random.normal(ks[6], (DEPTH, RET_HEADS), f32),
        'ret_decay_b': base_logit + 0.1 * jax.random.normal(ks[7], (DEPTH, RET_HEADS), f32),
        'ret_gn_w': gain(ks[8], (DEPTH, RET_V_WIDTH)),
        'w_out': normal(ks[9], (DEPTH, MIX_WIDTH, D_MODEL), MIX_WIDTH ** -0.5),
        'norm_mem_w': gain(ks[10], (DEPTH, D_MODEL)),
        'norm_memkv_w': gain(ks[11], (DEPTH, D_MODEL)),
        'w_mq': normal(ks[12], (DEPTH, D_MODEL, D_MODEL), D_MODEL ** -0.5),
        'w_mkv': normal(ks[13], (DEPTH, D_MODEL, 2 * D_MODEL), D_MODEL ** -0.5),
        'mem_q_norm_w': gain(ks[14], (DEPTH, MEM_HEAD_DIM)),
        'mem_k_norm_w': gain(ks[15], (DEPTH, MEM_HEAD_DIM)),
        'w_mo': normal(ks[16], (DEPTH, D_MODEL, D_MODEL), D_MODEL ** -0.5),
        'norm_moe_w': gain(ks[17], (DEPTH, D_MODEL)),
        'w_group': normal(ks[18], (DEPTH, D_MODEL, N_GROUPS), D_MODEL ** -0.5),
        'w_router': normal(ks[19], (DEPTH, N_GROUPS, D_MODEL, EXPERTS_PER_GROUP), D_MODEL ** -0.5),
        'w_exp_gate': normal(ks[20], (DEPTH, N_EXPERTS, D_MODEL, EXPERT_FF), D_MODEL ** -0.5),
        'w_exp_up': normal(ks[21], (DEPTH, N_EXPERTS, D_MODEL, EXPERT_FF), D_MODEL ** -0.5),
        'w_exp_down': normal(ks[22], (DEPTH, N_EXPERTS, EXPERT_FF, D_MODEL), EXPERT_FF ** -0.5),
    }


def reference(x, mem, norm_mix_w, w_in, attn_q_norm_w, attn_k_norm_w, ret_decay_f, ret_decay_b,
              ret_gn_w, w_out, norm_mem_w, norm_memkv_w, w_mq, w_mkv, mem_q_norm_w, mem_k_norm_w,
              w_mo, norm_moe_w, w_group, w_router, w_exp_gate, w_exp_up, w_exp_down):
    for l in range(DEPTH):
        x = x + parallel_mixer(rms_norm(x, norm_mix_w[l]), w_in[l], attn_q_norm_w[l], attn_k_norm_w[l],
                               ret_decay_f[l], ret_decay_b[l], ret_gn_w[l], w_out[l])
        x = x + memory_cross_attention(rms_norm(x, norm_mem_w[l]), rms_norm(mem, norm_memkv_w[l]),
                                       w_mq[l], w_mkv[l], mem_q_norm_w[l], mem_k_norm_w[l], w_mo[l])
        x = x + hierarchical_moe(rms_norm(x, norm_moe_w[l]), w_group[l], w_router[l],
                                 w_exp_gate[l], w_exp_up[l], w_exp_down[l])
    return x
```

```python
import functools

import numpy as np
import jax
import jax.numpy as jnp
from jax import lax
from jax.experimental import pallas as pl
from jax.experimental.pallas import tpu as pltpu

F32 = jnp.float32
BF16 = jnp.bfloat16
I32 = jnp.int32

NORM_EPS = 1e-6
GN_EPS = 1e-5
NEG_INF = -1e30

ATTN_HEADS = 8
ATTN_HEAD_DIM = 64
ATTN_WIDTH = ATTN_HEADS * ATTN_HEAD_DIM
ATTN_HALF = 64
DILATIONS = (1, 4, 16)
RET_HEADS = 4
RET_QK_DIM = 64
RET_V_DIM = 128
RET_QK_WIDTH = RET_HEADS * RET_QK_DIM
RET_V_WIDTH = RET_HEADS * RET_V_DIM
MEM_HEADS = 4
N_GROUPS = 4
EXPERTS_PER_GROUP = 8
N_EXPERTS = N_GROUPS * EXPERTS_PER_GROUP

LANES = 128
TOKEN_TILE = 512
ATTN_Q_TILE = 512
RET_CHUNK = 256
EXPERT_BLOCK = 256
DISPATCH_TILE = 512
COMBINE_TILE = 256
VMEM_LIMIT = 56 * 1024 * 1024


def _cparams(sem, vmem=None):
    return pltpu.CompilerParams(dimension_semantics=sem, vmem_limit_bytes=vmem)


def _split_hi_lo(x):
    hi = x.astype(BF16)
    lo = (x - hi.astype(F32)).astype(BF16)
    return hi, lo


def _rms(x, w):
    ms = jnp.mean(x * x, axis=-1, keepdims=True)
    return x * lax.rsqrt(ms + NORM_EPS) * w


def _in_proj_body(x_ref, nw_ref, w_ref, qw_ref, kw_ref, g_ref,
                  q_out, k_out, v_out, rq_out, rk_out, rv_out, rg_out):
    h = _rms(x_ref[...], nw_ref[...]).astype(BF16)

    def proj(a, b):
        return jnp.dot(h, w_ref[:, a:b], preferred_element_type=F32)

    def head_norm(p, w):
        hi, lo = _split_hi_lo(p * p)
        s = (jnp.dot(hi, g_ref[...], preferred_element_type=F32)
             + jnp.dot(lo, g_ref[...], preferred_element_type=F32))
        return p * lax.rsqrt(s * (1.0 / ATTN_HEAD_DIM) + NORM_EPS) * w

    a = ATTN_WIDTH
    q_out[...] = head_norm(proj(0, a), qw_ref[...]).astype(BF16)
    k_out[...] = head_norm(proj(a, 2 * a), kw_ref[...]).astype(BF16)
    v_out[...] = proj(2 * a, 3 * a).astype(BF16)
    c = 3 * a
    rq_out[...] = proj(c, c + RET_QK_WIDTH).astype(BF16)
    rk_out[...] = proj(c + RET_QK_WIDTH, c + 2 * RET_QK_WIDTH).astype(BF16)
    c += 2 * RET_QK_WIDTH
    rv_out[...] = proj(c, c + RET_V_WIDTH).astype(BF16)
    rg_out[...] = proj(c + RET_V_WIDTH, c + 2 * RET_V_WIDTH).astype(BF16)


def _in_proj(x2d, norm_w, w_in, q_norm_w, k_norm_w):
    n, d = x2d.shape
    cols = w_in.shape[1]
    tm = TOKEN_TILE
    qw = (jnp.tile(q_norm_w, ATTN_HEADS) * (ATTN_HEAD_DIM ** -0.5)).reshape(1, ATTN_WIDTH)
    kw = jnp.tile(k_norm_w, ATTN_HEADS).reshape(1, ATTN_WIDTH)
    head_of = np.arange(ATTN_WIDTH) // ATTN_HEAD_DIM
    gmat = jnp.asarray(head_of[:, None] == head_of[None, :], dtype=BF16)
    row = lambda w: pl.BlockSpec((tm, w), lambda i: (i, 0))
    full = lambda r, c: pl.BlockSpec((r, c), lambda i: (0, 0))
    widths = (ATTN_WIDTH, ATTN_WIDTH, ATTN_WIDTH, RET_QK_WIDTH, RET_QK_WIDTH, RET_V_WIDTH, RET_V_WIDTH)
    return pl.pallas_call(
        _in_proj_body,
        grid=(n // tm,),
        in_specs=[row(d), full(1, d), full(d, cols), full(1, ATTN_WIDTH), full(1, ATTN_WIDTH),
                  full(ATTN_WIDTH, ATTN_WIDTH)],
        out_specs=[row(w) for w in widths],
        out_shape=[jax.ShapeDtypeStruct((n, w), BF16) for w in widths],
        compiler_params=_cparams(("parallel",), VMEM_LIMIT),
        name="in_proj",
    )(x2d, norm_w.reshape(1, d), w_in.astype(BF16), qw, kw, gmat)


def _attn_bias(dilation):
    half = ATTN_HALF
    qi = np.arange(half)[:, None]
    kc = np.arange(3 * half)[None, :]
    delta = kc - half - qi
    band = np.abs(delta) <= half
    slopes = np.exp2(-8.0 * np.arange(1, ATTN_HEADS + 1) / ATTN_HEADS)
    valid = [band & (kc >= half), band, band & (kc < 2 * half)]
    out = np.empty((ATTN_HEADS // 2, 3, 2 * half, 3 * half), np.float32)
    for p in range(ATTN_HEADS // 2):
        for v in range(3):
            for s in range(2):
                b = -slopes[2 * p + s] * (dilation * np.abs(delta)).astype(np.float32)
                out[p, v, s * half:(s + 1) * half] = np.where(valid[v], b, NEG_INF)
    return jnp.asarray(out.reshape(-1, 2 * half, 3 * half))


def _attn_body(q_ref, kp_ref, km_ref, kn_ref, vp_ref, vm_ref, vn_ref, bias_ref, o_ref, lse_ref,
               kcat, vcat, *, tq):
    half = ATTN_HALF
    i = pl.program_id(1)
    n_blocks = pl.num_programs(1) * (tq // half)
    kcat[0:half] = kp_ref[0]
    kcat[half:half + tq] = km_ref[0]
    kcat[half + tq:] = kn_ref[0]
    vcat[0:half] = vp_ref[0]
    vcat[half:half + tq] = vm_ref[0]
    vcat[half + tq:] = vn_ref[0]
    lo_lanes = lax.broadcasted_iota(I32, (half, LANES), 1) < ATTN_HEAD_DIM
    ones = jnp.ones((3 * half, LANES), BF16)

    def block(jb, carry):
        r0 = pl.multiple_of(jb * half, half)
        gb = i * (tq // half) + jb
        variant = jnp.where(gb == 0, 0, jnp.where(gb == n_blocks - 1, 2, 1))
        for p in range(ATTN_HEADS // 2):
            cs = slice(p * LANES, (p + 1) * LANES)
            qp = q_ref[0, pl.ds(r0, half), cs]
            zero = jnp.zeros_like(qp)
            qs = jnp.concatenate([jnp.where(lo_lanes, qp, zero), jnp.where(lo_lanes, zero, qp)], axis=0)
            kp = kcat[pl.ds(r0, 3 * half), cs]
            s = lax.dot_general(qs, kp, (((1,), (1,)), ((), ())), preferred_element_type=F32)
            s = s + bias_ref[p * 3 + variant]
            m = jnp.max(s, axis=-1, keepdims=True)
            e = jnp.exp(s - m).astype(BF16)
            vext = jnp.concatenate([vcat[pl.ds(r0, 3 * half), cs], ones], axis=1)
            r = jnp.dot(e, vext, preferred_element_type=F32)
            denom = r[:, LANES:]
            o = r[:, :LANES] / denom
            lse = m + jnp.log(denom)
            o_ref[0, pl.ds(r0, half), cs] = jnp.where(lo_lanes, o[:half], o[half:]).astype(o_ref.dtype)
            lse_ref[0, pl.ds(r0, half), cs] = jnp.where(lo_lanes, lse[:half], lse[half:])
        return carry

    lax.fori_loop(0, tq // half, block, 0)


def _attention_pattern(q, k, v, dilation):
    g, l, w = q.shape
    half = ATTN_HALF
    tq = min(ATTN_Q_TILE, l)
    hb = tq // half
    last = l // half - 1
    main = pl.BlockSpec((1, tq, w), lambda a, i: (a, i, 0))
    prev = pl.BlockSpec((1, half, w), lambda a, i: (a, jnp.maximum(i * hb - 1, 0), 0))
    nxt = pl.BlockSpec((1, half, w), lambda a, i: (a, jnp.minimum((i + 1) * hb, last), 0))
    bias = _attn_bias(dilation)
    return pl.pallas_call(
        functools.partial(_attn_body, tq=tq),
        grid=(g, l // tq),
        in_specs=[main, prev, main, nxt, prev, main, nxt,
                  pl.BlockSpec(bias.shape, lambda a, i: (0, 0, 0))],
        out_specs=[main, main],
        out_shape=[jax.ShapeDtypeStruct((g, l, w), BF16), jax.ShapeDtypeStruct((g, l, w), F32)],
        scratch_shapes=[pltpu.VMEM((tq + 2 * half, w), BF16), pltpu.VMEM((tq + 2 * half, w), BF16)],
        compiler_params=_cparams(("parallel", "parallel"), VMEM_LIMIT),
        name=f"attention_d{dilation}",
    )(q, k, k, k, v, v, v, bias)


def _to_subsequences(t, b, s, dilation):
    w = t.shape[-1]
    if dilation == 1:
        return t.reshape(b, s, w)
    return t.reshape(b, s // dilation, dilation, w).transpose(0, 2, 1, 3).reshape(b * dilation, s // dilation, w)


def _from_subsequences(t, b, s, dilation):
    w = t.shape[-1]
    if dilation == 1:
        return t.reshape(b * s, w)
    return t.reshape(b, dilation, s // dilation, w).transpose(0, 2, 1, 3).reshape(b * s, w)


def _ret_body(lg_ref, q_ref, k_ref, v_ref, g_ref, gnw_ref, out_ref,
              dmat, qdec, kdec, fstate, rstate, rall, *, chunk):
    c = chunk
    b = pl.program_id(0)
    ph = pl.program_id(1)
    n = pl.program_id(2)
    n_chunks = pl.num_programs(2)
    k_scale = RET_QK_DIM ** -0.5

    @pl.when((b == 0) & (ph == 0) & (n == 0))
    def _init_tables():
        ii = lax.broadcasted_iota(I32, (c, c), 0)
        jj = lax.broadcasted_iota(I32, (c, c), 1)
        fwd = (ii - jj).astype(F32)
        ri = lax.broadcasted_iota(I32, (c, RET_QK_DIM), 0).astype(F32)
        for h in range(RET_HEADS):
            lf = lg_ref[0, h]
            lb = lg_ref[1, h]
            dmat[h] = jnp.where(ii >= jj, jnp.exp(lf * fwd), jnp.exp(-lb * fwd)) * k_scale
            qdec[0, h] = jnp.exp(lf * (ri + 1.0))
            qdec[1, h] = jnp.exp(lb * (c - ri))
            kdec[0, h] = jnp.exp(lf * (c - 1.0 - ri)) * k_scale
            kdec[1, h] = jnp.exp(lb * ri) * k_scale

    def chunk_decay(direction, h):
        return jnp.exp(jnp.full((RET_QK_DIM, RET_V_DIM), lg_ref[direction, h] * c, F32))

    def kv_update(direction, h):
        kh = k_ref[0, :, h * RET_QK_DIM:(h + 1) * RET_QK_DIM].astype(F32)
        ks = (kh * kdec[direction, h]).astype(BF16)
        vh = v_ref[0, :, h * RET_V_DIM:(h + 1) * RET_V_DIM]
        return lax.dot_general(ks, vh, (((0,), (0,)), ((), ())), preferred_element_type=F32)

    @pl.when(ph == 0)
    def _right_to_left():
        @pl.when(n == 0)
        def _():
            rstate[...] = jnp.zeros_like(rstate)

        ci = n_chunks - 1 - n
        for h in range(RET_HEADS):
            st = rstate[h]
            rall[ci, h] = st
            rstate[h] = st * chunk_decay(1, h) + kv_update(1, h)

    @pl.when(ph == 1)
    def _left_to_right():
        @pl.when(n == 0)
        def _():
            fstate[...] = jnp.zeros_like(fstate)

        for h in range(RET_HEADS):
            qs = slice(h * RET_QK_DIM, (h + 1) * RET_QK_DIM)
            vs = slice(h * RET_V_DIM, (h + 1) * RET_V_DIM)
            qh = q_ref[0, :, qs]
            kh = k_ref[0, :, qs]
            vh = v_ref[0, :, vs]
            s = lax.dot_general(qh, kh, (((1,), (1,)), ((), ())), preferred_element_type=F32) * dmat[h]
            y = jnp.dot(s.astype(BF16), vh, preferred_element_type=F32)
            qf = qh.astype(F32)
            qcat = jnp.concatenate([(qf * qdec[0, h]).astype(BF16), (qf * qdec[1, h]).astype(BF16)], axis=1)
            st = fstate[h]
            states = jnp.concatenate([st, rall[n, h]], axis=0).astype(BF16)
            y = y + jnp.dot(qcat, states, preferred_element_type=F32)
            fstate[h] = st * chunk_decay(0, h) + kv_update(0, h)
            mu = jnp.mean(y, axis=-1, keepdims=True)
            yc = y - mu
            var = jnp.mean(yc * yc, axis=-1, keepdims=True)
            yn = yc * lax.rsqrt(var + GN_EPS) * gnw_ref[:, vs]
            gate = g_ref[0, :, vs].astype(F32)
            out_ref[0, :, vs] = (gate * jax.nn.sigmoid(gate) * yn).astype(out_ref.dtype)


def _retention(rq, rk, rv, rg, decay_f, decay_b, gn_w, b, s):
    c = RET_CHUNK
    nc = s // c
    lg = jnp.stack([jax.nn.log_sigmoid(decay_f.astype(F32)), jax.nn.log_sigmoid(decay_b.astype(F32))])
    qk_w, v_w = RET_QK_WIDTH, RET_V_WIDTH

    def both(bb, ph, n):
        return (bb, jnp.where(ph == 0, nc - 1 - n, n), 0)

    def fwd_only(bb, ph, n):
        return (bb, jnp.where(ph == 0, 0, n), 0)

    out = pl.pallas_call(
        functools.partial(_ret_body, chunk=c),
        grid=(b, 2, nc),
        in_specs=[pl.BlockSpec(memory_space=pltpu.SMEM),
                  pl.BlockSpec((1, c, qk_w), fwd_only),
                  pl.BlockSpec((1, c, qk_w), both),
                  pl.BlockSpec((1, c, v_w), both),
                  pl.BlockSpec((1, c, v_w), fwd_only),
                  pl.BlockSpec((1, v_w), lambda bb, ph, n: (0, 0))],
        out_specs=pl.BlockSpec((1, c, v_w), fwd_only),
        out_shape=jax.ShapeDtypeStruct((b, s, v_w), BF16),
        scratch_shapes=[pltpu.VMEM((RET_HEADS, c, c), F32),
                        pltpu.VMEM((2, RET_HEADS, c, RET_QK_DIM), F32),
                        pltpu.VMEM((2, RET_HEADS, c, RET_QK_DIM), F32),
                        pltpu.VMEM((RET_HEADS, RET_QK_DIM, RET_V_DIM), F32),
                        pltpu.VMEM((RET_HEADS, RET_QK_DIM, RET_V_DIM), F32),
                        pltpu.VMEM((nc, RET_HEADS, RET_QK_DIM, RET_V_DIM), F32)],
        compiler_params=_cparams(("arbitrary", "arbitrary", "arbitrary"), VMEM_LIMIT),
        name="retention",
    )(lg, rq.reshape(b, s, qk_w), rk.reshape(b, s, qk_w), rv.reshape(b, s, v_w), rg.reshape(b, s, v_w),
      gn_w.reshape(1, v_w))
    return out.reshape(b * s, v_w)


def _out_proj_body(o1, o2, o3, l1, l2, l3, ret_ref, x_ref, w_ref, out_ref):
    a1, a2, a3 = l1[...], l2[...], l3[...]
    m = jnp.maximum(jnp.maximum(a1, a2), a3)
    e1, e2, e3 = jnp.exp(a1 - m), jnp.exp(a2 - m), jnp.exp(a3 - m)
    attn = (e1 * o1[...].astype(F32) + e2 * o2[...].astype(F32) + e3 * o3[...].astype(F32)) / (e1 + e2 + e3)
    y = jnp.dot(attn.astype(BF16), w_ref[:ATTN_WIDTH], preferred_element_type=F32)
    y = y + jnp.dot(ret_ref[...], w_ref[ATTN_WIDTH:], preferred_element_type=F32)
    out_ref[...] = x_ref[...] + y


def _out_proj(outs, lses, ret, x2d, w_out):
    n, d = x2d.shape
    tm = TOKEN_TILE
    row = lambda w: pl.BlockSpec((tm, w), lambda i: (i, 0))
    return pl.pallas_call(
        _out_proj_body,
        grid=(n // tm,),
        in_specs=[row(ATTN_WIDTH)] * 6 + [row(RET_V_WIDTH), row(d),
                                          pl.BlockSpec(w_out.shape, lambda i: (0, 0))],
        out_specs=row(d),
        out_shape=jax.ShapeDtypeStruct((n, d), F32),
        compiler_params=_cparams(("parallel",), VMEM_LIMIT),
        name="out_proj",
    )(*outs, *lses, ret, x2d, w_out.astype(BF16))


def _mem_kv_body(mem_ref, nw_ref, w_ref, kw_ref, k_out, v_out):
    d = mem_ref.shape[-1]
    hd = d // MEM_HEADS
    h = _rms(mem_ref[0], nw_ref[...]).astype(BF16)
    kv = jnp.dot(h, w_ref[...], preferred_element_type=F32)
    for i in range(MEM_HEADS):
        k_out[0, :, i * hd:(i + 1) * hd] = _rms(kv[:, i * hd:(i + 1) * hd], kw_ref[...]).astype(BF16)
    v_out[0] = kv[:, d:].astype(BF16)


def _mem_kv(mem, norm_w, w_mkv, k_norm_w):
    b, m, d = mem.shape
    return pl.pallas_call(
        _mem_kv_body,
        grid=(b,),
        in_specs=[pl.BlockSpec((1, m, d), lambda i: (i, 0, 0)),
                  pl.BlockSpec((1, d), lambda i: (0, 0)),
                  pl.BlockSpec((d, 2 * d), lambda i: (0, 0)),
                  pl.BlockSpec((1, d // MEM_HEADS), lambda i: (0, 0))],
        out_specs=[pl.BlockSpec((1, m, d), lambda i: (i, 0, 0))] * 2,
        out_shape=[jax.ShapeDtypeStruct((b, m, d), BF16)] * 2,
        compiler_params=_cparams(("parallel",), VMEM_LIMIT),
        name="mem_kv",
    )(mem, norm_w.reshape(1, d), w_mkv.astype(BF16), k_norm_w.reshape(1, -1))


def _mem_attn_body(x_ref, nw_ref, wq_ref, qw_ref, k_ref, v_ref, wo_ref, out_ref):
    d = x_ref.shape[-1]
    hd = d // MEM_HEADS
    x = x_ref[0]
    h = _rms(x, nw_ref[...]).astype(BF16)
    q = jnp.dot(h, wq_ref[...], preferred_element_type=F32)
    heads = []
    for i in range(MEM_HEADS):
        cs = slice(i * hd, (i + 1) * hd)
        qn = _rms(q[:, cs], qw_ref[...]).astype(BF16)
        s = lax.dot_general(qn, k_ref[0, :, cs], (((1,), (1,)), ((), ())), preferred_element_type=F32)
        e = jnp.exp(s - jnp.max(s, axis=-1, keepdims=True))
        o = jnp.dot(e.astype(BF16), v_ref[0, :, cs], preferred_element_type=F32)
        heads.append((o / jnp.sum(e, axis=-1, keepdims=True)).astype(BF16))
    o = jnp.concatenate(heads, axis=1)
    out_ref[0] = x + jnp.dot(o, wo_ref[...], preferred_element_type=F32)


def _mem_attn(x1, mk, mv, norm_w, w_mq, q_norm_w, w_mo):
    b, s, d = x1.shape
    m = mk.shape[1]
    tm = TOKEN_TILE
    hd = d // MEM_HEADS
    qw = (q_norm_w * (hd ** -0.5)).reshape(1, hd)
    tok = pl.BlockSpec((1, tm, d), lambda bb, i: (bb, i, 0))
    const = lambda r, c: pl.BlockSpec((r, c), lambda bb, i: (0, 0))
    mem = pl.BlockSpec((1, m, d), lambda bb, i: (bb, 0, 0))
    return pl.pallas_call(
        _mem_attn_body,
        grid=(b, s // tm),
        in_specs=[tok, const(1, d), const(d, d), const(1, hd), mem, mem, const(d, d)],
        out_specs=tok,
        out_shape=jax.ShapeDtypeStruct((b, s, d), F32),
        compiler_params=_cparams(("parallel", "parallel"), VMEM_LIMIT),
        name="mem_attn",
    )(x1, norm_w.reshape(1, d), w_mq.astype(BF16), qw, mk, mv, w_mo.astype(BF16))


def _router_body(x_ref, nw_ref, whi_ref, wlo_ref, h_out, gate_out, meta_out, cnt_out, carry):
    tm = x_ref.shape[0]
    i = pl.program_id(0)

    @pl.when(i == 0)
    def _():
        carry[...] = jnp.zeros_like(carry)

    h = _rms(x_ref[...], nw_ref[...])
    h_out[...] = h
    hi, lo = _split_hi_lo(h)
    logits = (jnp.dot(hi, whi_ref[...], preferred_element_type=F32)
              + jnp.dot(lo, whi_ref[...], preferred_element_type=F32)
              + jnp.dot(hi, wlo_ref[...], preferred_element_type=F32))
    lane = lax.broadcasted_iota(I32, (tm, LANES), 1)

    def argmax_lanes(vals):
        top = jnp.max(vals, axis=-1, keepdims=True)
        idx = jnp.min(jnp.where(vals == top, lane, LANES), axis=-1, keepdims=True)
        return top, idx

    is_group = lane < N_GROUPS
    g_top, grp = argmax_lanes(jnp.where(is_group, logits, -jnp.inf))
    g_gate = 1.0 / jnp.sum(jnp.where(is_group, jnp.exp(logits - g_top), 0.0), axis=-1, keepdims=True)
    in_group = (lane >= N_GROUPS) & (lane < N_GROUPS + N_EXPERTS) & (
        ((lane - N_GROUPS) // EXPERTS_PER_GROUP) == grp)
    el = jnp.where(in_group, logits, -jnp.inf)
    t1, i1 = argmax_lanes(el)
    t2, i2 = argmax_lanes(jnp.where(lane == i1, -jnp.inf, el))
    z = jnp.exp(t2 - t1)
    g1 = g_gate / (1.0 + z)
    g2 = g_gate * z / (1.0 + z)
    e1 = i1 - N_GROUPS
    e2 = i2 - N_GROUPS

    oh1 = (lane == e1)
    oh2 = (lane == e2)
    oh1b = oh1.astype(F32).astype(BF16)
    oh2b = oh2.astype(F32).astype(BF16)
    rr = lax.broadcasted_iota(I32, (tm, tm), 0)
    cc = lax.broadcasted_iota(I32, (tm, tm), 1)
    below = (cc < rr).astype(F32).astype(BF16)
    pre1 = jnp.dot(below, oh1b, preferred_element_type=F32)
    pre2 = jnp.dot(below, oh2b, preferred_element_type=F32)
    cnt1 = jnp.sum(oh1.astype(F32), axis=0, keepdims=True)
    cnt2 = jnp.sum(oh2.astype(F32), axis=0, keepdims=True)
    base = carry[...]
    r1 = jnp.sum(jnp.where(oh1, pre1 + base, 0.0), axis=-1, keepdims=True)
    r2 = jnp.sum(jnp.where(oh2, pre2 + base + cnt1, 0.0), axis=-1, keepdims=True)
    total = base + cnt1 + cnt2
    carry[...] = total
    cnt_out[...] = total

    gate_out[...] = jnp.where(lane == 0, g1, jnp.where(lane == 1, g2, 0.0))
    meta_out[...] = jnp.where(lane == 0, e1, jnp.where(lane == 1, e2, jnp.where(
        lane == 2, r1.astype(I32), jnp.where(lane == 3, r2.astype(I32), 0))))


def _router(x2d, norm_w, w_group, w_router):
    n, d = x2d.shape
    tm = TOKEN_TILE
    w_all = jnp.concatenate([w_group, w_router.transpose(1, 0, 2).reshape(d, N_EXPERTS)], axis=1)
    w_all = jnp.pad(w_all, ((0, 0), (0, LANES - w_all.shape[1])))
    whi = w_all.astype(BF16)
    wlo = (w_all - whi.astype(F32)).astype(BF16)
    row = lambda w: pl.BlockSpec((tm, w), lambda i: (i, 0))
    const = lambda r, c: pl.BlockSpec((r, c), lambda i: (0, 0))
    return pl.pallas_call(
        _router_body,
        grid=(n // tm,),
        in_specs=[row(d), const(1, d), const(d, LANES), const(d, LANES)],
        out_specs=[row(d), row(LANES), row(LANES), const(1, LANES)],
        out_shape=[jax.ShapeDtypeStruct((n, d), F32), jax.ShapeDtypeStruct((n, LANES), F32),
                   jax.ShapeDtypeStruct((n, LANES), I32), jax.ShapeDtypeStruct((1, LANES), F32)],
        scratch_shapes=[pltpu.VMEM((1, LANES), F32)],
        compiler_params=_cparams(("arbitrary",), VMEM_LIMIT),
        name="router",
    )(x2d, norm_w.reshape(1, d), whi, wlo)


def _row_copy(src_ref, src_row, dst_ref, dst_row, sem):
    return pltpu.make_async_copy(src_ref.at[pl.ds(src_row, 1)], dst_ref.at[pl.ds(dst_row, 1)], sem)


def _dispatch_body(cnt_ref, eid_ref, rank_ref, h_ref, xs_ref, dest_ref, pstart, zbuf, sems):
    i = pl.program_id(0)
    tt = h_ref.shape[0]
    blk = EXPERT_BLOCK

    def pad_copy(pend):
        return pltpu.make_async_copy(zbuf, xs_ref.at[pl.ds(pl.multiple_of(pend - blk, blk), blk)], sems.at[2])

    @pl.when(i == 0)
    def _():
        zbuf[...] = jnp.zeros_like(zbuf)

        def start(e, acc):
            pstart[e] = acc
            pend = acc + ((cnt_ref[e] + (blk - 1)) // blk) * blk

            @pl.when(pend > acc)
            def _():
                pad_copy(pend).start()
            return pend

        used = lax.fori_loop(0, N_EXPERTS, start, jnp.int32(0))

        def tail_start(j, carry):
            pad_copy((j + 1) * blk).start()
            return carry

        lax.fori_loop(used // blk, xs_ref.shape[0] // blk, tail_start, 0)

        def finish(e, acc):
            pend = acc + ((cnt_ref[e] + (blk - 1)) // blk) * blk

            @pl.when(pend > acc)
            def _():
                pad_copy(pend).wait()
            return pend

        lax.fori_loop(0, N_EXPERTS, finish, jnp.int32(0))

        def tail_wait(j, carry):
            pad_copy((j + 1) * blk).wait()
            return carry

        lax.fori_loop(used // blk, xs_ref.shape[0] // blk, tail_wait, 0)

    def token(t, carry):
        for k in range(2):
            a = 2 * t + k
            d = pstart[eid_ref[a]] + rank_ref[a]
            dest_ref[a] = d
            _row_copy(h_ref, t, xs_ref, d, sems.at[k]).start()
        return carry

    lax.fori_loop(0, tt, token, 0)
    for k in range(2):
        pltpu.make_async_copy(h_ref, xs_ref.at[pl.ds(0, tt)], sems.at[k]).wait()


def _dispatch(counts, eid, rank, h3, n_slots):
    n, d = h3.shape
    tt = DISPATCH_TILE
    smem_blk = pl.BlockSpec((2 * tt,), lambda i, cnt: (i,), memory_space=pltpu.SMEM)
    grid_spec = pltpu.PrefetchScalarGridSpec(
        num_scalar_prefetch=1,
        grid=(n // tt,),
        in_specs=[smem_blk, smem_blk, pl.BlockSpec((tt, d), lambda i, cnt: (i, 0))],
        out_specs=[pl.BlockSpec(memory_space=pl.ANY), smem_blk],
        scratch_shapes=[pltpu.SMEM((N_EXPERTS,), I32), pltpu.VMEM((EXPERT_BLOCK, d), F32),
                        pltpu.SemaphoreType.DMA((3,))],
    )
    return pl.pallas_call(
        _dispatch_body,
        grid_spec=grid_spec,
        out_shape=[jax.ShapeDtypeStruct((n_slots, d), F32), jax.ShapeDtypeStruct((2 * n,), I32)],
        compiler_params=_cparams(("arbitrary",), VMEM_LIMIT),
        name="dispatch",
    )(counts, eid, rank, h3)


def _experts_body(be_ref, nu_ref, xs_ref, w1_ref, w3_ref, w2_ref, ys_ref, w1b, w3b, w2b):
    i = pl.program_id(0)
    n_used = nu_ref[0]

    @pl.when(i < n_used)
    def _():
        changed = (i == 0) | (be_ref[i] != be_ref[jnp.maximum(i - 1, 0)])

        @pl.when(changed)
        def _():
            w1b[...] = w1_ref[0].astype(BF16)
            w3b[...] = w3_ref[0].astype(BF16)
            w2b[...] = w2_ref[0].astype(BF16)

        x = xs_ref[...].astype(BF16)
        a = jnp.dot(x, w1b[...], preferred_element_type=F32)
        u = jnp.dot(x, w3b[...], preferred_element_type=F32)
        mid = (a * jax.nn.sigmoid(a) * u).astype(BF16)
        ys_ref[...] = jnp.dot(mid, w2b[...], preferred_element_type=F32)

    @pl.when(i >= n_used)
    def _():
        ys_ref[...] = jnp.zeros_like(ys_ref)


def _experts(block_expert, n_used, xs, w1, w3, w2):
    n_slots, d = xs.shape
    ff = w1.shape[-1]
    blk = EXPERT_BLOCK
    n_blocks = n_slots // blk

    def slot_map(i, be, nu):
        return (jnp.minimum(i, nu[0] - 1), 0)

    def w_map(i, be, nu):
        return (be[jnp.minimum(i, nu[0] - 1)], 0, 0)

    grid_spec = pltpu.PrefetchScalarGridSpec(
        num_scalar_prefetch=2,
        grid=(n_blocks,),
        in_specs=[pl.BlockSpec((blk, d), slot_map),
                  pl.BlockSpec((1, d, ff), w_map), pl.BlockSpec((1, d, ff), w_map),
                  pl.BlockSpec((1, ff, d), w_map)],
        out_specs=pl.BlockSpec((blk, d), lambda i, be, nu: (i, 0)),
        scratch_shapes=[pltpu.VMEM((d, ff), BF16), pltpu.VMEM((d, ff), BF16), pltpu.VMEM((ff, d), BF16)],
    )
    return pl.pallas_call(
        _experts_body,
        grid_spec=grid_spec,
        out_shape=jax.ShapeDtypeStruct((n_slots, d), F32),
        compiler_params=_cparams(("arbitrary",), VMEM_LIMIT),
        name="experts",
    )(block_expert, n_used, xs, w1, w3, w2)


def _combine_body(dest_ref, gate_ref, x_ref, ys_ref, out_ref, buf, sems):
    tt = x_ref.shape[0]

    def token(t, carry):
        for k in range(2):
            _row_copy(ys_ref, dest_ref[2 * t + k], buf.at[k], t, sems.at[k]).start()
        return carry

    lax.fori_loop(0, tt, token, 0)
    for k in range(2):
        pltpu.make_async_copy(ys_ref.at[pl.ds(0, tt)], buf.at[k], sems.at[k]).wait()
    g = gate_ref[...]
    out_ref[...] = x_ref[...] + g[:, 0:1] * buf[0] + g[:, 1:2] * buf[1]


def _combine(dest, gates, x2d, ys):
    n, d = x2d.shape
    tt = COMBINE_TILE
    row = lambda w: pl.BlockSpec((tt, w), lambda i: (i, 0))
    return pl.pallas_call(
        _combine_body,
        grid=(n // tt,),
        in_specs=[pl.BlockSpec((2 * tt,), lambda i: (i,), memory_space=pltpu.SMEM), row(LANES), row(d),
                  pl.BlockSpec(memory_space=pl.ANY)],
        out_specs=row(d),
        out_shape=jax.ShapeDtypeStruct((n, d), F32),
        scratch_shapes=[pltpu.VMEM((2, tt, d), F32), pltpu.SemaphoreType.DMA((2,))],
        compiler_params=_cparams(("arbitrary",), VMEM_LIMIT),
        name="combine",
    )(dest, gates, x2d, ys)


def _moe(x2d, norm_w, w_group, w_router, w1, w3, w2):
    n, d = x2d.shape
    blk = EXPERT_BLOCK
    h3, gates, meta, cnt = _router(x2d, norm_w, w_group, w_router)
    counts = cnt[0, :N_EXPERTS].astype(I32)
    eid = meta[:, 0:2].reshape(-1)
    rank = meta[:, 2:4].reshape(-1)
    n_slots = 2 * n + N_EXPERTS * blk
    n_blocks = n_slots // blk
    pends = jnp.cumsum((counts + blk - 1) // blk * blk)
    block_expert = jnp.clip(jnp.searchsorted(pends, jnp.arange(n_blocks, dtype=I32) * blk, side="right"),
                            0, N_EXPERTS - 1).astype(I32)
    n_used = (pends[-1:] // blk).astype(I32)
    xs, dest = _dispatch(counts, eid, rank, h3, n_slots)
    ys = _experts(block_expert, n_used, xs, w1, w3, w2)
    return _combine(dest, gates, x2d, ys)


def _layer(x, mem, norm_mix_w, w_in, attn_q_norm_w, attn_k_norm_w, ret_decay_f, ret_decay_b, ret_gn_w, w_out,
           norm_mem_w, norm_memkv_w, w_mq, w_mkv, mem_q_norm_w, mem_k_norm_w, w_mo,
           norm_moe_w, w_group, w_router, w_exp_gate, w_exp_up, w_exp_down):
    b, s, d = x.shape
    x2d = x.reshape(b * s, d)
    q, k, v, rq, rk, rv, rg = _in_proj(x2d, norm_mix_w, w_in, attn_q_norm_w, attn_k_norm_w)
    outs, lses = [], []
    for dil in DILATIONS:
        o, lse = _attention_pattern(*(_to_subsequences(t, b, s, dil) for t in (q, k, v)), dil)
        outs.append(_from_subsequences(o, b, s, dil))
        lses.append(_from_subsequences(lse, b, s, dil))
    ret = _retention(rq, rk, rv, rg, ret_decay_f, ret_decay_b, ret_gn_w, b, s)
    x1 = _out_proj(outs, lses, ret, x2d, w_out)
    mk, mv = _mem_kv(mem, norm_memkv_w, w_mkv, mem_k_norm_w)
    x2 = _mem_attn(x1.reshape(b, s, d), mk, mv, norm_mem_w, w_mq, mem_q_norm_w, w_mo)
    x3 = _moe(x2.reshape(b * s, d), norm_moe_w, w_group, w_router, w_exp_gate, w_exp_up, w_exp_down)
    return x3.reshape(b, s, d)


def kernel(x, mem, norm_mix_w, w_in, attn_q_norm_w, attn_k_norm_w, ret_decay_f, ret_decay_b, ret_gn_w, w_out,
           norm_mem_w, norm_memkv_w, w_mq, w_mkv, mem_q_norm_w, mem_k_norm_w, w_mo, norm_moe_w, w_group,
           w_router, w_exp_gate, w_exp_up, w_exp_down):
    depth = norm_mix_w.shape[0]
    for l in range(depth):
        x = _layer(x, mem, norm_mix_w[l], w_in[l], attn_q_norm_w[l], attn_k_norm_w[l], ret_decay_f[l],
                   ret_decay_b[l], ret_gn_w[l], w_out[l], norm_mem_w[l], norm_memkv_w[l], w_mq[l], w_mkv[l],
                   mem_q_norm_w[l], mem_k_norm_w[l], w_mo[l], norm_moe_w[l], w_group[l], w_router[l],
                   w_exp_gate[l], w_exp_up[l], w_exp_down[l])
    return x
```

```python
import functools

import numpy as np
import jax
import jax.numpy as jnp
from jax import lax
from jax.experimental import pallas as pl
from jax.experimental.pallas import tpu as pltpu

F32 = jnp.float32
BF16 = jnp.bfloat16
I32 = jnp.int32

NORM_EPS = 1e-6
GN_EPS = 1e-5
NEG_INF = -1e30

ATTN_HEADS = 8
ATTN_HEAD_DIM = 64
ATTN_WIDTH = ATTN_HEADS * ATTN_HEAD_DIM
ATTN_HALF = 64
DILATIONS = (1, 4, 16)
RET_HEADS = 4
RET_QK_DIM = 64
RET_V_DIM = 128
RET_QK_WIDTH = RET_HEADS * RET_QK_DIM
RET_V_WIDTH = RET_HEADS * RET_V_DIM
MEM_HEADS = 4
N_GROUPS = 4
EXPERTS_PER_GROUP = 8
N_EXPERTS = N_GROUPS * EXPERTS_PER_GROUP

LANES = 128
TOKEN_TILE = 512
ATTN_Q_TILE = 512
ATTN_BLOCKS_PER_STEP = 4
LOG2_E = 1.4426950408889634
RET_CHUNK = 256
EXPERT_BLOCK = 256
DISPATCH_TILE = 512
COMBINE_TILE = 256
VMEM_LIMIT = 56 * 1024 * 1024


def _cparams(sem, vmem=None):
    return pltpu.CompilerParams(dimension_semantics=sem, vmem_limit_bytes=vmem)


def _split_hi_lo(x):
    hi = x.astype(BF16)
    lo = (x - hi.astype(F32)).astype(BF16)
    return hi, lo


def _rms(x, w):
    ms = jnp.mean(x * x, axis=-1, keepdims=True)
    return x * lax.rsqrt(ms + NORM_EPS) * w


def _in_proj_body(x_ref, nw_ref, w_ref, qw_ref, kw_ref, g_ref,
                  q_out, k_out, v_out, rq_out, rk_out, rv_out, rg_out):
    h = _rms(x_ref[...], nw_ref[...]).astype(BF16)

    def proj(a, b):
        return jnp.dot(h, w_ref[:, a:b], preferred_element_type=F32)

    def head_norm(p, w):
        hi, lo = _split_hi_lo(p * p)
        s = (jnp.dot(hi, g_ref[...], preferred_element_type=F32)
             + jnp.dot(lo, g_ref[...], preferred_element_type=F32))
        return p * lax.rsqrt(s * (1.0 / ATTN_HEAD_DIM) + NORM_EPS) * w

    a = ATTN_WIDTH
    q_out[...] = head_norm(proj(0, a), qw_ref[...]).astype(BF16)
    k_out[...] = head_norm(proj(a, 2 * a), kw_ref[...]).astype(BF16)
    v_out[...] = proj(2 * a, 3 * a).astype(BF16)
    c = 3 * a
    rq_out[...] = proj(c, c + RET_QK_WIDTH).astype(BF16)
    rk_out[...] = proj(c + RET_QK_WIDTH, c + 2 * RET_QK_WIDTH).astype(BF16)
    c += 2 * RET_QK_WIDTH
    rv_out[...] = proj(c, c + RET_V_WIDTH).astype(BF16)
    rg_out[...] = proj(c + RET_V_WIDTH, c + 2 * RET_V_WIDTH).astype(BF16)


def _in_proj(x2d, norm_w, w_in, q_norm_w, k_norm_w):
    n, d = x2d.shape
    cols = w_in.shape[1]
    tm = TOKEN_TILE
    qw = (jnp.tile(q_norm_w, ATTN_HEADS) * (ATTN_HEAD_DIM ** -0.5 * LOG2_E)).reshape(1, ATTN_WIDTH)
    kw = jnp.tile(k_norm_w, ATTN_HEADS).reshape(1, ATTN_WIDTH)
    head_of = np.arange(ATTN_WIDTH) // ATTN_HEAD_DIM
    gmat = jnp.asarray(head_of[:, None] == head_of[None, :], dtype=BF16)
    row = lambda w: pl.BlockSpec((tm, w), lambda i: (i, 0))
    full = lambda r, c: pl.BlockSpec((r, c), lambda i: (0, 0))
    widths = (ATTN_WIDTH, ATTN_WIDTH, ATTN_WIDTH, RET_QK_WIDTH, RET_QK_WIDTH, RET_V_WIDTH, RET_V_WIDTH)
    return pl.pallas_call(
        _in_proj_body,
        grid=(n // tm,),
        in_specs=[row(d), full(1, d), full(d, cols), full(1, ATTN_WIDTH), full(1, ATTN_WIDTH),
                  full(ATTN_WIDTH, ATTN_WIDTH)],
        out_specs=[row(w) for w in widths],
        out_shape=[jax.ShapeDtypeStruct((n, w), BF16) for w in widths],
        compiler_params=_cparams(("parallel",), VMEM_LIMIT),
        name="in_proj",
    )(x2d, norm_w.reshape(1, d), w_in.astype(BF16), qw, kw, gmat)


def _attn_bias(dilation):
    half = ATTN_HALF
    qi = np.arange(half)[:, None]
    kc = np.arange(3 * half)[None, :]
    delta = kc - half - qi
    band = np.abs(delta) <= half
    slopes = np.exp2(-8.0 * np.arange(1, ATTN_HEADS + 1) / ATTN_HEADS)
    valid = [band & (kc >= half), band, band & (kc < 2 * half)]
    out = np.empty((ATTN_HEADS // 2, 3, 2 * half, 3 * half), np.float32)
    for p in range(ATTN_HEADS // 2):
        for v in range(3):
            for s in range(2):
                b = -slopes[2 * p + s] * LOG2_E * (dilation * np.abs(delta)).astype(np.float32)
                out[p, v, s * half:(s + 1) * half] = np.where(valid[v], b, NEG_INF)
    return jnp.asarray(out.reshape(-1, 2 * half, 3 * half))


def _attn_body(q_ref, kp_ref, km_ref, kn_ref, vp_ref, vm_ref, vn_ref, bias_ref, o_ref, lse_ref,
               kcat, vcat, *, tq):
    half = ATTN_HALF
    i = pl.program_id(1)
    n_blocks = pl.num_programs(1) * (tq // half)
    kcat[0:half] = kp_ref[0]
    kcat[half:half + tq] = km_ref[0]
    kcat[half + tq:] = kn_ref[0]
    vcat[0:half] = vp_ref[0]
    vcat[half:half + tq] = vm_ref[0]
    vcat[half + tq:] = vn_ref[0]
    lo_lanes = lax.broadcasted_iota(I32, (half, LANES), 1) < ATTN_HEAD_DIM
    ones = jnp.ones((3 * half, LANES), BF16)
    unroll = min(ATTN_BLOCKS_PER_STEP, tq // half)

    def step(it, carry):
        chains = []
        for u in range(unroll):
            jb = it * unroll + u
            r0 = pl.multiple_of(jb * half, half)
            gb = i * (tq // half) + jb
            variant = jnp.where(gb == 0, 0, jnp.where(gb == n_blocks - 1, 2, 1))
            for p in range(ATTN_HEADS // 2):
                chains.append((r0, variant, p, slice(p * LANES, (p + 1) * LANES)))
        scores = []
        for r0, variant, p, cs in chains:
            qp = q_ref[0, pl.ds(r0, half), cs]
            zero = jnp.zeros_like(qp)
            qs = jnp.concatenate([jnp.where(lo_lanes, qp, zero), jnp.where(lo_lanes, zero, qp)], axis=0)
            kp = kcat[pl.ds(r0, 3 * half), cs]
            scores.append(lax.dot_general(qs, kp, (((1,), (1,)), ((), ())), preferred_element_type=F32))
        probs, maxes = [], []
        for (r0, variant, p, cs), s in zip(chains, scores):
            s = s + bias_ref[p * 3 + variant]
            m = jnp.max(s, axis=-1, keepdims=True)
            probs.append(jnp.exp2(s - m).astype(BF16))
            maxes.append(m)
        results = []
        for (r0, variant, p, cs), e in zip(chains, probs):
            vext = jnp.concatenate([vcat[pl.ds(r0, 3 * half), cs], ones], axis=1)
            results.append(jnp.dot(e, vext, preferred_element_type=F32))
        for (r0, variant, p, cs), r, m in zip(chains, results, maxes):
            acc = jnp.where(lo_lanes, r[:half, :LANES], r[half:, :LANES])
            denom = jnp.where(lo_lanes, r[:half, LANES:], r[half:, LANES:])
            mm = jnp.where(lo_lanes, m[:half], m[half:])
            o_ref[0, pl.ds(r0, half), cs] = (acc / denom).astype(o_ref.dtype)
            lse_ref[0, pl.ds(r0, half), cs] = mm + jnp.log2(denom)
        return carry

    lax.fori_loop(0, tq // half // unroll, step, 0)


def _attention_pattern(q, k, v, dilation):
    g, l, w = q.shape
    half = ATTN_HALF
    tq = min(ATTN_Q_TILE, l)
    hb = tq // half
    last = l // half - 1
    main = pl.BlockSpec((1, tq, w), lambda a, i: (a, i, 0))
    prev = pl.BlockSpec((1, half, w), lambda a, i: (a, jnp.maximum(i * hb - 1, 0), 0))
    nxt = pl.BlockSpec((1, half, w), lambda a, i: (a, jnp.minimum((i + 1) * hb, last), 0))
    bias = _attn_bias(dilation)
    return pl.pallas_call(
        functools.partial(_attn_body, tq=tq),
        grid=(g, l // tq),
        in_specs=[main, prev, main, nxt, prev, main, nxt,
                  pl.BlockSpec(bias.shape, lambda a, i: (0, 0, 0))],
        out_specs=[main, main],
        out_shape=[jax.ShapeDtypeStruct((g, l, w), BF16), jax.ShapeDtypeStruct((g, l, w), F32)],
        scratch_shapes=[pltpu.VMEM((tq + 2 * half, w), BF16), pltpu.VMEM((tq + 2 * half, w), BF16)],
        compiler_params=_cparams(("parallel", "parallel"), VMEM_LIMIT),
        name=f"attention_d{dilation}",
    )(q, k, k, k, v, v, v, bias)


def _to_subsequences(t, b, s, dilation):
    w = t.shape[-1]
    if dilation == 1:
        return t.reshape(b, s, w)
    return t.reshape(b, s // dilation, dilation, w).transpose(0, 2, 1, 3).reshape(b * dilation, s // dilation, w)


def _from_subsequences(t, b, s, dilation):
    w = t.shape[-1]
    if dilation == 1:
        return t.reshape(b * s, w)
    return t.reshape(b, dilation, s // dilation, w).transpose(0, 2, 1, 3).reshape(b * s, w)


def _ret_body(lg_ref, q_ref, k_ref, v_ref, g_ref, gnw_ref, out_ref,
              dmat, qdec, kdec, fstate, rstate, rall, *, chunk):
    c = chunk
    b = pl.program_id(0)
    ph = pl.program_id(1)
    n = pl.program_id(2)
    n_chunks = pl.num_programs(2)
    k_scale = RET_QK_DIM ** -0.5

    @pl.when((b == 0) & (ph == 0) & (n == 0))
    def _init_tables():
        ii = lax.broadcasted_iota(I32, (c, c), 0)
        jj = lax.broadcasted_iota(I32, (c, c), 1)
        fwd = (ii - jj).astype(F32)
        ri = lax.broadcasted_iota(I32, (c, RET_QK_DIM), 0).astype(F32)
        for h in range(RET_HEADS):
            lf = lg_ref[0, h]
            lb = lg_ref[1, h]
            dmat[h] = jnp.where(ii >= jj, jnp.exp(lf * fwd), jnp.exp(-lb * fwd)) * k_scale
            qdec[0, h] = jnp.exp(lf * (ri + 1.0))
            qdec[1, h] = jnp.exp(lb * (c - ri))
            kdec[0, h] = jnp.exp(lf * (c - 1.0 - ri)) * k_scale
            kdec[1, h] = jnp.exp(lb * ri) * k_scale

    def chunk_decay(direction, h):
        return jnp.exp(jnp.full((RET_QK_DIM, RET_V_DIM), lg_ref[direction, h] * c, F32))

    def kv_update(direction, h):
        kh = k_ref[0, :, h * RET_QK_DIM:(h + 1) * RET_QK_DIM].astype(F32)
        ks = (kh * kdec[direction, h]).astype(BF16)
        vh = v_ref[0, :, h * RET_V_DIM:(h + 1) * RET_V_DIM]
        return lax.dot_general(ks, vh, (((0,), (0,)), ((), ())), preferred_element_type=F32)

    @pl.when(ph == 0)
    def _right_to_left():
        @pl.when(n == 0)
        def _():
            rstate[...] = jnp.zeros_like(rstate)

        ci = n_chunks - 1 - n
        for h in range(RET_HEADS):
            st = rstate[h]
            rall[ci, h] = st
            rstate[h] = st * chunk_decay(1, h) + kv_update(1, h)

    @pl.when(ph == 1)
    def _left_to_right():
        @pl.when(n == 0)
        def _():
            fstate[...] = jnp.zeros_like(fstate)

        for h in range(RET_HEADS):
            qs = slice(h * RET_QK_DIM, (h + 1) * RET_QK_DIM)
            vs = slice(h * RET_V_DIM, (h + 1) * RET_V_DIM)
            qh = q_ref[0, :, qs]
            kh = k_ref[0, :, qs]
            vh = v_ref[0, :, vs]
            s = lax.dot_general(qh, kh, (((1,), (1,)), ((), ())), preferred_element_type=F32) * dmat[h]
            y = jnp.dot(s.astype(BF16), vh, preferred_element_type=F32)
            qf = qh.astype(F32)
            qcat = jnp.concatenate([(qf * qdec[0, h]).astype(BF16), (qf * qdec[1, h]).astype(BF16)], axis=1)
            st = fstate[h]
            states = jnp.concatenate([st, rall[n, h]], axis=0).astype(BF16)
            y = y + jnp.dot(qcat, states, preferred_element_type=F32)
            fstate[h] = st * chunk_decay(0, h) + kv_update(0, h)
            mu = jnp.mean(y, axis=-1, keepdims=True)
            yc = y - mu
            var = jnp.mean(yc * yc, axis=-1, keepdims=True)
            yn = yc * lax.rsqrt(var + GN_EPS) * gnw_ref[:, vs]
            gate = g_ref[0, :, vs].astype(F32)
            out_ref[0, :, vs] = (gate * jax.nn.sigmoid(gate) * yn).astype(out_ref.dtype)


def _retention(rq, rk, rv, rg, decay_f, decay_b, gn_w, b, s):
    c = RET_CHUNK
    nc = s // c
    lg = jnp.stack([jax.nn.log_sigmoid(decay_f.astype(F32)), jax.nn.log_sigmoid(decay_b.astype(F32))])
    qk_w, v_w = RET_QK_WIDTH, RET_V_WIDTH

    def both(bb, ph, n):
        return (bb, jnp.where(ph == 0, nc - 1 - n, n), 0)

    def fwd_only(bb, ph, n):
        return (bb, jnp.where(ph == 0, 0, n), 0)

    out = pl.pallas_call(
        functools.partial(_ret_body, chunk=c),
        grid=(b, 2, nc),
        in_specs=[pl.BlockSpec(memory_space=pltpu.SMEM),
                  pl.BlockSpec((1, c, qk_w), fwd_only),
                  pl.BlockSpec((1, c, qk_w), both),
                  pl.BlockSpec((1, c, v_w), both),
                  pl.BlockSpec((1, c, v_w), fwd_only),
                  pl.BlockSpec((1, v_w), lambda bb, ph, n: (0, 0))],
        out_specs=pl.BlockSpec((1, c, v_w), fwd_only),
        out_shape=jax.ShapeDtypeStruct((b, s, v_w), BF16),
        scratch_shapes=[pltpu.VMEM((RET_HEADS, c, c), F32),
                        pltpu.VMEM((2, RET_HEADS, c, RET_QK_DIM), F32),
                        pltpu.VMEM((2, RET_HEADS, c, RET_QK_DIM), F32),
                        pltpu.VMEM((RET_HEADS, RET_QK_DIM, RET_V_DIM), F32),
                        pltpu.VMEM((RET_HEADS, RET_QK_DIM, RET_V_DIM), F32),
                        pltpu.VMEM((nc, RET_HEADS, RET_QK_DIM, RET_V_DIM), F32)],
        compiler_params=_cparams(("arbitrary", "arbitrary", "arbitrary"), VMEM_LIMIT),
        name="retention",
    )(lg, rq.reshape(b, s, qk_w), rk.reshape(b, s, qk_w), rv.reshape(b, s, v_w), rg.reshape(b, s, v_w),
      gn_w.reshape(1, v_w))
    return out.reshape(b * s, v_w)


def _out_proj_body(o1, o2, o3, l1, l2, l3, ret_ref, x_ref, w_ref, out_ref):
    a1, a2, a3 = l1[...], l2[...], l3[...]
    m = jnp.maximum(jnp.maximum(a1, a2), a3)
    e1, e2, e3 = jnp.exp2(a1 - m), jnp.exp2(a2 - m), jnp.exp2(a3 - m)
    attn = (e1 * o1[...].astype(F32) + e2 * o2[...].astype(F32) + e3 * o3[...].astype(F32)) / (e1 + e2 + e3)
    y = jnp.dot(attn.astype(BF16), w_ref[:ATTN_WIDTH], preferred_element_type=F32)
    y = y + jnp.dot(ret_ref[...], w_ref[ATTN_WIDTH:], preferred_element_type=F32)
    out_ref[...] = x_ref[...] + y


def _out_proj(outs, lses, ret, x2d, w_out):
    n, d = x2d.shape
    tm = TOKEN_TILE
    row = lambda w: pl.BlockSpec((tm, w), lambda i: (i, 0))
    return pl.pallas_call(
        _out_proj_body,
        grid=(n // tm,),
        in_specs=[row(ATTN_WIDTH)] * 6 + [row(RET_V_WIDTH), row(d),
                                          pl.BlockSpec(w_out.shape, lambda i: (0, 0))],
        out_specs=row(d),
        out_shape=jax.ShapeDtypeStruct((n, d), F32),
        compiler_params=_cparams(("parallel",), VMEM_LIMIT),
        name="out_proj",
    )(*outs, *lses, ret, x2d, w_out.astype(BF16))


def _mem_kv_body(mem_ref, nw_ref, w_ref, kw_ref, k_out, v_out):
    d = mem_ref.shape[-1]
    hd = d // MEM_HEADS
    h = _rms(mem_ref[0], nw_ref[...]).astype(BF16)
    kv = jnp.dot(h, w_ref[...], preferred_element_type=F32)
    for i in range(MEM_HEADS):
        k_out[0, :, i * hd:(i + 1) * hd] = _rms(kv[:, i * hd:(i + 1) * hd], kw_ref[...]).astype(BF16)
    v_out[0] = kv[:, d:].astype(BF16)


def _mem_kv(mem, norm_w, w_mkv, k_norm_w):
    b, m, d = mem.shape
    return pl.pallas_call(
        _mem_kv_body,
        grid=(b,),
        in_specs=[pl.BlockSpec((1, m, d), lambda i: (i, 0, 0)),
                  pl.BlockSpec((1, d), lambda i: (0, 0)),
                  pl.BlockSpec((d, 2 * d), lambda i: (0, 0)),
                  pl.BlockSpec((1, d // MEM_HEADS), lambda i: (0, 0))],
        out_specs=[pl.BlockSpec((1, m, d), lambda i: (i, 0, 0))] * 2,
        out_shape=[jax.ShapeDtypeStruct((b, m, d), BF16)] * 2,
        compiler_params=_cparams(("parallel",), VMEM_LIMIT),
        name="mem_kv",
    )(mem, norm_w.reshape(1, d), w_mkv.astype(BF16), k_norm_w.reshape(1, -1))


def _mem_attn_body(x_ref, nw_ref, wq_ref, qw_ref, k_ref, v_ref, wo_ref, out_ref):
    d = x_ref.shape[-1]
    hd = d // MEM_HEADS
    x = x_ref[0]
    h = _rms(x, nw_ref[...]).astype(BF16)
    q = jnp.dot(h, wq_ref[...], preferred_element_type=F32)
    heads = []
    for i in range(MEM_HEADS):
        cs = slice(i * hd, (i + 1) * hd)
        qn = _rms(q[:, cs], qw_ref[...]).astype(BF16)
        s = lax.dot_general(qn, k_ref[0, :, cs], (((1,), (1,)), ((), ())), preferred_element_type=F32)
        e = jnp.exp(s - jnp.max(s, axis=-1, keepdims=True))
        o = jnp.dot(e.astype(BF16), v_ref[0, :, cs], preferred_element_type=F32)
        heads.append((o / jnp.sum(e, axis=-1, keepdims=True)).astype(BF16))
    o = jnp.concatenate(heads, axis=1)
    out_ref[0] = x + jnp.dot(o, wo_ref[...], preferred_element_type=F32)


def _mem_attn(x1, mk, mv, norm_w, w_mq, q_norm_w, w_mo):
    b, s, d = x1.shape
    m = mk.shape[1]
    tm = TOKEN_TILE
    hd = d // MEM_HEADS
    qw = (q_norm_w * (hd ** -0.5)).reshape(1, hd)
    tok = pl.BlockSpec((1, tm, d), lambda bb, i: (bb, i, 0))
    const = lambda r, c: pl.BlockSpec((r, c), lambda bb, i: (0, 0))
    mem = pl.BlockSpec((1, m, d), lambda bb, i: (bb, 0, 0))
    return pl.pallas_call(
        _mem_attn_body,
        grid=(b, s // tm),
        in_specs=[tok, const(1, d), const(d, d), const(1, hd), mem, mem, const(d, d)],
        out_specs=tok,
        out_shape=jax.ShapeDtypeStruct((b, s, d), F32),
        compiler_params=_cparams(("parallel", "parallel"), VMEM_LIMIT),
        name="mem_attn",
    )(x1, norm_w.reshape(1, d), w_mq.astype(BF16), qw, mk, mv, w_mo.astype(BF16))


def _router_body(x_ref, nw_ref, whi_ref, wlo_ref, h_out, gate_out, meta_out, cnt_out, carry):
    tm = x_ref.shape[0]
    i = pl.program_id(0)

    @pl.when(i == 0)
    def _():
        carry[...] = jnp.zeros_like(carry)

    h = _rms(x_ref[...], nw_ref[...])
    h_out[...] = h
    hi, lo = _split_hi_lo(h)
    logits = (jnp.dot(hi, whi_ref[...], preferred_element_type=F32)
              + jnp.dot(lo, whi_ref[...], preferred_element_type=F32)
              + jnp.dot(hi, wlo_ref[...], preferred_element_type=F32))
    lane = lax.broadcasted_iota(I32, (tm, LANES), 1)

    def argmax_lanes(vals):
        top = jnp.max(vals, axis=-1, keepdims=True)
        idx = jnp.min(jnp.where(vals == top, lane, LANES), axis=-1, keepdims=True)
        return top, idx

    is_group = lane < N_GROUPS
    g_top, grp = argmax_lanes(jnp.where(is_group, logits, -jnp.inf))
    g_gate = 1.0 / jnp.sum(jnp.where(is_group, jnp.exp(logits - g_top), 0.0), axis=-1, keepdims=True)
    in_group = (lane >= N_GROUPS) & (lane < N_GROUPS + N_EXPERTS) & (
        ((lane - N_GROUPS) // EXPERTS_PER_GROUP) == grp)
    el = jnp.where(in_group, logits, -jnp.inf)
    t1, i1 = argmax_lanes(el)
    t2, i2 = argmax_lanes(jnp.where(lane == i1, -jnp.inf, el))
    z = jnp.exp(t2 - t1)
    g1 = g_gate / (1.0 + z)
    g2 = g_gate * z / (1.0 + z)
    e1 = i1 - N_GROUPS
    e2 = i2 - N_GROUPS

    oh1 = (lane == e1)
    oh2 = (lane == e2)
    oh1b = oh1.astype(F32).astype(BF16)
    oh2b = oh2.astype(F32).astype(BF16)
    rr = lax.broadcasted_iota(I32, (tm, tm), 0)
    cc = lax.broadcasted_iota(I32, (tm, tm), 1)
    below = (cc < rr).astype(F32).astype(BF16)
    pre1 = jnp.dot(below, oh1b, preferred_element_type=F32)
    pre2 = jnp.dot(below, oh2b, preferred_element_type=F32)
    cnt1 = jnp.sum(oh1.astype(F32), axis=0, keepdims=True)
    cnt2 = jnp.sum(oh2.astype(F32), axis=0, keepdims=True)
    base = carry[...]
    r1 = jnp.sum(jnp.where(oh1, pre1 + base, 0.0), axis=-1, keepdims=True)
    r2 = jnp.sum(jnp.where(oh2, pre2 + base + cnt1, 0.0), axis=-1, keepdims=True)
    total = base + cnt1 + cnt2
    carry[...] = total
    cnt_out[...] = total

    gate_out[...] = jnp.where(lane == 0, g1, jnp.where(lane == 1, g2, 0.0))
    meta_out[...] = jnp.where(lane == 0, e1, jnp.where(lane == 1, e2, jnp.where(
        lane == 2, r1.astype(I32), jnp.where(lane == 3, r2.astype(I32), 0))))


def _router(x2d, norm_w, w_group, w_router):
    n, d = x2d.shape
    tm = TOKEN_TILE
    w_all = jnp.concatenate([w_group, w_router.transpose(1, 0, 2).reshape(d, N_EXPERTS)], axis=1)
    w_all = jnp.pad(w_all, ((0, 0), (0, LANES - w_all.shape[1])))
    whi = w_all.astype(BF16)
    wlo = (w_all - whi.astype(F32)).astype(BF16)
    row = lambda w: pl.BlockSpec((tm, w), lambda i: (i, 0))
    const = lambda r, c: pl.BlockSpec((r, c), lambda i: (0, 0))
    return pl.pallas_call(
        _router_body,
        grid=(n // tm,),
        in_specs=[row(d), const(1, d), const(d, LANES), const(d, LANES)],
        out_specs=[row(d), row(LANES), row(LANES), const(1, LANES)],
        out_shape=[jax.ShapeDtypeStruct((n, d), F32), jax.ShapeDtypeStruct((n, LANES), F32),
                   jax.ShapeDtypeStruct((n, LANES), I32), jax.ShapeDtypeStruct((1, LANES), F32)],
        scratch_shapes=[pltpu.VMEM((1, LANES), F32)],
        compiler_params=_cparams(("arbitrary",), VMEM_LIMIT),
        name="router",
    )(x2d, norm_w.reshape(1, d), whi, wlo)


def _row_copy(src_ref, src_row, dst_ref, dst_row, sem):
    return pltpu.make_async_copy(src_ref.at[pl.ds(src_row, 1)], dst_ref.at[pl.ds(dst_row, 1)], sem)


def _dispatch_body(cnt_ref, eid_ref, rank_ref, h_ref, xs_ref, dest_ref, pstart, zbuf, sems):
    i = pl.program_id(0)
    tt = h_ref.shape[0]
    blk = EXPERT_BLOCK

    def pad_copy(pend):
        return pltpu.make_async_copy(zbuf, xs_ref.at[pl.ds(pl.multiple_of(pend - blk, blk), blk)], sems.at[2])

    @pl.when(i == 0)
    def _():
        zbuf[...] = jnp.zeros_like(zbuf)

        def start(e, acc):
            pstart[e] = acc
            pend = acc + ((cnt_ref[e] + (blk - 1)) // blk) * blk

            @pl.when(pend > acc)
            def _():
                pad_copy(pend).start()
            return pend

        used = lax.fori_loop(0, N_EXPERTS, start, jnp.int32(0))

        def tail_start(j, carry):
            pad_copy((j + 1) * blk).start()
            return carry

        lax.fori_loop(used // blk, xs_ref.shape[0] // blk, tail_start, 0)

        def finish(e, acc):
            pend = acc + ((cnt_ref[e] + (blk - 1)) // blk) * blk

            @pl.when(pend > acc)
            def _():
                pad_copy(pend).wait()
            return pend

        lax.fori_loop(0, N_EXPERTS, finish, jnp.int32(0))

        def tail_wait(j, carry):
            pad_copy((j + 1) * blk).wait()
            return carry

        lax.fori_loop(used // blk, xs_ref.shape[0] // blk, tail_wait, 0)

    def token(t, carry):
        for k in range(2):
            a = 2 * t + k
            d = pstart[eid_ref[a]] + rank_ref[a]
            dest_ref[a] = d
            _row_copy(h_ref, t, xs_ref, d, sems.at[k]).start()
        return carry

    lax.fori_loop(0, tt, token, 0)
    for k in range(2):
        pltpu.make_async_copy(h_ref, xs_ref.at[pl.ds(0, tt)], sems.at[k]).wait()


def _dispatch(counts, eid, rank, h3, n_slots):
    n, d = h3.shape
    tt = DISPATCH_TILE
    smem_blk = pl.BlockSpec((2 * tt,), lambda i, cnt: (i,), memory_space=pltpu.SMEM)
    grid_spec = pltpu.PrefetchScalarGridSpec(
        num_scalar_prefetch=1,
        grid=(n // tt,),
        in_specs=[smem_blk, smem_blk, pl.BlockSpec((tt, d), lambda i, cnt: (i, 0))],
        out_specs=[pl.BlockSpec(memory_space=pl.ANY), smem_blk],
        scratch_shapes=[pltpu.SMEM((N_EXPERTS,), I32), pltpu.VMEM((EXPERT_BLOCK, d), F32),
                        pltpu.SemaphoreType.DMA((3,))],
    )
    return pl.pallas_call(
        _dispatch_body,
        grid_spec=grid_spec,
        out_shape=[jax.ShapeDtypeStruct((n_slots, d), F32), jax.ShapeDtypeStruct((2 * n,), I32)],
        compiler_params=_cparams(("arbitrary",), VMEM_LIMIT),
        name="dispatch",
    )(counts, eid, rank, h3)


def _experts_body(be_ref, nu_ref, xs_ref, w1_ref, w3_ref, w2_ref, ys_ref, w1b, w3b, w2b):
    i = pl.program_id(0)
    n_used = nu_ref[0]

    @pl.when(i < n_used)
    def _():
        changed = (i == 0) | (be_ref[i] != be_ref[jnp.maximum(i - 1, 0)])

        @pl.when(changed)
        def _():
            w1b[...] = w1_ref[0].astype(BF16)
            w3b[...] = w3_ref[0].astype(BF16)
            w2b[...] = w2_ref[0].astype(BF16)

        x = xs_ref[...].astype(BF16)
        a = jnp.dot(x, w1b[...], preferred_element_type=F32)
        u = jnp.dot(x, w3b[...], preferred_element_type=F32)
        mid = (a * jax.nn.sigmoid(a) * u).astype(BF16)
        ys_ref[...] = jnp.dot(mid, w2b[...], preferred_element_type=F32)

    @pl.when(i >= n_used)
    def _():
        ys_ref[...] = jnp.zeros_like(ys_ref)


def _experts(block_expert, n_used, xs, w1, w3, w2):
    n_slots, d = xs.shape
    ff = w1.shape[-1]
    blk = EXPERT_BLOCK
    n_blocks = n_slots // blk

    def slot_map(i, be, nu):
        return (jnp.minimum(i, nu[0] - 1), 0)

    def w_map(i, be, nu):
        return (be[jnp.minimum(i, nu[0] - 1)], 0, 0)

    grid_spec = pltpu.PrefetchScalarGridSpec(
        num_scalar_prefetch=2,
        grid=(n_blocks,),
        in_specs=[pl.BlockSpec((blk, d), slot_map),
                  pl.BlockSpec((1, d, ff), w_map), pl.BlockSpec((1, d, ff), w_map),
                  pl.BlockSpec((1, ff, d), w_map)],
        out_specs=pl.BlockSpec((blk, d), lambda i, be, nu: (i, 0)),
        scratch_shapes=[pltpu.VMEM((d, ff), BF16), pltpu.VMEM((d, ff), BF16), pltpu.VMEM((ff, d), BF16)],
    )
    return pl.pallas_call(
        _experts_body,
        grid_spec=grid_spec,
        out_shape=jax.ShapeDtypeStruct((n_slots, d), F32),
        compiler_params=_cparams(("arbitrary",), VMEM_LIMIT),
        name="experts",
    )(block_expert, n_used, xs, w1, w3, w2)


def _combine_body(dest_ref, gate_ref, x_ref, ys_ref, out_ref, buf, sems):
    tt = x_ref.shape[0]

    def token(t, carry):
        for k in range(2):
            _row_copy(ys_ref, dest_ref[2 * t + k], buf.at[k], t, sems.at[k]).start()
        return carry

    lax.fori_loop(0, tt, token, 0)
    for k in range(2):
        pltpu.make_async_copy(ys_ref.at[pl.ds(0, tt)], buf.at[k], sems.at[k]).wait()
    g = gate_ref[...]
    out_ref[...] = x_ref[...] + g[:, 0:1] * buf[0] + g[:, 1:2] * buf[1]


def _combine(dest, gates, x2d, ys):
    n, d = x2d.shape
    tt = COMBINE_TILE
    row = lambda w: pl.BlockSpec((tt, w), lambda i: (i, 0))
    return pl.pallas_call(
        _combine_body,
        grid=(n // tt,),
        in_specs=[pl.BlockSpec((2 * tt,), lambda i: (i,), memory_space=pltpu.SMEM), row(LANES), row(d),
                  pl.BlockSpec(memory_space=pl.ANY)],
        out_specs=row(d),
        out_shape=jax.ShapeDtypeStruct((n, d), F32),
        scratch_shapes=[pltpu.VMEM((2, tt, d), F32), pltpu.SemaphoreType.DMA((2,))],
        compiler_params=_cparams(("arbitrary",), VMEM_LIMIT),
        name="combine",
    )(dest, gates, x2d, ys)


def _moe(x2d, norm_w, w_group, w_router, w1, w3, w2):
    n, d = x2d.shape
    blk = EXPERT_BLOCK
    h3, gates, meta, cnt = _router(x2d, norm_w, w_group, w_router)
    counts = cnt[0, :N_EXPERTS].astype(I32)
    eid = meta[:, 0:2].reshape(-1)
    rank = meta[:, 2:4].reshape(-1)
    n_slots = 2 * n + N_EXPERTS * blk
    n_blocks = n_slots // blk
    pends = jnp.cumsum((counts + blk - 1) // blk * blk)
    block_start = jnp.arange(n_blocks, dtype=I32) * blk
    block_expert = jnp.minimum(jnp.sum((pends[None, :] <= block_start[:, None]).astype(I32), axis=1),
                               N_EXPERTS - 1)
    n_used = (pends[-1:] // blk).astype(I32)
    xs, dest = _dispatch(counts, eid, rank, h3, n_slots)
    ys = _experts(block_expert, n_used, xs, w1, w3, w2)
    return _combine(dest, gates, x2d, ys)


def _layer(x, mem, norm_mix_w, w_in, attn_q_norm_w, attn_k_norm_w, ret_decay_f, ret_decay_b, ret_gn_w, w_out,
           norm_mem_w, norm_memkv_w, w_mq, w_mkv, mem_q_norm_w, mem_k_norm_w, w_mo,
           norm_moe_w, w_group, w_router, w_exp_gate, w_exp_up, w_exp_down):
    b, s, d = x.shape
    x2d = x.reshape(b * s, d)
    q, k, v, rq, rk, rv, rg = _in_proj(x2d, norm_mix_w, w_in, attn_q_norm_w, attn_k_norm_w)
    outs, lses = [], []
    for dil in DILATIONS:
        o, lse = _attention_pattern(*(_to_subsequences(t, b, s, dil) for t in (q, k, v)), dil)
        outs.append(_from_subsequences(o, b, s, dil))
        lses.append(_from_subsequences(lse, b, s, dil))
    ret = _retention(rq, rk, rv, rg, ret_decay_f, ret_decay_b, ret_gn_w, b, s)
    x1 = _out_proj(outs, lses, ret, x2d, w_out)
    mk, mv = _mem_kv(mem, norm_memkv_w, w_mkv, mem_k_norm_w)
    x2 = _mem_attn(x1.reshape(b, s, d), mk, mv, norm_mem_w, w_mq, mem_q_norm_w, w_mo)
    x3 = _moe(x2.reshape(b * s, d), norm_moe_w, w_group, w_router, w_exp_gate, w_exp_up, w_exp_down)
    return x3.reshape(b, s, d)


def kernel(x, mem, norm_mix_w, w_in, attn_q_norm_w, attn_k_norm_w, ret_decay_f, ret_decay_b, ret_gn_w, w_out,
           norm_mem_w, norm_memkv_w, w_mq, w_mkv, mem_q_norm_w, mem_k_norm_w, w_mo, norm_moe_w, w_group,
           w_router, w_exp_gate, w_exp_up, w_exp_down):
    depth = norm_mix_w.shape[0]
    for l in range(depth):
        x = _layer(x, mem, norm_mix_w[l], w_in[l], attn_q_norm_w[l], attn_k_norm_w[l], ret_decay_f[l],
                   ret_decay_b[l], ret_gn_w[l], w_out[l], norm_mem_w[l], norm_memkv_w[l], w_mq[l], w_mkv[l],
                   mem_q_norm_w[l], mem_k_norm_w[l], w_mo[l], norm_moe_w[l], w_group[l], w_router[l],
                   w_exp_gate[l], w_exp_up[l], w_exp_down[l])
    return x
```

```python
import functools

import numpy as np
import jax
import jax.numpy as jnp
from jax import lax
from jax.experimental import pallas as pl
from jax.experimental.pallas import tpu as pltpu

F32 = jnp.float32
BF16 = jnp.bfloat16
I32 = jnp.int32

NORM_EPS = 1e-6
GN_EPS = 1e-5
NEG_INF = -1e30

ATTN_HEADS = 8
ATTN_HEAD_DIM = 64
ATTN_WIDTH = ATTN_HEADS * ATTN_HEAD_DIM
ATTN_HALF = 64
DILATIONS = (1, 4, 16)
RET_HEADS = 4
RET_QK_DIM = 64
RET_V_DIM = 128
RET_QK_WIDTH = RET_HEADS * RET_QK_DIM
RET_V_WIDTH = RET_HEADS * RET_V_DIM
MEM_HEADS = 4
N_GROUPS = 4
EXPERTS_PER_GROUP = 8
N_EXPERTS = N_GROUPS * EXPERTS_PER_GROUP

LANES = 128
TOKEN_TILE = 512
ATTN_Q_TILE = 512
ATTN_BLOCKS_PER_STEP = 4
LOG2_E = 1.4426950408889634
RET_CHUNK = 256
EXPERT_BLOCK = 256
DISPATCH_TILE = 512
COMBINE_TILE = 256
ROW_DMA_UNROLL = 16
VMEM_LIMIT = 56 * 1024 * 1024


def _cparams(sem, vmem=None):
    return pltpu.CompilerParams(dimension_semantics=sem, vmem_limit_bytes=vmem)


def _split_hi_lo(x):
    hi = x.astype(BF16)
    lo = (x - hi.astype(F32)).astype(BF16)
    return hi, lo


def _rms(x, w):
    ms = jnp.mean(x * x, axis=-1, keepdims=True)
    return x * lax.rsqrt(ms + NORM_EPS) * w


ROW_SLABS = 8


def _store_row_tiles(ref, val):
    rows = val.shape[0]
    for j in range(ROW_SLABS):
        ref[pl.ds(j, rows, stride=ROW_SLABS), :] = val[:, j * LANES:(j + 1) * LANES]


def _load_row_tiles(ref):
    rows = ref.shape[0] // ROW_SLABS
    return jnp.concatenate([ref[pl.ds(j, rows, stride=ROW_SLABS), :] for j in range(ROW_SLABS)], axis=1)


def _tile_rows(ref, row, count=1):
    return ref.at[pl.ds(pl.multiple_of(row * ROW_SLABS, ROW_SLABS), count * ROW_SLABS)]


def _in_proj_body(x_ref, nw_ref, w_ref, qw_ref, kw_ref, g_ref,
                  q_out, k_out, v_out, rq_out, rk_out, rv_out, rg_out):
    h = _rms(x_ref[...], nw_ref[...]).astype(BF16)

    def proj(a, b):
        return jnp.dot(h, w_ref[:, a:b], preferred_element_type=F32)

    def head_norm(p, w):
        hi, lo = _split_hi_lo(p * p)
        s = (jnp.dot(hi, g_ref[...], preferred_element_type=F32)
             + jnp.dot(lo, g_ref[...], preferred_element_type=F32))
        return p * lax.rsqrt(s * (1.0 / ATTN_HEAD_DIM) + NORM_EPS) * w

    a = ATTN_WIDTH
    q_out[...] = head_norm(proj(0, a), qw_ref[...]).astype(BF16)
    k_out[...] = head_norm(proj(a, 2 * a), kw_ref[...]).astype(BF16)
    v_out[...] = proj(2 * a, 3 * a).astype(BF16)
    c = 3 * a
    rq_out[...] = proj(c, c + RET_QK_WIDTH).astype(BF16)
    rk_out[...] = proj(c + RET_QK_WIDTH, c + 2 * RET_QK_WIDTH).astype(BF16)
    c += 2 * RET_QK_WIDTH
    rv_out[...] = proj(c, c + RET_V_WIDTH).astype(BF16)
    rg_out[...] = proj(c + RET_V_WIDTH, c + 2 * RET_V_WIDTH).astype(BF16)


def _in_proj(x2d, norm_w, w_in, q_norm_w, k_norm_w):
    n, d = x2d.shape
    cols = w_in.shape[1]
    tm = TOKEN_TILE
    qw = (jnp.tile(q_norm_w, ATTN_HEADS) * (ATTN_HEAD_DIM ** -0.5 * LOG2_E)).reshape(1, ATTN_WIDTH)
    kw = jnp.tile(k_norm_w, ATTN_HEADS).reshape(1, ATTN_WIDTH)
    head_of = np.arange(ATTN_WIDTH) // ATTN_HEAD_DIM
    gmat = jnp.asarray(head_of[:, None] == head_of[None, :], dtype=BF16)
    row = lambda w: pl.BlockSpec((tm, w), lambda i: (i, 0))
    full = lambda r, c: pl.BlockSpec((r, c), lambda i: (0, 0))
    widths = (ATTN_WIDTH, ATTN_WIDTH, ATTN_WIDTH, RET_QK_WIDTH, RET_QK_WIDTH, RET_V_WIDTH, RET_V_WIDTH)
    return pl.pallas_call(
        _in_proj_body,
        grid=(n // tm,),
        in_specs=[row(d), full(1, d), full(d, cols), full(1, ATTN_WIDTH), full(1, ATTN_WIDTH),
                  full(ATTN_WIDTH, ATTN_WIDTH)],
        out_specs=[row(w) for w in widths],
        out_shape=[jax.ShapeDtypeStruct((n, w), BF16) for w in widths],
        compiler_params=_cparams(("parallel",), VMEM_LIMIT),
        name="in_proj",
    )(x2d, norm_w.reshape(1, d), w_in.astype(BF16), qw, kw, gmat)


def _attn_bias(dilation):
    half = ATTN_HALF
    qi = np.arange(half)[:, None]
    kc = np.arange(3 * half)[None, :]
    delta = kc - half - qi
    band = np.abs(delta) <= half
    slopes = np.exp2(-8.0 * np.arange(1, ATTN_HEADS + 1) / ATTN_HEADS)
    valid = [band & (kc >= half), band, band & (kc < 2 * half)]
    out = np.empty((ATTN_HEADS // 2, 3, 2 * half, 3 * half), np.float32)
    for p in range(ATTN_HEADS // 2):
        for v in range(3):
            for s in range(2):
                b = -slopes[2 * p + s] * LOG2_E * (dilation * np.abs(delta)).astype(np.float32)
                out[p, v, s * half:(s + 1) * half] = np.where(valid[v], b, NEG_INF)
    return jnp.asarray(out.reshape(-1, 2 * half, 3 * half))


def _attn_body(q_ref, kp_ref, km_ref, kn_ref, vp_ref, vm_ref, vn_ref, bias_ref, o_ref, lse_ref,
               kcat, vcat, *, tq):
    half = ATTN_HALF
    i = pl.program_id(1)
    n_blocks = pl.num_programs(1) * (tq // half)
    kcat[0:half] = kp_ref[0]
    kcat[half:half + tq] = km_ref[0]
    kcat[half + tq:] = kn_ref[0]
    vcat[0:half] = vp_ref[0]
    vcat[half:half + tq] = vm_ref[0]
    vcat[half + tq:] = vn_ref[0]
    lo_lanes = lax.broadcasted_iota(I32, (half, LANES), 1) < ATTN_HEAD_DIM
    ones = jnp.ones((3 * half, LANES), BF16)
    unroll = min(ATTN_BLOCKS_PER_STEP, tq // half)

    def step(it, carry):
        chains = []
        for u in range(unroll):
            jb = it * unroll + u
            r0 = pl.multiple_of(jb * half, half)
            gb = i * (tq // half) + jb
            variant = jnp.where(gb == 0, 0, jnp.where(gb == n_blocks - 1, 2, 1))
            for p in range(ATTN_HEADS // 2):
                chains.append((r0, variant, p, slice(p * LANES, (p + 1) * LANES)))
        scores = []
        for r0, variant, p, cs in chains:
            qp = q_ref[0, pl.ds(r0, half), cs]
            zero = jnp.zeros_like(qp)
            qs = jnp.concatenate([jnp.where(lo_lanes, qp, zero), jnp.where(lo_lanes, zero, qp)], axis=0)
            kp = kcat[pl.ds(r0, 3 * half), cs]
            scores.append(lax.dot_general(qs, kp, (((1,), (1,)), ((), ())), preferred_element_type=F32))
        probs, maxes = [], []
        for (r0, variant, p, cs), s in zip(chains, scores):
            s = s + bias_ref[p * 3 + variant]
            m = jnp.max(s, axis=-1, keepdims=True)
            probs.append(jnp.exp2(s - m).astype(BF16))
            maxes.append(m)
        results = []
        for (r0, variant, p, cs), e in zip(chains, probs):
            vext = jnp.concatenate([vcat[pl.ds(r0, 3 * half), cs], ones], axis=1)
            results.append(jnp.dot(e, vext, preferred_element_type=F32))
        for (r0, variant, p, cs), r, m in zip(chains, results, maxes):
            acc = jnp.where(lo_lanes, r[:half, :LANES], r[half:, :LANES])
            denom = jnp.where(lo_lanes, r[:half, LANES:], r[half:, LANES:])
            mm = jnp.where(lo_lanes, m[:half], m[half:])
            o_ref[0, pl.ds(r0, half), cs] = (acc / denom).astype(o_ref.dtype)
            lse_ref[0, pl.ds(r0, half), cs] = mm + jnp.log2(denom)
        return carry

    lax.fori_loop(0, tq // half // unroll, step, 0)


def _attention_pattern(q, k, v, dilation):
    g, l, w = q.shape
    half = ATTN_HALF
    tq = min(ATTN_Q_TILE, l)
    hb = tq // half
    last = l // half - 1
    main = pl.BlockSpec((1, tq, w), lambda a, i: (a, i, 0))
    prev = pl.BlockSpec((1, half, w), lambda a, i: (a, jnp.maximum(i * hb - 1, 0), 0))
    nxt = pl.BlockSpec((1, half, w), lambda a, i: (a, jnp.minimum((i + 1) * hb, last), 0))
    bias = _attn_bias(dilation)
    return pl.pallas_call(
        functools.partial(_attn_body, tq=tq),
        grid=(g, l // tq),
        in_specs=[main, prev, main, nxt, prev, main, nxt,
                  pl.BlockSpec(bias.shape, lambda a, i: (0, 0, 0))],
        out_specs=[main, main],
        out_shape=[jax.ShapeDtypeStruct((g, l, w), BF16), jax.ShapeDtypeStruct((g, l, w), F32)],
        scratch_shapes=[pltpu.VMEM((tq + 2 * half, w), BF16), pltpu.VMEM((tq + 2 * half, w), BF16)],
        compiler_params=_cparams(("parallel", "parallel"), VMEM_LIMIT),
        name=f"attention_d{dilation}",
    )(q, k, k, k, v, v, v, bias)


def _to_subsequences(t, b, s, dilation):
    w = t.shape[-1]
    if dilation == 1:
        return t.reshape(b, s, w)
    return t.reshape(b, s // dilation, dilation, w).transpose(0, 2, 1, 3).reshape(b * dilation, s // dilation, w)


def _from_subsequences(t, b, s, dilation):
    w = t.shape[-1]
    if dilation == 1:
        return t.reshape(b * s, w)
    return t.reshape(b, dilation, s // dilation, w).transpose(0, 2, 1, 3).reshape(b * s, w)


def _ret_body(lg_ref, q_ref, k_ref, v_ref, g_ref, gnw_ref, out_ref,
              dmat, qdec, kdec, fstate, rstate, rall, *, chunk):
    c = chunk
    b = pl.program_id(0)
    ph = pl.program_id(1)
    n = pl.program_id(2)
    n_chunks = pl.num_programs(2)
    k_scale = RET_QK_DIM ** -0.5

    @pl.when((b == 0) & (ph == 0) & (n == 0))
    def _init_tables():
        ii = lax.broadcasted_iota(I32, (c, c), 0)
        jj = lax.broadcasted_iota(I32, (c, c), 1)
        fwd = (ii - jj).astype(F32)
        ri = lax.broadcasted_iota(I32, (c, RET_QK_DIM), 0).astype(F32)
        for h in range(RET_HEADS):
            lf = lg_ref[0, h]
            lb = lg_ref[1, h]
            dmat[h] = jnp.where(ii >= jj, jnp.exp(lf * fwd), jnp.exp(-lb * fwd)) * k_scale
            qdec[0, h] = jnp.exp(lf * (ri + 1.0))
            qdec[1, h] = jnp.exp(lb * (c - ri))
            kdec[0, h] = jnp.exp(lf * (c - 1.0 - ri)) * k_scale
            kdec[1, h] = jnp.exp(lb * ri) * k_scale

    def chunk_decay(direction, h):
        return jnp.exp(jnp.full((RET_QK_DIM, RET_V_DIM), lg_ref[direction, h] * c, F32))

    def kv_update(direction, h):
        kh = k_ref[0, :, h * RET_QK_DIM:(h + 1) * RET_QK_DIM].astype(F32)
        ks = (kh * kdec[direction, h]).astype(BF16)
        vh = v_ref[0, :, h * RET_V_DIM:(h + 1) * RET_V_DIM]
        return lax.dot_general(ks, vh, (((0,), (0,)), ((), ())), preferred_element_type=F32)

    @pl.when(ph == 0)
    def _right_to_left():
        @pl.when(n == 0)
        def _():
            rstate[...] = jnp.zeros_like(rstate)

        ci = n_chunks - 1 - n
        for h in range(RET_HEADS):
            st = rstate[h]
            rall[ci, h] = st
            rstate[h] = st * chunk_decay(1, h) + kv_update(1, h)

    @pl.when(ph == 1)
    def _left_to_right():
        @pl.when(n == 0)
        def _():
            fstate[...] = jnp.zeros_like(fstate)

        for h in range(RET_HEADS):
            qs = slice(h * RET_QK_DIM, (h + 1) * RET_QK_DIM)
            vs = slice(h * RET_V_DIM, (h + 1) * RET_V_DIM)
            qh = q_ref[0, :, qs]
            kh = k_ref[0, :, qs]
            vh = v_ref[0, :, vs]
            s = lax.dot_general(qh, kh, (((1,), (1,)), ((), ())), preferred_element_type=F32) * dmat[h]
            y = jnp.dot(s.astype(BF16), vh, preferred_element_type=F32)
            qf = qh.astype(F32)
            qcat = jnp.concatenate([(qf * qdec[0, h]).astype(BF16), (qf * qdec[1, h]).astype(BF16)], axis=1)
            st = fstate[h]
            states = jnp.concatenate([st, rall[n, h]], axis=0).astype(BF16)
            y = y + jnp.dot(qcat, states, preferred_element_type=F32)
            fstate[h] = st * chunk_decay(0, h) + kv_update(0, h)
            mu = jnp.mean(y, axis=-1, keepdims=True)
            yc = y - mu
            var = jnp.mean(yc * yc, axis=-1, keepdims=True)
            yn = yc * lax.rsqrt(var + GN_EPS) * gnw_ref[:, vs]
            gate = g_ref[0, :, vs].astype(F32)
            out_ref[0, :, vs] = (gate * jax.nn.sigmoid(gate) * yn).astype(out_ref.dtype)


def _retention(rq, rk, rv, rg, decay_f, decay_b, gn_w, b, s):
    c = RET_CHUNK
    nc = s // c
    lg = jnp.stack([jax.nn.log_sigmoid(decay_f.astype(F32)), jax.nn.log_sigmoid(decay_b.astype(F32))])
    qk_w, v_w = RET_QK_WIDTH, RET_V_WIDTH

    def both(bb, ph, n):
        return (bb, jnp.where(ph == 0, nc - 1 - n, n), 0)

    def fwd_only(bb, ph, n):
        return (bb, jnp.where(ph == 0, 0, n), 0)

    out = pl.pallas_call(
        functools.partial(_ret_body, chunk=c),
        grid=(b, 2, nc),
        in_specs=[pl.BlockSpec(memory_space=pltpu.SMEM),
                  pl.BlockSpec((1, c, qk_w), fwd_only),
                  pl.BlockSpec((1, c, qk_w), both),
                  pl.BlockSpec((1, c, v_w), both),
                  pl.BlockSpec((1, c, v_w), fwd_only),
                  pl.BlockSpec((1, v_w), lambda bb, ph, n: (0, 0))],
        out_specs=pl.BlockSpec((1, c, v_w), fwd_only),
        out_shape=jax.ShapeDtypeStruct((b, s, v_w), BF16),
        scratch_shapes=[pltpu.VMEM((RET_HEADS, c, c), F32),
                        pltpu.VMEM((2, RET_HEADS, c, RET_QK_DIM), F32),
                        pltpu.VMEM((2, RET_HEADS, c, RET_QK_DIM), F32),
                        pltpu.VMEM((RET_HEADS, RET_QK_DIM, RET_V_DIM), F32),
                        pltpu.VMEM((RET_HEADS, RET_QK_DIM, RET_V_DIM), F32),
                        pltpu.VMEM((nc, RET_HEADS, RET_QK_DIM, RET_V_DIM), F32)],
        compiler_params=_cparams(("arbitrary", "arbitrary", "arbitrary"), VMEM_LIMIT),
        name="retention",
    )(lg, rq.reshape(b, s, qk_w), rk.reshape(b, s, qk_w), rv.reshape(b, s, v_w), rg.reshape(b, s, v_w),
      gn_w.reshape(1, v_w))
    return out.reshape(b * s, v_w)


def _out_proj_body(o1, o2, o3, l1, l2, l3, ret_ref, x_ref, w_ref, out_ref):
    a1, a2, a3 = l1[...], l2[...], l3[...]
    m = jnp.maximum(jnp.maximum(a1, a2), a3)
    e1, e2, e3 = jnp.exp2(a1 - m), jnp.exp2(a2 - m), jnp.exp2(a3 - m)
    attn = (e1 * o1[...].astype(F32) + e2 * o2[...].astype(F32) + e3 * o3[...].astype(F32)) / (e1 + e2 + e3)
    y = jnp.dot(attn.astype(BF16), w_ref[:ATTN_WIDTH], preferred_element_type=F32)
    y = y + jnp.dot(ret_ref[...], w_ref[ATTN_WIDTH:], preferred_element_type=F32)
    out_ref[...] = x_ref[...] + y


def _out_proj(outs, lses, ret, x2d, w_out):
    n, d = x2d.shape
    tm = TOKEN_TILE
    row = lambda w: pl.BlockSpec((tm, w), lambda i: (i, 0))
    return pl.pallas_call(
        _out_proj_body,
        grid=(n // tm,),
        in_specs=[row(ATTN_WIDTH)] * 6 + [row(RET_V_WIDTH), row(d),
                                          pl.BlockSpec(w_out.shape, lambda i: (0, 0))],
        out_specs=row(d),
        out_shape=jax.ShapeDtypeStruct((n, d), F32),
        compiler_params=_cparams(("parallel",), VMEM_LIMIT),
        name="out_proj",
    )(*outs, *lses, ret, x2d, w_out.astype(BF16))


def _mem_kv_body(mem_ref, nw_ref, w_ref, kw_ref, k_out, v_out):
    d = mem_ref.shape[-1]
    hd = d // MEM_HEADS
    h = _rms(mem_ref[0], nw_ref[...]).astype(BF16)
    kv = jnp.dot(h, w_ref[...], preferred_element_type=F32)
    for i in range(MEM_HEADS):
        k_out[0, :, i * hd:(i + 1) * hd] = _rms(kv[:, i * hd:(i + 1) * hd], kw_ref[...]).astype(BF16)
    v_out[0] = kv[:, d:].astype(BF16)


def _mem_kv(mem, norm_w, w_mkv, k_norm_w):
    b, m, d = mem.shape
    return pl.pallas_call(
        _mem_kv_body,
        grid=(b,),
        in_specs=[pl.BlockSpec((1, m, d), lambda i: (i, 0, 0)),
                  pl.BlockSpec((1, d), lambda i: (0, 0)),
                  pl.BlockSpec((d, 2 * d), lambda i: (0, 0)),
                  pl.BlockSpec((1, d // MEM_HEADS), lambda i: (0, 0))],
        out_specs=[pl.BlockSpec((1, m, d), lambda i: (i, 0, 0))] * 2,
        out_shape=[jax.ShapeDtypeStruct((b, m, d), BF16)] * 2,
        compiler_params=_cparams(("parallel",), VMEM_LIMIT),
        name="mem_kv",
    )(mem, norm_w.reshape(1, d), w_mkv.astype(BF16), k_norm_w.reshape(1, -1))


def _mem_attn_body(x_ref, nw_ref, wq_ref, qw_ref, k_ref, v_ref, wo_ref, out_ref):
    d = x_ref.shape[-1]
    hd = d // MEM_HEADS
    x = x_ref[0]
    h = _rms(x, nw_ref[...]).astype(BF16)
    q = jnp.dot(h, wq_ref[...], preferred_element_type=F32)
    heads = []
    for i in range(MEM_HEADS):
        cs = slice(i * hd, (i + 1) * hd)
        qn = _rms(q[:, cs], qw_ref[...]).astype(BF16)
        s = lax.dot_general(qn, k_ref[0, :, cs], (((1,), (1,)), ((), ())), preferred_element_type=F32)
        e = jnp.exp(s - jnp.max(s, axis=-1, keepdims=True))
        o = jnp.dot(e.astype(BF16), v_ref[0, :, cs], preferred_element_type=F32)
        heads.append((o / jnp.sum(e, axis=-1, keepdims=True)).astype(BF16))
    o = jnp.concatenate(heads, axis=1)
    out_ref[0] = x + jnp.dot(o, wo_ref[...], preferred_element_type=F32)


def _mem_attn(x1, mk, mv, norm_w, w_mq, q_norm_w, w_mo):
    b, s, d = x1.shape
    m = mk.shape[1]
    tm = TOKEN_TILE
    hd = d // MEM_HEADS
    qw = (q_norm_w * (hd ** -0.5)).reshape(1, hd)
    tok = pl.BlockSpec((1, tm, d), lambda bb, i: (bb, i, 0))
    const = lambda r, c: pl.BlockSpec((r, c), lambda bb, i: (0, 0))
    mem = pl.BlockSpec((1, m, d), lambda bb, i: (bb, 0, 0))
    return pl.pallas_call(
        _mem_attn_body,
        grid=(b, s // tm),
        in_specs=[tok, const(1, d), const(d, d), const(1, hd), mem, mem, const(d, d)],
        out_specs=tok,
        out_shape=jax.ShapeDtypeStruct((b, s, d), F32),
        compiler_params=_cparams(("parallel", "parallel"), VMEM_LIMIT),
        name="mem_attn",
    )(x1, norm_w.reshape(1, d), w_mq.astype(BF16), qw, mk, mv, w_mo.astype(BF16))


def _router_body(x_ref, nw_ref, whi_ref, wlo_ref, h_out, gate_out, meta_out, cnt_out, carry):
    tm = x_ref.shape[0]
    i = pl.program_id(0)

    @pl.when(i == 0)
    def _():
        carry[...] = jnp.zeros_like(carry)

    h = _rms(x_ref[...], nw_ref[...])
    _store_row_tiles(h_out, h)
    hi, lo = _split_hi_lo(h)
    logits = (jnp.dot(hi, whi_ref[...], preferred_element_type=F32)
              + jnp.dot(lo, whi_ref[...], preferred_element_type=F32)
              + jnp.dot(hi, wlo_ref[...], preferred_element_type=F32))
    lane = lax.broadcasted_iota(I32, (tm, LANES), 1)

    def argmax_lanes(vals):
        top = jnp.max(vals, axis=-1, keepdims=True)
        idx = jnp.min(jnp.where(vals == top, lane, LANES), axis=-1, keepdims=True)
        return top, idx

    is_group = lane < N_GROUPS
    g_top, grp = argmax_lanes(jnp.where(is_group, logits, -jnp.inf))
    g_gate = 1.0 / jnp.sum(jnp.where(is_group, jnp.exp(logits - g_top), 0.0), axis=-1, keepdims=True)
    in_group = (lane >= N_GROUPS) & (lane < N_GROUPS + N_EXPERTS) & (
        ((lane - N_GROUPS) // EXPERTS_PER_GROUP) == grp)
    el = jnp.where(in_group, logits, -jnp.inf)
    t1, i1 = argmax_lanes(el)
    t2, i2 = argmax_lanes(jnp.where(lane == i1, -jnp.inf, el))
    z = jnp.exp(t2 - t1)
    g1 = g_gate / (1.0 + z)
    g2 = g_gate * z / (1.0 + z)
    e1 = i1 - N_GROUPS
    e2 = i2 - N_GROUPS

    oh1 = (lane == e1)
    oh2 = (lane == e2)
    oh1b = oh1.astype(F32).astype(BF16)
    oh2b = oh2.astype(F32).astype(BF16)
    rr = lax.broadcasted_iota(I32, (tm, tm), 0)
    cc = lax.broadcasted_iota(I32, (tm, tm), 1)
    below = (cc < rr).astype(F32).astype(BF16)
    pre1 = jnp.dot(below, oh1b, preferred_element_type=F32)
    pre2 = jnp.dot(below, oh2b, preferred_element_type=F32)
    cnt1 = jnp.sum(oh1.astype(F32), axis=0, keepdims=True)
    cnt2 = jnp.sum(oh2.astype(F32), axis=0, keepdims=True)
    base = carry[...]
    r1 = jnp.sum(jnp.where(oh1, pre1 + base, 0.0), axis=-1, keepdims=True)
    r2 = jnp.sum(jnp.where(oh2, pre2 + base + cnt1, 0.0), axis=-1, keepdims=True)
    total = base + cnt1 + cnt2
    carry[...] = total
    cnt_out[...] = total

    gate_out[...] = jnp.where(lane == 0, g1, jnp.where(lane == 1, g2, 0.0))
    meta_out[...] = jnp.where(lane == 0, e1, jnp.where(lane == 1, e2, jnp.where(
        lane == 2, r1.astype(I32), jnp.where(lane == 3, r2.astype(I32), 0))))


def _router(x2d, norm_w, w_group, w_router):
    n, d = x2d.shape
    tm = TOKEN_TILE
    w_all = jnp.concatenate([w_group, w_router.transpose(1, 0, 2).reshape(d, N_EXPERTS)], axis=1)
    w_all = jnp.pad(w_all, ((0, 0), (0, LANES - w_all.shape[1])))
    whi = w_all.astype(BF16)
    wlo = (w_all - whi.astype(F32)).astype(BF16)
    row = lambda w: pl.BlockSpec((tm, w), lambda i: (i, 0))
    const = lambda r, c: pl.BlockSpec((r, c), lambda i: (0, 0))
    return pl.pallas_call(
        _router_body,
        grid=(n // tm,),
        in_specs=[row(d), const(1, d), const(d, LANES), const(d, LANES)],
        out_specs=[pl.BlockSpec((tm * ROW_SLABS, LANES), lambda i: (i, 0)), row(LANES), row(LANES), const(1, LANES)],
        out_shape=[jax.ShapeDtypeStruct((n * ROW_SLABS, LANES), F32), jax.ShapeDtypeStruct((n, LANES), F32),
                   jax.ShapeDtypeStruct((n, LANES), I32), jax.ShapeDtypeStruct((1, LANES), F32)],
        scratch_shapes=[pltpu.VMEM((1, LANES), F32)],
        compiler_params=_cparams(("arbitrary",), VMEM_LIMIT),
        name="router",
    )(x2d, norm_w.reshape(1, d), whi, wlo)


def _row_copy(src_ref, src_row, dst_ref, dst_row, sem):
    return pltpu.make_async_copy(_tile_rows(src_ref, src_row), _tile_rows(dst_ref, dst_row), sem)


def _dispatch_body(cnt_ref, eid_ref, rank_ref, h_ref, xs_ref, dest_ref, pstart, zbuf, sems):
    i = pl.program_id(0)
    tt = h_ref.shape[0] // ROW_SLABS
    blk = EXPERT_BLOCK
    n_slots = xs_ref.shape[0] // ROW_SLABS

    def pad_copy(pend):
        return pltpu.make_async_copy(zbuf, _tile_rows(xs_ref, pend - blk, blk), sems.at[2])

    @pl.when(i == 0)
    def _():
        zbuf[...] = jnp.zeros_like(zbuf)

        def start(e, acc):
            pstart[e] = acc
            pend = acc + ((cnt_ref[e] + (blk - 1)) // blk) * blk

            @pl.when(pend > acc)
            def _():
                pad_copy(pend).start()
            return pend

        used = lax.fori_loop(0, N_EXPERTS, start, jnp.int32(0))

        def tail_start(j, carry):
            pad_copy((j + 1) * blk).start()
            return carry

        lax.fori_loop(used // blk, n_slots // blk, tail_start, 0)

        def finish(e, acc):
            pend = acc + ((cnt_ref[e] + (blk - 1)) // blk) * blk

            @pl.when(pend > acc)
            def _():
                pad_copy(pend).wait()
            return pend

        lax.fori_loop(0, N_EXPERTS, finish, jnp.int32(0))

        def tail_wait(j, carry):
            pad_copy((j + 1) * blk).wait()
            return carry

        lax.fori_loop(used // blk, n_slots // blk, tail_wait, 0)

    def tokens(g, carry):
        t0 = pl.multiple_of(g * ROW_DMA_UNROLL, ROW_DMA_UNROLL)
        for u in range(ROW_DMA_UNROLL):
            for k in range(2):
                a = 2 * (t0 + u) + k
                d = pstart[eid_ref[a]] + rank_ref[a]
                dest_ref[a] = d
                _row_copy(h_ref, t0 + u, xs_ref, d, sems.at[k]).start(priority=k)
        return carry

    lax.fori_loop(0, tt // ROW_DMA_UNROLL, tokens, 0)
    for k in range(2):
        pltpu.make_async_copy(h_ref, _tile_rows(xs_ref, 0, tt), sems.at[k]).wait()


def _dispatch(counts, eid, rank, h3, n_slots):
    n = h3.shape[0] // ROW_SLABS
    tt = DISPATCH_TILE
    smem_blk = pl.BlockSpec((2 * tt,), lambda i, cnt: (i,), memory_space=pltpu.SMEM)
    grid_spec = pltpu.PrefetchScalarGridSpec(
        num_scalar_prefetch=1,
        grid=(n // tt,),
        in_specs=[smem_blk, smem_blk, pl.BlockSpec((tt * ROW_SLABS, LANES), lambda i, cnt: (i, 0))],
        out_specs=[pl.BlockSpec(memory_space=pl.ANY), smem_blk],
        scratch_shapes=[pltpu.SMEM((N_EXPERTS,), I32), pltpu.VMEM((EXPERT_BLOCK * ROW_SLABS, LANES), F32),
                        pltpu.SemaphoreType.DMA((3,))],
    )
    return pl.pallas_call(
        _dispatch_body,
        grid_spec=grid_spec,
        out_shape=[jax.ShapeDtypeStruct((n_slots * ROW_SLABS, LANES), F32), jax.ShapeDtypeStruct((2 * n,), I32)],
        compiler_params=_cparams(("arbitrary",), VMEM_LIMIT),
        name="dispatch",
    )(counts, eid, rank, h3)


def _experts_body(be_ref, nu_ref, xs_ref, w1_ref, w3_ref, w2_ref, ys_ref, w1b, w3b, w2b):
    i = pl.program_id(0)
    n_used = nu_ref[0]

    @pl.when(i < n_used)
    def _():
        changed = (i == 0) | (be_ref[i] != be_ref[jnp.maximum(i - 1, 0)])

        @pl.when(changed)
        def _():
            w1b[...] = w1_ref[0].astype(BF16)
            w3b[...] = w3_ref[0].astype(BF16)
            w2b[...] = w2_ref[0].astype(BF16)

        x = _load_row_tiles(xs_ref).astype(BF16)
        a = jnp.dot(x, w1b[...], preferred_element_type=F32)
        u = jnp.dot(x, w3b[...], preferred_element_type=F32)
        mid = (a * jax.nn.sigmoid(a) * u).astype(BF16)
        _store_row_tiles(ys_ref, jnp.dot(mid, w2b[...], preferred_element_type=F32))

    @pl.when(i >= n_used)
    def _():
        ys_ref[...] = jnp.zeros_like(ys_ref)


def _experts(block_expert, n_used, xs, w1, w3, w2):
    _, d, ff = w1.shape
    assert d == ROW_SLABS * LANES
    blk = EXPERT_BLOCK
    n_blocks = xs.shape[0] // (blk * ROW_SLABS)

    def slot_map(i, be, nu):
        return (jnp.minimum(i, nu[0] - 1), 0)

    def w_map(i, be, nu):
        return (be[jnp.minimum(i, nu[0] - 1)], 0, 0)

    grid_spec = pltpu.PrefetchScalarGridSpec(
        num_scalar_prefetch=2,
        grid=(n_blocks,),
        in_specs=[pl.BlockSpec((blk * ROW_SLABS, LANES), slot_map),
                  pl.BlockSpec((1, d, ff), w_map), pl.BlockSpec((1, d, ff), w_map),
                  pl.BlockSpec((1, ff, d), w_map)],
        out_specs=pl.BlockSpec((blk * ROW_SLABS, LANES), lambda i, be, nu: (i, 0)),
        scratch_shapes=[pltpu.VMEM((d, ff), BF16), pltpu.VMEM((d, ff), BF16), pltpu.VMEM((ff, d), BF16)],
    )
    return pl.pallas_call(
        _experts_body,
        grid_spec=grid_spec,
        out_shape=jax.ShapeDtypeStruct(xs.shape, F32),
        compiler_params=_cparams(("arbitrary",), VMEM_LIMIT),
        name="experts",
    )(block_expert, n_used, xs, w1, w3, w2)


def _combine_body(dest_ref, gate_ref, x_ref, ys_ref, out_ref, buf0, buf1, sems):
    tt = x_ref.shape[0]
    bufs = (buf0, buf1)

    def tokens(g, carry):
        t0 = pl.multiple_of(g * ROW_DMA_UNROLL, ROW_DMA_UNROLL)
        for u in range(ROW_DMA_UNROLL):
            for k in range(2):
                _row_copy(ys_ref, dest_ref[2 * (t0 + u) + k], bufs[k], t0 + u, sems.at[k]).start(priority=k)
        return carry

    lax.fori_loop(0, tt // ROW_DMA_UNROLL, tokens, 0)
    for k in range(2):
        pltpu.make_async_copy(_tile_rows(ys_ref, 0, tt), bufs[k], sems.at[k]).wait()
    g = gate_ref[...]
    out_ref[...] = x_ref[...] + g[:, 0:1] * _load_row_tiles(buf0) + g[:, 1:2] * _load_row_tiles(buf1)


def _combine(dest, gates, x2d, ys):
    n, d = x2d.shape
    tt = COMBINE_TILE
    row = lambda w: pl.BlockSpec((tt, w), lambda i: (i, 0))
    return pl.pallas_call(
        _combine_body,
        grid=(n // tt,),
        in_specs=[pl.BlockSpec((2 * tt,), lambda i: (i,), memory_space=pltpu.SMEM), row(LANES), row(d),
                  pl.BlockSpec(memory_space=pl.ANY)],
        out_specs=row(d),
        out_shape=jax.ShapeDtypeStruct((n, d), F32),
        scratch_shapes=[pltpu.VMEM((tt * ROW_SLABS, LANES), F32), pltpu.VMEM((tt * ROW_SLABS, LANES), F32),
                        pltpu.SemaphoreType.DMA((2,))],
        compiler_params=_cparams(("arbitrary",), VMEM_LIMIT),
        name="combine",
    )(dest, gates, x2d, ys)


def _moe(x2d, norm_w, w_group, w_router, w1, w3, w2):
    n, d = x2d.shape
    blk = EXPERT_BLOCK
    h3, gates, meta, cnt = _router(x2d, norm_w, w_group, w_router)
    counts = cnt[0, :N_EXPERTS].astype(I32)
    eid = meta[:, 0:2].reshape(-1)
    rank = meta[:, 2:4].reshape(-1)
    n_slots = 2 * n + N_EXPERTS * blk
    n_blocks = n_slots // blk
    pends = jnp.cumsum((counts + blk - 1) // blk * blk)
    block_start = jnp.arange(n_blocks, dtype=I32) * blk
    block_expert = jnp.minimum(jnp.sum((pends[None, :] <= block_start[:, None]).astype(I32), axis=1),
                               N_EXPERTS - 1)
    n_used = (pends[-1:] // blk).astype(I32)
    xs, dest = _dispatch(counts, eid, rank, h3, n_slots)
    ys = _experts(block_expert, n_used, xs, w1, w3, w2)
    return _combine(dest, gates, x2d, ys)


def _layer(x, mem, norm_mix_w, w_in, attn_q_norm_w, attn_k_norm_w, ret_decay_f, ret_decay_b, ret_gn_w, w_out,
           norm_mem_w, norm_memkv_w, w_mq, w_mkv, mem_q_norm_w, mem_k_norm_w, w_mo,
           norm_moe_w, w_group, w_router, w_exp_gate, w_exp_up, w_exp_down):
    b, s, d = x.shape
    x2d = x.reshape(b * s, d)
    q, k, v, rq, rk, rv, rg = _in_proj(x2d, norm_mix_w, w_in, attn_q_norm_w, attn_k_norm_w)
    outs, lses = [], []
    for dil in DILATIONS:
        o, lse = _attention_pattern(*(_to_subsequences(t, b, s, dil) for t in (q, k, v)), dil)
        outs.append(_from_subsequences(o, b, s, dil))
        lses.append(_from_subsequences(lse, b, s, dil))
    ret = _retention(rq, rk, rv, rg, ret_decay_f, ret_decay_b, ret_gn_w, b, s)
    x1 = _out_proj(outs, lses, ret, x2d, w_out)
    mk, mv = _mem_kv(mem, norm_memkv_w, w_mkv, mem_k_norm_w)
    x2 = _mem_attn(x1.reshape(b, s, d), mk, mv, norm_mem_w, w_mq, mem_q_norm_w, w_mo)
    x3 = _moe(x2.reshape(b * s, d), norm_moe_w, w_group, w_router, w_exp_gate, w_exp_up, w_exp_down)
    return x3.reshape(b, s, d)


def kernel(x, mem, norm_mix_w, w_in, attn_q_norm_w, attn_k_norm_w, ret_decay_f, ret_decay_b, ret_gn_w, w_out,
           norm_mem_w, norm_memkv_w, w_mq, w_mkv, mem_q_norm_w, mem_k_norm_w, w_mo, norm_moe_w, w_group,
           w_router, w_exp_gate, w_exp_up, w_exp_down):
    depth = norm_mix_w.shape[0]
    for l in range(depth):
        x = _layer(x, mem, norm_mix_w[l], w_in[l], attn_q_norm_w[l], attn_k_norm_w[l], ret_decay_f[l],
                   ret_decay_b[l], ret_gn_w[l], w_out[l], norm_mem_w[l], norm_memkv_w[l], w_mq[l], w_mkv[l],
                   mem_q_norm_w[l], mem_k_norm_w[l], w_mo[l], norm_moe_w[l], w_group[l], w_router[l],
                   w_exp_gate[l], w_exp_up[l], w_exp_down[l])
    return x
```

```python
import functools

import numpy as np
import jax
import jax.numpy as jnp
from jax import lax
from jax.experimental import pallas as pl
from jax.experimental.pallas import tpu as pltpu

F32 = jnp.float32
BF16 = jnp.bfloat16
I32 = jnp.int32

NORM_EPS = 1e-6
GN_EPS = 1e-5
NEG_INF = -1e30

ATTN_HEADS = 8
ATTN_HEAD_DIM = 64
ATTN_WIDTH = ATTN_HEADS * ATTN_HEAD_DIM
ATTN_HALF = 64
PLANES = 16
ATTN_TILE = PLANES * ATTN_HALF
RET_HEADS = 4
RET_QK_DIM = 64
RET_V_DIM = 128
RET_QK_WIDTH = RET_HEADS * RET_QK_DIM
RET_V_WIDTH = RET_HEADS * RET_V_DIM
MEM_HEADS = 4
N_GROUPS = 4
EXPERTS_PER_GROUP = 8
N_EXPERTS = N_GROUPS * EXPERTS_PER_GROUP

LANES = 128
TOKEN_TILE = 512
ATTN_BLOCKS_PER_STEP = 4
LOG2_E = 1.4426950408889634
RET_CHUNK = 256
EXPERT_BLOCK = 256
DISPATCH_TILE = 512
COMBINE_TILE = 256
ROW_DMA_UNROLL = 16
VMEM_LIMIT = 56 * 1024 * 1024


def _cparams(sem, vmem=None):
    return pltpu.CompilerParams(dimension_semantics=sem, vmem_limit_bytes=vmem)


def _split_hi_lo(x):
    hi = x.astype(BF16)
    lo = (x - hi.astype(F32)).astype(BF16)
    return hi, lo


def _rms(x, w):
    ms = jnp.mean(x * x, axis=-1, keepdims=True)
    return x * lax.rsqrt(ms + NORM_EPS) * w


ROW_SLABS = 8


def _store_row_tiles(ref, val):
    rows = val.shape[0]
    for j in range(ROW_SLABS):
        ref[pl.ds(j, rows, stride=ROW_SLABS), :] = val[:, j * LANES:(j + 1) * LANES]


def _load_row_tiles(ref):
    rows = ref.shape[0] // ROW_SLABS
    return jnp.concatenate([ref[pl.ds(j, rows, stride=ROW_SLABS), :] for j in range(ROW_SLABS)], axis=1)


def _tile_rows(ref, row, count=1):
    return ref.at[pl.ds(pl.multiple_of(row * ROW_SLABS, ROW_SLABS), count * ROW_SLABS)]


def _in_proj_body(x_ref, nw_ref, w_ref, qw_ref, kw_ref, g_ref,
                  q_out, k_out, v_out, qp_out, kp_out, vp_out, rq_out, rk_out, rv_out, rg_out, slabs):
    tm = x_ref.shape[0]
    h = _rms(x_ref[...], nw_ref[...]).astype(BF16)

    def emit(val, nat_out, plane_out):
        nat_out[...] = val.astype(BF16)
        for j in range(ATTN_WIDTH // LANES):
            slabs[j] = val[:, j * LANES:(j + 1) * LANES]
        for c in range(PLANES):
            for j in range(ATTN_WIDTH // LANES):
                rows = slabs[j, pl.ds(c, tm // PLANES, stride=PLANES), :]
                plane_out[0, c, :, j * LANES:(j + 1) * LANES] = rows.astype(BF16)

    def proj(a, b):
        return jnp.dot(h, w_ref[:, a:b], preferred_element_type=F32)

    def head_norm(p, w):
        hi, lo = _split_hi_lo(p * p)
        s = (jnp.dot(hi, g_ref[...], preferred_element_type=F32)
             + jnp.dot(lo, g_ref[...], preferred_element_type=F32))
        return p * lax.rsqrt(s * (1.0 / ATTN_HEAD_DIM) + NORM_EPS) * w

    a = ATTN_WIDTH
    emit(head_norm(proj(0, a), qw_ref[...]), q_out, qp_out)
    emit(head_norm(proj(a, 2 * a), kw_ref[...]), k_out, kp_out)
    emit(proj(2 * a, 3 * a), v_out, vp_out)
    c = 3 * a
    rq_out[...] = proj(c, c + RET_QK_WIDTH).astype(BF16)
    rk_out[...] = proj(c + RET_QK_WIDTH, c + 2 * RET_QK_WIDTH).astype(BF16)
    c += 2 * RET_QK_WIDTH
    rv_out[...] = proj(c, c + RET_V_WIDTH).astype(BF16)
    rg_out[...] = proj(c + RET_V_WIDTH, c + 2 * RET_V_WIDTH).astype(BF16)


def _in_proj(x2d, norm_w, w_in, q_norm_w, k_norm_w):
    n, d = x2d.shape
    cols = w_in.shape[1]
    tm = TOKEN_TILE
    qw = (jnp.tile(q_norm_w, ATTN_HEADS) * (ATTN_HEAD_DIM ** -0.5 * LOG2_E)).reshape(1, ATTN_WIDTH)
    kw = jnp.tile(k_norm_w, ATTN_HEADS).reshape(1, ATTN_WIDTH)
    head_of = np.arange(ATTN_WIDTH) // ATTN_HEAD_DIM
    gmat = jnp.asarray(head_of[:, None] == head_of[None, :], dtype=BF16)
    row = lambda w: pl.BlockSpec((tm, w), lambda i: (i, 0))
    full = lambda r, c: pl.BlockSpec((r, c), lambda i: (0, 0))
    per_tile = ATTN_TILE // tm
    plane = pl.BlockSpec((1, PLANES, tm // PLANES, ATTN_WIDTH), lambda i: (i // per_tile, 0, i % per_tile, 0))
    plane_shape = jax.ShapeDtypeStruct((n // ATTN_TILE, PLANES, ATTN_TILE // PLANES, ATTN_WIDTH), BF16)
    ret_widths = (RET_QK_WIDTH, RET_QK_WIDTH, RET_V_WIDTH, RET_V_WIDTH)
    return pl.pallas_call(
        _in_proj_body,
        grid=(n // tm,),
        in_specs=[row(d), full(1, d), full(d, cols), full(1, ATTN_WIDTH), full(1, ATTN_WIDTH),
                  full(ATTN_WIDTH, ATTN_WIDTH)],
        out_specs=[row(ATTN_WIDTH)] * 3 + [plane] * 3 + [row(w) for w in ret_widths],
        out_shape=([jax.ShapeDtypeStruct((n, ATTN_WIDTH), BF16)] * 3 + [plane_shape] * 3
                   + [jax.ShapeDtypeStruct((n, w), BF16) for w in ret_widths]),
        scratch_shapes=[pltpu.VMEM((ATTN_WIDTH // LANES, tm, LANES), F32)],
        compiler_params=_cparams(("parallel",), VMEM_LIMIT),
        name="in_proj",
    )(x2d, norm_w.reshape(1, d), w_in.astype(BF16), qw, kw, gmat)


def _attn_bias(dilation, interleave=1):
    half = ATTN_HALF
    idx = np.arange(half)
    sub = (idx % (half // interleave)) * interleave + idx // (half // interleave)
    qi = sub[:, None]
    kc = (np.arange(3)[:, None] * half + sub[None, :]).reshape(1, -1)
    seg = np.repeat(np.arange(3), half)[None, :]
    delta = kc - half - qi
    band = np.abs(delta) <= half
    slopes = np.exp2(-8.0 * np.arange(1, ATTN_HEADS + 1) / ATTN_HEADS)
    valid = [band & (seg >= 1), band, band & (seg <= 1)]
    out = np.empty((ATTN_HEADS // 2, 3, 2 * half, 3 * half), np.float32)
    for p in range(ATTN_HEADS // 2):
        for v in range(3):
            for s in range(2):
                b = -slopes[2 * p + s] * LOG2_E * (dilation * np.abs(delta)).astype(np.float32)
                out[p, v, s * half:(s + 1) * half] = np.where(valid[v], b, NEG_INF)
    return jnp.asarray(out.reshape(-1, 2 * half, 3 * half))


def _attend(chains):
    half = ATTN_HALF
    lo_lanes = lax.broadcasted_iota(I32, (half, LANES), 1) < ATTN_HEAD_DIM
    ones = jnp.ones((3 * half, LANES), BF16)
    scores = []
    for q, k, v, bias in chains:
        zero = jnp.zeros_like(q)
        qs = jnp.concatenate([jnp.where(lo_lanes, q, zero), jnp.where(lo_lanes, zero, q)], axis=0)
        scores.append(lax.dot_general(qs, k, (((1,), (1,)), ((), ())), preferred_element_type=F32))
    probs, maxes = [], []
    for (q, k, v, bias), s in zip(chains, scores):
        s = s + bias
        m = jnp.max(s, axis=-1, keepdims=True)
        probs.append(jnp.exp2(s - m).astype(BF16))
        maxes.append(m)
    results = []
    for (q, k, v, bias), e in zip(chains, probs):
        vext = jnp.concatenate([v, ones], axis=1)
        results.append(jnp.dot(e, vext, preferred_element_type=F32))
    outs = []
    for r, m in zip(results, maxes):
        acc = jnp.where(lo_lanes, r[:half, :LANES], r[half:, :LANES])
        denom = jnp.where(lo_lanes, r[:half, LANES:], r[half:, LANES:])
        mm = jnp.where(lo_lanes, m[:half], m[half:])
        outs.append((acc / denom, mm + jnp.log2(denom)))
    return outs


def _merge(oa, la, ob, lb):
    m = jnp.maximum(la, lb)
    ea = jnp.exp2(la - m)
    eb = jnp.exp2(lb - m)
    den = ea + eb
    return (ea * oa + eb * ob) / den, m + jnp.log2(den)


def _attn_body(q_ref, kp_ref, km_ref, kn_ref, vp_ref, vm_ref, vn_ref,
               qpl_ref, kplp_ref, kplm_ref, kpln_ref, vplp_ref, vplm_ref, vpln_ref,
               b1_ref, b4_ref, b16_ref, o_ref, kcat, vcat, o_far, l_far):
    half = ATTN_HALF
    t = pl.program_id(1)
    n_tiles = pl.num_programs(1)
    tile = q_ref.shape[1]
    pairs = ATTN_HEADS // 2
    piece = half // 4

    kcat[0:half] = kp_ref[0]
    kcat[half:half + tile] = km_ref[0]
    kcat[half + tile:] = kn_ref[0]
    vcat[0:half] = vp_ref[0]
    vcat[half:half + tile] = vm_ref[0]
    vcat[half + tile:] = vn_ref[0]

    def edge(first, last):
        return jnp.where(first, 0, jnp.where(last, 2, 1))

    def far_step(r, carry):
        v16 = edge(t == 0, t == n_tiles - 1)
        for pr in range(pairs):
            cs = slice(pr * LANES, (pr + 1) * LANES)
            chains = []
            for m in range(4):
                c = r + 4 * m
                k3 = jnp.concatenate([kplp_ref[0, c, :, cs], kplm_ref[0, c, :, cs], kpln_ref[0, c, :, cs]], axis=0)
                v3 = jnp.concatenate([vplp_ref[0, c, :, cs], vplm_ref[0, c, :, cs], vpln_ref[0, c, :, cs]], axis=0)
                chains.append((qpl_ref[0, c, :, cs], k3, v3, b16_ref[pr * 3 + v16]))
            for nb in range(4):
                def rows(main, before, after, a0):
                    ref, lo = (before, a0 + half) if a0 < 0 else (after, a0 - half) if a0 >= half else (main, a0)
                    return [ref[0, r + 4 * m, lo:lo + piece, cs] for m in range(4)]
                a0 = nb * piece
                q4 = jnp.concatenate(rows(qpl_ref, None, None, a0), axis=0)
                k4 = jnp.concatenate(sum((rows(kplm_ref, kplp_ref, kpln_ref, a0 + d) for d in (-piece, 0, piece)), []),
                                     axis=0)
                v4 = jnp.concatenate(sum((rows(vplm_ref, vplp_ref, vpln_ref, a0 + d) for d in (-piece, 0, piece)), []),
                                     axis=0)
                v4e = edge((t == 0) & (nb == 0), (t == n_tiles - 1) & (nb == 3))
                chains.append((q4, k4, v4, b4_ref[pr * 3 + v4e]))
            res = _attend(chains)
            for m in range(4):
                o16, l16 = res[m]
                o_rows, l_rows = [], []
                for nb in range(4):
                    o4, l4 = res[4 + nb]
                    sl = slice(nb * piece, (nb + 1) * piece)
                    s4 = slice(m * piece, (m + 1) * piece)
                    om, lm = _merge(o16[sl], l16[sl], o4[s4], l4[s4])
                    o_rows.append(om)
                    l_rows.append(lm)
                dst = pl.ds(r + 4 * m, half, stride=PLANES)
                o_far[pr, dst, :] = jnp.concatenate(o_rows, axis=0)
                l_far[pr, dst, :] = jnp.concatenate(l_rows, axis=0)
        return carry

    lax.fori_loop(0, 4, far_step, 0)

    unroll = ATTN_BLOCKS_PER_STEP
    n_blocks = n_tiles * (tile // half)

    def near_step(it, carry):
        chains, where = [], []
        for u in range(unroll):
            jb = it * unroll + u
            r0 = pl.multiple_of(jb * half, half)
            gb = t * (tile // half) + jb
            variant = edge(gb == 0, gb == n_blocks - 1)
            for pr in range(pairs):
                cs = slice(pr * LANES, (pr + 1) * LANES)
                chains.append((q_ref[0, pl.ds(r0, half), cs], kcat[pl.ds(r0, 3 * half), cs],
                               vcat[pl.ds(r0, 3 * half), cs], b1_ref[pr * 3 + variant]))
                where.append((r0, pr, cs))
        for (r0, pr, cs), (o1, l1) in zip(where, _attend(chains)):
            o, _ = _merge(o1, l1, o_far[pr, pl.ds(r0, half), :], l_far[pr, pl.ds(r0, half), :])
            o_ref[0, pl.ds(r0, half), cs] = o.astype(o_ref.dtype)
        return carry

    lax.fori_loop(0, tile // half // unroll, near_step, 0)


def _attention(q, k, v, q_pl, k_pl, v_pl, b, s):
    w = ATTN_WIDTH
    half = ATTN_HALF
    tile = ATTN_TILE
    n_tiles = s // tile
    hb = tile // half
    last = s // half - 1
    main = pl.BlockSpec((1, tile, w), lambda bb, t: (bb, t, 0))
    prev = pl.BlockSpec((1, half, w), lambda bb, t: (bb, jnp.maximum(t * hb - 1, 0), 0))
    nxt = pl.BlockSpec((1, half, w), lambda bb, t: (bb, jnp.minimum((t + 1) * hb, last), 0))
    pshape = (1, PLANES, tile // PLANES, w)
    pl_main = pl.BlockSpec(pshape, lambda bb, t: (bb * n_tiles + t, 0, 0, 0))
    pl_prev = pl.BlockSpec(pshape, lambda bb, t: (bb * n_tiles + jnp.maximum(t - 1, 0), 0, 0, 0))
    pl_next = pl.BlockSpec(pshape, lambda bb, t: (bb * n_tiles + jnp.minimum(t + 1, n_tiles - 1), 0, 0, 0))
    biases = [_attn_bias(1), _attn_bias(4, interleave=4), _attn_bias(16)]
    bias_spec = pl.BlockSpec(biases[0].shape, lambda bb, t: (0, 0, 0))
    nat = lambda a: a.reshape(b, s, w)
    out = pl.pallas_call(
        _attn_body,
        grid=(b, n_tiles),
        in_specs=[main, prev, main, nxt, prev, main, nxt,
                  pl_main, pl_prev, pl_main, pl_next, pl_prev, pl_main, pl_next,
                  bias_spec, bias_spec, bias_spec],
        out_specs=main,
        out_shape=jax.ShapeDtypeStruct((b, s, w), BF16),
        scratch_shapes=[pltpu.VMEM((tile + 2 * half, w), BF16), pltpu.VMEM((tile + 2 * half, w), BF16),
                        pltpu.VMEM((ATTN_HEADS // 2, tile, LANES), F32),
                        pltpu.VMEM((ATTN_HEADS // 2, tile, LANES), F32)],
        compiler_params=_cparams(("parallel", "parallel"), VMEM_LIMIT),
        name="attention",
    )(nat(q), nat(k), nat(k), nat(k), nat(v), nat(v), nat(v),
      q_pl, k_pl, k_pl, k_pl, v_pl, v_pl, v_pl, *biases)
    return out.reshape(b * s, w)


def _ret_body(lg_ref, q_ref, k_ref, v_ref, g_ref, gnw_ref, out_ref,
              dmat, qdec, kdec, fstate, rstate, rall, *, chunk):
    c = chunk
    b = pl.program_id(0)
    ph = pl.program_id(1)
    n = pl.program_id(2)
    n_chunks = pl.num_programs(2)
    k_scale = RET_QK_DIM ** -0.5

    @pl.when((b == 0) & (ph == 0) & (n == 0))
    def _init_tables():
        ii = lax.broadcasted_iota(I32, (c, c), 0)
        jj = lax.broadcasted_iota(I32, (c, c), 1)
        fwd = (ii - jj).astype(F32)
        ri = lax.broadcasted_iota(I32, (c, RET_QK_DIM), 0).astype(F32)
        for h in range(RET_HEADS):
            lf = lg_ref[0, h]
            lb = lg_ref[1, h]
            dmat[h] = jnp.where(ii >= jj, jnp.exp(lf * fwd), jnp.exp(-lb * fwd)) * k_scale
            qdec[0, h] = jnp.exp(lf * (ri + 1.0))
            qdec[1, h] = jnp.exp(lb * (c - ri))
            kdec[0, h] = jnp.exp(lf * (c - 1.0 - ri)) * k_scale
            kdec[1, h] = jnp.exp(lb * ri) * k_scale

    def chunk_decay(direction, h):
        return jnp.exp(jnp.full((RET_QK_DIM, RET_V_DIM), lg_ref[direction, h] * c, F32))

    def kv_update(direction, h):
        kh = k_ref[0, :, h * RET_QK_DIM:(h + 1) * RET_QK_DIM].astype(F32)
        ks = (kh * kdec[direction, h]).astype(BF16)
        vh = v_ref[0, :, h * RET_V_DIM:(h + 1) * RET_V_DIM]
        return lax.dot_general(ks, vh, (((0,), (0,)), ((), ())), preferred_element_type=F32)

    @pl.when(ph == 0)
    def _right_to_left():
        @pl.when(n == 0)
        def _():
            rstate[...] = jnp.zeros_like(rstate)

        ci = n_chunks - 1 - n
        for h in range(RET_HEADS):
            st = rstate[h]
            rall[ci, h] = st
            rstate[h] = st * chunk_decay(1, h) + kv_update(1, h)

    @pl.when(ph == 1)
    def _left_to_right():
        @pl.when(n == 0)
        def _():
            fstate[...] = jnp.zeros_like(fstate)

        for h in range(RET_HEADS):
            qs = slice(h * RET_QK_DIM, (h + 1) * RET_QK_DIM)
            vs = slice(h * RET_V_DIM, (h + 1) * RET_V_DIM)
            qh = q_ref[0, :, qs]
            kh = k_ref[0, :, qs]
            vh = v_ref[0, :, vs]
            s = lax.dot_general(qh, kh, (((1,), (1,)), ((), ())), preferred_element_type=F32) * dmat[h]
            y = jnp.dot(s.astype(BF16), vh, preferred_element_type=F32)
            qf = qh.astype(F32)
            qcat = jnp.concatenate([(qf * qdec[0, h]).astype(BF16), (qf * qdec[1, h]).astype(BF16)], axis=1)
            st = fstate[h]
            states = jnp.concatenate([st, rall[n, h]], axis=0).astype(BF16)
            y = y + jnp.dot(qcat, states, preferred_element_type=F32)
            fstate[h] = st * chunk_decay(0, h) + kv_update(0, h)
            mu = jnp.mean(y, axis=-1, keepdims=True)
            yc = y - mu
            var = jnp.mean(yc * yc, axis=-1, keepdims=True)
            yn = yc * lax.rsqrt(var + GN_EPS) * gnw_ref[:, vs]
            gate = g_ref[0, :, vs].astype(F32)
            out_ref[0, :, vs] = (gate * jax.nn.sigmoid(gate) * yn).astype(out_ref.dtype)


def _retention(rq, rk, rv, rg, decay_f, decay_b, gn_w, b, s):
    c = RET_CHUNK
    nc = s // c
    lg = jnp.stack([jax.nn.log_sigmoid(decay_f.astype(F32)), jax.nn.log_sigmoid(decay_b.astype(F32))])
    qk_w, v_w = RET_QK_WIDTH, RET_V_WIDTH

    def both(bb, ph, n):
        return (bb, jnp.where(ph == 0, nc - 1 - n, n), 0)

    def fwd_only(bb, ph, n):
        return (bb, jnp.where(ph == 0, 0, n), 0)

    out = pl.pallas_call(
        functools.partial(_ret_body, chunk=c),
        grid=(b, 2, nc),
        in_specs=[pl.BlockSpec(memory_space=pltpu.SMEM),
                  pl.BlockSpec((1, c, qk_w), fwd_only),
                  pl.BlockSpec((1, c, qk_w), both),
                  pl.BlockSpec((1, c, v_w), both),
                  pl.BlockSpec((1, c, v_w), fwd_only),
                  pl.BlockSpec((1, v_w), lambda bb, ph, n: (0, 0))],
        out_specs=pl.BlockSpec((1, c, v_w), fwd_only),
        out_shape=jax.ShapeDtypeStruct((b, s, v_w), BF16),
        scratch_shapes=[pltpu.VMEM((RET_HEADS, c, c), F32),
                        pltpu.VMEM((2, RET_HEADS, c, RET_QK_DIM), F32),
                        pltpu.VMEM((2, RET_HEADS, c, RET_QK_DIM), F32),
                        pltpu.VMEM((RET_HEADS, RET_QK_DIM, RET_V_DIM), F32),
                        pltpu.VMEM((RET_HEADS, RET_QK_DIM, RET_V_DIM), F32),
                        pltpu.VMEM((nc, RET_HEADS, RET_QK_DIM, RET_V_DIM), F32)],
        compiler_params=_cparams(("arbitrary", "arbitrary", "arbitrary"), VMEM_LIMIT),
        name="retention",
    )(lg, rq.reshape(b, s, qk_w), rk.reshape(b, s, qk_w), rv.reshape(b, s, v_w), rg.reshape(b, s, v_w),
      gn_w.reshape(1, v_w))
    return out.reshape(b * s, v_w)


def _out_proj_body(attn_ref, ret_ref, x_ref, w_ref, out_ref):
    y = jnp.dot(attn_ref[...], w_ref[:ATTN_WIDTH], preferred_element_type=F32)
    y = y + jnp.dot(ret_ref[...], w_ref[ATTN_WIDTH:], preferred_element_type=F32)
    out_ref[...] = x_ref[...] + y


def _out_proj(attn, ret, x2d, w_out):
    n, d = x2d.shape
    tm = TOKEN_TILE
    row = lambda w: pl.BlockSpec((tm, w), lambda i: (i, 0))
    return pl.pallas_call(
        _out_proj_body,
        grid=(n // tm,),
        in_specs=[row(ATTN_WIDTH), row(RET_V_WIDTH), row(d), pl.BlockSpec(w_out.shape, lambda i: (0, 0))],
        out_specs=row(d),
        out_shape=jax.ShapeDtypeStruct((n, d), F32),
        compiler_params=_cparams(("parallel",), VMEM_LIMIT),
        name="out_proj",
    )(attn, ret, x2d, w_out.astype(BF16))


def _mem_kv_body(mem_ref, nw_ref, w_ref, kw_ref, k_out, v_out):
    d = mem_ref.shape[-1]
    hd = d // MEM_HEADS
    h = _rms(mem_ref[0], nw_ref[...]).astype(BF16)
    kv = jnp.dot(h, w_ref[...], preferred_element_type=F32)
    for i in range(MEM_HEADS):
        k_out[0, :, i * hd:(i + 1) * hd] = _rms(kv[:, i * hd:(i + 1) * hd], kw_ref[...]).astype(BF16)
    v_out[0] = kv[:, d:].astype(BF16)


def _mem_kv(mem, norm_w, w_mkv, k_norm_w):
    b, m, d = mem.shape
    return pl.pallas_call(
        _mem_kv_body,
        grid=(b,),
        in_specs=[pl.BlockSpec((1, m, d), lambda i: (i, 0, 0)),
                  pl.BlockSpec((1, d), lambda i: (0, 0)),
                  pl.BlockSpec((d, 2 * d), lambda i: (0, 0)),
                  pl.BlockSpec((1, d // MEM_HEADS), lambda i: (0, 0))],
        out_specs=[pl.BlockSpec((1, m, d), lambda i: (i, 0, 0))] * 2,
        out_shape=[jax.ShapeDtypeStruct((b, m, d), BF16)] * 2,
        compiler_params=_cparams(("parallel",), VMEM_LIMIT),
        name="mem_kv",
    )(mem, norm_w.reshape(1, d), w_mkv.astype(BF16), k_norm_w.reshape(1, -1))


def _mem_attn_body(x_ref, nw_ref, wq_ref, qw_ref, k_ref, v_ref, wo_ref, out_ref):
    d = x_ref.shape[-1]
    hd = d // MEM_HEADS
    x = x_ref[0]
    h = _rms(x, nw_ref[...]).astype(BF16)
    q = jnp.dot(h, wq_ref[...], preferred_element_type=F32)
    heads = []
    for i in range(MEM_HEADS):
        cs = slice(i * hd, (i + 1) * hd)
        qn = _rms(q[:, cs], qw_ref[...]).astype(BF16)
        s = lax.dot_general(qn, k_ref[0, :, cs], (((1,), (1,)), ((), ())), preferred_element_type=F32)
        e = jnp.exp(s - jnp.max(s, axis=-1, keepdims=True))
        o = jnp.dot(e.astype(BF16), v_ref[0, :, cs], preferred_element_type=F32)
        heads.append((o / jnp.sum(e, axis=-1, keepdims=True)).astype(BF16))
    o = jnp.concatenate(heads, axis=1)
    out_ref[0] = x + jnp.dot(o, wo_ref[...], preferred_element_type=F32)


def _mem_attn(x1, mk, mv, norm_w, w_mq, q_norm_w, w_mo):
    b, s, d = x1.shape
    m = mk.shape[1]
    tm = TOKEN_TILE
    hd = d // MEM_HEADS
    qw = (q_norm_w * (hd ** -0.5)).reshape(1, hd)
    tok = pl.BlockSpec((1, tm, d), lambda bb, i: (bb, i, 0))
    const = lambda r, c: pl.BlockSpec((r, c), lambda bb, i: (0, 0))
    mem = pl.BlockSpec((1, m, d), lambda bb, i: (bb, 0, 0))
    return pl.pallas_call(
        _mem_attn_body,
        grid=(b, s // tm),
        in_specs=[tok, const(1, d), const(d, d), const(1, hd), mem, mem, const(d, d)],
        out_specs=tok,
        out_shape=jax.ShapeDtypeStruct((b, s, d), F32),
        compiler_params=_cparams(("parallel", "parallel"), VMEM_LIMIT),
        name="mem_attn",
    )(x1, norm_w.reshape(1, d), w_mq.astype(BF16), qw, mk, mv, w_mo.astype(BF16))


def _router_body(x_ref, nw_ref, whi_ref, wlo_ref, h_out, gate_out, meta_out, cnt_out, carry):
    tm = x_ref.shape[0]
    i = pl.program_id(0)

    @pl.when(i == 0)
    def _():
        carry[...] = jnp.zeros_like(carry)

    h = _rms(x_ref[...], nw_ref[...])
    _store_row_tiles(h_out, h)
    hi, lo = _split_hi_lo(h)
    logits = (jnp.dot(hi, whi_ref[...], preferred_element_type=F32)
              + jnp.dot(lo, whi_ref[...], preferred_element_type=F32)
              + jnp.dot(hi, wlo_ref[...], preferred_element_type=F32))
    lane = lax.broadcasted_iota(I32, (tm, LANES), 1)

    def argmax_lanes(vals):
        top = jnp.max(vals, axis=-1, keepdims=True)
        idx = jnp.min(jnp.where(vals == top, lane, LANES), axis=-1, keepdims=True)
        return top, idx

    is_group = lane < N_GROUPS
    g_top, grp = argmax_lanes(jnp.where(is_group, logits, -jnp.inf))
    g_gate = 1.0 / jnp.sum(jnp.where(is_group, jnp.exp(logits - g_top), 0.0), axis=-1, keepdims=True)
    in_group = (lane >= N_GROUPS) & (lane < N_GROUPS + N_EXPERTS) & (
        ((lane - N_GROUPS) // EXPERTS_PER_GROUP) == grp)
    el = jnp.where(in_group, logits, -jnp.inf)
    t1, i1 = argmax_lanes(el)
    t2, i2 = argmax_lanes(jnp.where(lane == i1, -jnp.inf, el))
    z = jnp.exp(t2 - t1)
    g1 = g_gate / (1.0 + z)
    g2 = g_gate * z / (1.0 + z)
    e1 = i1 - N_GROUPS
    e2 = i2 - N_GROUPS

    oh1 = (lane == e1)
    oh2 = (lane == e2)
    oh1b = oh1.astype(F32).astype(BF16)
    oh2b = oh2.astype(F32).astype(BF16)
    rr = lax.broadcasted_iota(I32, (tm, tm), 0)
    cc = lax.broadcasted_iota(I32, (tm, tm), 1)
    below = (cc < rr).astype(F32).astype(BF16)
    pre1 = jnp.dot(below, oh1b, preferred_element_type=F32)
    pre2 = jnp.dot(below, oh2b, preferred_element_type=F32)
    cnt1 = jnp.sum(oh1.astype(F32), axis=0, keepdims=True)
    cnt2 = jnp.sum(oh2.astype(F32), axis=0, keepdims=True)
    base = carry[...]
    r1 = jnp.sum(jnp.where(oh1, pre1 + base, 0.0), axis=-1, keepdims=True)
    r2 = jnp.sum(jnp.where(oh2, pre2 + base + cnt1, 0.0), axis=-1, keepdims=True)
    total = base + cnt1 + cnt2
    carry[...] = total
    cnt_out[...] = total

    gate_out[...] = jnp.where(lane == 0, g1, jnp.where(lane == 1, g2, 0.0))
    meta_out[...] = jnp.where(lane == 0, e1, jnp.where(lane == 1, e2, jnp.where(
        lane == 2, r1.astype(I32), jnp.where(lane == 3, r2.astype(I32), 0))))


def _router(x2d, norm_w, w_group, w_router):
    n, d = x2d.shape
    tm = TOKEN_TILE
    w_all = jnp.concatenate([w_group, w_router.transpose(1, 0, 2).reshape(d, N_EXPERTS)], axis=1)
    w_all = jnp.pad(w_all, ((0, 0), (0, LANES - w_all.shape[1])))
    whi = w_all.astype(BF16)
    wlo = (w_all - whi.astype(F32)).astype(BF16)
    row = lambda w: pl.BlockSpec((tm, w), lambda i: (i, 0))
    const = lambda r, c: pl.BlockSpec((r, c), lambda i: (0, 0))
    return pl.pallas_call(
        _router_body,
        grid=(n // tm,),
        in_specs=[row(d), const(1, d), const(d, LANES), const(d, LANES)],
        out_specs=[pl.BlockSpec((tm * ROW_SLABS, LANES), lambda i: (i, 0)), row(LANES), row(LANES), const(1, LANES)],
        out_shape=[jax.ShapeDtypeStruct((n * ROW_SLABS, LANES), F32), jax.ShapeDtypeStruct((n, LANES), F32),
                   jax.ShapeDtypeStruct((n, LANES), I32), jax.ShapeDtypeStruct((1, LANES), F32)],
        scratch_shapes=[pltpu.VMEM((1, LANES), F32)],
        compiler_params=_cparams(("arbitrary",), VMEM_LIMIT),
        name="router",
    )(x2d, norm_w.reshape(1, d), whi, wlo)


def _row_copy(src_ref, src_row, dst_ref, dst_row, sem):
    return pltpu.make_async_copy(_tile_rows(src_ref, src_row), _tile_rows(dst_ref, dst_row), sem)


def _dispatch_body(cnt_ref, eid_ref, rank_ref, h_ref, xs_ref, dest_ref, pstart, zbuf, sems):
    i = pl.program_id(0)
    tt = h_ref.shape[0] // ROW_SLABS
    blk = EXPERT_BLOCK
    n_slots = xs_ref.shape[0] // ROW_SLABS

    def pad_copy(pend):
        return pltpu.make_async_copy(zbuf, _tile_rows(xs_ref, pend - blk, blk), sems.at[2])

    @pl.when(i == 0)
    def _():
        zbuf[...] = jnp.zeros_like(zbuf)

        def start(e, acc):
            pstart[e] = acc
            pend = acc + ((cnt_ref[e] + (blk - 1)) // blk) * blk

            @pl.when(pend > acc)
            def _():
                pad_copy(pend).start()
            return pend

        used = lax.fori_loop(0, N_EXPERTS, start, jnp.int32(0))

        def tail_start(j, carry):
            pad_copy((j + 1) * blk).start()
            return carry

        lax.fori_loop(used // blk, n_slots // blk, tail_start, 0)

        def finish(e, acc):
            pend = acc + ((cnt_ref[e] + (blk - 1)) // blk) * blk

            @pl.when(pend > acc)
            def _():
                pad_copy(pend).wait()
            return pend

        lax.fori_loop(0, N_EXPERTS, finish, jnp.int32(0))

        def tail_wait(j, carry):
            pad_copy((j + 1) * blk).wait()
            return carry

        lax.fori_loop(used // blk, n_slots // blk, tail_wait, 0)

    def tokens(g, carry):
        t0 = pl.multiple_of(g * ROW_DMA_UNROLL, ROW_DMA_UNROLL)
        for u in range(ROW_DMA_UNROLL):
            for k in range(2):
                a = 2 * (t0 + u) + k
                d = pstart[eid_ref[a]] + rank_ref[a]
                dest_ref[a] = d
                _row_copy(h_ref, t0 + u, xs_ref, d, sems.at[k]).start(priority=k)
        return carry

    lax.fori_loop(0, tt // ROW_DMA_UNROLL, tokens, 0)
    for k in range(2):
        pltpu.make_async_copy(h_ref, _tile_rows(xs_ref, 0, tt), sems.at[k]).wait()


def _dispatch(counts, eid, rank, h3, n_slots):
    n = h3.shape[0] // ROW_SLABS
    tt = DISPATCH_TILE
    smem_blk = pl.BlockSpec((2 * tt,), lambda i, cnt: (i,), memory_space=pltpu.SMEM)
    grid_spec = pltpu.PrefetchScalarGridSpec(
        num_scalar_prefetch=1,
        grid=(n // tt,),
        in_specs=[smem_blk, smem_blk, pl.BlockSpec((tt * ROW_SLABS, LANES), lambda i, cnt: (i, 0))],
        out_specs=[pl.BlockSpec(memory_space=pl.ANY), smem_blk],
        scratch_shapes=[pltpu.SMEM((N_EXPERTS,), I32), pltpu.VMEM((EXPERT_BLOCK * ROW_SLABS, LANES), F32),
                        pltpu.SemaphoreType.DMA((3,))],
    )
    return pl.pallas_call(
        _dispatch_body,
        grid_spec=grid_spec,
        out_shape=[jax.ShapeDtypeStruct((n_slots * ROW_SLABS, LANES), F32), jax.ShapeDtypeStruct((2 * n,), I32)],
        compiler_params=_cparams(("arbitrary",), VMEM_LIMIT),
        name="dispatch",
    )(counts, eid, rank, h3)


def _experts_body(be_ref, nu_ref, xs_ref, w1_ref, w3_ref, w2_ref, ys_ref, w1b, w3b, w2b):
    i = pl.program_id(0)
    n_used = nu_ref[0]

    @pl.when(i < n_used)
    def _():
        changed = (i == 0) | (be_ref[i] != be_ref[jnp.maximum(i - 1, 0)])

        @pl.when(changed)
        def _():
            w1b[...] = w1_ref[0].astype(BF16)
            w3b[...] = w3_ref[0].astype(BF16)
            w2b[...] = w2_ref[0].astype(BF16)

        x = _load_row_tiles(xs_ref).astype(BF16)
        a = jnp.dot(x, w1b[...], preferred_element_type=F32)
        u = jnp.dot(x, w3b[...], preferred_element_type=F32)
        mid = (a * jax.nn.sigmoid(a) * u).astype(BF16)
        _store_row_tiles(ys_ref, jnp.dot(mid, w2b[...], preferred_element_type=F32))

    @pl.when(i >= n_used)
    def _():
        ys_ref[...] = jnp.zeros_like(ys_ref)


def _experts(block_expert, n_used, xs, w1, w3, w2):
    _, d, ff = w1.shape
    assert d == ROW_SLABS * LANES
    blk = EXPERT_BLOCK
    n_blocks = xs.shape[0] // (blk * ROW_SLABS)

    def slot_map(i, be, nu):
        return (jnp.minimum(i, nu[0] - 1), 0)

    def w_map(i, be, nu):
        return (be[jnp.minimum(i, nu[0] - 1)], 0, 0)

    grid_spec = pltpu.PrefetchScalarGridSpec(
        num_scalar_prefetch=2,
        grid=(n_blocks,),
        in_specs=[pl.BlockSpec((blk * ROW_SLABS, LANES), slot_map),
                  pl.BlockSpec((1, d, ff), w_map), pl.BlockSpec((1, d, ff), w_map),
                  pl.BlockSpec((1, ff, d), w_map)],
        out_specs=pl.BlockSpec((blk * ROW_SLABS, LANES), lambda i, be, nu: (i, 0)),
        scratch_shapes=[pltpu.VMEM((d, ff), BF16), pltpu.VMEM((d, ff), BF16), pltpu.VMEM((ff, d), BF16)],
    )
    return pl.pallas_call(
        _experts_body,
        grid_spec=grid_spec,
        out_shape=jax.ShapeDtypeStruct(xs.shape, F32),
        compiler_params=_cparams(("arbitrary",), VMEM_LIMIT),
        name="experts",
    )(block_expert, n_used, xs, w1, w3, w2)


def _combine_body(dest_ref, gate_ref, x_ref, ys_ref, out_ref, buf0, buf1, sems):
    tt = x_ref.shape[0]
    bufs = (buf0, buf1)

    def tokens(g, carry):
        t0 = pl.multiple_of(g * ROW_DMA_UNROLL, ROW_DMA_UNROLL)
        for u in range(ROW_DMA_UNROLL):
            for k in range(2):
                _row_copy(ys_ref, dest_ref[2 * (t0 + u) + k], bufs[k], t0 + u, sems.at[k]).start(priority=k)
        return carry

    lax.fori_loop(0, tt // ROW_DMA_UNROLL, tokens, 0)
    for k in range(2):
        pltpu.make_async_copy(_tile_rows(ys_ref, 0, tt), bufs[k], sems.at[k]).wait()
    g = gate_ref[...]
    out_ref[...] = x_ref[...] + g[:, 0:1] * _load_row_tiles(buf0) + g[:, 1:2] * _load_row_tiles(buf1)


def _combine(dest, gates, x2d, ys):
    n, d = x2d.shape
    tt = COMBINE_TILE
    row = lambda w: pl.BlockSpec((tt, w), lambda i: (i, 0))
    return pl.pallas_call(
        _combine_body,
        grid=(n // tt,),
        in_specs=[pl.BlockSpec((2 * tt,), lambda i: (i,), memory_space=pltpu.SMEM), row(LANES), row(d),
                  pl.BlockSpec(memory_space=pl.ANY)],
        out_specs=row(d),
        out_shape=jax.ShapeDtypeStruct((n, d), F32),
        scratch_shapes=[pltpu.VMEM((tt * ROW_SLABS, LANES), F32), pltpu.VMEM((tt * ROW_SLABS, LANES), F32),
                        pltpu.SemaphoreType.DMA((2,))],
        compiler_params=_cparams(("arbitrary",), VMEM_LIMIT),
        name="combine",
    )(dest, gates, x2d, ys)


def _moe(x2d, norm_w, w_group, w_router, w1, w3, w2):
    n, d = x2d.shape
    blk = EXPERT_BLOCK
    h3, gates, meta, cnt = _router(x2d, norm_w, w_group, w_router)
    counts = cnt[0, :N_EXPERTS].astype(I32)
    eid = meta[:, 0:2].reshape(-1)
    rank = meta[:, 2:4].reshape(-1)
    n_slots = 2 * n + N_EXPERTS * blk
    n_blocks = n_slots // blk
    pends = jnp.cumsum((counts + blk - 1) // blk * blk)
    block_start = jnp.arange(n_blocks, dtype=I32) * blk
    block_expert = jnp.minimum(jnp.sum((pends[None, :] <= block_start[:, None]).astype(I32), axis=1),
                               N_EXPERTS - 1)
    n_used = (pends[-1:] // blk).astype(I32)
    xs, dest = _dispatch(counts, eid, rank, h3, n_slots)
    ys = _experts(block_expert, n_used, xs, w1, w3, w2)
    return _combine(dest, gates, x2d, ys)


def _layer(x, mem, norm_mix_w, w_in, attn_q_norm_w, attn_k_norm_w, ret_decay_f, ret_decay_b, ret_gn_w, w_out,
           norm_mem_w, norm_memkv_w, w_mq, w_mkv, mem_q_norm_w, mem_k_norm_w, w_mo,
           norm_moe_w, w_group, w_router, w_exp_gate, w_exp_up, w_exp_down):
    b, s, d = x.shape
    x2d = x.reshape(b * s, d)
    q, k, v, q_pl, k_pl, v_pl, rq, rk, rv, rg = _in_proj(x2d, norm_mix_w, w_in, attn_q_norm_w, attn_k_norm_w)
    attn = _attention(q, k, v, q_pl, k_pl, v_pl, b, s)
    ret = _retention(rq, rk, rv, rg, ret_decay_f, ret_decay_b, ret_gn_w, b, s)
    x1 = _out_proj(attn, ret, x2d, w_out)
    mk, mv = _mem_kv(mem, norm_memkv_w, w_mkv, mem_k_norm_w)
    x2 = _mem_attn(x1.reshape(b, s, d), mk, mv, norm_mem_w, w_mq, mem_q_norm_w, w_mo)
    x3 = _moe(x2.reshape(b * s, d), norm_moe_w, w_group, w_router, w_exp_gate, w_exp_up, w_exp_down)
    return x3.reshape(b, s, d)


def kernel(x, mem, norm_mix_w, w_in, attn_q_norm_w, attn_k_norm_w, ret_decay_f, ret_decay_b, ret_gn_w, w_out,
           norm_mem_w, norm_memkv_w, w_mq, w_mkv, mem_q_norm_w, mem_k_norm_w, w_mo, norm_moe_w, w_group,
           w_router, w_exp_gate, w_exp_up, w_exp_down):
    depth = norm_mix_w.shape[0]
    for l in range(depth):
        x = _layer(x, mem, norm_mix_w[l], w_in[l], attn_q_norm_w[l], attn_k_norm_w[l], ret_decay_f[l],
                   ret_decay_b[l], ret_gn_w[l], w_out[l], norm_mem_w[l], norm_memkv_w[l], w_mq[l], w_mkv[l],
                   mem_q_norm_w[l], mem_k_norm_w[l], w_mo[l], norm_moe_w[l], w_group[l], w_router[l],
                   w_exp_gate[l], w_exp_up[l], w_exp_down[l])
    return x
```

```python
import functools

import numpy as np
import jax
import jax.numpy as jnp
from jax import lax
from jax.experimental import pallas as pl
from jax.experimental.pallas import tpu as pltpu

F32 = jnp.float32
BF16 = jnp.bfloat16
I32 = jnp.int32

NORM_EPS = 1e-6
GN_EPS = 1e-5
NEG_INF = -1e30

ATTN_HEADS = 8
ATTN_HEAD_DIM = 64
ATTN_WIDTH = ATTN_HEADS * ATTN_HEAD_DIM
ATTN_HALF = 64
PLANES = 16
ATTN_TILE = PLANES * ATTN_HALF
RET_HEADS = 4
RET_QK_DIM = 64
RET_V_DIM = 128
RET_QK_WIDTH = RET_HEADS * RET_QK_DIM
RET_V_WIDTH = RET_HEADS * RET_V_DIM
MEM_HEADS = 4
N_GROUPS = 4
EXPERTS_PER_GROUP = 8
N_EXPERTS = N_GROUPS * EXPERTS_PER_GROUP

LANES = 128
TOKEN_TILE = 512
ATTN_BLOCKS_PER_STEP = 4
LOG2_E = 1.4426950408889634
RET_CHUNK = 256
EXPERT_BLOCK = 512
DISPATCH_TILE = 512
COMBINE_TILE = 256
ROW_DMA_UNROLL = 16
VMEM_LIMIT = 56 * 1024 * 1024


def _cparams(sem, vmem=None):
    return pltpu.CompilerParams(dimension_semantics=sem, vmem_limit_bytes=vmem)


def _split_hi_lo(x):
    hi = x.astype(BF16)
    lo = (x - hi.astype(F32)).astype(BF16)
    return hi, lo


def _rms(x, w):
    ms = jnp.mean(x * x, axis=-1, keepdims=True)
    return x * lax.rsqrt(ms + NORM_EPS) * w


ROW_SLABS = 8


def _store_row_tiles(ref, val):
    rows = val.shape[0]
    for j in range(ROW_SLABS):
        ref[pl.ds(j, rows, stride=ROW_SLABS), :] = val[:, j * LANES:(j + 1) * LANES]


def _load_row_tiles(ref):
    rows = ref.shape[0] // ROW_SLABS
    return jnp.concatenate([ref[pl.ds(j, rows, stride=ROW_SLABS), :] for j in range(ROW_SLABS)], axis=1)


def _tile_rows(ref, row, count=1):
    return ref.at[pl.ds(pl.multiple_of(row * ROW_SLABS, ROW_SLABS), count * ROW_SLABS)]


def _in_proj_body(x_ref, nw_ref, w_ref, qw_ref, kw_ref, g_ref,
                  q_out, k_out, v_out, qp_out, kp_out, vp_out, rq_out, rk_out, rv_out, rg_out, slabs):
    tm = x_ref.shape[0]
    h = _rms(x_ref[...], nw_ref[...]).astype(BF16)

    def emit(val, nat_out, plane_out):
        nat_out[...] = val.astype(BF16)
        for j in range(ATTN_WIDTH // LANES):
            slabs[j] = val[:, j * LANES:(j + 1) * LANES]
        for c in range(PLANES):
            for j in range(ATTN_WIDTH // LANES):
                rows = slabs[j, pl.ds(c, tm // PLANES, stride=PLANES), :]
                plane_out[0, c, :, j * LANES:(j + 1) * LANES] = rows.astype(BF16)

    def proj(a, b):
        return jnp.dot(h, w_ref[:, a:b], preferred_element_type=F32)

    def head_norm(p, w):
        hi, lo = _split_hi_lo(p * p)
        s = (jnp.dot(hi, g_ref[...], preferred_element_type=F32)
             + jnp.dot(lo, g_ref[...], preferred_element_type=F32))
        return p * lax.rsqrt(s * (1.0 / ATTN_HEAD_DIM) + NORM_EPS) * w

    a = ATTN_WIDTH
    emit(head_norm(proj(0, a), qw_ref[...]), q_out, qp_out)
    emit(head_norm(proj(a, 2 * a), kw_ref[...]), k_out, kp_out)
    emit(proj(2 * a, 3 * a), v_out, vp_out)
    c = 3 * a
    rq_out[...] = proj(c, c + RET_QK_WIDTH).astype(BF16)
    rk_out[...] = proj(c + RET_QK_WIDTH, c + 2 * RET_QK_WIDTH).astype(BF16)
    c += 2 * RET_QK_WIDTH
    rv_out[...] = proj(c, c + RET_V_WIDTH).astype(BF16)
    rg_out[...] = proj(c + RET_V_WIDTH, c + 2 * RET_V_WIDTH).astype(BF16)


def _in_proj(x2d, norm_w, w_in, q_norm_w, k_norm_w):
    n, d = x2d.shape
    cols = w_in.shape[1]
    tm = TOKEN_TILE
    qw = (jnp.tile(q_norm_w, ATTN_HEADS) * (ATTN_HEAD_DIM ** -0.5 * LOG2_E)).reshape(1, ATTN_WIDTH)
    kw = jnp.tile(k_norm_w, ATTN_HEADS).reshape(1, ATTN_WIDTH)
    head_of = np.arange(ATTN_WIDTH) // ATTN_HEAD_DIM
    gmat = jnp.asarray(head_of[:, None] == head_of[None, :], dtype=BF16)
    row = lambda w: pl.BlockSpec((tm, w), lambda i: (i, 0))
    full = lambda r, c: pl.BlockSpec((r, c), lambda i: (0, 0))
    per_tile = ATTN_TILE // tm
    plane = pl.BlockSpec((1, PLANES, tm // PLANES, ATTN_WIDTH), lambda i: (i // per_tile, 0, i % per_tile, 0))
    plane_shape = jax.ShapeDtypeStruct((n // ATTN_TILE, PLANES, ATTN_TILE // PLANES, ATTN_WIDTH), BF16)
    ret_widths = (RET_QK_WIDTH, RET_QK_WIDTH, RET_V_WIDTH, RET_V_WIDTH)
    return pl.pallas_call(
        _in_proj_body,
        grid=(n // tm,),
        in_specs=[row(d), full(1, d), full(d, cols), full(1, ATTN_WIDTH), full(1, ATTN_WIDTH),
                  full(ATTN_WIDTH, ATTN_WIDTH)],
        out_specs=[row(ATTN_WIDTH)] * 3 + [plane] * 3 + [row(w) for w in ret_widths],
        out_shape=([jax.ShapeDtypeStruct((n, ATTN_WIDTH), BF16)] * 3 + [plane_shape] * 3
                   + [jax.ShapeDtypeStruct((n, w), BF16) for w in ret_widths]),
        scratch_shapes=[pltpu.VMEM((ATTN_WIDTH // LANES, tm, LANES), F32)],
        compiler_params=_cparams(("parallel",), VMEM_LIMIT),
        name="in_proj",
    )(x2d, norm_w.reshape(1, d), w_in.astype(BF16), qw, kw, gmat)


def _attn_bias(dilation, interleave=1):
    half = ATTN_HALF
    idx = np.arange(half)
    sub = (idx % (half // interleave)) * interleave + idx // (half // interleave)
    qi = sub[:, None]
    kc = (np.arange(3)[:, None] * half + sub[None, :]).reshape(1, -1)
    seg = np.repeat(np.arange(3), half)[None, :]
    delta = kc - half - qi
    band = np.abs(delta) <= half
    slopes = np.exp2(-8.0 * np.arange(1, ATTN_HEADS + 1) / ATTN_HEADS)
    valid = [band & (seg >= 1), band, band & (seg <= 1)]
    out = np.empty((ATTN_HEADS // 2, 3, 2 * half, 3 * half), np.float32)
    for p in range(ATTN_HEADS // 2):
        for v in range(3):
            for s in range(2):
                b = -slopes[2 * p + s] * LOG2_E * (dilation * np.abs(delta)).astype(np.float32)
                out[p, v, s * half:(s + 1) * half] = np.where(valid[v], b, NEG_INF)
    return jnp.asarray(out.reshape(-1, 2 * half, 3 * half))


def _attend(chains):
    half = ATTN_HALF
    lo_lanes = lax.broadcasted_iota(I32, (half, LANES), 1) < ATTN_HEAD_DIM
    ones = jnp.ones((3 * half, LANES), BF16)
    scores = []
    for q, k, v, bias in chains:
        zero = jnp.zeros_like(q)
        qs = jnp.concatenate([jnp.where(lo_lanes, q, zero), jnp.where(lo_lanes, zero, q)], axis=0)
        scores.append(lax.dot_general(qs, k, (((1,), (1,)), ((), ())), preferred_element_type=F32))
    probs, maxes = [], []
    for (q, k, v, bias), s in zip(chains, scores):
        s = s + bias
        m = jnp.max(s, axis=-1, keepdims=True)
        probs.append(jnp.exp2(s - m).astype(BF16))
        maxes.append(m)
    results = []
    for (q, k, v, bias), e in zip(chains, probs):
        vext = jnp.concatenate([v, ones], axis=1)
        results.append(jnp.dot(e, vext, preferred_element_type=F32))
    outs = []
    for r, m in zip(results, maxes):
        acc = jnp.where(lo_lanes, r[:half, :LANES], r[half:, :LANES])
        denom = jnp.where(lo_lanes, r[:half, LANES:], r[half:, LANES:])
        mm = jnp.where(lo_lanes, m[:half], m[half:])
        outs.append((acc / denom, mm + jnp.log2(denom)))
    return outs


def _merge(oa, la, ob, lb):
    m = jnp.maximum(la, lb)
    ea = jnp.exp2(la - m)
    eb = jnp.exp2(lb - m)
    den = ea + eb
    return (ea * oa + eb * ob) / den, m + jnp.log2(den)


def _attn_body(q_ref, kp_ref, km_ref, kn_ref, vp_ref, vm_ref, vn_ref,
               qpl_ref, kplp_ref, kplm_ref, kpln_ref, vplp_ref, vplm_ref, vpln_ref,
               b1_ref, b4_ref, b16_ref, o_ref, kcat, vcat, o_far, l_far):
    half = ATTN_HALF
    t = pl.program_id(1)
    n_tiles = pl.num_programs(1)
    tile = q_ref.shape[1]
    pairs = ATTN_HEADS // 2
    piece = half // 4

    kcat[0:half] = kp_ref[0]
    kcat[half:half + tile] = km_ref[0]
    kcat[half + tile:] = kn_ref[0]
    vcat[0:half] = vp_ref[0]
    vcat[half:half + tile] = vm_ref[0]
    vcat[half + tile:] = vn_ref[0]

    def edge(first, last):
        return jnp.where(first, 0, jnp.where(last, 2, 1))

    def far_step(r, carry):
        v16 = edge(t == 0, t == n_tiles - 1)
        for pr in range(pairs):
            cs = slice(pr * LANES, (pr + 1) * LANES)
            chains = []
            for m in range(4):
                c = r + 4 * m
                k3 = jnp.concatenate([kplp_ref[0, c, :, cs], kplm_ref[0, c, :, cs], kpln_ref[0, c, :, cs]], axis=0)
                v3 = jnp.concatenate([vplp_ref[0, c, :, cs], vplm_ref[0, c, :, cs], vpln_ref[0, c, :, cs]], axis=0)
                chains.append((qpl_ref[0, c, :, cs], k3, v3, b16_ref[pr * 3 + v16]))
            for nb in range(4):
                def rows(main, before, after, a0):
                    ref, lo = (before, a0 + half) if a0 < 0 else (after, a0 - half) if a0 >= half else (main, a0)
                    return [ref[0, r + 4 * m, lo:lo + piece, cs] for m in range(4)]
                a0 = nb * piece
                q4 = jnp.concatenate(rows(qpl_ref, None, None, a0), axis=0)
                k4 = jnp.concatenate(sum((rows(kplm_ref, kplp_ref, kpln_ref, a0 + d) for d in (-piece, 0, piece)), []),
                                     axis=0)
                v4 = jnp.concatenate(sum((rows(vplm_ref, vplp_ref, vpln_ref, a0 + d) for d in (-piece, 0, piece)), []),
                                     axis=0)
                v4e = edge((t == 0) & (nb == 0), (t == n_tiles - 1) & (nb == 3))
                chains.append((q4, k4, v4, b4_ref[pr * 3 + v4e]))
            res = _attend(chains)
            for m in range(4):
                o16, l16 = res[m]
                o_rows, l_rows = [], []
                for nb in range(4):
                    o4, l4 = res[4 + nb]
                    sl = slice(nb * piece, (nb + 1) * piece)
                    s4 = slice(m * piece, (m + 1) * piece)
                    om, lm = _merge(o16[sl], l16[sl], o4[s4], l4[s4])
                    o_rows.append(om)
                    l_rows.append(lm)
                dst = pl.ds(r + 4 * m, half, stride=PLANES)
                o_far[pr, dst, :] = jnp.concatenate(o_rows, axis=0)
                l_far[pr, dst, :] = jnp.concatenate(l_rows, axis=0)
        return carry

    lax.fori_loop(0, 4, far_step, 0)

    unroll = ATTN_BLOCKS_PER_STEP
    n_blocks = n_tiles * (tile // half)

    def near_step(it, carry):
        chains, where = [], []
        for u in range(unroll):
            jb = it * unroll + u
            r0 = pl.multiple_of(jb * half, half)
            gb = t * (tile // half) + jb
            variant = edge(gb == 0, gb == n_blocks - 1)
            for pr in range(pairs):
                cs = slice(pr * LANES, (pr + 1) * LANES)
                chains.append((q_ref[0, pl.ds(r0, half), cs], kcat[pl.ds(r0, 3 * half), cs],
                               vcat[pl.ds(r0, 3 * half), cs], b1_ref[pr * 3 + variant]))
                where.append((r0, pr, cs))
        for (r0, pr, cs), (o1, l1) in zip(where, _attend(chains)):
            o, _ = _merge(o1, l1, o_far[pr, pl.ds(r0, half), :], l_far[pr, pl.ds(r0, half), :])
            o_ref[0, pl.ds(r0, half), cs] = o.astype(o_ref.dtype)
        return carry

    lax.fori_loop(0, tile // half // unroll, near_step, 0)


def _attention(q, k, v, q_pl, k_pl, v_pl, b, s):
    w = ATTN_WIDTH
    half = ATTN_HALF
    tile = ATTN_TILE
    n_tiles = s // tile
    hb = tile // half
    last = s // half - 1
    main = pl.BlockSpec((1, tile, w), lambda bb, t: (bb, t, 0))
    prev = pl.BlockSpec((1, half, w), lambda bb, t: (bb, jnp.maximum(t * hb - 1, 0), 0))
    nxt = pl.BlockSpec((1, half, w), lambda bb, t: (bb, jnp.minimum((t + 1) * hb, last), 0))
    pshape = (1, PLANES, tile // PLANES, w)
    pl_main = pl.BlockSpec(pshape, lambda bb, t: (bb * n_tiles + t, 0, 0, 0))
    pl_prev = pl.BlockSpec(pshape, lambda bb, t: (bb * n_tiles + jnp.maximum(t - 1, 0), 0, 0, 0))
    pl_next = pl.BlockSpec(pshape, lambda bb, t: (bb * n_tiles + jnp.minimum(t + 1, n_tiles - 1), 0, 0, 0))
    biases = [_attn_bias(1), _attn_bias(4, interleave=4), _attn_bias(16)]
    bias_spec = pl.BlockSpec(biases[0].shape, lambda bb, t: (0, 0, 0))
    nat = lambda a: a.reshape(b, s, w)
    out = pl.pallas_call(
        _attn_body,
        grid=(b, n_tiles),
        in_specs=[main, prev, main, nxt, prev, main, nxt,
                  pl_main, pl_prev, pl_main, pl_next, pl_prev, pl_main, pl_next,
                  bias_spec, bias_spec, bias_spec],
        out_specs=main,
        out_shape=jax.ShapeDtypeStruct((b, s, w), BF16),
        scratch_shapes=[pltpu.VMEM((tile + 2 * half, w), BF16), pltpu.VMEM((tile + 2 * half, w), BF16),
                        pltpu.VMEM((ATTN_HEADS // 2, tile, LANES), F32),
                        pltpu.VMEM((ATTN_HEADS // 2, tile, LANES), F32)],
        compiler_params=_cparams(("parallel", "parallel"), VMEM_LIMIT),
        name="attention",
    )(nat(q), nat(k), nat(k), nat(k), nat(v), nat(v), nat(v),
      q_pl, k_pl, k_pl, k_pl, v_pl, v_pl, v_pl, *biases)
    return out.reshape(b * s, w)


def _ret_body(lg_ref, q_ref, k_ref, v_ref, g_ref, gnw_ref, out_ref,
              dmat, qdec, kdec, fstate, rstate, rall, *, chunk):
    c = chunk
    b = pl.program_id(0)
    ph = pl.program_id(1)
    n = pl.program_id(2)
    n_chunks = pl.num_programs(2)
    k_scale = RET_QK_DIM ** -0.5

    @pl.when((b == 0) & (ph == 0) & (n == 0))
    def _init_tables():
        ii = lax.broadcasted_iota(I32, (c, c), 0)
        jj = lax.broadcasted_iota(I32, (c, c), 1)
        fwd = (ii - jj).astype(F32)
        ri = lax.broadcasted_iota(I32, (c, RET_QK_DIM), 0).astype(F32)
        for h in range(RET_HEADS):
            lf = lg_ref[0, h]
            lb = lg_ref[1, h]
            dmat[h] = jnp.where(ii >= jj, jnp.exp(lf * fwd), jnp.exp(-lb * fwd)) * k_scale
            qdec[0, h] = jnp.exp(lf * (ri + 1.0))
            qdec[1, h] = jnp.exp(lb * (c - ri))
            kdec[0, h] = jnp.exp(lf * (c - 1.0 - ri)) * k_scale
            kdec[1, h] = jnp.exp(lb * ri) * k_scale

    def chunk_decay(direction, h):
        return jnp.exp(jnp.full((RET_QK_DIM, RET_V_DIM), lg_ref[direction, h] * c, F32))

    def kv_update(direction, h):
        kh = k_ref[0, :, h * RET_QK_DIM:(h + 1) * RET_QK_DIM].astype(F32)
        ks = (kh * kdec[direction, h]).astype(BF16)
        vh = v_ref[0, :, h * RET_V_DIM:(h + 1) * RET_V_DIM]
        return lax.dot_general(ks, vh, (((0,), (0,)), ((), ())), preferred_element_type=F32)

    @pl.when(ph == 0)
    def _right_to_left():
        @pl.when(n == 0)
        def _():
            rstate[...] = jnp.zeros_like(rstate)

        ci = n_chunks - 1 - n
        for h in range(RET_HEADS):
            st = rstate[h]
            rall[ci, h] = st
            rstate[h] = st * chunk_decay(1, h) + kv_update(1, h)

    @pl.when(ph == 1)
    def _left_to_right():
        @pl.when(n == 0)
        def _():
            fstate[...] = jnp.zeros_like(fstate)

        for h in range(RET_HEADS):
            qs = slice(h * RET_QK_DIM, (h + 1) * RET_QK_DIM)
            vs = slice(h * RET_V_DIM, (h + 1) * RET_V_DIM)
            qh = q_ref[0, :, qs]
            kh = k_ref[0, :, qs]
            vh = v_ref[0, :, vs]
            s = lax.dot_general(qh, kh, (((1,), (1,)), ((), ())), preferred_element_type=F32) * dmat[h]
            y = jnp.dot(s.astype(BF16), vh, preferred_element_type=F32)
            qf = qh.astype(F32)
            qcat = jnp.concatenate([(qf * qdec[0, h]).astype(BF16), (qf * qdec[1, h]).astype(BF16)], axis=1)
            st = fstate[h]
            states = jnp.concatenate([st, rall[n, h]], axis=0).astype(BF16)
            y = y + jnp.dot(qcat, states, preferred_element_type=F32)
            fstate[h] = st * chunk_decay(0, h) + kv_update(0, h)
            mu = jnp.mean(y, axis=-1, keepdims=True)
            yc = y - mu
            var = jnp.mean(yc * yc, axis=-1, keepdims=True)
            yn = yc * lax.rsqrt(var + GN_EPS) * gnw_ref[:, vs]
            gate = g_ref[0, :, vs].astype(F32)
            out_ref[0, :, vs] = (gate * jax.nn.sigmoid(gate) * yn).astype(out_ref.dtype)


def _retention(rq, rk, rv, rg, decay_f, decay_b, gn_w, b, s):
    c = RET_CHUNK
    nc = s // c
    lg = jnp.stack([jax.nn.log_sigmoid(decay_f.astype(F32)), jax.nn.log_sigmoid(decay_b.astype(F32))])
    qk_w, v_w = RET_QK_WIDTH, RET_V_WIDTH

    def both(bb, ph, n):
        return (bb, jnp.where(ph == 0, nc - 1 - n, n), 0)

    def fwd_only(bb, ph, n):
        return (bb, jnp.where(ph == 0, 0, n), 0)

    out = pl.pallas_call(
        functools.partial(_ret_body, chunk=c),
        grid=(b, 2, nc),
        in_specs=[pl.BlockSpec(memory_space=pltpu.SMEM),
                  pl.BlockSpec((1, c, qk_w), fwd_only),
                  pl.BlockSpec((1, c, qk_w), both),
                  pl.BlockSpec((1, c, v_w), both),
                  pl.BlockSpec((1, c, v_w), fwd_only),
                  pl.BlockSpec((1, v_w), lambda bb, ph, n: (0, 0))],
        out_specs=pl.BlockSpec((1, c, v_w), fwd_only),
        out_shape=jax.ShapeDtypeStruct((b, s, v_w), BF16),
        scratch_shapes=[pltpu.VMEM((RET_HEADS, c, c), F32),
                        pltpu.VMEM((2, RET_HEADS, c, RET_QK_DIM), F32),
                        pltpu.VMEM((2, RET_HEADS, c, RET_QK_DIM), F32),
                        pltpu.VMEM((RET_HEADS, RET_QK_DIM, RET_V_DIM), F32),
                        pltpu.VMEM((RET_HEADS, RET_QK_DIM, RET_V_DIM), F32),
                        pltpu.VMEM((nc, RET_HEADS, RET_QK_DIM, RET_V_DIM), F32)],
        compiler_params=_cparams(("arbitrary", "arbitrary", "arbitrary"), VMEM_LIMIT),
        name="retention",
    )(lg, rq.reshape(b, s, qk_w), rk.reshape(b, s, qk_w), rv.reshape(b, s, v_w), rg.reshape(b, s, v_w),
      gn_w.reshape(1, v_w))
    return out.reshape(b * s, v_w)


def _out_proj_body(attn_ref, ret_ref, x_ref, w_ref, out_ref):
    y = jnp.dot(attn_ref[...], w_ref[:ATTN_WIDTH], preferred_element_type=F32)
    y = y + jnp.dot(ret_ref[...], w_ref[ATTN_WIDTH:], preferred_element_type=F32)
    out_ref[...] = x_ref[...] + y


def _out_proj(attn, ret, x2d, w_out):
    n, d = x2d.shape
    tm = TOKEN_TILE
    row = lambda w: pl.BlockSpec((tm, w), lambda i: (i, 0))
    return pl.pallas_call(
        _out_proj_body,
        grid=(n // tm,),
        in_specs=[row(ATTN_WIDTH), row(RET_V_WIDTH), row(d), pl.BlockSpec(w_out.shape, lambda i: (0, 0))],
        out_specs=row(d),
        out_shape=jax.ShapeDtypeStruct((n, d), F32),
        compiler_params=_cparams(("parallel",), VMEM_LIMIT),
        name="out_proj",
    )(attn, ret, x2d, w_out.astype(BF16))


def _mem_kv_body(mem_ref, nw_ref, w_ref, kw_ref, k_out, v_out):
    d = mem_ref.shape[-1]
    hd = d // MEM_HEADS
    h = _rms(mem_ref[0], nw_ref[...]).astype(BF16)
    kv = jnp.dot(h, w_ref[...], preferred_element_type=F32)
    for i in range(MEM_HEADS):
        k_out[0, :, i * hd:(i + 1) * hd] = _rms(kv[:, i * hd:(i + 1) * hd], kw_ref[...]).astype(BF16)
    v_out[0] = kv[:, d:].astype(BF16)


def _mem_kv(mem, norm_w, w_mkv, k_norm_w):
    b, m, d = mem.shape
    return pl.pallas_call(
        _mem_kv_body,
        grid=(b,),
        in_specs=[pl.BlockSpec((1, m, d), lambda i: (i, 0, 0)),
                  pl.BlockSpec((1, d), lambda i: (0, 0)),
                  pl.BlockSpec((d, 2 * d), lambda i: (0, 0)),
                  pl.BlockSpec((1, d // MEM_HEADS), lambda i: (0, 0))],
        out_specs=[pl.BlockSpec((1, m, d), lambda i: (i, 0, 0))] * 2,
        out_shape=[jax.ShapeDtypeStruct((b, m, d), BF16)] * 2,
        compiler_params=_cparams(("parallel",), VMEM_LIMIT),
        name="mem_kv",
    )(mem, norm_w.reshape(1, d), w_mkv.astype(BF16), k_norm_w.reshape(1, -1))


def _mem_attn_body(x_ref, nw_ref, wq_ref, qw_ref, k_ref, v_ref, wo_ref, out_ref):
    d = x_ref.shape[-1]
    hd = d // MEM_HEADS
    x = x_ref[0]
    h = _rms(x, nw_ref[...]).astype(BF16)
    q = jnp.dot(h, wq_ref[...], preferred_element_type=F32)
    heads = []
    for i in range(MEM_HEADS):
        cs = slice(i * hd, (i + 1) * hd)
        qn = _rms(q[:, cs], qw_ref[...]).astype(BF16)
        s = lax.dot_general(qn, k_ref[0, :, cs], (((1,), (1,)), ((), ())), preferred_element_type=F32)
        e = jnp.exp(s - jnp.max(s, axis=-1, keepdims=True))
        o = jnp.dot(e.astype(BF16), v_ref[0, :, cs], preferred_element_type=F32)
        heads.append((o / jnp.sum(e, axis=-1, keepdims=True)).astype(BF16))
    o = jnp.concatenate(heads, axis=1)
    out_ref[0] = x + jnp.dot(o, wo_ref[...], preferred_element_type=F32)


def _mem_attn(x1, mk, mv, norm_w, w_mq, q_norm_w, w_mo):
    b, s, d = x1.shape
    m = mk.shape[1]
    tm = TOKEN_TILE
    hd = d // MEM_HEADS
    qw = (q_norm_w * (hd ** -0.5)).reshape(1, hd)
    tok = pl.BlockSpec((1, tm, d), lambda bb, i: (bb, i, 0))
    const = lambda r, c: pl.BlockSpec((r, c), lambda bb, i: (0, 0))
    mem = pl.BlockSpec((1, m, d), lambda bb, i: (bb, 0, 0))
    return pl.pallas_call(
        _mem_attn_body,
        grid=(b, s // tm),
        in_specs=[tok, const(1, d), const(d, d), const(1, hd), mem, mem, const(d, d)],
        out_specs=tok,
        out_shape=jax.ShapeDtypeStruct((b, s, d), F32),
        compiler_params=_cparams(("parallel", "parallel"), VMEM_LIMIT),
        name="mem_attn",
    )(x1, norm_w.reshape(1, d), w_mq.astype(BF16), qw, mk, mv, w_mo.astype(BF16))


def _router_body(x_ref, nw_ref, whi_ref, wlo_ref, h_out, gate_out, meta_out, cnt_out, carry, below):
    tm = x_ref.shape[0]
    i = pl.program_id(0)

    @pl.when(i == 0)
    def _():
        carry[...] = jnp.zeros_like(carry)
        rr = lax.broadcasted_iota(I32, (tm, tm), 0)
        cc = lax.broadcasted_iota(I32, (tm, tm), 1)
        below[...] = (cc < rr).astype(F32).astype(BF16)

    h = _rms(x_ref[...], nw_ref[...])
    _store_row_tiles(h_out, h)
    hi, lo = _split_hi_lo(h)
    logits = (jnp.dot(hi, whi_ref[...], preferred_element_type=F32)
              + jnp.dot(lo, whi_ref[...], preferred_element_type=F32)
              + jnp.dot(hi, wlo_ref[...], preferred_element_type=F32))
    lane = lax.broadcasted_iota(I32, (tm, LANES), 1)
    lane_f = lane.astype(F32)

    def argmax_lanes(vals):
        top = jnp.max(vals, axis=-1, keepdims=True)
        idx = jnp.min(jnp.where(vals == top, lane_f, float(LANES)), axis=-1, keepdims=True)
        return top, idx

    is_group = lane < N_GROUPS
    g_top, grp = argmax_lanes(jnp.where(is_group, logits, -jnp.inf))
    g_gate = 1.0 / jnp.sum(jnp.where(is_group, jnp.exp(logits - g_top), 0.0), axis=-1, keepdims=True)
    first = N_GROUPS + EXPERTS_PER_GROUP * grp
    in_group = (lane_f >= first) & (lane_f < first + EXPERTS_PER_GROUP)
    el = jnp.where(in_group, logits, -jnp.inf)
    t1, i1 = argmax_lanes(el)
    t2, i2 = argmax_lanes(jnp.where(lane_f == i1, -jnp.inf, el))
    z = jnp.exp(t2 - t1)
    g1 = g_gate / (1.0 + z)
    g2 = g_gate * z / (1.0 + z)
    e1 = i1 - N_GROUPS
    e2 = i2 - N_GROUPS

    oh1 = (lane_f == e1)
    oh2 = (lane_f == e2)
    oh1b = oh1.astype(F32).astype(BF16)
    oh2b = oh2.astype(F32).astype(BF16)
    pre1 = jnp.dot(below[...], oh1b, preferred_element_type=F32)
    pre2 = jnp.dot(below[...], oh2b, preferred_element_type=F32)
    cnt1 = jnp.sum(oh1.astype(F32), axis=0, keepdims=True)
    cnt2 = jnp.sum(oh2.astype(F32), axis=0, keepdims=True)
    base = carry[...]
    r1 = jnp.sum(jnp.where(oh1, pre1 + base, 0.0), axis=-1, keepdims=True)
    r2 = jnp.sum(jnp.where(oh2, pre2 + base + cnt1, 0.0), axis=-1, keepdims=True)
    total = base + cnt1 + cnt2
    carry[...] = total
    cnt_out[...] = total

    gate_out[...] = jnp.where(lane == 0, g1, jnp.where(lane == 1, g2, 0.0))
    meta = jnp.where(lane == 0, e1, jnp.where(lane == 1, e2, jnp.where(lane == 2, r1, jnp.where(lane == 3, r2, 0.0))))
    meta_out[...] = meta.astype(I32)


def _router(x2d, norm_w, w_group, w_router):
    n, d = x2d.shape
    tm = TOKEN_TILE
    w_all = jnp.concatenate([w_group, w_router.transpose(1, 0, 2).reshape(d, N_EXPERTS)], axis=1)
    w_all = jnp.pad(w_all, ((0, 0), (0, LANES - w_all.shape[1])))
    whi = w_all.astype(BF16)
    wlo = (w_all - whi.astype(F32)).astype(BF16)
    row = lambda w: pl.BlockSpec((tm, w), lambda i: (i, 0))
    const = lambda r, c: pl.BlockSpec((r, c), lambda i: (0, 0))
    return pl.pallas_call(
        _router_body,
        grid=(n // tm,),
        in_specs=[row(d), const(1, d), const(d, LANES), const(d, LANES)],
        out_specs=[pl.BlockSpec((tm * ROW_SLABS, LANES), lambda i: (i, 0)), row(LANES), row(LANES), const(1, LANES)],
        out_shape=[jax.ShapeDtypeStruct((n * ROW_SLABS, LANES), F32), jax.ShapeDtypeStruct((n, LANES), F32),
                   jax.ShapeDtypeStruct((n, LANES), I32), jax.ShapeDtypeStruct((1, LANES), F32)],
        scratch_shapes=[pltpu.VMEM((1, LANES), F32), pltpu.VMEM((tm, tm), BF16)],
        compiler_params=_cparams(("arbitrary",), VMEM_LIMIT),
        name="router",
    )(x2d, norm_w.reshape(1, d), whi, wlo)


def _row_copy(src_ref, src_row, dst_ref, dst_row, sem):
    return pltpu.make_async_copy(_tile_rows(src_ref, src_row), _tile_rows(dst_ref, dst_row), sem)


def _dispatch_body(cnt_ref, eid_ref, rank_ref, h_ref, xs_ref, dest_ref, pstart, zbuf, sems):
    i = pl.program_id(0)
    tt = h_ref.shape[0] // ROW_SLABS
    blk = EXPERT_BLOCK
    n_slots = xs_ref.shape[0] // ROW_SLABS

    def pad_copy(pend):
        return pltpu.make_async_copy(zbuf, _tile_rows(xs_ref, pend - blk, blk), sems.at[2])

    @pl.when(i == 0)
    def _():
        zbuf[...] = jnp.zeros_like(zbuf)

        def start(e, acc):
            pstart[e] = acc
            pend = acc + ((cnt_ref[e] + (blk - 1)) // blk) * blk

            @pl.when(pend > acc)
            def _():
                pad_copy(pend).start()
            return pend

        used = lax.fori_loop(0, N_EXPERTS, start, jnp.int32(0))

        def tail_start(j, carry):
            pad_copy((j + 1) * blk).start()
            return carry

        lax.fori_loop(used // blk, n_slots // blk, tail_start, 0)

        def finish(e, acc):
            pend = acc + ((cnt_ref[e] + (blk - 1)) // blk) * blk

            @pl.when(pend > acc)
            def _():
                pad_copy(pend).wait()
            return pend

        lax.fori_loop(0, N_EXPERTS, finish, jnp.int32(0))

        def tail_wait(j, carry):
            pad_copy((j + 1) * blk).wait()
            return carry

        lax.fori_loop(used // blk, n_slots // blk, tail_wait, 0)

    def tokens(g, carry):
        t0 = pl.multiple_of(g * ROW_DMA_UNROLL, ROW_DMA_UNROLL)
        for u in range(ROW_DMA_UNROLL):
            for k in range(2):
                a = 2 * (t0 + u) + k
                d = pstart[eid_ref[a]] + rank_ref[a]
                dest_ref[a] = d
                _row_copy(h_ref, t0 + u, xs_ref, d, sems.at[k]).start(priority=k)
        return carry

    lax.fori_loop(0, tt // ROW_DMA_UNROLL, tokens, 0)
    for k in range(2):
        pltpu.make_async_copy(h_ref, _tile_rows(xs_ref, 0, tt), sems.at[k]).wait()


def _dispatch(counts, eid, rank, h3, n_slots):
    n = h3.shape[0] // ROW_SLABS
    tt = DISPATCH_TILE
    smem_blk = pl.BlockSpec((2 * tt,), lambda i, cnt: (i,), memory_space=pltpu.SMEM)
    grid_spec = pltpu.PrefetchScalarGridSpec(
        num_scalar_prefetch=1,
        grid=(n // tt,),
        in_specs=[smem_blk, smem_blk, pl.BlockSpec((tt * ROW_SLABS, LANES), lambda i, cnt: (i, 0))],
        out_specs=[pl.BlockSpec(memory_space=pl.ANY), smem_blk],
        scratch_shapes=[pltpu.SMEM((N_EXPERTS,), I32), pltpu.VMEM((EXPERT_BLOCK * ROW_SLABS, LANES), F32),
                        pltpu.SemaphoreType.DMA((3,))],
    )
    return pl.pallas_call(
        _dispatch_body,
        grid_spec=grid_spec,
        out_shape=[jax.ShapeDtypeStruct((n_slots * ROW_SLABS, LANES), F32), jax.ShapeDtypeStruct((2 * n,), I32)],
        compiler_params=_cparams(("arbitrary",), VMEM_LIMIT),
        name="dispatch",
    )(counts, eid, rank, h3)


def _experts_body(be_ref, nu_ref, xs_ref, w1_ref, w3_ref, w2_ref, ys_ref, w1b, w3b, w2b):
    i = pl.program_id(0)
    n_used = nu_ref[0]

    @pl.when(i < n_used)
    def _():
        changed = (i == 0) | (be_ref[i] != be_ref[jnp.maximum(i - 1, 0)])

        @pl.when(changed)
        def _():
            w1b[...] = w1_ref[0].astype(BF16)
            w3b[...] = w3_ref[0].astype(BF16)
            w2b[...] = w2_ref[0].astype(BF16)

        x = _load_row_tiles(xs_ref).astype(BF16)
        a = jnp.dot(x, w1b[...], preferred_element_type=F32)
        u = jnp.dot(x, w3b[...], preferred_element_type=F32)
        mid = (a * jax.nn.sigmoid(a) * u).astype(BF16)
        _store_row_tiles(ys_ref, jnp.dot(mid, w2b[...], preferred_element_type=F32))

    @pl.when(i >= n_used)
    def _():
        ys_ref[...] = jnp.zeros_like(ys_ref)


def _experts(block_expert, n_used, xs, w1, w3, w2):
    _, d, ff = w1.shape
    assert d == ROW_SLABS * LANES
    blk = EXPERT_BLOCK
    n_blocks = xs.shape[0] // (blk * ROW_SLABS)

    def slot_map(i, be, nu):
        return (jnp.minimum(i, nu[0] - 1), 0)

    def w_map(i, be, nu):
        return (be[jnp.minimum(i, nu[0] - 1)], 0, 0)

    grid_spec = pltpu.PrefetchScalarGridSpec(
        num_scalar_prefetch=2,
        grid=(n_blocks,),
        in_specs=[pl.BlockSpec((blk * ROW_SLABS, LANES), slot_map),
                  pl.BlockSpec((1, d, ff), w_map), pl.BlockSpec((1, d, ff), w_map),
                  pl.BlockSpec((1, ff, d), w_map)],
        out_specs=pl.BlockSpec((blk * ROW_SLABS, LANES), lambda i, be, nu: (i, 0)),
        scratch_shapes=[pltpu.VMEM((d, ff), BF16), pltpu.VMEM((d, ff), BF16), pltpu.VMEM((ff, d), BF16)],
    )
    return pl.pallas_call(
        _experts_body,
        grid_spec=grid_spec,
        out_shape=jax.ShapeDtypeStruct(xs.shape, F32),
        compiler_params=_cparams(("arbitrary",), VMEM_LIMIT),
        name="experts",
    )(block_expert, n_used, xs, w1, w3, w2)


def _combine_body(dest_ref, gate_ref, x_ref, ys_ref, out_ref, buf0, buf1, sems):
    tt = x_ref.shape[0]
    bufs = (buf0, buf1)

    def tokens(g, carry):
        t0 = pl.multiple_of(g * ROW_DMA_UNROLL, ROW_DMA_UNROLL)
        for u in range(ROW_DMA_UNROLL):
            for k in range(2):
                _row_copy(ys_ref, dest_ref[2 * (t0 + u) + k], bufs[k], t0 + u, sems.at[k]).start(priority=k)
        return carry

    lax.fori_loop(0, tt // ROW_DMA_UNROLL, tokens, 0)
    for k in range(2):
        pltpu.make_async_copy(_tile_rows(ys_ref, 0, tt), bufs[k], sems.at[k]).wait()
    g = gate_ref[...]
    out_ref[...] = x_ref[...] + g[:, 0:1] * _load_row_tiles(buf0) + g[:, 1:2] * _load_row_tiles(buf1)


def _combine(dest, gates, x2d, ys):
    n, d = x2d.shape
    tt = COMBINE_TILE
    row = lambda w: pl.BlockSpec((tt, w), lambda i: (i, 0))
    return pl.pallas_call(
        _combine_body,
        grid=(n // tt,),
        in_specs=[pl.BlockSpec((2 * tt,), lambda i: (i,), memory_space=pltpu.SMEM), row(LANES), row(d),
                  pl.BlockSpec(memory_space=pl.ANY)],
        out_specs=row(d),
        out_shape=jax.ShapeDtypeStruct((n, d), F32),
        scratch_shapes=[pltpu.VMEM((tt * ROW_SLABS, LANES), F32), pltpu.VMEM((tt * ROW_SLABS, LANES), F32),
                        pltpu.SemaphoreType.DMA((2,))],
        compiler_params=_cparams(("arbitrary",), VMEM_LIMIT),
        name="combine",
    )(dest, gates, x2d, ys)


def _moe(x2d, norm_w, w_group, w_router, w1, w3, w2):
    n, d = x2d.shape
    blk = EXPERT_BLOCK
    h3, gates, meta, cnt = _router(x2d, norm_w, w_group, w_router)
    counts = cnt[0, :N_EXPERTS].astype(I32)
    eid = meta[:, 0:2].reshape(-1)
    rank = meta[:, 2:4].reshape(-1)
    n_slots = 2 * n + N_EXPERTS * blk
    n_blocks = n_slots // blk
    pends = jnp.cumsum((counts + blk - 1) // blk * blk)
    block_start = jnp.arange(n_blocks, dtype=I32) * blk
    block_expert = jnp.minimum(jnp.sum((pends[None, :] <= block_start[:, None]).astype(I32), axis=1),
                               N_EXPERTS - 1)
    n_used = (pends[-1:] // blk).astype(I32)
    xs, dest = _dispatch(counts, eid, rank, h3, n_slots)
    ys = _experts(block_expert, n_used, xs, w1, w3, w2)
    return _combine(dest, gates, x2d, ys)


def _layer(x, mem, norm_mix_w, w_in, attn_q_norm_w, attn_k_norm_w, ret_decay_f, ret_decay_b, ret_gn_w, w_out,
           norm_mem_w, norm_memkv_w, w_mq, w_mkv, mem_q_norm_w, mem_k_norm_w, w_mo,
           norm_moe_w, w_group, w_router, w_exp_gate, w_exp_up, w_exp_down):
    b, s, d = x.shape
    x2d = x.reshape(b * s, d)
    q, k, v, q_pl, k_pl, v_pl, rq, rk, rv, rg = _in_proj(x2d, norm_mix_w, w_in, attn_q_norm_w, attn_k_norm_w)
    attn = _attention(q, k, v, q_pl, k_pl, v_pl, b, s)
    ret = _retention(rq, rk, rv, rg, ret_decay_f, ret_decay_b, ret_gn_w, b, s)
    x1 = _out_proj(attn, ret, x2d, w_out)
    mk, mv = _mem_kv(mem, norm_memkv_w, w_mkv, mem_k_norm_w)
    x2 = _mem_attn(x1.reshape(b, s, d), mk, mv, norm_mem_w, w_mq, mem_q_norm_w, w_mo)
    x3 = _moe(x2.reshape(b * s, d), norm_moe_w, w_group, w_router, w_exp_gate, w_exp_up, w_exp_down)
    return x3.reshape(b, s, d)


def kernel(x, mem, norm_mix_w, w_in, attn_q_norm_w, attn_k_norm_w, ret_decay_f, ret_decay_b, ret_gn_w, w_out,
           norm_mem_w, norm_memkv_w, w_mq, w_mkv, mem_q_norm_w, mem_k_norm_w, w_mo, norm_moe_w, w_group,
           w_router, w_exp_gate, w_exp_up, w_exp_down):
    depth = norm_mix_w.shape[0]
    for l in range(depth):
        x = _layer(x, mem, norm_mix_w[l], w_in[l], attn_q_norm_w[l], attn_k_norm_w[l], ret_decay_f[l],
                   ret_decay_b[l], ret_gn_w[l], w_out[l], norm_mem_w[l], norm_memkv_w[l], w_mq[l], w_mkv[l],
                   mem_q_norm_w[l], mem_k_norm_w[l], w_mo[l], norm_moe_w[l], w_group[l], w_router[l],
                   w_exp_gate[l], w_exp_up[l], w_exp_down[l])
    return x
```

```python
import functools

import numpy as np
import jax
import jax.numpy as jnp
from jax import lax
from jax.experimental import pallas as pl
from jax.experimental.pallas import tpu as pltpu

F32 = jnp.float32
BF16 = jnp.bfloat16
I32 = jnp.int32

NORM_EPS = 1e-6
GN_EPS = 1e-5
NEG_INF = -1e30

ATTN_HEADS = 8
ATTN_HEAD_DIM = 64
ATTN_WIDTH = ATTN_HEADS * ATTN_HEAD_DIM
ATTN_HALF = 64
PLANES = 16
ATTN_TILE = PLANES * ATTN_HALF
RET_HEADS = 4
RET_QK_DIM = 64
RET_V_DIM = 128
RET_QK_WIDTH = RET_HEADS * RET_QK_DIM
RET_V_WIDTH = RET_HEADS * RET_V_DIM
MEM_HEADS = 4
N_GROUPS = 4
EXPERTS_PER_GROUP = 8
N_EXPERTS = N_GROUPS * EXPERTS_PER_GROUP

LANES = 128
TOKEN_TILE = 512
ATTN_BLOCKS_PER_STEP = 4
LOG2_E = 1.4426950408889634
RET_CHUNK = 256
EXPERT_BLOCK = 512
DISPATCH_TILE = 1024
COMBINE_TILE = 256
ROW_DMA_UNROLL = 16
VMEM_LIMIT = 56 * 1024 * 1024


def _cparams(sem, vmem=None):
    return pltpu.CompilerParams(dimension_semantics=sem, vmem_limit_bytes=vmem)


def _split_hi_lo(x):
    hi = x.astype(BF16)
    lo = (x - hi.astype(F32)).astype(BF16)
    return hi, lo


def _rms(x, w):
    ms = jnp.mean(x * x, axis=-1, keepdims=True)
    return x * lax.rsqrt(ms + NORM_EPS) * w


ROW_SLABS = 8


def _store_row_tiles(ref, val):
    rows = val.shape[0]
    for j in range(ROW_SLABS):
        ref[pl.ds(j, rows, stride=ROW_SLABS), :] = val[:, j * LANES:(j + 1) * LANES]


def _load_row_tiles(ref):
    rows = ref.shape[0] // ROW_SLABS
    return jnp.concatenate([ref[pl.ds(j, rows, stride=ROW_SLABS), :] for j in range(ROW_SLABS)], axis=1)


def _tile_rows(ref, row, count=1):
    return ref.at[pl.ds(pl.multiple_of(row * ROW_SLABS, ROW_SLABS), count * ROW_SLABS)]


def _in_proj_body(x_ref, nw_ref, w_ref, qw_ref, kw_ref, g_ref,
                  q_out, k_out, v_out, qp_out, kp_out, vp_out, rq_out, rk_out, rv_out, rg_out, slabs):
    tm = x_ref.shape[0]
    h = _rms(x_ref[...], nw_ref[...]).astype(BF16)

    def emit(val, nat_out, plane_out):
        nat_out[...] = val.astype(BF16)
        for j in range(ATTN_WIDTH // LANES):
            slabs[j] = val[:, j * LANES:(j + 1) * LANES]
        for c in range(PLANES):
            for j in range(ATTN_WIDTH // LANES):
                rows = slabs[j, pl.ds(c, tm // PLANES, stride=PLANES), :]
                plane_out[0, c, :, j * LANES:(j + 1) * LANES] = rows.astype(BF16)

    def proj(a, b):
        return jnp.dot(h, w_ref[:, a:b], preferred_element_type=F32)

    def head_norm(p, w):
        hi, lo = _split_hi_lo(p * p)
        s = (jnp.dot(hi, g_ref[...], preferred_element_type=F32)
             + jnp.dot(lo, g_ref[...], preferred_element_type=F32))
        return p * lax.rsqrt(s * (1.0 / ATTN_HEAD_DIM) + NORM_EPS) * w

    a = ATTN_WIDTH
    emit(head_norm(proj(0, a), qw_ref[...]), q_out, qp_out)
    emit(head_norm(proj(a, 2 * a), kw_ref[...]), k_out, kp_out)
    emit(proj(2 * a, 3 * a), v_out, vp_out)
    c = 3 * a
    rq_out[...] = proj(c, c + RET_QK_WIDTH).astype(BF16)
    rk_out[...] = proj(c + RET_QK_WIDTH, c + 2 * RET_QK_WIDTH).astype(BF16)
    c += 2 * RET_QK_WIDTH
    rv_out[...] = proj(c, c + RET_V_WIDTH).astype(BF16)
    rg_out[...] = proj(c + RET_V_WIDTH, c + 2 * RET_V_WIDTH).astype(BF16)


def _in_proj(x2d, norm_w, w_in, q_norm_w, k_norm_w):
    n, d = x2d.shape
    cols = w_in.shape[1]
    tm = TOKEN_TILE
    qw = (jnp.tile(q_norm_w, ATTN_HEADS) * (ATTN_HEAD_DIM ** -0.5 * LOG2_E)).reshape(1, ATTN_WIDTH)
    kw = jnp.tile(k_norm_w, ATTN_HEADS).reshape(1, ATTN_WIDTH)
    head_of = np.arange(ATTN_WIDTH) // ATTN_HEAD_DIM
    gmat = jnp.asarray(head_of[:, None] == head_of[None, :], dtype=BF16)
    row = lambda w: pl.BlockSpec((tm, w), lambda i: (i, 0))
    full = lambda r, c: pl.BlockSpec((r, c), lambda i: (0, 0))
    per_tile = ATTN_TILE // tm
    plane = pl.BlockSpec((1, PLANES, tm // PLANES, ATTN_WIDTH), lambda i: (i // per_tile, 0, i % per_tile, 0))
    plane_shape = jax.ShapeDtypeStruct((n // ATTN_TILE, PLANES, ATTN_TILE // PLANES, ATTN_WIDTH), BF16)
    ret_widths = (RET_QK_WIDTH, RET_QK_WIDTH, RET_V_WIDTH, RET_V_WIDTH)
    return pl.pallas_call(
        _in_proj_body,
        grid=(n // tm,),
        in_specs=[row(d), full(1, d), full(d, cols), full(1, ATTN_WIDTH), full(1, ATTN_WIDTH),
                  full(ATTN_WIDTH, ATTN_WIDTH)],
        out_specs=[row(ATTN_WIDTH)] * 3 + [plane] * 3 + [row(w) for w in ret_widths],
        out_shape=([jax.ShapeDtypeStruct((n, ATTN_WIDTH), BF16)] * 3 + [plane_shape] * 3
                   + [jax.ShapeDtypeStruct((n, w), BF16) for w in ret_widths]),
        scratch_shapes=[pltpu.VMEM((ATTN_WIDTH // LANES, tm, LANES), F32)],
        compiler_params=_cparams(("parallel",), VMEM_LIMIT),
        name="in_proj",
    )(x2d, norm_w.reshape(1, d), w_in.astype(BF16), qw, kw, gmat)


def _attn_bias(dilation, interleave=1):
    half = ATTN_HALF
    idx = np.arange(half)
    sub = (idx % (half // interleave)) * interleave + idx // (half // interleave)
    qi = sub[:, None]
    kc = (np.arange(3)[:, None] * half + sub[None, :]).reshape(1, -1)
    seg = np.repeat(np.arange(3), half)[None, :]
    delta = kc - half - qi
    band = np.abs(delta) <= half
    slopes = np.exp2(-8.0 * np.arange(1, ATTN_HEADS + 1) / ATTN_HEADS)
    valid = [band & (seg >= 1), band, band & (seg <= 1)]
    out = np.empty((ATTN_HEADS // 2, 3, 2 * half, 3 * half), np.float32)
    for p in range(ATTN_HEADS // 2):
        for v in range(3):
            for s in range(2):
                b = -slopes[2 * p + s] * LOG2_E * (dilation * np.abs(delta)).astype(np.float32)
                out[p, v, s * half:(s + 1) * half] = np.where(valid[v], b, NEG_INF)
    return jnp.asarray(out.reshape(-1, 2 * half, 3 * half))


def _attend(chains):
    half = ATTN_HALF
    lo_lanes = lax.broadcasted_iota(I32, (half, LANES), 1) < ATTN_HEAD_DIM
    ones = jnp.ones((3 * half, LANES), BF16)
    scores = []
    for q, k, v, bias in chains:
        zero = jnp.zeros_like(q)
        qs = jnp.concatenate([jnp.where(lo_lanes, q, zero), jnp.where(lo_lanes, zero, q)], axis=0)
        scores.append(lax.dot_general(qs, k, (((1,), (1,)), ((), ())), preferred_element_type=F32))
    probs, maxes = [], []
    for (q, k, v, bias), s in zip(chains, scores):
        s = s + bias
        m = jnp.max(s, axis=-1, keepdims=True)
        probs.append(jnp.exp2(s - m).astype(BF16))
        maxes.append(m)
    results = []
    for (q, k, v, bias), e in zip(chains, probs):
        vext = jnp.concatenate([v, ones], axis=1)
        results.append(jnp.dot(e, vext, preferred_element_type=F32))
    outs = []
    for r, m in zip(results, maxes):
        acc = jnp.where(lo_lanes, r[:half, :LANES], r[half:, :LANES])
        denom = jnp.where(lo_lanes, r[:half, LANES:], r[half:, LANES:])
        mm = jnp.where(lo_lanes, m[:half], m[half:])
        outs.append((acc / denom, mm + jnp.log2(denom)))
    return outs


def _merge(oa, la, ob, lb):
    m = jnp.maximum(la, lb)
    ea = jnp.exp2(la - m)
    eb = jnp.exp2(lb - m)
    den = ea + eb
    return (ea * oa + eb * ob) / den, m + jnp.log2(den)


def _attn_body(q_ref, kp_ref, km_ref, kn_ref, vp_ref, vm_ref, vn_ref,
               qpl_ref, kplp_ref, kplm_ref, kpln_ref, vplp_ref, vplm_ref, vpln_ref,
               b1_ref, b4_ref, b16_ref, o_ref, kcat, vcat, o_far, l_far):
    half = ATTN_HALF
    t = pl.program_id(1)
    n_tiles = pl.num_programs(1)
    tile = q_ref.shape[1]
    pairs = ATTN_HEADS // 2
    piece = half // 4

    kcat[0:half] = kp_ref[0]
    kcat[half:half + tile] = km_ref[0]
    kcat[half + tile:] = kn_ref[0]
    vcat[0:half] = vp_ref[0]
    vcat[half:half + tile] = vm_ref[0]
    vcat[half + tile:] = vn_ref[0]

    def edge(first, last):
        return jnp.where(first, 0, jnp.where(last, 2, 1))

    def far_step(r, carry):
        v16 = edge(t == 0, t == n_tiles - 1)
        for pr in range(pairs):
            cs = slice(pr * LANES, (pr + 1) * LANES)
            chains = []
            for m in range(4):
                c = r + 4 * m
                k3 = jnp.concatenate([kplp_ref[0, c, :, cs], kplm_ref[0, c, :, cs], kpln_ref[0, c, :, cs]], axis=0)
                v3 = jnp.concatenate([vplp_ref[0, c, :, cs], vplm_ref[0, c, :, cs], vpln_ref[0, c, :, cs]], axis=0)
                chains.append((qpl_ref[0, c, :, cs], k3, v3, b16_ref[pr * 3 + v16]))
            for nb in range(4):
                def rows(main, before, after, a0):
                    ref, lo = (before, a0 + half) if a0 < 0 else (after, a0 - half) if a0 >= half else (main, a0)
                    return [ref[0, r + 4 * m, lo:lo + piece, cs] for m in range(4)]
                a0 = nb * piece
                q4 = jnp.concatenate(rows(qpl_ref, None, None, a0), axis=0)
                k4 = jnp.concatenate(sum((rows(kplm_ref, kplp_ref, kpln_ref, a0 + d) for d in (-piece, 0, piece)), []),
                                     axis=0)
                v4 = jnp.concatenate(sum((rows(vplm_ref, vplp_ref, vpln_ref, a0 + d) for d in (-piece, 0, piece)), []),
                                     axis=0)
                v4e = edge((t == 0) & (nb == 0), (t == n_tiles - 1) & (nb == 3))
                chains.append((q4, k4, v4, b4_ref[pr * 3 + v4e]))
            res = _attend(chains)
            for m in range(4):
                o16, l16 = res[m]
                o_rows, l_rows = [], []
                for nb in range(4):
                    o4, l4 = res[4 + nb]
                    sl = slice(nb * piece, (nb + 1) * piece)
                    s4 = slice(m * piece, (m + 1) * piece)
                    om, lm = _merge(o16[sl], l16[sl], o4[s4], l4[s4])
                    o_rows.append(om)
                    l_rows.append(lm)
                dst = pl.ds(r + 4 * m, half, stride=PLANES)
                o_far[pr, dst, :] = jnp.concatenate(o_rows, axis=0)
                l_far[pr, dst, :] = jnp.concatenate(l_rows, axis=0)
        return carry

    lax.fori_loop(0, 4, far_step, 0)

    unroll = ATTN_BLOCKS_PER_STEP
    n_blocks = n_tiles * (tile // half)

    def near_step(it, carry):
        chains, where = [], []
        for u in range(unroll):
            jb = it * unroll + u
            r0 = pl.multiple_of(jb * half, half)
            gb = t * (tile // half) + jb
            variant = edge(gb == 0, gb == n_blocks - 1)
            for pr in range(pairs):
                cs = slice(pr * LANES, (pr + 1) * LANES)
                chains.append((q_ref[0, pl.ds(r0, half), cs], kcat[pl.ds(r0, 3 * half), cs],
                               vcat[pl.ds(r0, 3 * half), cs], b1_ref[pr * 3 + variant]))
                where.append((r0, pr, cs))
        for (r0, pr, cs), (o1, l1) in zip(where, _attend(chains)):
            o, _ = _merge(o1, l1, o_far[pr, pl.ds(r0, half), :], l_far[pr, pl.ds(r0, half), :])
            o_ref[0, pl.ds(r0, half), cs] = o.astype(o_ref.dtype)
        return carry

    lax.fori_loop(0, tile // half // unroll, near_step, 0)


def _attention(q, k, v, q_pl, k_pl, v_pl, b, s):
    w = ATTN_WIDTH
    half = ATTN_HALF
    tile = ATTN_TILE
    n_tiles = s // tile
    hb = tile // half
    last = s // half - 1
    main = pl.BlockSpec((1, tile, w), lambda bb, t: (bb, t, 0))
    prev = pl.BlockSpec((1, half, w), lambda bb, t: (bb, jnp.maximum(t * hb - 1, 0), 0))
    nxt = pl.BlockSpec((1, half, w), lambda bb, t: (bb, jnp.minimum((t + 1) * hb, last), 0))
    pshape = (1, PLANES, tile // PLANES, w)
    pl_main = pl.BlockSpec(pshape, lambda bb, t: (bb * n_tiles + t, 0, 0, 0))
    pl_prev = pl.BlockSpec(pshape, lambda bb, t: (bb * n_tiles + jnp.maximum(t - 1, 0), 0, 0, 0))
    pl_next = pl.BlockSpec(pshape, lambda bb, t: (bb * n_tiles + jnp.minimum(t + 1, n_tiles - 1), 0, 0, 0))
    biases = [_attn_bias(1), _attn_bias(4, interleave=4), _attn_bias(16)]
    bias_spec = pl.BlockSpec(biases[0].shape, lambda bb, t: (0, 0, 0))
    nat = lambda a: a.reshape(b, s, w)
    out = pl.pallas_call(
        _attn_body,
        grid=(b, n_tiles),
        in_specs=[main, prev, main, nxt, prev, main, nxt,
                  pl_main, pl_prev, pl_main, pl_next, pl_prev, pl_main, pl_next,
                  bias_spec, bias_spec, bias_spec],
        out_specs=main,
        out_shape=jax.ShapeDtypeStruct((b, s, w), BF16),
        scratch_shapes=[pltpu.VMEM((tile + 2 * half, w), BF16), pltpu.VMEM((tile + 2 * half, w), BF16),
                        pltpu.VMEM((ATTN_HEADS // 2, tile, LANES), F32),
                        pltpu.VMEM((ATTN_HEADS // 2, tile, LANES), F32)],
        compiler_params=_cparams(("parallel", "parallel"), VMEM_LIMIT),
        name="attention",
    )(nat(q), nat(k), nat(k), nat(k), nat(v), nat(v), nat(v),
      q_pl, k_pl, k_pl, k_pl, v_pl, v_pl, v_pl, *biases)
    return out.reshape(b * s, w)


def _ret_body(lg_ref, q_ref, k_ref, v_ref, g_ref, gnw_ref, out_ref,
              dmat, qdec, kdec, cdec, fstate, rstate, rall, *, chunk):
    c = chunk
    b = pl.program_id(0)
    ph = pl.program_id(1)
    n = pl.program_id(2)
    n_chunks = pl.num_programs(2)
    k_scale = RET_QK_DIM ** -0.5
    pairs = RET_HEADS // 2
    pw = 2 * RET_QK_DIM
    vw = 2 * RET_V_DIM
    first = lax.broadcasted_iota(I32, (c, pw), 1) < RET_QK_DIM
    diag = ((lax.broadcasted_iota(I32, (pw, vw), 0) < RET_QK_DIM)
            == (lax.broadcasted_iota(I32, (pw, vw), 1) < RET_V_DIM))

    @pl.when((b == 0) & (ph == 0) & (n == 0))
    def _init_tables():
        ii = lax.broadcasted_iota(I32, (c, c), 0)
        jj = lax.broadcasted_iota(I32, (c, c), 1)
        fwd = (ii - jj).astype(F32)
        ri = lax.broadcasted_iota(I32, (c, pw), 0).astype(F32)
        top = lax.broadcasted_iota(I32, (pw, vw), 0) < RET_QK_DIM
        for h in range(RET_HEADS):
            dmat[h] = jnp.where(ii >= jj, jnp.exp(lg_ref[0, h] * fwd), jnp.exp(-lg_ref[1, h] * fwd)) * k_scale
        for pr in range(pairs):
            lf = jnp.where(first, lg_ref[0, 2 * pr], lg_ref[0, 2 * pr + 1])
            lb = jnp.where(first, lg_ref[1, 2 * pr], lg_ref[1, 2 * pr + 1])
            qdec[0, pr] = jnp.exp(lf * (ri + 1.0))
            qdec[1, pr] = jnp.exp(lb * (c - ri))
            kdec[0, pr] = jnp.exp(lf * (c - 1.0 - ri)) * k_scale
            kdec[1, pr] = jnp.exp(lb * ri) * k_scale
            for direction in range(2):
                cdec[direction, pr] = jnp.exp(jnp.where(top, lg_ref[direction, 2 * pr],
                                                        lg_ref[direction, 2 * pr + 1]) * c)

    def kv_update(direction, pr):
        kp = k_ref[0, :, pr * pw:(pr + 1) * pw].astype(F32)
        ks = (kp * kdec[direction, pr]).astype(BF16)
        new = lax.dot_general(ks, v_ref[0, :, pr * vw:(pr + 1) * vw], (((0,), (0,)), ((), ())),
                              preferred_element_type=F32)
        return jnp.where(diag, new, 0.0)

    @pl.when(ph == 0)
    def _right_to_left():
        @pl.when(n == 0)
        def _():
            rstate[...] = jnp.zeros_like(rstate)

        ci = n_chunks - 1 - n
        for pr in range(pairs):
            st = rstate[pr]
            rall[ci, pr] = st
            rstate[pr] = st * cdec[1, pr] + kv_update(1, pr)

    @pl.when(ph == 1)
    def _left_to_right():
        @pl.when(n == 0)
        def _():
            fstate[...] = jnp.zeros_like(fstate)

        for pr in range(pairs):
            qp = q_ref[0, :, pr * pw:(pr + 1) * pw]
            kp = k_ref[0, :, pr * pw:(pr + 1) * pw]
            qf = qp.astype(F32)
            qcat = jnp.concatenate([(qf * qdec[0, pr]).astype(BF16), (qf * qdec[1, pr]).astype(BF16)], axis=1)
            st = fstate[pr]
            states = jnp.concatenate([st, rall[n, pr]], axis=0).astype(BF16)
            cross = jnp.dot(qcat, states, preferred_element_type=F32)
            fstate[pr] = st * cdec[0, pr] + kv_update(0, pr)
            zero = jnp.zeros_like(qp)
            for a in range(2):
                h = 2 * pr + a
                vs = slice(h * RET_V_DIM, (h + 1) * RET_V_DIM)
                qm = jnp.where(first, qp, zero) if a == 0 else jnp.where(first, zero, qp)
                s = lax.dot_general(qm, kp, (((1,), (1,)), ((), ())), preferred_element_type=F32) * dmat[h]
                y = jnp.dot(s.astype(BF16), v_ref[0, :, vs], preferred_element_type=F32)
                y = y + cross[:, a * RET_V_DIM:(a + 1) * RET_V_DIM]
                mu = jnp.mean(y, axis=-1, keepdims=True)
                yc = y - mu
                var = jnp.mean(yc * yc, axis=-1, keepdims=True)
                yn = yc * lax.rsqrt(var + GN_EPS) * gnw_ref[:, vs]
                gate = g_ref[0, :, vs].astype(F32)
                out_ref[0, :, vs] = (gate * jax.nn.sigmoid(gate) * yn).astype(out_ref.dtype)


def _retention(rq, rk, rv, rg, decay_f, decay_b, gn_w, b, s):
    c = RET_CHUNK
    nc = s // c
    lg = jnp.stack([jax.nn.log_sigmoid(decay_f.astype(F32)), jax.nn.log_sigmoid(decay_b.astype(F32))])
    qk_w, v_w = RET_QK_WIDTH, RET_V_WIDTH
    pairs, pair_qk, pair_v = RET_HEADS // 2, 2 * RET_QK_DIM, 2 * RET_V_DIM

    def both(bb, ph, n):
        return (bb, jnp.where(ph == 0, nc - 1 - n, n), 0)

    def fwd_only(bb, ph, n):
        return (bb, jnp.where(ph == 0, 0, n), 0)

    out = pl.pallas_call(
        functools.partial(_ret_body, chunk=c),
        grid=(b, 2, nc),
        in_specs=[pl.BlockSpec(memory_space=pltpu.SMEM),
                  pl.BlockSpec((1, c, qk_w), fwd_only),
                  pl.BlockSpec((1, c, qk_w), both),
                  pl.BlockSpec((1, c, v_w), both),
                  pl.BlockSpec((1, c, v_w), fwd_only),
                  pl.BlockSpec((1, v_w), lambda bb, ph, n: (0, 0))],
        out_specs=pl.BlockSpec((1, c, v_w), fwd_only),
        out_shape=jax.ShapeDtypeStruct((b, s, v_w), BF16),
        scratch_shapes=[pltpu.VMEM((RET_HEADS, c, c), F32),
                        pltpu.VMEM((2, pairs, c, pair_qk), F32),
                        pltpu.VMEM((2, pairs, c, pair_qk), F32),
                        pltpu.VMEM((2, pairs, pair_qk, pair_v), F32),
                        pltpu.VMEM((pairs, pair_qk, pair_v), F32),
                        pltpu.VMEM((pairs, pair_qk, pair_v), F32),
                        pltpu.VMEM((nc, pairs, pair_qk, pair_v), F32)],
        compiler_params=_cparams(("arbitrary", "arbitrary", "arbitrary"), VMEM_LIMIT),
        name="retention",
    )(lg, rq.reshape(b, s, qk_w), rk.reshape(b, s, qk_w), rv.reshape(b, s, v_w), rg.reshape(b, s, v_w),
      gn_w.reshape(1, v_w))
    return out.reshape(b * s, v_w)


def _out_proj_body(attn_ref, ret_ref, x_ref, w_ref, out_ref):
    y = jnp.dot(attn_ref[...], w_ref[:ATTN_WIDTH], preferred_element_type=F32)
    y = y + jnp.dot(ret_ref[...], w_ref[ATTN_WIDTH:], preferred_element_type=F32)
    out_ref[...] = x_ref[...] + y


def _out_proj(attn, ret, x2d, w_out):
    n, d = x2d.shape
    tm = TOKEN_TILE
    row = lambda w: pl.BlockSpec((tm, w), lambda i: (i, 0))
    return pl.pallas_call(
        _out_proj_body,
        grid=(n // tm,),
        in_specs=[row(ATTN_WIDTH), row(RET_V_WIDTH), row(d), pl.BlockSpec(w_out.shape, lambda i: (0, 0))],
        out_specs=row(d),
        out_shape=jax.ShapeDtypeStruct((n, d), F32),
        compiler_params=_cparams(("parallel",), VMEM_LIMIT),
        name="out_proj",
    )(attn, ret, x2d, w_out.astype(BF16))


def _mem_kv_body(mem_ref, nw_ref, w_ref, kw_ref, k_out, v_out):
    d = mem_ref.shape[-1]
    hd = d // MEM_HEADS
    h = _rms(mem_ref[0], nw_ref[...]).astype(BF16)
    kv = jnp.dot(h, w_ref[...], preferred_element_type=F32)
    for i in range(MEM_HEADS):
        k_out[0, :, i * hd:(i + 1) * hd] = _rms(kv[:, i * hd:(i + 1) * hd], kw_ref[...]).astype(BF16)
    v_out[0] = kv[:, d:].astype(BF16)


def _mem_kv(mem, norm_w, w_mkv, k_norm_w):
    b, m, d = mem.shape
    return pl.pallas_call(
        _mem_kv_body,
        grid=(b,),
        in_specs=[pl.BlockSpec((1, m, d), lambda i: (i, 0, 0)),
                  pl.BlockSpec((1, d), lambda i: (0, 0)),
                  pl.BlockSpec((d, 2 * d), lambda i: (0, 0)),
                  pl.BlockSpec((1, d // MEM_HEADS), lambda i: (0, 0))],
        out_specs=[pl.BlockSpec((1, m, d), lambda i: (i, 0, 0))] * 2,
        out_shape=[jax.ShapeDtypeStruct((b, m, d), BF16)] * 2,
        compiler_params=_cparams(("parallel",), VMEM_LIMIT),
        name="mem_kv",
    )(mem, norm_w.reshape(1, d), w_mkv.astype(BF16), k_norm_w.reshape(1, -1))


def _mem_attn_body(x_ref, nw_ref, wq_ref, qw_ref, k_ref, v_ref, wo_ref, out_ref):
    d = x_ref.shape[-1]
    hd = d // MEM_HEADS
    x = x_ref[0]
    h = _rms(x, nw_ref[...]).astype(BF16)
    q = jnp.dot(h, wq_ref[...], preferred_element_type=F32)
    heads = []
    for i in range(MEM_HEADS):
        cs = slice(i * hd, (i + 1) * hd)
        qn = _rms(q[:, cs], qw_ref[...]).astype(BF16)
        s = lax.dot_general(qn, k_ref[0, :, cs], (((1,), (1,)), ((), ())), preferred_element_type=F32)
        e = jnp.exp(s - jnp.max(s, axis=-1, keepdims=True))
        o = jnp.dot(e.astype(BF16), v_ref[0, :, cs], preferred_element_type=F32)
        heads.append((o / jnp.sum(e, axis=-1, keepdims=True)).astype(BF16))
    o = jnp.concatenate(heads, axis=1)
    out_ref[0] = x + jnp.dot(o, wo_ref[...], preferred_element_type=F32)


def _mem_attn(x1, mk, mv, norm_w, w_mq, q_norm_w, w_mo):
    b, s, d = x1.shape
    m = mk.shape[1]
    tm = TOKEN_TILE
    hd = d // MEM_HEADS
    qw = (q_norm_w * (hd ** -0.5)).reshape(1, hd)
    tok = pl.BlockSpec((1, tm, d), lambda bb, i: (bb, i, 0))
    const = lambda r, c: pl.BlockSpec((r, c), lambda bb, i: (0, 0))
    mem = pl.BlockSpec((1, m, d), lambda bb, i: (bb, 0, 0))
    return pl.pallas_call(
        _mem_attn_body,
        grid=(b, s // tm),
        in_specs=[tok, const(1, d), const(d, d), const(1, hd), mem, mem, const(d, d)],
        out_specs=tok,
        out_shape=jax.ShapeDtypeStruct((b, s, d), F32),
        compiler_params=_cparams(("parallel", "parallel"), VMEM_LIMIT),
        name="mem_attn",
    )(x1, norm_w.reshape(1, d), w_mq.astype(BF16), qw, mk, mv, w_mo.astype(BF16))


def _router_body(x_ref, nw_ref, whi_ref, wlo_ref, h_out, gate_out, meta_out, cnt_out, carry, below):
    tm = x_ref.shape[0]
    i = pl.program_id(0)

    @pl.when(i == 0)
    def _():
        carry[...] = jnp.zeros_like(carry)
        rr = lax.broadcasted_iota(I32, (tm, tm), 0)
        cc = lax.broadcasted_iota(I32, (tm, tm), 1)
        below[...] = (cc < rr).astype(F32).astype(BF16)

    h = _rms(x_ref[...], nw_ref[...])
    _store_row_tiles(h_out, h)
    hi, lo = _split_hi_lo(h)
    logits = (jnp.dot(hi, whi_ref[...], preferred_element_type=F32)
              + jnp.dot(lo, whi_ref[...], preferred_element_type=F32)
              + jnp.dot(hi, wlo_ref[...], preferred_element_type=F32))
    lane = lax.broadcasted_iota(I32, (tm, LANES), 1)
    lane_f = lane.astype(F32)

    def argmax_lanes(vals):
        top = jnp.max(vals, axis=-1, keepdims=True)
        idx = jnp.min(jnp.where(vals == top, lane_f, float(LANES)), axis=-1, keepdims=True)
        return top, idx

    is_group = lane < N_GROUPS
    g_top, grp = argmax_lanes(jnp.where(is_group, logits, -jnp.inf))
    g_gate = 1.0 / jnp.sum(jnp.where(is_group, jnp.exp(logits - g_top), 0.0), axis=-1, keepdims=True)
    first = N_GROUPS + EXPERTS_PER_GROUP * grp
    in_group = (lane_f >= first) & (lane_f < first + EXPERTS_PER_GROUP)
    el = jnp.where(in_group, logits, -jnp.inf)
    t1, i1 = argmax_lanes(el)
    t2, i2 = argmax_lanes(jnp.where(lane_f == i1, -jnp.inf, el))
    z = jnp.exp(t2 - t1)
    g1 = g_gate / (1.0 + z)
    g2 = g_gate * z / (1.0 + z)
    e1 = i1 - N_GROUPS
    e2 = i2 - N_GROUPS

    oh1 = (lane_f == e1)
    oh2 = (lane_f == e2)
    oh1b = oh1.astype(F32).astype(BF16)
    oh2b = oh2.astype(F32).astype(BF16)
    pre1 = jnp.dot(below[...], oh1b, preferred_element_type=F32)
    pre2 = jnp.dot(below[...], oh2b, preferred_element_type=F32)
    cnt1 = jnp.sum(oh1.astype(F32), axis=0, keepdims=True)
    cnt2 = jnp.sum(oh2.astype(F32), axis=0, keepdims=True)
    base = carry[...]
    r1 = jnp.sum(jnp.where(oh1, pre1 + base, 0.0), axis=-1, keepdims=True)
    r2 = jnp.sum(jnp.where(oh2, pre2 + base + cnt1, 0.0), axis=-1, keepdims=True)
    total = base + cnt1 + cnt2
    carry[...] = total
    cnt_out[...] = total

    gate_out[...] = jnp.where(lane == 0, g1, jnp.where(lane == 1, g2, 0.0))
    meta = jnp.where(lane == 0, e1, jnp.where(lane == 1, e2, jnp.where(lane == 2, r1, jnp.where(lane == 3, r2, 0.0))))
    meta_out[...] = meta.astype(I32)


def _router(x2d, norm_w, w_group, w_router):
    n, d = x2d.shape
    tm = TOKEN_TILE
    w_all = jnp.concatenate([w_group, w_router.transpose(1, 0, 2).reshape(d, N_EXPERTS)], axis=1)
    w_all = jnp.pad(w_all, ((0, 0), (0, LANES - w_all.shape[1])))
    whi = w_all.astype(BF16)
    wlo = (w_all - whi.astype(F32)).astype(BF16)
    row = lambda w: pl.BlockSpec((tm, w), lambda i: (i, 0))
    const = lambda r, c: pl.BlockSpec((r, c), lambda i: (0, 0))
    return pl.pallas_call(
        _router_body,
        grid=(n // tm,),
        in_specs=[row(d), const(1, d), const(d, LANES), const(d, LANES)],
        out_specs=[pl.BlockSpec((tm * ROW_SLABS, LANES), lambda i: (i, 0)), row(LANES), row(LANES), const(1, LANES)],
        out_shape=[jax.ShapeDtypeStruct((n * ROW_SLABS, LANES), F32), jax.ShapeDtypeStruct((n, LANES), F32),
                   jax.ShapeDtypeStruct((n, LANES), I32), jax.ShapeDtypeStruct((1, LANES), F32)],
        scratch_shapes=[pltpu.VMEM((1, LANES), F32), pltpu.VMEM((tm, tm), BF16)],
        compiler_params=_cparams(("arbitrary",), VMEM_LIMIT),
        name="router",
    )(x2d, norm_w.reshape(1, d), whi, wlo)


def _row_copy(src_ref, src_row, dst_ref, dst_row, sem):
    return pltpu.make_async_copy(_tile_rows(src_ref, src_row), _tile_rows(dst_ref, dst_row), sem)


def _dispatch_body(cnt_ref, eid_ref, rank_ref, h_ref, xs_ref, dest_ref, pstart, zbuf, sems):
    i = pl.program_id(0)
    tt = h_ref.shape[0] // ROW_SLABS
    blk = EXPERT_BLOCK
    n_slots = xs_ref.shape[0] // ROW_SLABS

    def pad_copy(pend):
        return pltpu.make_async_copy(zbuf, _tile_rows(xs_ref, pend - blk, blk), sems.at[2])

    @pl.when(i == 0)
    def _():
        zbuf[...] = jnp.zeros_like(zbuf)

        def start(e, acc):
            pstart[e] = acc
            pend = acc + ((cnt_ref[e] + (blk - 1)) // blk) * blk

            @pl.when(pend > acc)
            def _():
                pad_copy(pend).start()
            return pend

        used = lax.fori_loop(0, N_EXPERTS, start, jnp.int32(0))

        def tail_start(j, carry):
            pad_copy((j + 1) * blk).start()
            return carry

        lax.fori_loop(used // blk, n_slots // blk, tail_start, 0)

        def finish(e, acc):
            pend = acc + ((cnt_ref[e] + (blk - 1)) // blk) * blk

            @pl.when(pend > acc)
            def _():
                pad_copy(pend).wait()
            return pend

        lax.fori_loop(0, N_EXPERTS, finish, jnp.int32(0))

        def tail_wait(j, carry):
            pad_copy((j + 1) * blk).wait()
            return carry

        lax.fori_loop(used // blk, n_slots // blk, tail_wait, 0)

    def tokens(g, carry):
        t0 = pl.multiple_of(g * ROW_DMA_UNROLL, ROW_DMA_UNROLL)
        for u in range(ROW_DMA_UNROLL):
            for k in range(2):
                a = 2 * (t0 + u) + k
                d = pstart[eid_ref[a]] + rank_ref[a]
                dest_ref[a] = d
                _row_copy(h_ref, t0 + u, xs_ref, d, sems.at[k]).start(priority=k)
        return carry

    lax.fori_loop(0, tt // ROW_DMA_UNROLL, tokens, 0)
    for k in range(2):
        pltpu.make_async_copy(h_ref, _tile_rows(xs_ref, 0, tt), sems.at[k]).wait()


def _dispatch(counts, eid, rank, h3, n_slots):
    n = h3.shape[0] // ROW_SLABS
    tt = DISPATCH_TILE
    smem_blk = pl.BlockSpec((2 * tt,), lambda i, cnt: (i,), memory_space=pltpu.SMEM)
    grid_spec = pltpu.PrefetchScalarGridSpec(
        num_scalar_prefetch=1,
        grid=(n // tt,),
        in_specs=[smem_blk, smem_blk, pl.BlockSpec((tt * ROW_SLABS, LANES), lambda i, cnt: (i, 0))],
        out_specs=[pl.BlockSpec(memory_space=pl.ANY), smem_blk],
        scratch_shapes=[pltpu.SMEM((N_EXPERTS,), I32), pltpu.VMEM((EXPERT_BLOCK * ROW_SLABS, LANES), F32),
                        pltpu.SemaphoreType.DMA((3,))],
    )
    return pl.pallas_call(
        _dispatch_body,
        grid_spec=grid_spec,
        out_shape=[jax.ShapeDtypeStruct((n_slots * ROW_SLABS, LANES), F32), jax.ShapeDtypeStruct((2 * n,), I32)],
        compiler_params=_cparams(("arbitrary",), VMEM_LIMIT),
        name="dispatch",
    )(counts, eid, rank, h3)


def _experts_body(be_ref, nu_ref, xs_ref, w1_ref, w3_ref, w2_ref, ys_ref, w1b, w3b, w2b):
    i = pl.program_id(0)
    n_used = nu_ref[0]

    @pl.when(i < n_used)
    def _():
        changed = (i == 0) | (be_ref[i] != be_ref[jnp.maximum(i - 1, 0)])

        @pl.when(changed)
        def _():
            w1b[...] = w1_ref[0].astype(BF16)
            w3b[...] = w3_ref[0].astype(BF16)
            w2b[...] = w2_ref[0].astype(BF16)

        x = _load_row_tiles(xs_ref).astype(BF16)
        a = jnp.dot(x, w1b[...], preferred_element_type=F32)
        u = jnp.dot(x, w3b[...], preferred_element_type=F32)
        mid = (a * jax.nn.sigmoid(a) * u).astype(BF16)
        _store_row_tiles(ys_ref, jnp.dot(mid, w2b[...], preferred_element_type=F32))

    @pl.when(i >= n_used)
    def _():
        ys_ref[...] = jnp.zeros_like(ys_ref)


def _experts(block_expert, n_used, xs, w1, w3, w2):
    _, d, ff = w1.shape
    assert d == ROW_SLABS * LANES
    blk = EXPERT_BLOCK
    n_blocks = xs.shape[0] // (blk * ROW_SLABS)

    def slot_map(i, be, nu):
        return (jnp.minimum(i, nu[0] - 1), 0)

    def w_map(i, be, nu):
        return (be[jnp.minimum(i, nu[0] - 1)], 0, 0)

    grid_spec = pltpu.PrefetchScalarGridSpec(
        num_scalar_prefetch=2,
        grid=(n_blocks,),
        in_specs=[pl.BlockSpec((blk * ROW_SLABS, LANES), slot_map),
                  pl.BlockSpec((1, d, ff), w_map), pl.BlockSpec((1, d, ff), w_map),
                  pl.BlockSpec((1, ff, d), w_map)],
        out_specs=pl.BlockSpec((blk * ROW_SLABS, LANES), lambda i, be, nu: (i, 0)),
        scratch_shapes=[pltpu.VMEM((d, ff), BF16), pltpu.VMEM((d, ff), BF16), pltpu.VMEM((ff, d), BF16)],
    )
    return pl.pallas_call(
        _experts_body,
        grid_spec=grid_spec,
        out_shape=jax.ShapeDtypeStruct(xs.shape, F32),
        compiler_params=_cparams(("arbitrary",), VMEM_LIMIT),
        name="experts",
    )(block_expert, n_used, xs, w1, w3, w2)


def _combine_body(dest_ref, dest_next_ref, gate_ref, x_ref, ys_ref, out_ref, buf, sems):
    i = pl.program_id(0)
    n_steps = pl.num_programs(0)
    tt = x_ref.shape[0]

    def issue(dref, slot):
        def tokens(g, carry):
            t0 = pl.multiple_of(g * ROW_DMA_UNROLL, ROW_DMA_UNROLL)
            for u in range(ROW_DMA_UNROLL):
                for k in range(2):
                    _row_copy(ys_ref, dref[2 * (t0 + u) + k], buf.at[slot, k], t0 + u,
                              sems.at[slot, k]).start(priority=k)
            return carry

        lax.fori_loop(0, tt // ROW_DMA_UNROLL, tokens, 0)

    @pl.when(i == 0)
    def _():
        issue(dest_ref, 0)

    @pl.when(i + 1 < n_steps)
    def _():
        issue(dest_next_ref, (i + 1) % 2)

    slot = i % 2
    for k in range(2):
        pltpu.make_async_copy(_tile_rows(ys_ref, 0, tt), buf.at[slot, k], sems.at[slot, k]).wait()
    g = gate_ref[...]
    out_ref[...] = (x_ref[...] + g[:, 0:1] * _load_row_tiles(buf.at[slot, 0])
                    + g[:, 1:2] * _load_row_tiles(buf.at[slot, 1]))


def _combine(dest, gates, x2d, ys):
    n, d = x2d.shape
    tt = COMBINE_TILE
    row = lambda w: pl.BlockSpec((tt, w), lambda i: (i, 0))
    last = n // tt - 1
    return pl.pallas_call(
        _combine_body,
        grid=(n // tt,),
        in_specs=[pl.BlockSpec((2 * tt,), lambda i: (i,), memory_space=pltpu.SMEM),
                  pl.BlockSpec((2 * tt,), lambda i: (jnp.minimum(i + 1, last),), memory_space=pltpu.SMEM),
                  row(LANES), row(d), pl.BlockSpec(memory_space=pl.ANY)],
        out_specs=row(d),
        out_shape=jax.ShapeDtypeStruct((n, d), F32),
        scratch_shapes=[pltpu.VMEM((2, 2, tt * ROW_SLABS, LANES), F32), pltpu.SemaphoreType.DMA((2, 2))],
        compiler_params=_cparams(("arbitrary",), VMEM_LIMIT),
        name="combine",
    )(dest, dest, gates, x2d, ys)


def _moe(x2d, norm_w, w_group, w_router, w1, w3, w2):
    n, d = x2d.shape
    blk = EXPERT_BLOCK
    h3, gates, meta, cnt = _router(x2d, norm_w, w_group, w_router)
    counts = cnt[0, :N_EXPERTS].astype(I32)
    eid = meta[:, 0:2].reshape(-1)
    rank = meta[:, 2:4].reshape(-1)
    n_slots = 2 * n + N_EXPERTS * blk
    n_blocks = n_slots // blk
    pends = jnp.cumsum((counts + blk - 1) // blk * blk)
    block_start = jnp.arange(n_blocks, dtype=I32) * blk
    block_expert = jnp.minimum(jnp.sum((pends[None, :] <= block_start[:, None]).astype(I32), axis=1),
                               N_EXPERTS - 1)
    n_used = (pends[-1:] // blk).astype(I32)
    xs, dest = _dispatch(counts, eid, rank, h3, n_slots)
    ys = _experts(block_expert, n_used, xs, w1, w3, w2)
    return _combine(dest, gates, x2d, ys)


def _layer(x, mem, norm_mix_w, w_in, attn_q_norm_w, attn_k_norm_w, ret_decay_f, ret_decay_b, ret_gn_w, w_out,
           norm_mem_w, norm_memkv_w, w_mq, w_mkv, mem_q_norm_w, mem_k_norm_w, w_mo,
           norm_moe_w, w_group, w_router, w_exp_gate, w_exp_up, w_exp_down):
    b, s, d = x.shape
    x2d = x.reshape(b * s, d)
    q, k, v, q_pl, k_pl, v_pl, rq, rk, rv, rg = _in_proj(x2d, norm_mix_w, w_in, attn_q_norm_w, attn_k_norm_w)
    attn = _attention(q, k, v, q_pl, k_pl, v_pl, b, s)
    ret = _retention(rq, rk, rv, rg, ret_decay_f, ret_decay_b, ret_gn_w, b, s)
    x1 = _out_proj(attn, ret, x2d, w_out)
    mk, mv = _mem_kv(mem, norm_memkv_w, w_mkv, mem_k_norm_w)
    x2 = _mem_attn(x1.reshape(b, s, d), mk, mv, norm_mem_w, w_mq, mem_q_norm_w, w_mo)
    x3 = _moe(x2.reshape(b * s, d), norm_moe_w, w_group, w_router, w_exp_gate, w_exp_up, w_exp_down)
    return x3.reshape(b, s, d)


def kernel(x, mem, norm_mix_w, w_in, attn_q_norm_w, attn_k_norm_w, ret_decay_f, ret_decay_b, ret_gn_w, w_out,
           norm_mem_w, norm_memkv_w, w_mq, w_mkv, mem_q_norm_w, mem_k_norm_w, w_mo, norm_moe_w, w_group,
           w_router, w_exp_gate, w_exp_up, w_exp_down):
    depth = norm_mix_w.shape[0]
    for l in range(depth):
        x = _layer(x, mem, norm_mix_w[l], w_in[l], attn_q_norm_w[l], attn_k_norm_w[l], ret_decay_f[l],
                   ret_decay_b[l], ret_gn_w[l], w_out[l], norm_mem_w[l], norm_memkv_w[l], w_mq[l], w_mkv[l],
                   mem_q_norm_w[l], mem_k_norm_w[l], w_mo[l], norm_moe_w[l], w_group[l], w_router[l],
                   w_exp_gate[l], w_exp_up[l], w_exp_down[l])
    return x
```

```python
import functools

import numpy as np
import jax
import jax.numpy as jnp
from jax import lax
from jax.experimental import pallas as pl
from jax.experimental.pallas import tpu as pltpu

F32 = jnp.float32
BF16 = jnp.bfloat16
I32 = jnp.int32

NORM_EPS = 1e-6
GN_EPS = 1e-5
NEG_INF = -1e30

ATTN_HEADS = 8
ATTN_HEAD_DIM = 64
ATTN_WIDTH = ATTN_HEADS * ATTN_HEAD_DIM
ATTN_HALF = 64
PLANES = 16
ATTN_TILE = PLANES * ATTN_HALF
RET_HEADS = 4
RET_QK_DIM = 64
RET_V_DIM = 128
RET_QK_WIDTH = RET_HEADS * RET_QK_DIM
RET_V_WIDTH = RET_HEADS * RET_V_DIM
MEM_HEADS = 4
N_GROUPS = 4
EXPERTS_PER_GROUP = 8
N_EXPERTS = N_GROUPS * EXPERTS_PER_GROUP

LANES = 128
TOKEN_TILE = 512
MATMUL_TILE = 1024
ATTN_BLOCKS_PER_STEP = 4
LOG2_E = 1.4426950408889634
RET_CHUNK = 256
EXPERT_BLOCK = 512
DISPATCH_TILE = 1024
COMBINE_TILE = 256
ROW_DMA_UNROLL = 16
SLOTS_TILE = 4096
VMEM_LIMIT = 56 * 1024 * 1024


def _cparams(sem, vmem=None):
    return pltpu.CompilerParams(dimension_semantics=sem, vmem_limit_bytes=vmem)


def _split_hi_lo(x):
    hi = x.astype(BF16)
    lo = (x - hi.astype(F32)).astype(BF16)
    return hi, lo


def _rms(x, w):
    ms = jnp.mean(x * x, axis=-1, keepdims=True)
    return x * lax.rsqrt(ms + NORM_EPS) * w


ROW_SLABS = 8


def _store_row_tiles(ref, val):
    rows = val.shape[0]
    for j in range(ROW_SLABS):
        ref[pl.ds(j, rows, stride=ROW_SLABS), :] = val[:, j * LANES:(j + 1) * LANES]


def _load_row_tiles(ref):
    rows = ref.shape[0] // ROW_SLABS
    return jnp.concatenate([ref[pl.ds(j, rows, stride=ROW_SLABS), :] for j in range(ROW_SLABS)], axis=1)


def _tile_rows(ref, row, count=1):
    return ref.at[pl.ds(pl.multiple_of(row * ROW_SLABS, ROW_SLABS), count * ROW_SLABS)]


def _in_proj_body(x_ref, nw_ref, w_ref, qw_ref, kw_ref, g_ref,
                  q_out, k_out, v_out, qp_out, kp_out, vp_out, rq_out, rk_out, rv_out, rg_out, slabs):
    tm = x_ref.shape[0]
    h = _rms(x_ref[...], nw_ref[...]).astype(BF16)

    def emit(val, nat_out, plane_out):
        nat_out[...] = val.astype(BF16)
        for j in range(ATTN_WIDTH // LANES):
            slabs[j] = val[:, j * LANES:(j + 1) * LANES]
        for c in range(PLANES):
            for j in range(ATTN_WIDTH // LANES):
                rows = slabs[j, pl.ds(c, tm // PLANES, stride=PLANES), :]
                plane_out[0, c, :, j * LANES:(j + 1) * LANES] = rows.astype(BF16)

    def proj(a, b):
        return jnp.dot(h, w_ref[:, a:b], preferred_element_type=F32)

    def head_norm(p, w):
        s = jnp.dot((p * p).astype(BF16), g_ref[...], preferred_element_type=F32)
        return p * lax.rsqrt(s * (1.0 / ATTN_HEAD_DIM) + NORM_EPS) * w

    a = ATTN_WIDTH
    emit(head_norm(proj(0, a), qw_ref[...]), q_out, qp_out)
    emit(head_norm(proj(a, 2 * a), kw_ref[...]), k_out, kp_out)
    emit(proj(2 * a, 3 * a), v_out, vp_out)
    c = 3 * a
    rq_out[...] = proj(c, c + RET_QK_WIDTH).astype(BF16)
    rk_out[...] = proj(c + RET_QK_WIDTH, c + 2 * RET_QK_WIDTH).astype(BF16)
    c += 2 * RET_QK_WIDTH
    rv_out[...] = proj(c, c + RET_V_WIDTH).astype(BF16)
    rg_out[...] = proj(c + RET_V_WIDTH, c + 2 * RET_V_WIDTH).astype(BF16)


def _in_proj(x2d, norm_w, w_in, q_norm_w, k_norm_w):
    n, d = x2d.shape
    cols = w_in.shape[1]
    tm = MATMUL_TILE
    qw = (jnp.tile(q_norm_w, ATTN_HEADS) * (ATTN_HEAD_DIM ** -0.5 * LOG2_E)).reshape(1, ATTN_WIDTH)
    kw = jnp.tile(k_norm_w, ATTN_HEADS).reshape(1, ATTN_WIDTH)
    head_of = np.arange(ATTN_WIDTH) // ATTN_HEAD_DIM
    gmat = jnp.asarray(head_of[:, None] == head_of[None, :], dtype=BF16)
    row = lambda w: pl.BlockSpec((tm, w), lambda i: (i, 0))
    full = lambda r, c: pl.BlockSpec((r, c), lambda i: (0, 0))
    per_tile = ATTN_TILE // tm
    plane = pl.BlockSpec((1, PLANES, tm // PLANES, ATTN_WIDTH), lambda i: (i // per_tile, 0, i % per_tile, 0))
    plane_shape = jax.ShapeDtypeStruct((n // ATTN_TILE, PLANES, ATTN_TILE // PLANES, ATTN_WIDTH), BF16)
    ret_widths = (RET_QK_WIDTH, RET_QK_WIDTH, RET_V_WIDTH, RET_V_WIDTH)
    return pl.pallas_call(
        _in_proj_body,
        grid=(n // tm,),
        in_specs=[row(d), full(1, d), full(d, cols), full(1, ATTN_WIDTH), full(1, ATTN_WIDTH),
                  full(ATTN_WIDTH, ATTN_WIDTH)],
        out_specs=[row(ATTN_WIDTH)] * 3 + [plane] * 3 + [row(w) for w in ret_widths],
        out_shape=([jax.ShapeDtypeStruct((n, ATTN_WIDTH), BF16)] * 3 + [plane_shape] * 3
                   + [jax.ShapeDtypeStruct((n, w), BF16) for w in ret_widths]),
        scratch_shapes=[pltpu.VMEM((ATTN_WIDTH // LANES, tm, LANES), F32)],
        compiler_params=_cparams(("parallel",), VMEM_LIMIT),
        name="in_proj",
    )(x2d, norm_w.reshape(1, d), w_in.astype(BF16), qw, kw, gmat)


def _attn_bias(dilation, interleave=1):
    half = ATTN_HALF
    idx = np.arange(half)
    sub = (idx % (half // interleave)) * interleave + idx // (half // interleave)
    qi = sub[:, None]
    kc = (np.arange(3)[:, None] * half + sub[None, :]).reshape(1, -1)
    seg = np.repeat(np.arange(3), half)[None, :]
    delta = kc - half - qi
    band = np.abs(delta) <= half
    slopes = np.exp2(-8.0 * np.arange(1, ATTN_HEADS + 1) / ATTN_HEADS)
    valid = [band & (seg >= 1), band, band & (seg <= 1)]
    out = np.empty((ATTN_HEADS // 2, 3, 2 * half, 3 * half), np.float32)
    for p in range(ATTN_HEADS // 2):
        for v in range(3):
            for s in range(2):
                b = -slopes[2 * p + s] * LOG2_E * (dilation * np.abs(delta)).astype(np.float32)
                out[p, v, s * half:(s + 1) * half] = np.where(valid[v], b, NEG_INF)
    return jnp.asarray(out.reshape(-1, 2 * half, 3 * half))


def _attend(chains):
    half = ATTN_HALF
    lo_lanes = lax.broadcasted_iota(I32, (half, LANES), 1) < ATTN_HEAD_DIM
    ones = jnp.ones((3 * half, LANES), BF16)
    scores = []
    for q, k, v, bias in chains:
        zero = jnp.zeros_like(q)
        qs = jnp.concatenate([jnp.where(lo_lanes, q, zero), jnp.where(lo_lanes, zero, q)], axis=0)
        scores.append(lax.dot_general(qs, k, (((1,), (1,)), ((), ())), preferred_element_type=F32))
    probs, maxes = [], []
    for (q, k, v, bias), s in zip(chains, scores):
        s = s + bias
        m = jnp.max(s, axis=-1, keepdims=True)
        probs.append(jnp.exp2(s - m).astype(BF16))
        maxes.append(m)
    results = []
    for (q, k, v, bias), e in zip(chains, probs):
        vext = jnp.concatenate([v, ones], axis=1)
        results.append(jnp.dot(e, vext, preferred_element_type=F32))
    outs = []
    for r, m in zip(results, maxes):
        acc = jnp.where(lo_lanes, r[:half, :LANES], r[half:, :LANES])
        denom = jnp.where(lo_lanes, r[:half, LANES:], r[half:, LANES:])
        mm = jnp.where(lo_lanes, m[:half], m[half:])
        outs.append((acc / denom, mm + jnp.log2(denom)))
    return outs


def _merge(oa, la, ob, lb):
    m = jnp.maximum(la, lb)
    ea = jnp.exp2(la - m)
    eb = jnp.exp2(lb - m)
    den = ea + eb
    return (ea * oa + eb * ob) / den, m + jnp.log2(den)


def _attn_body(q_ref, kp_ref, km_ref, kn_ref, vp_ref, vm_ref, vn_ref,
               qpl_ref, kplp_ref, kplm_ref, kpln_ref, vplp_ref, vplm_ref, vpln_ref,
               b1_ref, b4_ref, b16_ref, o_ref, kcat, vcat, o_far, l_far):
    half = ATTN_HALF
    t = pl.program_id(1)
    n_tiles = pl.num_programs(1)
    tile = q_ref.shape[1]
    pairs = ATTN_HEADS // 2
    piece = half // 4

    kcat[0:half] = kp_ref[0]
    kcat[half:half + tile] = km_ref[0]
    kcat[half + tile:] = kn_ref[0]
    vcat[0:half] = vp_ref[0]
    vcat[half:half + tile] = vm_ref[0]
    vcat[half + tile:] = vn_ref[0]

    def edge(first, last):
        return jnp.where(first, 0, jnp.where(last, 2, 1))

    def far_step(r, carry):
        v16 = edge(t == 0, t == n_tiles - 1)
        for pr in range(pairs):
            cs = slice(pr * LANES, (pr + 1) * LANES)
            chains = []
            for m in range(4):
                c = r + 4 * m
                k3 = jnp.concatenate([kplp_ref[0, c, :, cs], kplm_ref[0, c, :, cs], kpln_ref[0, c, :, cs]], axis=0)
                v3 = jnp.concatenate([vplp_ref[0, c, :, cs], vplm_ref[0, c, :, cs], vpln_ref[0, c, :, cs]], axis=0)
                chains.append((qpl_ref[0, c, :, cs], k3, v3, b16_ref[pr * 3 + v16]))
            for nb in range(4):
                def rows(main, before, after, a0):
                    ref, lo = (before, a0 + half) if a0 < 0 else (after, a0 - half) if a0 >= half else (main, a0)
                    return [ref[0, r + 4 * m, lo:lo + piece, cs] for m in range(4)]
                a0 = nb * piece
                q4 = jnp.concatenate(rows(qpl_ref, None, None, a0), axis=0)
                k4 = jnp.concatenate(sum((rows(kplm_ref, kplp_ref, kpln_ref, a0 + d) for d in (-piece, 0, piece)), []),
                                     axis=0)
                v4 = jnp.concatenate(sum((rows(vplm_ref, vplp_ref, vpln_ref, a0 + d) for d in (-piece, 0, piece)), []),
                                     axis=0)
                v4e = edge((t == 0) & (nb == 0), (t == n_tiles - 1) & (nb == 3))
                chains.append((q4, k4, v4, b4_ref[pr * 3 + v4e]))
            res = _attend(chains)
            for m in range(4):
                o16, l16 = res[m]
                o_rows, l_rows = [], []
                for nb in range(4):
                    o4, l4 = res[4 + nb]
                    sl = slice(nb * piece, (nb + 1) * piece)
                    s4 = slice(m * piece, (m + 1) * piece)
                    om, lm = _merge(o16[sl], l16[sl], o4[s4], l4[s4])
                    o_rows.append(om)
                    l_rows.append(lm)
                dst = pl.ds(r + 4 * m, half, stride=PLANES)
                o_far[pr, dst, :] = jnp.concatenate(o_rows, axis=0)
                l_far[pr, dst, :] = jnp.concatenate(l_rows, axis=0)
        return carry

    lax.fori_loop(0, 4, far_step, 0)

    unroll = ATTN_BLOCKS_PER_STEP
    n_blocks = n_tiles * (tile // half)

    def near_step(it, carry):
        chains, where = [], []
        for u in range(unroll):
            jb = it * unroll + u
            r0 = pl.multiple_of(jb * half, half)
            gb = t * (tile // half) + jb
            variant = edge(gb == 0, gb == n_blocks - 1)
            for pr in range(pairs):
                cs = slice(pr * LANES, (pr + 1) * LANES)
                chains.append((q_ref[0, pl.ds(r0, half), cs], kcat[pl.ds(r0, 3 * half), cs],
                               vcat[pl.ds(r0, 3 * half), cs], b1_ref[pr * 3 + variant]))
                where.append((r0, pr, cs))
        for (r0, pr, cs), (o1, l1) in zip(where, _attend(chains)):
            o, _ = _merge(o1, l1, o_far[pr, pl.ds(r0, half), :], l_far[pr, pl.ds(r0, half), :])
            o_ref[0, pl.ds(r0, half), cs] = o.astype(o_ref.dtype)
        return carry

    lax.fori_loop(0, tile // half // unroll, near_step, 0)


def _attention(q, k, v, q_pl, k_pl, v_pl, b, s):
    w = ATTN_WIDTH
    half = ATTN_HALF
    tile = ATTN_TILE
    n_tiles = s // tile
    hb = tile // half
    last = s // half - 1
    main = pl.BlockSpec((1, tile, w), lambda bb, t: (bb, t, 0))
    prev = pl.BlockSpec((1, half, w), lambda bb, t: (bb, jnp.maximum(t * hb - 1, 0), 0))
    nxt = pl.BlockSpec((1, half, w), lambda bb, t: (bb, jnp.minimum((t + 1) * hb, last), 0))
    pshape = (1, PLANES, tile // PLANES, w)
    pl_main = pl.BlockSpec(pshape, lambda bb, t: (bb * n_tiles + t, 0, 0, 0))
    pl_prev = pl.BlockSpec(pshape, lambda bb, t: (bb * n_tiles + jnp.maximum(t - 1, 0), 0, 0, 0))
    pl_next = pl.BlockSpec(pshape, lambda bb, t: (bb * n_tiles + jnp.minimum(t + 1, n_tiles - 1), 0, 0, 0))
    biases = [_attn_bias(1), _attn_bias(4, interleave=4), _attn_bias(16)]
    bias_spec = pl.BlockSpec(biases[0].shape, lambda bb, t: (0, 0, 0))
    nat = lambda a: a.reshape(b, s, w)
    out = pl.pallas_call(
        _attn_body,
        grid=(b, n_tiles),
        in_specs=[main, prev, main, nxt, prev, main, nxt,
                  pl_main, pl_prev, pl_main, pl_next, pl_prev, pl_main, pl_next,
                  bias_spec, bias_spec, bias_spec],
        out_specs=main,
        out_shape=jax.ShapeDtypeStruct((b, s, w), BF16),
        scratch_shapes=[pltpu.VMEM((tile + 2 * half, w), BF16), pltpu.VMEM((tile + 2 * half, w), BF16),
                        pltpu.VMEM((ATTN_HEADS // 2, tile, LANES), F32),
                        pltpu.VMEM((ATTN_HEADS // 2, tile, LANES), F32)],
        compiler_params=_cparams(("parallel", "parallel"), VMEM_LIMIT),
        name="attention",
    )(nat(q), nat(k), nat(k), nat(k), nat(v), nat(v), nat(v),
      q_pl, k_pl, k_pl, k_pl, v_pl, v_pl, v_pl, *biases)
    return out.reshape(b * s, w)


def _ret_body(lg_ref, q_ref, k_ref, v_ref, g_ref, gnw_ref, out_ref,
              dmat, qdec, kdec, cdec, fstate, rstate, rall, *, chunk):
    c = chunk
    b = pl.program_id(0)
    ph = pl.program_id(1)
    n = pl.program_id(2)
    n_chunks = pl.num_programs(2)
    k_scale = RET_QK_DIM ** -0.5
    pairs = RET_HEADS // 2
    pw = 2 * RET_QK_DIM
    vw = 2 * RET_V_DIM
    first = lax.broadcasted_iota(I32, (c, pw), 1) < RET_QK_DIM
    diag = ((lax.broadcasted_iota(I32, (pw, vw), 0) < RET_QK_DIM)
            == (lax.broadcasted_iota(I32, (pw, vw), 1) < RET_V_DIM))

    @pl.when((b == 0) & (ph == 0) & (n == 0))
    def _init_tables():
        ii = lax.broadcasted_iota(I32, (c, c), 0)
        jj = lax.broadcasted_iota(I32, (c, c), 1)
        fwd = (ii - jj).astype(F32)
        ri = lax.broadcasted_iota(I32, (c, pw), 0).astype(F32)
        top = lax.broadcasted_iota(I32, (pw, vw), 0) < RET_QK_DIM
        for h in range(RET_HEADS):
            dmat[h] = jnp.where(ii >= jj, jnp.exp(lg_ref[0, h] * fwd), jnp.exp(-lg_ref[1, h] * fwd)) * k_scale
        for pr in range(pairs):
            lf = jnp.where(first, lg_ref[0, 2 * pr], lg_ref[0, 2 * pr + 1])
            lb = jnp.where(first, lg_ref[1, 2 * pr], lg_ref[1, 2 * pr + 1])
            qdec[0, pr] = jnp.exp(lf * (ri + 1.0))
            qdec[1, pr] = jnp.exp(lb * (c - ri))
            kdec[0, pr] = jnp.exp(lf * (c - 1.0 - ri)) * k_scale
            kdec[1, pr] = jnp.exp(lb * ri) * k_scale
            for direction in range(2):
                cdec[direction, pr] = jnp.exp(jnp.where(top, lg_ref[direction, 2 * pr],
                                                        lg_ref[direction, 2 * pr + 1]) * c)

    def kv_update(direction, pr):
        kp = k_ref[0, :, pr * pw:(pr + 1) * pw].astype(F32)
        ks = (kp * kdec[direction, pr]).astype(BF16)
        new = lax.dot_general(ks, v_ref[0, :, pr * vw:(pr + 1) * vw], (((0,), (0,)), ((), ())),
                              preferred_element_type=F32)
        return jnp.where(diag, new, 0.0)

    @pl.when(ph == 0)
    def _right_to_left():
        @pl.when(n == 0)
        def _():
            rstate[...] = jnp.zeros_like(rstate)

        ci = n_chunks - 1 - n
        for pr in range(pairs):
            st = rstate[pr]
            rall[ci, pr] = st
            rstate[pr] = st * cdec[1, pr] + kv_update(1, pr)

    @pl.when(ph == 1)
    def _left_to_right():
        @pl.when(n == 0)
        def _():
            fstate[...] = jnp.zeros_like(fstate)

        for pr in range(pairs):
            qp = q_ref[0, :, pr * pw:(pr + 1) * pw]
            kp = k_ref[0, :, pr * pw:(pr + 1) * pw]
            qf = qp.astype(F32)
            qcat = jnp.concatenate([(qf * qdec[0, pr]).astype(BF16), (qf * qdec[1, pr]).astype(BF16)], axis=1)
            st = fstate[pr]
            states = jnp.concatenate([st, rall[n, pr]], axis=0).astype(BF16)
            cross = jnp.dot(qcat, states, preferred_element_type=F32)
            fstate[pr] = st * cdec[0, pr] + kv_update(0, pr)
            zero = jnp.zeros_like(qp)
            for a in range(2):
                h = 2 * pr + a
                vs = slice(h * RET_V_DIM, (h + 1) * RET_V_DIM)
                qm = jnp.where(first, qp, zero) if a == 0 else jnp.where(first, zero, qp)
                s = lax.dot_general(qm, kp, (((1,), (1,)), ((), ())), preferred_element_type=F32) * dmat[h]
                y = jnp.dot(s.astype(BF16), v_ref[0, :, vs], preferred_element_type=F32)
                y = y + cross[:, a * RET_V_DIM:(a + 1) * RET_V_DIM]
                mu = jnp.mean(y, axis=-1, keepdims=True)
                yc = y - mu
                var = jnp.mean(yc * yc, axis=-1, keepdims=True)
                yn = yc * lax.rsqrt(var + GN_EPS) * gnw_ref[:, vs]
                gate = g_ref[0, :, vs].astype(F32)
                out_ref[0, :, vs] = (gate * jax.nn.sigmoid(gate) * yn).astype(out_ref.dtype)


def _retention(rq, rk, rv, rg, decay_f, decay_b, gn_w, b, s):
    c = RET_CHUNK
    nc = s // c
    lg = jnp.stack([jax.nn.log_sigmoid(decay_f.astype(F32)), jax.nn.log_sigmoid(decay_b.astype(F32))])
    qk_w, v_w = RET_QK_WIDTH, RET_V_WIDTH
    pairs, pair_qk, pair_v = RET_HEADS // 2, 2 * RET_QK_DIM, 2 * RET_V_DIM

    def both(bb, ph, n):
        return (bb, jnp.where(ph == 0, nc - 1 - n, n), 0)

    def fwd_only(bb, ph, n):
        return (bb, jnp.where(ph == 0, 0, n), 0)

    out = pl.pallas_call(
        functools.partial(_ret_body, chunk=c),
        grid=(b, 2, nc),
        in_specs=[pl.BlockSpec(memory_space=pltpu.SMEM),
                  pl.BlockSpec((1, c, qk_w), fwd_only),
                  pl.BlockSpec((1, c, qk_w), both),
                  pl.BlockSpec((1, c, v_w), both),
                  pl.BlockSpec((1, c, v_w), fwd_only),
                  pl.BlockSpec((1, v_w), lambda bb, ph, n: (0, 0))],
        out_specs=pl.BlockSpec((1, c, v_w), fwd_only),
        out_shape=jax.ShapeDtypeStruct((b, s, v_w), BF16),
        scratch_shapes=[pltpu.VMEM((RET_HEADS, c, c), F32),
                        pltpu.VMEM((2, pairs, c, pair_qk), F32),
                        pltpu.VMEM((2, pairs, c, pair_qk), F32),
                        pltpu.VMEM((2, pairs, pair_qk, pair_v), F32),
                        pltpu.VMEM((pairs, pair_qk, pair_v), F32),
                        pltpu.VMEM((pairs, pair_qk, pair_v), F32),
                        pltpu.VMEM((nc, pairs, pair_qk, pair_v), F32)],
        compiler_params=_cparams(("arbitrary", "arbitrary", "arbitrary"), VMEM_LIMIT),
        name="retention",
    )(lg, rq.reshape(b, s, qk_w), rk.reshape(b, s, qk_w), rv.reshape(b, s, v_w), rg.reshape(b, s, v_w),
      gn_w.reshape(1, v_w))
    return out.reshape(b * s, v_w)


def _out_proj_body(attn_ref, ret_ref, x_ref, w_ref, out_ref):
    y = jnp.dot(attn_ref[...], w_ref[:ATTN_WIDTH], preferred_element_type=F32)
    y = y + jnp.dot(ret_ref[...], w_ref[ATTN_WIDTH:], preferred_element_type=F32)
    out_ref[...] = x_ref[...] + y


def _out_proj(attn, ret, x2d, w_out):
    n, d = x2d.shape
    tm = MATMUL_TILE
    row = lambda w: pl.BlockSpec((tm, w), lambda i: (i, 0))
    return pl.pallas_call(
        _out_proj_body,
        grid=(n // tm,),
        in_specs=[row(ATTN_WIDTH), row(RET_V_WIDTH), row(d), pl.BlockSpec(w_out.shape, lambda i: (0, 0))],
        out_specs=row(d),
        out_shape=jax.ShapeDtypeStruct((n, d), F32),
        compiler_params=_cparams(("parallel",), VMEM_LIMIT),
        name="out_proj",
    )(attn, ret, x2d, w_out.astype(BF16))


def _mem_kv_body(mem_ref, nw_ref, w_ref, kw_ref, k_out, v_out):
    d = mem_ref.shape[-1]
    hd = d // MEM_HEADS
    h = _rms(mem_ref[0], nw_ref[...]).astype(BF16)
    kv = jnp.dot(h, w_ref[...], preferred_element_type=F32)
    for i in range(MEM_HEADS):
        k_out[0, :, i * hd:(i + 1) * hd] = _rms(kv[:, i * hd:(i + 1) * hd], kw_ref[...]).astype(BF16)
    v_out[0] = kv[:, d:].astype(BF16)


def _mem_kv(mem, norm_w, w_mkv, k_norm_w):
    b, m, d = mem.shape
    return pl.pallas_call(
        _mem_kv_body,
        grid=(b,),
        in_specs=[pl.BlockSpec((1, m, d), lambda i: (i, 0, 0)),
                  pl.BlockSpec((1, d), lambda i: (0, 0)),
                  pl.BlockSpec((d, 2 * d), lambda i: (0, 0)),
                  pl.BlockSpec((1, d // MEM_HEADS), lambda i: (0, 0))],
        out_specs=[pl.BlockSpec((1, m, d), lambda i: (i, 0, 0))] * 2,
        out_shape=[jax.ShapeDtypeStruct((b, m, d), BF16)] * 2,
        compiler_params=_cparams(("parallel",), VMEM_LIMIT),
        name="mem_kv",
    )(mem, norm_w.reshape(1, d), w_mkv.astype(BF16), k_norm_w.reshape(1, -1))


def _mem_attn_body(x_ref, nw_ref, wq_ref, qw_ref, k_ref, v_ref, wo_ref, out_ref):
    d = x_ref.shape[-1]
    hd = d // MEM_HEADS
    x = x_ref[0]
    h = _rms(x, nw_ref[...]).astype(BF16)
    q = jnp.dot(h, wq_ref[...], preferred_element_type=F32)
    heads = []
    for i in range(MEM_HEADS):
        cs = slice(i * hd, (i + 1) * hd)
        qn = _rms(q[:, cs], qw_ref[...]).astype(BF16)
        s = lax.dot_general(qn, k_ref[0, :, cs], (((1,), (1,)), ((), ())), preferred_element_type=F32)
        e = jnp.exp(s - jnp.max(s, axis=-1, keepdims=True))
        o = jnp.dot(e.astype(BF16), v_ref[0, :, cs], preferred_element_type=F32)
        heads.append((o / jnp.sum(e, axis=-1, keepdims=True)).astype(BF16))
    o = jnp.concatenate(heads, axis=1)
    out_ref[0] = x + jnp.dot(o, wo_ref[...], preferred_element_type=F32)


def _mem_attn(x1, mk, mv, norm_w, w_mq, q_norm_w, w_mo):
    b, s, d = x1.shape
    m = mk.shape[1]
    tm = MATMUL_TILE
    hd = d // MEM_HEADS
    qw = (q_norm_w * (hd ** -0.5)).reshape(1, hd)
    tok = pl.BlockSpec((1, tm, d), lambda bb, i: (bb, i, 0))
    const = lambda r, c: pl.BlockSpec((r, c), lambda bb, i: (0, 0))
    mem = pl.BlockSpec((1, m, d), lambda bb, i: (bb, 0, 0))
    return pl.pallas_call(
        _mem_attn_body,
        grid=(b, s // tm),
        in_specs=[tok, const(1, d), const(d, d), const(1, hd), mem, mem, const(d, d)],
        out_specs=tok,
        out_shape=jax.ShapeDtypeStruct((b, s, d), F32),
        compiler_params=_cparams(("parallel", "parallel"), VMEM_LIMIT),
        name="mem_attn",
    )(x1, norm_w.reshape(1, d), w_mq.astype(BF16), qw, mk, mv, w_mo.astype(BF16))


def _router_body(x_ref, nw_ref, whi_ref, wlo_ref, h_out, gate_out, meta_out, cnt_out, carry, below):
    tm = x_ref.shape[0]
    i = pl.program_id(0)

    @pl.when(i == 0)
    def _():
        carry[...] = jnp.zeros_like(carry)
        rr = lax.broadcasted_iota(I32, (tm, tm), 0)
        cc = lax.broadcasted_iota(I32, (tm, tm), 1)
        below[...] = (cc < rr).astype(F32).astype(BF16)

    h = _rms(x_ref[...], nw_ref[...])
    _store_row_tiles(h_out, h)
    hi, lo = _split_hi_lo(h)
    logits = (jnp.dot(hi, whi_ref[...], preferred_element_type=F32)
              + jnp.dot(lo, whi_ref[...], preferred_element_type=F32)
              + jnp.dot(hi, wlo_ref[...], preferred_element_type=F32))
    lane = lax.broadcasted_iota(I32, (tm, LANES), 1)
    lane_f = lane.astype(F32)

    def argmax_lanes(vals):
        top = jnp.max(vals, axis=-1, keepdims=True)
        idx = jnp.min(jnp.where(vals == top, lane_f, float(LANES)), axis=-1, keepdims=True)
        return top, idx

    is_group = lane < N_GROUPS
    g_top, grp = argmax_lanes(jnp.where(is_group, logits, -jnp.inf))
    g_gate = 1.0 / jnp.sum(jnp.where(is_group, jnp.exp(logits - g_top), 0.0), axis=-1, keepdims=True)
    first = N_GROUPS + EXPERTS_PER_GROUP * grp
    in_group = (lane_f >= first) & (lane_f < first + EXPERTS_PER_GROUP)
    el = jnp.where(in_group, logits, -jnp.inf)
    t1, i1 = argmax_lanes(el)
    t2, i2 = argmax_lanes(jnp.where(lane_f == i1, -jnp.inf, el))
    z = jnp.exp(t2 - t1)
    g1 = g_gate / (1.0 + z)
    g2 = g_gate * z / (1.0 + z)
    e1 = i1 - N_GROUPS
    e2 = i2 - N_GROUPS

    oh1 = (lane_f == e1)
    oh2 = (lane_f == e2)
    oh1b = oh1.astype(F32).astype(BF16)
    oh2b = oh2.astype(F32).astype(BF16)
    pre1 = jnp.dot(below[...], oh1b, preferred_element_type=F32)
    pre2 = jnp.dot(below[...], oh2b, preferred_element_type=F32)
    cnt1 = jnp.sum(oh1.astype(F32), axis=0, keepdims=True)
    cnt2 = jnp.sum(oh2.astype(F32), axis=0, keepdims=True)
    base = carry[...]
    r1 = jnp.sum(jnp.where(oh1, pre1 + base, 0.0), axis=-1, keepdims=True)
    r2 = jnp.sum(jnp.where(oh2, pre2 + base + cnt1, 0.0), axis=-1, keepdims=True)
    total = base + cnt1 + cnt2
    carry[...] = total
    cnt_out[...] = total

    gate_out[...] = jnp.where(lane == 0, g1, jnp.where(lane == 1, g2, 0.0))
    meta = jnp.where(lane == 0, e1, jnp.where(lane == 1, e2, jnp.where(lane == 2, r1, jnp.where(lane == 3, r2, 0.0))))
    meta_out[...] = meta.astype(I32)


def _router(x2d, norm_w, w_group, w_router):
    n, d = x2d.shape
    tm = TOKEN_TILE
    w_all = jnp.concatenate([w_group, w_router.transpose(1, 0, 2).reshape(d, N_EXPERTS)], axis=1)
    w_all = jnp.pad(w_all, ((0, 0), (0, LANES - w_all.shape[1])))
    whi = w_all.astype(BF16)
    wlo = (w_all - whi.astype(F32)).astype(BF16)
    row = lambda w: pl.BlockSpec((tm, w), lambda i: (i, 0))
    const = lambda r, c: pl.BlockSpec((r, c), lambda i: (0, 0))
    return pl.pallas_call(
        _router_body,
        grid=(n // tm,),
        in_specs=[row(d), const(1, d), const(d, LANES), const(d, LANES)],
        out_specs=[pl.BlockSpec((tm * ROW_SLABS, LANES), lambda i: (i, 0)), row(LANES), row(LANES), const(1, LANES)],
        out_shape=[jax.ShapeDtypeStruct((n * ROW_SLABS, LANES), F32), jax.ShapeDtypeStruct((n, LANES), F32),
                   jax.ShapeDtypeStruct((n, LANES), I32), jax.ShapeDtypeStruct((1, LANES), F32)],
        scratch_shapes=[pltpu.VMEM((1, LANES), F32), pltpu.VMEM((tm, tm), BF16)],
        compiler_params=_cparams(("arbitrary",), VMEM_LIMIT),
        name="router",
    )(x2d, norm_w.reshape(1, d), whi, wlo)


def _row_copy(src_ref, src_row, dst_ref, dst_row, sem):
    return pltpu.make_async_copy(_tile_rows(src_ref, src_row), _tile_rows(dst_ref, dst_row), sem)


def _slots_body(meta_ref, cnt_ref, dest_ref):
    tm = meta_ref.shape[0]
    blk = float(EXPERT_BLOCK)
    lane = lax.broadcasted_iota(I32, (tm, LANES), 1)
    lane_f = lane.astype(F32)
    blocks = jnp.floor((cnt_ref[...] + (blk - 1.0)) * (1.0 / blk))
    rr = lax.broadcasted_iota(I32, (LANES, LANES), 0)
    cc = lax.broadcasted_iota(I32, (LANES, LANES), 1)
    before = (rr < cc).astype(F32).astype(BF16)
    first_block = jnp.dot(jnp.broadcast_to(blocks, (8, LANES)).astype(BF16), before,
                          preferred_element_type=F32)[0:1]
    start = first_block * blk
    meta = meta_ref[...].astype(F32)

    def slot_of(k):
        e = jnp.sum(jnp.where(lane == k, meta, 0.0), axis=-1, keepdims=True)
        r = jnp.sum(jnp.where(lane == 2 + k, meta, 0.0), axis=-1, keepdims=True)
        return jnp.sum(jnp.where(lane_f == e, start, 0.0), axis=-1, keepdims=True) + r

    dest_ref[...] = jnp.where(lane == 0, slot_of(0), jnp.where(lane == 1, slot_of(1), 0.0)).astype(I32)


def _slots(meta, cnt):
    n = meta.shape[0]
    tm = min(SLOTS_TILE, n)
    row = pl.BlockSpec((tm, LANES), lambda i: (i, 0))
    return pl.pallas_call(
        _slots_body,
        grid=(n // tm,),
        in_specs=[row, pl.BlockSpec((1, LANES), lambda i: (0, 0))],
        out_specs=row,
        out_shape=jax.ShapeDtypeStruct((n, LANES), I32),
        compiler_params=_cparams(("parallel",), VMEM_LIMIT),
        name="slots",
    )(meta, cnt)


def _dispatch_body(cnt_ref, dest_ref, h_ref, xs_ref, zbuf, sems):
    i = pl.program_id(0)
    tt = h_ref.shape[0] // ROW_SLABS
    blk = EXPERT_BLOCK
    n_slots = xs_ref.shape[0] // ROW_SLABS

    def pad_copy(pend):
        return pltpu.make_async_copy(zbuf, _tile_rows(xs_ref, pend - blk, blk), sems.at[2])

    @pl.when(i == 0)
    def _():
        zbuf[...] = jnp.zeros_like(zbuf)

        def start(e, acc):
            pend = acc + ((cnt_ref[e] + (blk - 1)) // blk) * blk

            @pl.when(pend > acc)
            def _():
                pad_copy(pend).start()
            return pend

        used = lax.fori_loop(0, N_EXPERTS, start, jnp.int32(0))

        def tail_start(j, carry):
            pad_copy((j + 1) * blk).start()
            return carry

        lax.fori_loop(used // blk, n_slots // blk, tail_start, 0)

        def finish(e, acc):
            pend = acc + ((cnt_ref[e] + (blk - 1)) // blk) * blk

            @pl.when(pend > acc)
            def _():
                pad_copy(pend).wait()
            return pend

        lax.fori_loop(0, N_EXPERTS, finish, jnp.int32(0))

        def tail_wait(j, carry):
            pad_copy((j + 1) * blk).wait()
            return carry

        lax.fori_loop(used // blk, n_slots // blk, tail_wait, 0)

    def tokens(g, carry):
        t0 = pl.multiple_of(g * ROW_DMA_UNROLL, ROW_DMA_UNROLL)
        for u in range(ROW_DMA_UNROLL):
            for k in range(2):
                _row_copy(h_ref, t0 + u, xs_ref, dest_ref[2 * (t0 + u) + k], sems.at[k]).start(priority=k)
        return carry

    lax.fori_loop(0, tt // ROW_DMA_UNROLL, tokens, 0)
    for k in range(2):
        pltpu.make_async_copy(h_ref, _tile_rows(xs_ref, 0, tt), sems.at[k]).wait()


def _dispatch(counts, dest, h3, n_slots):
    n = h3.shape[0] // ROW_SLABS
    tt = DISPATCH_TILE
    grid_spec = pltpu.PrefetchScalarGridSpec(
        num_scalar_prefetch=1,
        grid=(n // tt,),
        in_specs=[pl.BlockSpec((2 * tt,), lambda i, cnt: (i,), memory_space=pltpu.SMEM),
                  pl.BlockSpec((tt * ROW_SLABS, LANES), lambda i, cnt: (i, 0))],
        out_specs=pl.BlockSpec(memory_space=pl.ANY),
        scratch_shapes=[pltpu.VMEM((EXPERT_BLOCK * ROW_SLABS, LANES), F32), pltpu.SemaphoreType.DMA((3,))],
    )
    return pl.pallas_call(
        _dispatch_body,
        grid_spec=grid_spec,
        out_shape=jax.ShapeDtypeStruct((n_slots * ROW_SLABS, LANES), F32),
        compiler_params=_cparams(("arbitrary",), VMEM_LIMIT),
        name="dispatch",
    )(counts, dest, h3)


def _experts_body(be_ref, nu_ref, xs_ref, w1_ref, w3_ref, w2_ref, ys_ref, w1b, w3b, w2b):
    i = pl.program_id(0)
    n_used = nu_ref[0]

    @pl.when(i < n_used)
    def _():
        changed = (i == 0) | (be_ref[i] != be_ref[jnp.maximum(i - 1, 0)])

        @pl.when(changed)
        def _():
            w1b[...] = w1_ref[0].astype(BF16)
            w3b[...] = w3_ref[0].astype(BF16)
            w2b[...] = w2_ref[0].astype(BF16)

        x = _load_row_tiles(xs_ref).astype(BF16)
        a = jnp.dot(x, w1b[...], preferred_element_type=F32)
        u = jnp.dot(x, w3b[...], preferred_element_type=F32)
        mid = (a * jax.nn.sigmoid(a) * u).astype(BF16)
        _store_row_tiles(ys_ref, jnp.dot(mid, w2b[...], preferred_element_type=F32))

    @pl.when(i >= n_used)
    def _():
        ys_ref[...] = jnp.zeros_like(ys_ref)


def _experts(block_expert, n_used, xs, w1, w3, w2):
    _, d, ff = w1.shape
    assert d == ROW_SLABS * LANES
    blk = EXPERT_BLOCK
    n_blocks = xs.shape[0] // (blk * ROW_SLABS)

    def slot_map(i, be, nu):
        return (jnp.minimum(i, nu[0] - 1), 0)

    def w_map(i, be, nu):
        return (be[jnp.minimum(i, nu[0] - 1)], 0, 0)

    grid_spec = pltpu.PrefetchScalarGridSpec(
        num_scalar_prefetch=2,
        grid=(n_blocks,),
        in_specs=[pl.BlockSpec((blk * ROW_SLABS, LANES), slot_map),
                  pl.BlockSpec((1, d, ff), w_map), pl.BlockSpec((1, d, ff), w_map),
                  pl.BlockSpec((1, ff, d), w_map)],
        out_specs=pl.BlockSpec((blk * ROW_SLABS, LANES), lambda i, be, nu: (i, 0)),
        scratch_shapes=[pltpu.VMEM((d, ff), BF16), pltpu.VMEM((d, ff), BF16), pltpu.VMEM((ff, d), BF16)],
    )
    return pl.pallas_call(
        _experts_body,
        grid_spec=grid_spec,
        out_shape=jax.ShapeDtypeStruct(xs.shape, F32),
        compiler_params=_cparams(("arbitrary",), VMEM_LIMIT),
        name="experts",
    )(block_expert, n_used, xs, w1, w3, w2)


def _combine_body(dest_ref, dest_next_ref, gate_ref, x_ref, ys_ref, out_ref, buf, sems):
    i = pl.program_id(0)
    n_steps = pl.num_programs(0)
    tt = x_ref.shape[0]

    def issue(dref, slot):
        def tokens(g, carry):
            t0 = pl.multiple_of(g * ROW_DMA_UNROLL, ROW_DMA_UNROLL)
            for u in range(ROW_DMA_UNROLL):
                for k in range(2):
                    _row_copy(ys_ref, dref[2 * (t0 + u) + k], buf.at[slot, k], t0 + u,
                              sems.at[slot, k]).start(priority=k)
            return carry

        lax.fori_loop(0, tt // ROW_DMA_UNROLL, tokens, 0)

    @pl.when(i == 0)
    def _():
        issue(dest_ref, 0)

    @pl.when(i + 1 < n_steps)
    def _():
        issue(dest_next_ref, (i + 1) % 2)

    slot = i % 2
    for k in range(2):
        pltpu.make_async_copy(_tile_rows(ys_ref, 0, tt), buf.at[slot, k], sems.at[slot, k]).wait()
    g = gate_ref[...]
    out_ref[...] = (x_ref[...] + g[:, 0:1] * _load_row_tiles(buf.at[slot, 0])
                    + g[:, 1:2] * _load_row_tiles(buf.at[slot, 1]))


def _combine(dest, gates, x2d, ys):
    n, d = x2d.shape
    tt = COMBINE_TILE
    row = lambda w: pl.BlockSpec((tt, w), lambda i: (i, 0))
    last = n // tt - 1
    return pl.pallas_call(
        _combine_body,
        grid=(n // tt,),
        in_specs=[pl.BlockSpec((2 * tt,), lambda i: (i,), memory_space=pltpu.SMEM),
                  pl.BlockSpec((2 * tt,), lambda i: (jnp.minimum(i + 1, last),), memory_space=pltpu.SMEM),
                  row(LANES), row(d), pl.BlockSpec(memory_space=pl.ANY)],
        out_specs=row(d),
        out_shape=jax.ShapeDtypeStruct((n, d), F32),
        scratch_shapes=[pltpu.VMEM((2, 2, tt * ROW_SLABS, LANES), F32), pltpu.SemaphoreType.DMA((2, 2))],
        compiler_params=_cparams(("arbitrary",), VMEM_LIMIT),
        name="combine",
    )(dest, dest, gates, x2d, ys)


def _moe(x2d, norm_w, w_group, w_router, w1, w3, w2):
    n, d = x2d.shape
    blk = EXPERT_BLOCK
    h3, gates, meta, cnt = _router(x2d, norm_w, w_group, w_router)
    counts = cnt[0, :N_EXPERTS].astype(I32)
    dest = _slots(meta, cnt)[:, 0:2].reshape(-1)
    n_slots = 2 * n + N_EXPERTS * blk
    n_blocks = n_slots // blk
    pends = jnp.cumsum((counts + blk - 1) // blk * blk)
    block_start = jnp.arange(n_blocks, dtype=I32) * blk
    block_expert = jnp.minimum(jnp.sum((pends[None, :] <= block_start[:, None]).astype(I32), axis=1),
                               N_EXPERTS - 1)
    n_used = (pends[-1:] // blk).astype(I32)
    xs = _dispatch(counts, dest, h3, n_slots)
    ys = _experts(block_expert, n_used, xs, w1, w3, w2)
    return _combine(dest, gates, x2d, ys)


def _layer(x, mem, norm_mix_w, w_in, attn_q_norm_w, attn_k_norm_w, ret_decay_f, ret_decay_b, ret_gn_w, w_out,
           norm_mem_w, norm_memkv_w, w_mq, w_mkv, mem_q_norm_w, mem_k_norm_w, w_mo,
           norm_moe_w, w_group, w_router, w_exp_gate, w_exp_up, w_exp_down):
    b, s, d = x.shape
    x2d = x.reshape(b * s, d)
    q, k, v, q_pl, k_pl, v_pl, rq, rk, rv, rg = _in_proj(x2d, norm_mix_w, w_in, attn_q_norm_w, attn_k_norm_w)
    attn = _attention(q, k, v, q_pl, k_pl, v_pl, b, s)
    ret = _retention(rq, rk, rv, rg, ret_decay_f, ret_decay_b, ret_gn_w, b, s)
    x1 = _out_proj(attn, ret, x2d, w_out)
    mk, mv = _mem_kv(mem, norm_memkv_w, w_mkv, mem_k_norm_w)
    x2 = _mem_attn(x1.reshape(b, s, d), mk, mv, norm_mem_w, w_mq, mem_q_norm_w, w_mo)
    x3 = _moe(x2.reshape(b * s, d), norm_moe_w, w_group, w_router, w_exp_gate, w_exp_up, w_exp_down)
    return x3.reshape(b, s, d)


def kernel(x, mem, norm_mix_w, w_in, attn_q_norm_w, attn_k_norm_w, ret_decay_f, ret_decay_b, ret_gn_w, w_out,
           norm_mem_w, norm_memkv_w, w_mq, w_mkv, mem_q_norm_w, mem_k_norm_w, w_mo, norm_moe_w, w_group,
           w_router, w_exp_gate, w_exp_up, w_exp_down):
    depth = norm_mix_w.shape[0]
    for l in range(depth):
        x = _layer(x, mem, norm_mix_w[l], w_in[l], attn_q_norm_w[l], attn_k_norm_w[l], ret_decay_f[l],
                   ret_decay_b[l], ret_gn_w[l], w_out[l], norm_mem_w[l], norm_memkv_w[l], w_mq[l], w_mkv[l],
                   mem_q_norm_w[l], mem_k_norm_w[l], w_mo[l], norm_moe_w[l], w_group[l], w_router[l],
                   w_exp_gate[l], w_exp_up[l], w_exp_down[l])
    return x
```

```python
import functools

import numpy as np
import jax
import jax.numpy as jnp
from jax import lax
from jax.experimental import pallas as pl
from jax.experimental.pallas import tpu as pltpu

F32 = jnp.float32
BF16 = jnp.bfloat16
I32 = jnp.int32

NORM_EPS = 1e-6
GN_EPS = 1e-5
NEG_INF = -1e30

ATTN_HEADS = 8
ATTN_HEAD_DIM = 64
ATTN_WIDTH = ATTN_HEADS * ATTN_HEAD_DIM
ATTN_HALF = 64
PLANES = 16
ATTN_TILE = PLANES * ATTN_HALF
RET_HEADS = 4
RET_QK_DIM = 64
RET_V_DIM = 128
RET_QK_WIDTH = RET_HEADS * RET_QK_DIM
RET_V_WIDTH = RET_HEADS * RET_V_DIM
MEM_HEADS = 4
N_GROUPS = 4
EXPERTS_PER_GROUP = 8
N_EXPERTS = N_GROUPS * EXPERTS_PER_GROUP

LANES = 128
ROUTER_TILE = 512
ROUTER_SUBTILE = 512
MATMUL_TILE = 1024
ATTN_BLOCKS_PER_STEP = 4
LOG2_E = 1.4426950408889634
RET_CHUNK = 256
EXPERT_BLOCK = 512
DISPATCH_TILE = 1024
COMBINE_TILE = 256
ROW_DMA_UNROLL = 16
SLOTS_TILE = 4096
VMEM_LIMIT = 56 * 1024 * 1024


def _cparams(sem, vmem=None):
    return pltpu.CompilerParams(dimension_semantics=sem, vmem_limit_bytes=vmem)


def _split_hi_lo(x):
    hi = x.astype(BF16)
    lo = (x - hi.astype(F32)).astype(BF16)
    return hi, lo


def _rms(x, w):
    ms = jnp.mean(x * x, axis=-1, keepdims=True)
    return x * lax.rsqrt(ms + NORM_EPS) * w


ROW_SLABS = 8


def _store_row_tiles(ref, val):
    rows = val.shape[0]
    for j in range(ROW_SLABS):
        ref[pl.ds(j, rows, stride=ROW_SLABS), :] = val[:, j * LANES:(j + 1) * LANES]


def _load_row_tiles(ref):
    rows = ref.shape[0] // ROW_SLABS
    return jnp.concatenate([ref[pl.ds(j, rows, stride=ROW_SLABS), :] for j in range(ROW_SLABS)], axis=1)


def _tile_rows(ref, row, count=1):
    return ref.at[pl.ds(pl.multiple_of(row * ROW_SLABS, ROW_SLABS), count * ROW_SLABS)]


def _in_proj_body(x_ref, nw_ref, w_ref, qw_ref, kw_ref, g_ref,
                  q_out, k_out, v_out, qp_out, kp_out, vp_out, rq_out, rk_out, rv_out, rg_out, slabs, quarters):
    tm = x_ref.shape[0]
    h = _rms(x_ref[...], nw_ref[...]).astype(BF16)

    def emit(val, nat_out, plane_out):
        nat_out[...] = val.astype(BF16)
        for j in range(ATTN_WIDTH // LANES):
            slabs[j] = val[:, j * LANES:(j + 1) * LANES]
        for j in range(ATTN_WIDTH // LANES):
            for r in range(4):
                quarters[j, r] = slabs[j, pl.ds(r, tm // 4, stride=4), :]
        for r in range(4):
            for q in range(4):
                for j in range(ATTN_WIDTH // LANES):
                    rows = quarters[j, r, pl.ds(q, tm // PLANES, stride=4), :]
                    plane_out[0, r + 4 * q, :, j * LANES:(j + 1) * LANES] = rows.astype(BF16)

    def proj(a, b):
        return jnp.dot(h, w_ref[:, a:b], preferred_element_type=F32)

    def head_norm(p, w):
        s = jnp.dot((p * p).astype(BF16), g_ref[...], preferred_element_type=F32)
        return p * lax.rsqrt(s * (1.0 / ATTN_HEAD_DIM) + NORM_EPS) * w

    a = ATTN_WIDTH
    emit(head_norm(proj(0, a), qw_ref[...]), q_out, qp_out)
    emit(head_norm(proj(a, 2 * a), kw_ref[...]), k_out, kp_out)
    emit(proj(2 * a, 3 * a), v_out, vp_out)
    c = 3 * a
    rq_out[...] = proj(c, c + RET_QK_WIDTH).astype(BF16)
    rk_out[...] = proj(c + RET_QK_WIDTH, c + 2 * RET_QK_WIDTH).astype(BF16)
    c += 2 * RET_QK_WIDTH
    rv_out[...] = proj(c, c + RET_V_WIDTH).astype(BF16)
    rg_out[...] = proj(c + RET_V_WIDTH, c + 2 * RET_V_WIDTH).astype(BF16)


def _in_proj(x2d, norm_w, w_in, q_norm_w, k_norm_w):
    n, d = x2d.shape
    cols = w_in.shape[1]
    tm = MATMUL_TILE
    qw = (jnp.tile(q_norm_w, ATTN_HEADS) * (ATTN_HEAD_DIM ** -0.5 * LOG2_E)).reshape(1, ATTN_WIDTH)
    kw = jnp.tile(k_norm_w, ATTN_HEADS).reshape(1, ATTN_WIDTH)
    head_of = np.arange(ATTN_WIDTH) // ATTN_HEAD_DIM
    gmat = jnp.asarray(head_of[:, None] == head_of[None, :], dtype=BF16)
    row = lambda w: pl.BlockSpec((tm, w), lambda i: (i, 0))
    full = lambda r, c: pl.BlockSpec((r, c), lambda i: (0, 0))
    per_tile = ATTN_TILE // tm
    plane = pl.BlockSpec((1, PLANES, tm // PLANES, ATTN_WIDTH), lambda i: (i // per_tile, 0, i % per_tile, 0))
    plane_shape = jax.ShapeDtypeStruct((n // ATTN_TILE, PLANES, ATTN_TILE // PLANES, ATTN_WIDTH), BF16)
    ret_widths = (RET_QK_WIDTH, RET_QK_WIDTH, RET_V_WIDTH, RET_V_WIDTH)
    return pl.pallas_call(
        _in_proj_body,
        grid=(n // tm,),
        in_specs=[row(d), full(1, d), full(d, cols), full(1, ATTN_WIDTH), full(1, ATTN_WIDTH),
                  full(ATTN_WIDTH, ATTN_WIDTH)],
        out_specs=[row(ATTN_WIDTH)] * 3 + [plane] * 3 + [row(w) for w in ret_widths],
        out_shape=([jax.ShapeDtypeStruct((n, ATTN_WIDTH), BF16)] * 3 + [plane_shape] * 3
                   + [jax.ShapeDtypeStruct((n, w), BF16) for w in ret_widths]),
        scratch_shapes=[pltpu.VMEM((ATTN_WIDTH // LANES, tm, LANES), F32),
                        pltpu.VMEM((ATTN_WIDTH // LANES, 4, tm // 4, LANES), F32)],
        compiler_params=_cparams(("parallel",), VMEM_LIMIT),
        name="in_proj",
    )(x2d, norm_w.reshape(1, d), w_in.astype(BF16), qw, kw, gmat)


def _attn_bias(dilation, interleave=1):
    half = ATTN_HALF
    idx = np.arange(half)
    sub = (idx % (half // interleave)) * interleave + idx // (half // interleave)
    qi = sub[:, None]
    kc = (np.arange(3)[:, None] * half + sub[None, :]).reshape(1, -1)
    seg = np.repeat(np.arange(3), half)[None, :]
    delta = kc - half - qi
    band = np.abs(delta) <= half
    slopes = np.exp2(-8.0 * np.arange(1, ATTN_HEADS + 1) / ATTN_HEADS)
    valid = [band & (seg >= 1), band, band & (seg <= 1)]
    out = np.empty((ATTN_HEADS // 2, 3, 2 * half, 3 * half), np.float32)
    for p in range(ATTN_HEADS // 2):
        for v in range(3):
            for s in range(2):
                b = -slopes[2 * p + s] * LOG2_E * (dilation * np.abs(delta)).astype(np.float32)
                out[p, v, s * half:(s + 1) * half] = np.where(valid[v], b, NEG_INF)
    return jnp.asarray(out.reshape(-1, 2 * half, 3 * half))


def _attend(chains):
    half = ATTN_HALF
    lo_lanes = lax.broadcasted_iota(I32, (half, LANES), 1) < ATTN_HEAD_DIM
    ones = jnp.ones((3 * half, LANES), BF16)
    scores = []
    for q, k, v, bias in chains:
        zero = jnp.zeros_like(q)
        qs = jnp.concatenate([jnp.where(lo_lanes, q, zero), jnp.where(lo_lanes, zero, q)], axis=0)
        scores.append(lax.dot_general(qs, k, (((1,), (1,)), ((), ())), preferred_element_type=F32))
    probs, maxes = [], []
    for (q, k, v, bias), s in zip(chains, scores):
        s = s + bias
        m = jnp.max(s, axis=-1, keepdims=True)
        probs.append(jnp.exp2(s - m).astype(BF16))
        maxes.append(m)
    results = []
    for (q, k, v, bias), e in zip(chains, probs):
        vext = jnp.concatenate([v, ones], axis=1)
        results.append(jnp.dot(e, vext, preferred_element_type=F32))
    outs = []
    for r, m in zip(results, maxes):
        acc = jnp.where(lo_lanes, r[:half, :LANES], r[half:, :LANES])
        denom = jnp.where(lo_lanes, r[:half, LANES:], r[half:, LANES:])
        mm = jnp.where(lo_lanes, m[:half], m[half:])
        outs.append((acc / denom, mm + jnp.log2(denom)))
    return outs


def _merge(oa, la, ob, lb):
    m = jnp.maximum(la, lb)
    ea = jnp.exp2(la - m)
    eb = jnp.exp2(lb - m)
    den = ea + eb
    return (ea * oa + eb * ob) / den, m + jnp.log2(den)


def _attn_body(q_ref, kp_ref, km_ref, kn_ref, vp_ref, vm_ref, vn_ref,
               qpl_ref, kplp_ref, kplm_ref, kpln_ref, vplp_ref, vplm_ref, vpln_ref,
               b1_ref, b4_ref, b16_ref, o_ref, kcat, vcat, o_far, l_far):
    half = ATTN_HALF
    t = pl.program_id(1)
    n_tiles = pl.num_programs(1)
    tile = q_ref.shape[1]
    pairs = ATTN_HEADS // 2
    piece = half // 4

    kcat[0:half] = kp_ref[0]
    kcat[half:half + tile] = km_ref[0]
    kcat[half + tile:] = kn_ref[0]
    vcat[0:half] = vp_ref[0]
    vcat[half:half + tile] = vm_ref[0]
    vcat[half + tile:] = vn_ref[0]

    def edge(first, last):
        return jnp.where(first, 0, jnp.where(last, 2, 1))

    def far_step(r, carry):
        v16 = edge(t == 0, t == n_tiles - 1)
        for pr in range(pairs):
            cs = slice(pr * LANES, (pr + 1) * LANES)
            chains = []
            for m in range(4):
                c = r + 4 * m
                k3 = jnp.concatenate([kplp_ref[0, c, :, cs], kplm_ref[0, c, :, cs], kpln_ref[0, c, :, cs]], axis=0)
                v3 = jnp.concatenate([vplp_ref[0, c, :, cs], vplm_ref[0, c, :, cs], vpln_ref[0, c, :, cs]], axis=0)
                chains.append((qpl_ref[0, c, :, cs], k3, v3, b16_ref[pr * 3 + v16]))
            for nb in range(4):
                def rows(main, before, after, a0):
                    ref, lo = (before, a0 + half) if a0 < 0 else (after, a0 - half) if a0 >= half else (main, a0)
                    return [ref[0, r + 4 * m, lo:lo + piece, cs] for m in range(4)]
                a0 = nb * piece
                q4 = jnp.concatenate(rows(qpl_ref, None, None, a0), axis=0)
                k4 = jnp.concatenate(sum((rows(kplm_ref, kplp_ref, kpln_ref, a0 + d) for d in (-piece, 0, piece)), []),
                                     axis=0)
                v4 = jnp.concatenate(sum((rows(vplm_ref, vplp_ref, vpln_ref, a0 + d) for d in (-piece, 0, piece)), []),
                                     axis=0)
                v4e = edge((t == 0) & (nb == 0), (t == n_tiles - 1) & (nb == 3))
                chains.append((q4, k4, v4, b4_ref[pr * 3 + v4e]))
            res = _attend(chains)
            for m in range(4):
                o16, l16 = res[m]
                o_rows, l_rows = [], []
                for nb in range(4):
                    o4, l4 = res[4 + nb]
                    sl = slice(nb * piece, (nb + 1) * piece)
                    s4 = slice(m * piece, (m + 1) * piece)
                    om, lm = _merge(o16[sl], l16[sl], o4[s4], l4[s4])
                    o_rows.append(om)
                    l_rows.append(lm)
                dst = pl.ds(r + 4 * m, half, stride=PLANES)
                o_far[pr, dst, :] = jnp.concatenate(o_rows, axis=0)
                l_far[pr, dst, :] = jnp.concatenate(l_rows, axis=0)
        return carry

    for r in range(4):
        far_step(r, 0)

    unroll = ATTN_BLOCKS_PER_STEP
    n_blocks = n_tiles * (tile // half)

    def near_step(it, carry):
        chains, where = [], []
        for u in range(unroll):
            jb = it * unroll + u
            r0 = jb * half
            gb = t * (tile // half) + jb
            variant = edge(gb == 0, gb == n_blocks - 1)
            for pr in range(pairs):
                cs = slice(pr * LANES, (pr + 1) * LANES)
                chains.append((q_ref[0, pl.ds(r0, half), cs], kcat[pl.ds(r0, 3 * half), cs],
                               vcat[pl.ds(r0, 3 * half), cs], b1_ref[pr * 3 + variant]))
                where.append((r0, pr, cs))
        for (r0, pr, cs), (o1, l1) in zip(where, _attend(chains)):
            o, _ = _merge(o1, l1, o_far[pr, pl.ds(r0, half), :], l_far[pr, pl.ds(r0, half), :])
            o_ref[0, pl.ds(r0, half), cs] = o.astype(o_ref.dtype)
        return carry

    for it in range(tile // half // unroll):
        near_step(it, 0)


def _attention(q, k, v, q_pl, k_pl, v_pl, b, s):
    w = ATTN_WIDTH
    half = ATTN_HALF
    tile = ATTN_TILE
    n_tiles = s // tile
    hb = tile // half
    last = s // half - 1
    main = pl.BlockSpec((1, tile, w), lambda bb, t: (bb, t, 0))
    prev = pl.BlockSpec((1, half, w), lambda bb, t: (bb, jnp.maximum(t * hb - 1, 0), 0))
    nxt = pl.BlockSpec((1, half, w), lambda bb, t: (bb, jnp.minimum((t + 1) * hb, last), 0))
    pshape = (1, PLANES, tile // PLANES, w)
    pl_main = pl.BlockSpec(pshape, lambda bb, t: (bb * n_tiles + t, 0, 0, 0))
    pl_prev = pl.BlockSpec(pshape, lambda bb, t: (bb * n_tiles + jnp.maximum(t - 1, 0), 0, 0, 0))
    pl_next = pl.BlockSpec(pshape, lambda bb, t: (bb * n_tiles + jnp.minimum(t + 1, n_tiles - 1), 0, 0, 0))
    biases = [_attn_bias(1), _attn_bias(4, interleave=4), _attn_bias(16)]
    bias_spec = pl.BlockSpec(biases[0].shape, lambda bb, t: (0, 0, 0))
    nat = lambda a: a.reshape(b, s, w)
    out = pl.pallas_call(
        _attn_body,
        grid=(b, n_tiles),
        in_specs=[main, prev, main, nxt, prev, main, nxt,
                  pl_main, pl_prev, pl_main, pl_next, pl_prev, pl_main, pl_next,
                  bias_spec, bias_spec, bias_spec],
        out_specs=main,
        out_shape=jax.ShapeDtypeStruct((b, s, w), BF16),
        scratch_shapes=[pltpu.VMEM((tile + 2 * half, w), BF16), pltpu.VMEM((tile + 2 * half, w), BF16),
                        pltpu.VMEM((ATTN_HEADS // 2, tile, LANES), F32),
                        pltpu.VMEM((ATTN_HEADS // 2, tile, LANES), F32)],
        compiler_params=_cparams(("parallel", "parallel"), VMEM_LIMIT),
        name="attention",
    )(nat(q), nat(k), nat(k), nat(k), nat(v), nat(v), nat(v),
      q_pl, k_pl, k_pl, k_pl, v_pl, v_pl, v_pl, *biases)
    return out.reshape(b * s, w)


def _ret_body(lg_ref, q_ref, k_ref, v_ref, g_ref, gnw_ref, out_ref,
              dmat, qdec, kdec, cdec, fstate, rstate, rall, *, chunk):
    c = chunk
    b = pl.program_id(0)
    ph = pl.program_id(1)
    n = pl.program_id(2)
    n_chunks = pl.num_programs(2)
    k_scale = RET_QK_DIM ** -0.5
    pairs = RET_HEADS // 2
    pw = 2 * RET_QK_DIM
    vw = 2 * RET_V_DIM
    first = lax.broadcasted_iota(I32, (c, pw), 1) < RET_QK_DIM
    diag = ((lax.broadcasted_iota(I32, (pw, vw), 0) < RET_QK_DIM)
            == (lax.broadcasted_iota(I32, (pw, vw), 1) < RET_V_DIM))

    @pl.when((b == 0) & (ph == 0) & (n == 0))
    def _init_tables():
        ii = lax.broadcasted_iota(I32, (c, c), 0)
        jj = lax.broadcasted_iota(I32, (c, c), 1)
        fwd = (ii - jj).astype(F32)
        ri = lax.broadcasted_iota(I32, (c, pw), 0).astype(F32)
        top = lax.broadcasted_iota(I32, (pw, vw), 0) < RET_QK_DIM
        for h in range(RET_HEADS):
            dmat[h] = jnp.where(ii >= jj, jnp.exp(lg_ref[0, h] * fwd), jnp.exp(-lg_ref[1, h] * fwd)) * k_scale
        for pr in range(pairs):
            lf = jnp.where(first, lg_ref[0, 2 * pr], lg_ref[0, 2 * pr + 1])
            lb = jnp.where(first, lg_ref[1, 2 * pr], lg_ref[1, 2 * pr + 1])
            qdec[0, pr] = jnp.exp(lf * (ri + 1.0))
            qdec[1, pr] = jnp.exp(lb * (c - ri))
            kdec[0, pr] = jnp.exp(lf * (c - 1.0 - ri)) * k_scale
            kdec[1, pr] = jnp.exp(lb * ri) * k_scale
            for direction in range(2):
                cdec[direction, pr] = jnp.exp(jnp.where(top, lg_ref[direction, 2 * pr],
                                                        lg_ref[direction, 2 * pr + 1]) * c)

    def kv_update(direction, pr):
        kp = k_ref[0, :, pr * pw:(pr + 1) * pw].astype(F32)
        ks = (kp * kdec[direction, pr]).astype(BF16)
        new = lax.dot_general(ks, v_ref[0, :, pr * vw:(pr + 1) * vw], (((0,), (0,)), ((), ())),
                              preferred_element_type=F32)
        return jnp.where(diag, new, 0.0)

    @pl.when(ph == 0)
    def _right_to_left():
        @pl.when(n == 0)
        def _():
            rstate[...] = jnp.zeros_like(rstate)

        ci = n_chunks - 1 - n
        for pr in range(pairs):
            st = rstate[pr]
            rall[ci, pr] = st
            rstate[pr] = st * cdec[1, pr] + kv_update(1, pr)

    @pl.when(ph == 1)
    def _left_to_right():
        @pl.when(n == 0)
        def _():
            fstate[...] = jnp.zeros_like(fstate)

        for pr in range(pairs):
            qp = q_ref[0, :, pr * pw:(pr + 1) * pw]
            kp = k_ref[0, :, pr * pw:(pr + 1) * pw]
            qf = qp.astype(F32)
            qcat = jnp.concatenate([(qf * qdec[0, pr]).astype(BF16), (qf * qdec[1, pr]).astype(BF16)], axis=1)
            st = fstate[pr]
            states = jnp.concatenate([st, rall[n, pr]], axis=0).astype(BF16)
            cross = jnp.dot(qcat, states, preferred_element_type=F32)
            fstate[pr] = st * cdec[0, pr] + kv_update(0, pr)
            zero = jnp.zeros_like(qp)
            for a in range(2):
                h = 2 * pr + a
                vs = slice(h * RET_V_DIM, (h + 1) * RET_V_DIM)
                qm = jnp.where(first, qp, zero) if a == 0 else jnp.where(first, zero, qp)
                s = lax.dot_general(qm, kp, (((1,), (1,)), ((), ())), preferred_element_type=F32) * dmat[h]
                y = jnp.dot(s.astype(BF16), v_ref[0, :, vs], preferred_element_type=F32)
                y = y + cross[:, a * RET_V_DIM:(a + 1) * RET_V_DIM]
                mu = jnp.mean(y, axis=-1, keepdims=True)
                yc = y - mu
                var = jnp.mean(yc * yc, axis=-1, keepdims=True)
                yn = yc * lax.rsqrt(var + GN_EPS) * gnw_ref[:, vs]
                gate = g_ref[0, :, vs].astype(F32)
                out_ref[0, :, vs] = (gate * jax.nn.sigmoid(gate) * yn).astype(out_ref.dtype)


def _retention(rq, rk, rv, rg, decay_f, decay_b, gn_w, b, s):
    c = RET_CHUNK
    nc = s // c
    lg = jnp.stack([jax.nn.log_sigmoid(decay_f.astype(F32)), jax.nn.log_sigmoid(decay_b.astype(F32))])
    qk_w, v_w = RET_QK_WIDTH, RET_V_WIDTH
    pairs, pair_qk, pair_v = RET_HEADS // 2, 2 * RET_QK_DIM, 2 * RET_V_DIM

    def both(bb, ph, n):
        return (bb, jnp.where(ph == 0, nc - 1 - n, n), 0)

    def fwd_only(bb, ph, n):
        return (bb, jnp.where(ph == 0, 0, n), 0)

    out = pl.pallas_call(
        functools.partial(_ret_body, chunk=c),
        grid=(b, 2, nc),
        in_specs=[pl.BlockSpec(memory_space=pltpu.SMEM),
                  pl.BlockSpec((1, c, qk_w), fwd_only),
                  pl.BlockSpec((1, c, qk_w), both),
                  pl.BlockSpec((1, c, v_w), both),
                  pl.BlockSpec((1, c, v_w), fwd_only),
                  pl.BlockSpec((1, v_w), lambda bb, ph, n: (0, 0))],
        out_specs=pl.BlockSpec((1, c, v_w), fwd_only),
        out_shape=jax.ShapeDtypeStruct((b, s, v_w), BF16),
        scratch_shapes=[pltpu.VMEM((RET_HEADS, c, c), F32),
                        pltpu.VMEM((2, pairs, c, pair_qk), F32),
                        pltpu.VMEM((2, pairs, c, pair_qk), F32),
                        pltpu.VMEM((2, pairs, pair_qk, pair_v), F32),
                        pltpu.VMEM((pairs, pair_qk, pair_v), F32),
                        pltpu.VMEM((pairs, pair_qk, pair_v), F32),
                        pltpu.VMEM((nc, pairs, pair_qk, pair_v), F32)],
        compiler_params=_cparams(("arbitrary", "arbitrary", "arbitrary"), VMEM_LIMIT),
        name="retention",
    )(lg, rq.reshape(b, s, qk_w), rk.reshape(b, s, qk_w), rv.reshape(b, s, v_w), rg.reshape(b, s, v_w),
      gn_w.reshape(1, v_w))
    return out.reshape(b * s, v_w)


def _out_proj_body(attn_ref, ret_ref, x_ref, w_ref, out_ref):
    y = jnp.dot(attn_ref[...], w_ref[:ATTN_WIDTH], preferred_element_type=F32)
    y = y + jnp.dot(ret_ref[...], w_ref[ATTN_WIDTH:], preferred_element_type=F32)
    out_ref[...] = x_ref[...] + y


def _out_proj(attn, ret, x2d, w_out):
    n, d = x2d.shape
    tm = MATMUL_TILE
    row = lambda w: pl.BlockSpec((tm, w), lambda i: (i, 0))
    return pl.pallas_call(
        _out_proj_body,
        grid=(n // tm,),
        in_specs=[row(ATTN_WIDTH), row(RET_V_WIDTH), row(d), pl.BlockSpec(w_out.shape, lambda i: (0, 0))],
        out_specs=row(d),
        out_shape=jax.ShapeDtypeStruct((n, d), F32),
        compiler_params=_cparams(("parallel",), VMEM_LIMIT),
        name="out_proj",
    )(attn, ret, x2d, w_out.astype(BF16))


def _mem_kv_body(mem_ref, nw_ref, w_ref, kw_ref, k_out, v_out):
    d = mem_ref.shape[-1]
    hd = d // MEM_HEADS
    h = _rms(mem_ref[0], nw_ref[...]).astype(BF16)
    kv = jnp.dot(h, w_ref[...], preferred_element_type=F32)
    for i in range(MEM_HEADS):
        k_out[0, :, i * hd:(i + 1) * hd] = _rms(kv[:, i * hd:(i + 1) * hd], kw_ref[...]).astype(BF16)
    v_out[0] = kv[:, d:].astype(BF16)


def _mem_kv(mem, norm_w, w_mkv, k_norm_w):
    b, m, d = mem.shape
    return pl.pallas_call(
        _mem_kv_body,
        grid=(b,),
        in_specs=[pl.BlockSpec((1, m, d), lambda i: (i, 0, 0)),
                  pl.BlockSpec((1, d), lambda i: (0, 0)),
                  pl.BlockSpec((d, 2 * d), lambda i: (0, 0)),
                  pl.BlockSpec((1, d // MEM_HEADS), lambda i: (0, 0))],
        out_specs=[pl.BlockSpec((1, m, d), lambda i: (i, 0, 0))] * 2,
        out_shape=[jax.ShapeDtypeStruct((b, m, d), BF16)] * 2,
        compiler_params=_cparams(("parallel",), VMEM_LIMIT),
        name="mem_kv",
    )(mem, norm_w.reshape(1, d), w_mkv.astype(BF16), k_norm_w.reshape(1, -1))


def _mem_attn_body(x_ref, nw_ref, wq_ref, qw_ref, k_ref, v_ref, wo_ref, out_ref):
    d = x_ref.shape[-1]
    hd = d // MEM_HEADS
    x = x_ref[0]
    h = _rms(x, nw_ref[...]).astype(BF16)
    q = jnp.dot(h, wq_ref[...], preferred_element_type=F32)
    heads = []
    for i in range(MEM_HEADS):
        cs = slice(i * hd, (i + 1) * hd)
        qn = _rms(q[:, cs], qw_ref[...]).astype(BF16)
        s = lax.dot_general(qn, k_ref[0, :, cs], (((1,), (1,)), ((), ())), preferred_element_type=F32)
        e = jnp.exp(s - jnp.max(s, axis=-1, keepdims=True))
        o = jnp.dot(e.astype(BF16), v_ref[0, :, cs], preferred_element_type=F32)
        heads.append((o / jnp.sum(e, axis=-1, keepdims=True)).astype(BF16))
    o = jnp.concatenate(heads, axis=1)
    out_ref[0] = x + jnp.dot(o, wo_ref[...], preferred_element_type=F32)


def _mem_attn(x1, mk, mv, norm_w, w_mq, q_norm_w, w_mo):
    b, s, d = x1.shape
    m = mk.shape[1]
    tm = MATMUL_TILE
    hd = d // MEM_HEADS
    qw = (q_norm_w * (hd ** -0.5)).reshape(1, hd)
    tok = pl.BlockSpec((1, tm, d), lambda bb, i: (bb, i, 0))
    const = lambda r, c: pl.BlockSpec((r, c), lambda bb, i: (0, 0))
    mem = pl.BlockSpec((1, m, d), lambda bb, i: (bb, 0, 0))
    return pl.pallas_call(
        _mem_attn_body,
        grid=(b, s // tm),
        in_specs=[tok, const(1, d), const(d, d), const(1, hd), mem, mem, const(d, d)],
        out_specs=tok,
        out_shape=jax.ShapeDtypeStruct((b, s, d), F32),
        compiler_params=_cparams(("parallel", "parallel"), VMEM_LIMIT),
        name="mem_attn",
    )(x1, norm_w.reshape(1, d), w_mq.astype(BF16), qw, mk, mv, w_mo.astype(BF16))


def _router_body(x_ref, nw_ref, w_ref, h_out, gate_out, meta_out, cnt_out, carry, below):
    tm = x_ref.shape[0]
    i = pl.program_id(0)

    sub = below.shape[0]

    @pl.when(i == 0)
    def _():
        carry[...] = jnp.zeros_like(carry)
        rr = lax.broadcasted_iota(I32, (sub, sub), 0)
        cc = lax.broadcasted_iota(I32, (sub, sub), 1)
        below[...] = (cc < rr).astype(F32).astype(BF16)

    base = carry[...]
    for part in range(tm // sub):
        rows = slice(part * sub, (part + 1) * sub)
        base = _route_rows(x_ref[rows, :], nw_ref, w_ref, below, base,
                           h_out.at[pl.ds(part * sub * ROW_SLABS, sub * ROW_SLABS)],
                           gate_out.at[rows], meta_out.at[rows])
    carry[...] = base
    cnt_out[...] = base


def _route_rows(x, nw_ref, w_ref, below, base, h_out, gate_out, meta_out):
    tm = x.shape[0]
    h = _rms(x, nw_ref[...])
    _store_row_tiles(h_out, h)
    hi, lo = _split_hi_lo(h)
    both = jnp.dot(hi, w_ref[...], preferred_element_type=F32)
    logits = (both[:, :LANES] + both[:, LANES:]
              + jnp.dot(lo, w_ref[:, :LANES], preferred_element_type=F32))
    lane = lax.broadcasted_iota(I32, (tm, LANES), 1)
    lane_f = lane.astype(F32)

    def argmax_lanes(vals):
        top = jnp.max(vals, axis=-1, keepdims=True)
        idx = jnp.min(jnp.where(vals == top, lane_f, float(LANES)), axis=-1, keepdims=True)
        return top, idx

    is_group = lane < N_GROUPS
    g_top, grp = argmax_lanes(jnp.where(is_group, logits, -jnp.inf))
    g_gate = 1.0 / jnp.sum(jnp.where(is_group, jnp.exp(logits - g_top), 0.0), axis=-1, keepdims=True)
    first = N_GROUPS + EXPERTS_PER_GROUP * grp
    in_group = (lane_f >= first) & (lane_f < first + EXPERTS_PER_GROUP)
    el = jnp.where(in_group, logits, -jnp.inf)
    t1, i1 = argmax_lanes(el)
    t2, i2 = argmax_lanes(jnp.where(lane_f == i1, -jnp.inf, el))
    z = jnp.exp(t2 - t1)
    g1 = g_gate / (1.0 + z)
    g2 = g_gate * z / (1.0 + z)
    e1 = i1 - N_GROUPS
    e2 = i2 - N_GROUPS

    oh1 = (lane_f == e1)
    oh2 = (lane_f == e2)
    oh1b = oh1.astype(F32).astype(BF16)
    oh2b = oh2.astype(F32).astype(BF16)
    pre = jnp.dot(below[...], jnp.concatenate([oh1b, oh2b], axis=1), preferred_element_type=F32)
    pre1, pre2 = pre[:, :LANES], pre[:, LANES:]
    cnt1 = jnp.sum(oh1.astype(F32), axis=0, keepdims=True)
    cnt2 = jnp.sum(oh2.astype(F32), axis=0, keepdims=True)
    r1 = jnp.sum(jnp.where(oh1, pre1 + base, 0.0), axis=-1, keepdims=True)
    r2 = jnp.sum(jnp.where(oh2, pre2 + base + cnt1, 0.0), axis=-1, keepdims=True)

    gate_out[...] = jnp.where(lane == 0, g1, jnp.where(lane == 1, g2, 0.0))
    meta = jnp.where(lane == 0, e1, jnp.where(lane == 1, e2, jnp.where(lane == 2, r1, jnp.where(lane == 3, r2, 0.0))))
    meta_out[...] = meta.astype(I32)
    return base + cnt1 + cnt2


def _router(x2d, norm_w, w_group, w_router):
    n, d = x2d.shape
    tm = ROUTER_TILE
    w_all = jnp.concatenate([w_group, w_router.transpose(1, 0, 2).reshape(d, N_EXPERTS)], axis=1)
    w_all = jnp.pad(w_all, ((0, 0), (0, LANES - w_all.shape[1])))
    whi = w_all.astype(BF16)
    wlo = (w_all - whi.astype(F32)).astype(BF16)
    w_split = jnp.concatenate([whi, wlo], axis=1)
    row = lambda w: pl.BlockSpec((tm, w), lambda i: (i, 0))
    const = lambda r, c: pl.BlockSpec((r, c), lambda i: (0, 0))
    return pl.pallas_call(
        _router_body,
        grid=(n // tm,),
        in_specs=[row(d), const(1, d), const(d, 2 * LANES)],
        out_specs=[pl.BlockSpec((tm * ROW_SLABS, LANES), lambda i: (i, 0)), row(LANES), row(LANES), const(1, LANES)],
        out_shape=[jax.ShapeDtypeStruct((n * ROW_SLABS, LANES), F32), jax.ShapeDtypeStruct((n, LANES), F32),
                   jax.ShapeDtypeStruct((n, LANES), I32), jax.ShapeDtypeStruct((1, LANES), F32)],
        scratch_shapes=[pltpu.VMEM((1, LANES), F32), pltpu.VMEM((ROUTER_SUBTILE, ROUTER_SUBTILE), BF16)],
        compiler_params=_cparams(("arbitrary",), VMEM_LIMIT),
        name="router",
    )(x2d, norm_w.reshape(1, d), w_split)


def _row_copy(src_ref, src_row, dst_ref, dst_row, sem):
    return pltpu.make_async_copy(_tile_rows(src_ref, src_row), _tile_rows(dst_ref, dst_row), sem)


def _slots_body(meta_ref, cnt_ref, dest_ref):
    tm = meta_ref.shape[0]
    blk = float(EXPERT_BLOCK)
    lane = lax.broadcasted_iota(I32, (tm, LANES), 1)
    lane_f = lane.astype(F32)
    blocks = jnp.floor((cnt_ref[...] + (blk - 1.0)) * (1.0 / blk))
    rr = lax.broadcasted_iota(I32, (LANES, LANES), 0)
    cc = lax.broadcasted_iota(I32, (LANES, LANES), 1)
    before = (rr < cc).astype(F32).astype(BF16)
    first_block = jnp.dot(jnp.broadcast_to(blocks, (8, LANES)).astype(BF16), before,
                          preferred_element_type=F32)[0:1]
    start = first_block * blk
    meta = meta_ref[...].astype(F32)

    def slot_of(k):
        e = jnp.sum(jnp.where(lane == k, meta, 0.0), axis=-1, keepdims=True)
        r = jnp.sum(jnp.where(lane == 2 + k, meta, 0.0), axis=-1, keepdims=True)
        return jnp.sum(jnp.where(lane_f == e, start, 0.0), axis=-1, keepdims=True) + r

    dest_ref[...] = jnp.where(lane == 0, slot_of(0), jnp.where(lane == 1, slot_of(1), 0.0)).astype(I32)


def _slots(meta, cnt):
    n = meta.shape[0]
    tm = min(SLOTS_TILE, n)
    row = pl.BlockSpec((tm, LANES), lambda i: (i, 0))
    return pl.pallas_call(
        _slots_body,
        grid=(n // tm,),
        in_specs=[row, pl.BlockSpec((1, LANES), lambda i: (0, 0))],
        out_specs=row,
        out_shape=jax.ShapeDtypeStruct((n, LANES), I32),
        compiler_params=_cparams(("parallel",), VMEM_LIMIT),
        name="slots",
    )(meta, cnt)


def _dispatch_body(cnt_ref, dest_ref, h_ref, xs_ref, zbuf, sems):
    i = pl.program_id(0)
    tt = h_ref.shape[0] // ROW_SLABS
    blk = EXPERT_BLOCK
    n_slots = xs_ref.shape[0] // ROW_SLABS

    def pad_copy(pend):
        return pltpu.make_async_copy(zbuf, _tile_rows(xs_ref, pend - blk, blk), sems.at[2])

    @pl.when(i == 0)
    def _():
        zbuf[...] = jnp.zeros_like(zbuf)

        def start(e, acc):
            pend = acc + ((cnt_ref[e] + (blk - 1)) // blk) * blk

            @pl.when(pend > acc)
            def _():
                pad_copy(pend).start()
            return pend

        used = lax.fori_loop(0, N_EXPERTS, start, jnp.int32(0))

        def tail_start(j, carry):
            pad_copy((j + 1) * blk).start()
            return carry

        lax.fori_loop(used // blk, n_slots // blk, tail_start, 0)

        def finish(e, acc):
            pend = acc + ((cnt_ref[e] + (blk - 1)) // blk) * blk

            @pl.when(pend > acc)
            def _():
                pad_copy(pend).wait()
            return pend

        lax.fori_loop(0, N_EXPERTS, finish, jnp.int32(0))

        def tail_wait(j, carry):
            pad_copy((j + 1) * blk).wait()
            return carry

        lax.fori_loop(used // blk, n_slots // blk, tail_wait, 0)

    def tokens(g, carry):
        t0 = pl.multiple_of(g * ROW_DMA_UNROLL, ROW_DMA_UNROLL)
        for u in range(ROW_DMA_UNROLL):
            for k in range(2):
                _row_copy(h_ref, t0 + u, xs_ref, dest_ref[2 * (t0 + u) + k], sems.at[k]).start(priority=k)
        return carry

    lax.fori_loop(0, tt // ROW_DMA_UNROLL, tokens, 0)
    for k in range(2):
        pltpu.make_async_copy(h_ref, _tile_rows(xs_ref, 0, tt), sems.at[k]).wait()


def _dispatch(counts, dest, h3, n_slots):
    n = h3.shape[0] // ROW_SLABS
    tt = DISPATCH_TILE
    grid_spec = pltpu.PrefetchScalarGridSpec(
        num_scalar_prefetch=1,
        grid=(n // tt,),
        in_specs=[pl.BlockSpec((2 * tt,), lambda i, cnt: (i,), memory_space=pltpu.SMEM),
                  pl.BlockSpec((tt * ROW_SLABS, LANES), lambda i, cnt: (i, 0))],
        out_specs=pl.BlockSpec(memory_space=pl.ANY),
        scratch_shapes=[pltpu.VMEM((EXPERT_BLOCK * ROW_SLABS, LANES), F32), pltpu.SemaphoreType.DMA((3,))],
    )
    return pl.pallas_call(
        _dispatch_body,
        grid_spec=grid_spec,
        out_shape=jax.ShapeDtypeStruct((n_slots * ROW_SLABS, LANES), F32),
        compiler_params=_cparams(("arbitrary",), VMEM_LIMIT),
        name="dispatch",
    )(counts, dest, h3)


def _experts_body(be_ref, nu_ref, xs_ref, w1_ref, w3_ref, w2_ref, ys_ref, w1b, w3b, w2b):
    i = pl.program_id(0)
    n_used = nu_ref[0]

    @pl.when(i < n_used)
    def _():
        changed = (i == 0) | (be_ref[i] != be_ref[jnp.maximum(i - 1, 0)])

        @pl.when(changed)
        def _():
            w1b[...] = w1_ref[0].astype(BF16)
            w3b[...] = w3_ref[0].astype(BF16)
            w2b[...] = w2_ref[0].astype(BF16)

        x = _load_row_tiles(xs_ref).astype(BF16)
        a = jnp.dot(x, w1b[...], preferred_element_type=F32)
        u = jnp.dot(x, w3b[...], preferred_element_type=F32)
        mid = (a * jax.nn.sigmoid(a) * u).astype(BF16)
        _store_row_tiles(ys_ref, jnp.dot(mid, w2b[...], preferred_element_type=F32))

    @pl.when(i >= n_used)
    def _():
        ys_ref[...] = jnp.zeros_like(ys_ref)


def _experts(block_expert, n_used, xs, w1, w3, w2):
    _, d, ff = w1.shape
    assert d == ROW_SLABS * LANES
    blk = EXPERT_BLOCK
    n_blocks = xs.shape[0] // (blk * ROW_SLABS)

    def slot_map(i, be, nu):
        return (jnp.minimum(i, nu[0] - 1), 0)

    def w_map(i, be, nu):
        return (be[jnp.minimum(i, nu[0] - 1)], 0, 0)

    grid_spec = pltpu.PrefetchScalarGridSpec(
        num_scalar_prefetch=2,
        grid=(n_blocks,),
        in_specs=[pl.BlockSpec((blk * ROW_SLABS, LANES), slot_map),
                  pl.BlockSpec((1, d, ff), w_map), pl.BlockSpec((1, d, ff), w_map),
                  pl.BlockSpec((1, ff, d), w_map)],
        out_specs=pl.BlockSpec((blk * ROW_SLABS, LANES), lambda i, be, nu: (i, 0)),
        scratch_shapes=[pltpu.VMEM((d, ff), BF16), pltpu.VMEM((d, ff), BF16), pltpu.VMEM((ff, d), BF16)],
    )
    return pl.pallas_call(
        _experts_body,
        grid_spec=grid_spec,
        out_shape=jax.ShapeDtypeStruct(xs.shape, F32),
        compiler_params=_cparams(("arbitrary",), VMEM_LIMIT),
        name="experts",
    )(block_expert, n_used, xs, w1, w3, w2)


def _combine_body(dest_ref, dest_next_ref, gate_ref, x_ref, ys_ref, out_ref, buf, sems):
    i = pl.program_id(0)
    n_steps = pl.num_programs(0)
    tt = x_ref.shape[0]

    def issue(dref, slot):
        def tokens(g, carry):
            t0 = pl.multiple_of(g * ROW_DMA_UNROLL, ROW_DMA_UNROLL)
            for u in range(ROW_DMA_UNROLL):
                for k in range(2):
                    _row_copy(ys_ref, dref[2 * (t0 + u) + k], buf.at[slot, k], t0 + u,
                              sems.at[slot, k]).start(priority=k)
            return carry

        lax.fori_loop(0, tt // ROW_DMA_UNROLL, tokens, 0)

    @pl.when(i == 0)
    def _():
        issue(dest_ref, 0)

    @pl.when(i + 1 < n_steps)
    def _():
        issue(dest_next_ref, (i + 1) % 2)

    slot = i % 2
    for k in range(2):
        pltpu.make_async_copy(_tile_rows(ys_ref, 0, tt), buf.at[slot, k], sems.at[slot, k]).wait()
    g = gate_ref[...]
    out_ref[...] = (x_ref[...] + g[:, 0:1] * _load_row_tiles(buf.at[slot, 0])
                    + g[:, 1:2] * _load_row_tiles(buf.at[slot, 1]))


def _combine(dest, gates, x2d, ys):
    n, d = x2d.shape
    tt = COMBINE_TILE
    row = lambda w: pl.BlockSpec((tt, w), lambda i: (i, 0))
    last = n // tt - 1
    return pl.pallas_call(
        _combine_body,
        grid=(n // tt,),
        in_specs=[pl.BlockSpec((2 * tt,), lambda i: (i,), memory_space=pltpu.SMEM),
                  pl.BlockSpec((2 * tt,), lambda i: (jnp.minimum(i + 1, last),), memory_space=pltpu.SMEM),
                  row(LANES), row(d), pl.BlockSpec(memory_space=pl.ANY)],
        out_specs=row(d),
        out_shape=jax.ShapeDtypeStruct((n, d), F32),
        scratch_shapes=[pltpu.VMEM((2, 2, tt * ROW_SLABS, LANES), F32), pltpu.SemaphoreType.DMA((2, 2))],
        compiler_params=_cparams(("arbitrary",), VMEM_LIMIT),
        name="combine",
    )(dest, dest, gates, x2d, ys)


def _moe(x2d, norm_w, w_group, w_router, w1, w3, w2):
    n, d = x2d.shape
    blk = EXPERT_BLOCK
    h3, gates, meta, cnt = _router(x2d, norm_w, w_group, w_router)
    counts = cnt[0, :N_EXPERTS].astype(I32)
    dest = _slots(meta, cnt)[:, 0:2].reshape(-1)
    n_slots = 2 * n + N_EXPERTS * blk
    n_blocks = n_slots // blk
    pends = jnp.cumsum((counts + blk - 1) // blk * blk)
    block_start = jnp.arange(n_blocks, dtype=I32) * blk
    block_expert = jnp.minimum(jnp.sum((pends[None, :] <= block_start[:, None]).astype(I32), axis=1),
                               N_EXPERTS - 1)
    n_used = (pends[-1:] // blk).astype(I32)
    xs = _dispatch(counts, dest, h3, n_slots)
    ys = _experts(block_expert, n_used, xs, w1, w3, w2)
    return _combine(dest, gates, x2d, ys)


def _layer(x, mem, norm_mix_w, w_in, attn_q_norm_w, attn_k_norm_w, ret_decay_f, ret_decay_b, ret_gn_w, w_out,
           norm_mem_w, norm_memkv_w, w_mq, w_mkv, mem_q_norm_w, mem_k_norm_w, w_mo,
           norm_moe_w, w_group, w_router, w_exp_gate, w_exp_up, w_exp_down):
    b, s, d = x.shape
    x2d = x.reshape(b * s, d)
    q, k, v, q_pl, k_pl, v_pl, rq, rk, rv, rg = _in_proj(x2d, norm_mix_w, w_in, attn_q_norm_w, attn_k_norm_w)
    attn = _attention(q, k, v, q_pl, k_pl, v_pl, b, s)
    ret = _retention(rq, rk, rv, rg, ret_decay_f, ret_decay_b, ret_gn_w, b, s)
    x1 = _out_proj(attn, ret, x2d, w_out)
    mk, mv = _mem_kv(mem, norm_memkv_w, w_mkv, mem_k_norm_w)
    x2 = _mem_attn(x1.reshape(b, s, d), mk, mv, norm_mem_w, w_mq, mem_q_norm_w, w_mo)
    x3 = _moe(x2.reshape(b * s, d), norm_moe_w, w_group, w_router, w_exp_gate, w_exp_up, w_exp_down)
    return x3.reshape(b, s, d)


def kernel(x, mem, norm_mix_w, w_in, attn_q_norm_w, attn_k_norm_w, ret_decay_f, ret_decay_b, ret_gn_w, w_out,
           norm_mem_w, norm_memkv_w, w_mq, w_mkv, mem_q_norm_w, mem_k_norm_w, w_mo, norm_moe_w, w_group,
           w_router, w_exp_gate, w_exp_up, w_exp_down):
    depth = norm_mix_w.shape[0]
    for l in range(depth):
        x = _layer(x, mem, norm_mix_w[l], w_in[l], attn_q_norm_w[l], attn_k_norm_w[l], ret_decay_f[l],
                   ret_decay_b[l], ret_gn_w[l], w_out[l], norm_mem_w[l], norm_memkv_w[l], w_mq[l], w_mkv[l],
                   mem_q_norm_w[l], mem_k_norm_w[l], w_mo[l], norm_moe_w[l], w_group[l], w_router[l],
                   w_exp_gate[l], w_exp_up[l], w_exp_down[l])
    return x
```

```python
import functools

import numpy as np
import jax
import jax.numpy as jnp
from jax import lax
from jax.experimental import pallas as pl
from jax.experimental.pallas import tpu as pltpu

F32 = jnp.float32
BF16 = jnp.bfloat16
I32 = jnp.int32

NORM_EPS = 1e-6
GN_EPS = 1e-5
NEG_INF = -1e30

ATTN_HEADS = 8
ATTN_HEAD_DIM = 64
ATTN_WIDTH = ATTN_HEADS * ATTN_HEAD_DIM
ATTN_HALF = 64
PLANES = 16
ATTN_TILE = PLANES * ATTN_HALF
RET_HEADS = 4
RET_QK_DIM = 64
RET_V_DIM = 128
RET_QK_WIDTH = RET_HEADS * RET_QK_DIM
RET_V_WIDTH = RET_HEADS * RET_V_DIM
MEM_HEADS = 4
N_GROUPS = 4
EXPERTS_PER_GROUP = 8
N_EXPERTS = N_GROUPS * EXPERTS_PER_GROUP

LANES = 128
ROUTER_TILE = 512
ROUTER_SUBTILE = 512
MATMUL_TILE = 1024
ATTN_BLOCKS_PER_STEP = 4
LOG2_E = 1.4426950408889634
RET_CHUNK = 256
EXPERT_BLOCK = 512
EXPERT_FF_CHUNK = 256
DISPATCH_TILE = 1024
COMBINE_TILE = 256
ROW_DMA_UNROLL = 16
SLOTS_TILE = 4096
VMEM_LIMIT = 56 * 1024 * 1024


def _cparams(sem, vmem=None):
    return pltpu.CompilerParams(dimension_semantics=sem, vmem_limit_bytes=vmem)


def _split_hi_lo(x):
    hi = x.astype(BF16)
    lo = (x - hi.astype(F32)).astype(BF16)
    return hi, lo


def _rms(x, w):
    ms = jnp.mean(x * x, axis=-1, keepdims=True)
    return x * lax.rsqrt(ms + NORM_EPS) * w


ROW_SLABS = 8


def _store_row_tiles(ref, val):
    rows = val.shape[0]
    for j in range(ROW_SLABS):
        ref[pl.ds(j, rows, stride=ROW_SLABS), :] = val[:, j * LANES:(j + 1) * LANES]


def _load_row_tiles(ref):
    rows = ref.shape[0] // ROW_SLABS
    return jnp.concatenate([ref[pl.ds(j, rows, stride=ROW_SLABS), :] for j in range(ROW_SLABS)], axis=1)


def _tile_rows(ref, row, count=1):
    return ref.at[pl.ds(pl.multiple_of(row * ROW_SLABS, ROW_SLABS), count * ROW_SLABS)]


def _in_proj_body(x_ref, nw_ref, w_ref, qw_ref, kw_ref, g_ref,
                  q_out, k_out, v_out, qp_out, kp_out, vp_out, rq_out, rk_out, rv_out, rg_out, slabs, quarters):
    tm = x_ref.shape[0]
    h = _rms(x_ref[...], nw_ref[...]).astype(BF16)

    def emit(val, nat_out, plane_out):
        nat_out[...] = val.astype(BF16)
        for j in range(ATTN_WIDTH // LANES):
            slabs[j] = val[:, j * LANES:(j + 1) * LANES]
        for j in range(ATTN_WIDTH // LANES):
            for r in range(4):
                quarters[j, r] = slabs[j, pl.ds(r, tm // 4, stride=4), :]
        for r in range(4):
            for q in range(4):
                for j in range(ATTN_WIDTH // LANES):
                    rows = quarters[j, r, pl.ds(q, tm // PLANES, stride=4), :]
                    plane_out[0, r + 4 * q, :, j * LANES:(j + 1) * LANES] = rows.astype(BF16)

    def proj(a, b):
        return jnp.dot(h, w_ref[:, a:b], preferred_element_type=F32)

    def head_norm(p, w):
        s = jnp.dot((p * p).astype(BF16), g_ref[...], preferred_element_type=F32)
        return p * lax.rsqrt(s * (1.0 / ATTN_HEAD_DIM) + NORM_EPS) * w

    a = ATTN_WIDTH
    emit(head_norm(proj(0, a), qw_ref[...]), q_out, qp_out)
    emit(head_norm(proj(a, 2 * a), kw_ref[...]), k_out, kp_out)
    emit(proj(2 * a, 3 * a), v_out, vp_out)
    c = 3 * a
    rq_out[...] = proj(c, c + RET_QK_WIDTH).astype(BF16)
    rk_out[...] = proj(c + RET_QK_WIDTH, c + 2 * RET_QK_WIDTH).astype(BF16)
    c += 2 * RET_QK_WIDTH
    rv_out[...] = proj(c, c + RET_V_WIDTH).astype(BF16)
    rg_out[...] = proj(c + RET_V_WIDTH, c + 2 * RET_V_WIDTH).astype(BF16)


def _in_proj(x2d, norm_w, w_in, q_norm_w, k_norm_w):
    n, d = x2d.shape
    cols = w_in.shape[1]
    tm = MATMUL_TILE
    qw = (jnp.tile(q_norm_w, ATTN_HEADS) * (ATTN_HEAD_DIM ** -0.5 * LOG2_E)).reshape(1, ATTN_WIDTH)
    kw = jnp.tile(k_norm_w, ATTN_HEADS).reshape(1, ATTN_WIDTH)
    head_of = np.arange(ATTN_WIDTH) // ATTN_HEAD_DIM
    gmat = jnp.asarray(head_of[:, None] == head_of[None, :], dtype=BF16)
    row = lambda w: pl.BlockSpec((tm, w), lambda i: (i, 0))
    full = lambda r, c: pl.BlockSpec((r, c), lambda i: (0, 0))
    per_tile = ATTN_TILE // tm
    plane = pl.BlockSpec((1, PLANES, tm // PLANES, ATTN_WIDTH), lambda i: (i // per_tile, 0, i % per_tile, 0))
    plane_shape = jax.ShapeDtypeStruct((n // ATTN_TILE, PLANES, ATTN_TILE // PLANES, ATTN_WIDTH), BF16)
    ret_widths = (RET_QK_WIDTH, RET_QK_WIDTH, RET_V_WIDTH, RET_V_WIDTH)
    return pl.pallas_call(
        _in_proj_body,
        grid=(n // tm,),
        in_specs=[row(d), full(1, d), full(d, cols), full(1, ATTN_WIDTH), full(1, ATTN_WIDTH),
                  full(ATTN_WIDTH, ATTN_WIDTH)],
        out_specs=[row(ATTN_WIDTH)] * 3 + [plane] * 3 + [row(w) for w in ret_widths],
        out_shape=([jax.ShapeDtypeStruct((n, ATTN_WIDTH), BF16)] * 3 + [plane_shape] * 3
                   + [jax.ShapeDtypeStruct((n, w), BF16) for w in ret_widths]),
        scratch_shapes=[pltpu.VMEM((ATTN_WIDTH // LANES, tm, LANES), F32),
                        pltpu.VMEM((ATTN_WIDTH // LANES, 4, tm // 4, LANES), F32)],
        compiler_params=_cparams(("parallel",), VMEM_LIMIT),
        name="in_proj",
    )(x2d, norm_w.reshape(1, d), w_in.astype(BF16), qw, kw, gmat)


def _attn_bias(dilation, interleave=1):
    half = ATTN_HALF
    idx = np.arange(half)
    sub = (idx % (half // interleave)) * interleave + idx // (half // interleave)
    qi = sub[:, None]
    kc = (np.arange(3)[:, None] * half + sub[None, :]).reshape(1, -1)
    seg = np.repeat(np.arange(3), half)[None, :]
    delta = kc - half - qi
    band = np.abs(delta) <= half
    slopes = np.exp2(-8.0 * np.arange(1, ATTN_HEADS + 1) / ATTN_HEADS)
    valid = [band & (seg >= 1), band, band & (seg <= 1)]
    out = np.empty((ATTN_HEADS // 2, 3, 2 * half, 3 * half), np.float32)
    for p in range(ATTN_HEADS // 2):
        for v in range(3):
            for s in range(2):
                b = -slopes[2 * p + s] * LOG2_E * (dilation * np.abs(delta)).astype(np.float32)
                out[p, v, s * half:(s + 1) * half] = np.where(valid[v], b, NEG_INF)
    return jnp.asarray(out.reshape(-1, 2 * half, 3 * half))


def _attend(chains):
    half = ATTN_HALF
    lo_lanes = lax.broadcasted_iota(I32, (half, LANES), 1) < ATTN_HEAD_DIM
    ones = jnp.ones((3 * half, LANES), BF16)
    scores = []
    for q, k, v, bias in chains:
        zero = jnp.zeros_like(q)
        qs = jnp.concatenate([jnp.where(lo_lanes, q, zero), jnp.where(lo_lanes, zero, q)], axis=0)
        scores.append(lax.dot_general(qs, k, (((1,), (1,)), ((), ())), preferred_element_type=F32))
    probs, maxes = [], []
    for (q, k, v, bias), s in zip(chains, scores):
        s = s + bias
        m = jnp.max(s, axis=-1, keepdims=True)
        probs.append(jnp.exp2(s - m).astype(BF16))
        maxes.append(m)
    results = []
    for (q, k, v, bias), e in zip(chains, probs):
        vext = jnp.concatenate([v, ones], axis=1)
        results.append(jnp.dot(e, vext, preferred_element_type=F32))
    outs = []
    for r, m in zip(results, maxes):
        acc = jnp.where(lo_lanes, r[:half, :LANES], r[half:, :LANES])
        denom = jnp.where(lo_lanes, r[:half, LANES:], r[half:, LANES:])
        mm = jnp.where(lo_lanes, m[:half], m[half:])
        outs.append((acc / denom, mm + jnp.log2(denom)))
    return outs


def _merge(oa, la, ob, lb):
    m = jnp.maximum(la, lb)
    ea = jnp.exp2(la - m)
    eb = jnp.exp2(lb - m)
    den = ea + eb
    return (ea * oa + eb * ob) / den, m + jnp.log2(den)


def _attn_body(q_ref, kp_ref, km_ref, kn_ref, vp_ref, vm_ref, vn_ref,
               qpl_ref, kplp_ref, kplm_ref, kpln_ref, vplp_ref, vplm_ref, vpln_ref,
               b1_ref, b4_ref, b16_ref, o_ref, kcat, vcat, o_far, l_far):
    half = ATTN_HALF
    t = pl.program_id(1)
    n_tiles = pl.num_programs(1)
    tile = q_ref.shape[1]
    pairs = ATTN_HEADS // 2
    piece = half // 4

    kcat[0:half] = kp_ref[0]
    kcat[half:half + tile] = km_ref[0]
    kcat[half + tile:] = kn_ref[0]
    vcat[0:half] = vp_ref[0]
    vcat[half:half + tile] = vm_ref[0]
    vcat[half + tile:] = vn_ref[0]

    def edge(first, last):
        return jnp.where(first, 0, jnp.where(last, 2, 1))

    def far_step(r, carry):
        v16 = edge(t == 0, t == n_tiles - 1)
        for pr in range(pairs):
            cs = slice(pr * LANES, (pr + 1) * LANES)
            chains = []
            for m in range(4):
                c = r + 4 * m
                k3 = jnp.concatenate([kplp_ref[0, c, :, cs], kplm_ref[0, c, :, cs], kpln_ref[0, c, :, cs]], axis=0)
                v3 = jnp.concatenate([vplp_ref[0, c, :, cs], vplm_ref[0, c, :, cs], vpln_ref[0, c, :, cs]], axis=0)
                chains.append((qpl_ref[0, c, :, cs], k3, v3, b16_ref[pr * 3 + v16]))
            for nb in range(4):
                def rows(main, before, after, a0):
                    ref, lo = (before, a0 + half) if a0 < 0 else (after, a0 - half) if a0 >= half else (main, a0)
                    return [ref[0, r + 4 * m, lo:lo + piece, cs] for m in range(4)]
                a0 = nb * piece
                q4 = jnp.concatenate(rows(qpl_ref, None, None, a0), axis=0)
                k4 = jnp.concatenate(sum((rows(kplm_ref, kplp_ref, kpln_ref, a0 + d) for d in (-piece, 0, piece)), []),
                                     axis=0)
                v4 = jnp.concatenate(sum((rows(vplm_ref, vplp_ref, vpln_ref, a0 + d) for d in (-piece, 0, piece)), []),
                                     axis=0)
                v4e = edge((t == 0) & (nb == 0), (t == n_tiles - 1) & (nb == 3))
                chains.append((q4, k4, v4, b4_ref[pr * 3 + v4e]))
            res = _attend(chains)
            for m in range(4):
                o16, l16 = res[m]
                o_rows, l_rows = [], []
                for nb in range(4):
                    o4, l4 = res[4 + nb]
                    sl = slice(nb * piece, (nb + 1) * piece)
                    s4 = slice(m * piece, (m + 1) * piece)
                    om, lm = _merge(o16[sl], l16[sl], o4[s4], l4[s4])
                    o_rows.append(om)
                    l_rows.append(lm)
                dst = pl.ds(r + 4 * m, half, stride=PLANES)
                o_far[pr, dst, :] = jnp.concatenate(o_rows, axis=0)
                l_far[pr, dst, :] = jnp.concatenate(l_rows, axis=0)
        return carry

    for r in range(4):
        far_step(r, 0)

    unroll = ATTN_BLOCKS_PER_STEP
    n_blocks = n_tiles * (tile // half)

    def near_step(it, carry):
        chains, where = [], []
        for u in range(unroll):
            jb = it * unroll + u
            r0 = jb * half
            gb = t * (tile // half) + jb
            variant = edge(gb == 0, gb == n_blocks - 1)
            for pr in range(pairs):
                cs = slice(pr * LANES, (pr + 1) * LANES)
                chains.append((q_ref[0, pl.ds(r0, half), cs], kcat[pl.ds(r0, 3 * half), cs],
                               vcat[pl.ds(r0, 3 * half), cs], b1_ref[pr * 3 + variant]))
                where.append((r0, pr, cs))
        for (r0, pr, cs), (o1, l1) in zip(where, _attend(chains)):
            o, _ = _merge(o1, l1, o_far[pr, pl.ds(r0, half), :], l_far[pr, pl.ds(r0, half), :])
            o_ref[0, pl.ds(r0, half), cs] = o.astype(o_ref.dtype)
        return carry

    for it in range(tile // half // unroll):
        near_step(it, 0)


def _attention(q, k, v, q_pl, k_pl, v_pl, b, s):
    w = ATTN_WIDTH
    half = ATTN_HALF
    tile = ATTN_TILE
    n_tiles = s // tile
    hb = tile // half
    last = s // half - 1
    main = pl.BlockSpec((1, tile, w), lambda bb, t: (bb, t, 0))
    prev = pl.BlockSpec((1, half, w), lambda bb, t: (bb, jnp.maximum(t * hb - 1, 0), 0))
    nxt = pl.BlockSpec((1, half, w), lambda bb, t: (bb, jnp.minimum((t + 1) * hb, last), 0))
    pshape = (1, PLANES, tile // PLANES, w)
    pl_main = pl.BlockSpec(pshape, lambda bb, t: (bb * n_tiles + t, 0, 0, 0))
    pl_prev = pl.BlockSpec(pshape, lambda bb, t: (bb * n_tiles + jnp.maximum(t - 1, 0), 0, 0, 0))
    pl_next = pl.BlockSpec(pshape, lambda bb, t: (bb * n_tiles + jnp.minimum(t + 1, n_tiles - 1), 0, 0, 0))
    biases = [_attn_bias(1), _attn_bias(4, interleave=4), _attn_bias(16)]
    bias_spec = pl.BlockSpec(biases[0].shape, lambda bb, t: (0, 0, 0))
    nat = lambda a: a.reshape(b, s, w)
    out = pl.pallas_call(
        _attn_body,
        grid=(b, n_tiles),
        in_specs=[main, prev, main, nxt, prev, main, nxt,
                  pl_main, pl_prev, pl_main, pl_next, pl_prev, pl_main, pl_next,
                  bias_spec, bias_spec, bias_spec],
        out_specs=main,
        out_shape=jax.ShapeDtypeStruct((b, s, w), BF16),
        scratch_shapes=[pltpu.VMEM((tile + 2 * half, w), BF16), pltpu.VMEM((tile + 2 * half, w), BF16),
                        pltpu.VMEM((ATTN_HEADS // 2, tile, LANES), F32),
                        pltpu.VMEM((ATTN_HEADS // 2, tile, LANES), F32)],
        compiler_params=_cparams(("parallel", "parallel"), VMEM_LIMIT),
        name="attention",
    )(nat(q), nat(k), nat(k), nat(k), nat(v), nat(v), nat(v),
      q_pl, k_pl, k_pl, k_pl, v_pl, v_pl, v_pl, *biases)
    return out.reshape(b * s, w)


def _ret_body(lg_ref, q_ref, k_ref, v_ref, g_ref, gnw_ref, out_ref,
              dmat, qdec, kdec, cdec, fstate, rstate, rall, *, chunk):
    c = chunk
    b = pl.program_id(0)
    ph = pl.program_id(1)
    n = pl.program_id(2)
    n_chunks = pl.num_programs(2)
    k_scale = RET_QK_DIM ** -0.5
    pairs = RET_HEADS // 2
    pw = 2 * RET_QK_DIM
    vw = 2 * RET_V_DIM
    first = lax.broadcasted_iota(I32, (c, pw), 1) < RET_QK_DIM
    diag = ((lax.broadcasted_iota(I32, (pw, vw), 0) < RET_QK_DIM)
            == (lax.broadcasted_iota(I32, (pw, vw), 1) < RET_V_DIM))

    @pl.when((b == 0) & (ph == 0) & (n == 0))
    def _init_tables():
        ii = lax.broadcasted_iota(I32, (c, c), 0)
        jj = lax.broadcasted_iota(I32, (c, c), 1)
        fwd = (ii - jj).astype(F32)
        ri = lax.broadcasted_iota(I32, (c, pw), 0).astype(F32)
        top = lax.broadcasted_iota(I32, (pw, vw), 0) < RET_QK_DIM
        for h in range(RET_HEADS):
            dmat[h] = jnp.where(ii >= jj, jnp.exp(lg_ref[0, h] * fwd), jnp.exp(-lg_ref[1, h] * fwd)) * k_scale
        for pr in range(pairs):
            lf = jnp.where(first, lg_ref[0, 2 * pr], lg_ref[0, 2 * pr + 1])
            lb = jnp.where(first, lg_ref[1, 2 * pr], lg_ref[1, 2 * pr + 1])
            qdec[0, pr] = jnp.exp(lf * (ri + 1.0))
            qdec[1, pr] = jnp.exp(lb * (c - ri))
            kdec[0, pr] = jnp.exp(lf * (c - 1.0 - ri)) * k_scale
            kdec[1, pr] = jnp.exp(lb * ri) * k_scale
            for direction in range(2):
                cdec[direction, pr] = jnp.exp(jnp.where(top, lg_ref[direction, 2 * pr],
                                                        lg_ref[direction, 2 * pr + 1]) * c)

    def kv_update(direction, pr):
        kp = k_ref[0, :, pr * pw:(pr + 1) * pw].astype(F32)
        ks = (kp * kdec[direction, pr]).astype(BF16)
        new = lax.dot_general(ks, v_ref[0, :, pr * vw:(pr + 1) * vw], (((0,), (0,)), ((), ())),
                              preferred_element_type=F32)
        return jnp.where(diag, new, 0.0)

    @pl.when(ph == 0)
    def _right_to_left():
        @pl.when(n == 0)
        def _():
            rstate[...] = jnp.zeros_like(rstate)

        ci = n_chunks - 1 - n
        for pr in range(pairs):
            st = rstate[pr]
            rall[ci, pr] = st
            rstate[pr] = st * cdec[1, pr] + kv_update(1, pr)

    @pl.when(ph == 1)
    def _left_to_right():
        @pl.when(n == 0)
        def _():
            fstate[...] = jnp.zeros_like(fstate)

        for pr in range(pairs):
            qp = q_ref[0, :, pr * pw:(pr + 1) * pw]
            kp = k_ref[0, :, pr * pw:(pr + 1) * pw]
            qf = qp.astype(F32)
            qcat = jnp.concatenate([(qf * qdec[0, pr]).astype(BF16), (qf * qdec[1, pr]).astype(BF16)], axis=1)
            st = fstate[pr]
            states = jnp.concatenate([st, rall[n, pr]], axis=0).astype(BF16)
            cross = jnp.dot(qcat, states, preferred_element_type=F32)
            fstate[pr] = st * cdec[0, pr] + kv_update(0, pr)
            zero = jnp.zeros_like(qp)
            for a in range(2):
                h = 2 * pr + a
                vs = slice(h * RET_V_DIM, (h + 1) * RET_V_DIM)
                qm = jnp.where(first, qp, zero) if a == 0 else jnp.where(first, zero, qp)
                s = lax.dot_general(qm, kp, (((1,), (1,)), ((), ())), preferred_element_type=F32) * dmat[h]
                y = jnp.dot(s.astype(BF16), v_ref[0, :, vs], preferred_element_type=F32)
                y = y + cross[:, a * RET_V_DIM:(a + 1) * RET_V_DIM]
                mu = jnp.mean(y, axis=-1, keepdims=True)
                yc = y - mu
                var = jnp.mean(yc * yc, axis=-1, keepdims=True)
                yn = yc * lax.rsqrt(var + GN_EPS) * gnw_ref[:, vs]
                gate = g_ref[0, :, vs].astype(F32)
                out_ref[0, :, vs] = (gate * jax.nn.sigmoid(gate) * yn).astype(out_ref.dtype)


def _retention(rq, rk, rv, rg, decay_f, decay_b, gn_w, b, s):
    c = RET_CHUNK
    nc = s // c
    lg = jnp.stack([jax.nn.log_sigmoid(decay_f.astype(F32)), jax.nn.log_sigmoid(decay_b.astype(F32))])
    qk_w, v_w = RET_QK_WIDTH, RET_V_WIDTH
    pairs, pair_qk, pair_v = RET_HEADS // 2, 2 * RET_QK_DIM, 2 * RET_V_DIM

    def both(bb, ph, n):
        return (bb, jnp.where(ph == 0, nc - 1 - n, n), 0)

    def fwd_only(bb, ph, n):
        return (bb, jnp.where(ph == 0, 0, n), 0)

    out = pl.pallas_call(
        functools.partial(_ret_body, chunk=c),
        grid=(b, 2, nc),
        in_specs=[pl.BlockSpec(memory_space=pltpu.SMEM),
                  pl.BlockSpec((1, c, qk_w), fwd_only),
                  pl.BlockSpec((1, c, qk_w), both),
                  pl.BlockSpec((1, c, v_w), both),
                  pl.BlockSpec((1, c, v_w), fwd_only),
                  pl.BlockSpec((1, v_w), lambda bb, ph, n: (0, 0))],
        out_specs=pl.BlockSpec((1, c, v_w), fwd_only),
        out_shape=jax.ShapeDtypeStruct((b, s, v_w), BF16),
        scratch_shapes=[pltpu.VMEM((RET_HEADS, c, c), F32),
                        pltpu.VMEM((2, pairs, c, pair_qk), F32),
                        pltpu.VMEM((2, pairs, c, pair_qk), F32),
                        pltpu.VMEM((2, pairs, pair_qk, pair_v), F32),
                        pltpu.VMEM((pairs, pair_qk, pair_v), F32),
                        pltpu.VMEM((pairs, pair_qk, pair_v), F32),
                        pltpu.VMEM((nc, pairs, pair_qk, pair_v), F32)],
        compiler_params=_cparams(("arbitrary", "arbitrary", "arbitrary"), VMEM_LIMIT),
        name="retention",
    )(lg, rq.reshape(b, s, qk_w), rk.reshape(b, s, qk_w), rv.reshape(b, s, v_w), rg.reshape(b, s, v_w),
      gn_w.reshape(1, v_w))
    return out.reshape(b * s, v_w)


def _mem_kv_body(mem_ref, nw_ref, w_ref, kw_ref, k_out, v_out):
    d = mem_ref.shape[-1]
    hd = d // MEM_HEADS
    h = _rms(mem_ref[0], nw_ref[...]).astype(BF16)
    kv = jnp.dot(h, w_ref[...], preferred_element_type=F32)
    for i in range(MEM_HEADS):
        k_out[0, :, i * hd:(i + 1) * hd] = _rms(kv[:, i * hd:(i + 1) * hd], kw_ref[...]).astype(BF16)
    v_out[0] = kv[:, d:].astype(BF16)


def _mem_kv(mem, norm_w, w_mkv, k_norm_w):
    b, m, d = mem.shape
    return pl.pallas_call(
        _mem_kv_body,
        grid=(b,),
        in_specs=[pl.BlockSpec((1, m, d), lambda i: (i, 0, 0)),
                  pl.BlockSpec((1, d), lambda i: (0, 0)),
                  pl.BlockSpec((d, 2 * d), lambda i: (0, 0)),
                  pl.BlockSpec((1, d // MEM_HEADS), lambda i: (0, 0))],
        out_specs=[pl.BlockSpec((1, m, d), lambda i: (i, 0, 0))] * 2,
        out_shape=[jax.ShapeDtypeStruct((b, m, d), BF16)] * 2,
        compiler_params=_cparams(("parallel",), VMEM_LIMIT),
        name="mem_kv",
    )(mem, norm_w.reshape(1, d), w_mkv.astype(BF16), k_norm_w.reshape(1, -1))


def _mem_attn_body(x_ref, attn_ref, ret_ref, wout_ref, nw_ref, wq_ref, qw_ref, k_ref, v_ref, wo_ref, out_ref):
    d = x_ref.shape[-1]
    hd = d // MEM_HEADS
    x = (x_ref[0] + jnp.dot(attn_ref[0], wout_ref[:ATTN_WIDTH], preferred_element_type=F32)
         + jnp.dot(ret_ref[0], wout_ref[ATTN_WIDTH:], preferred_element_type=F32))
    h = _rms(x, nw_ref[...]).astype(BF16)
    q = jnp.dot(h, wq_ref[...], preferred_element_type=F32)
    heads = []
    for i in range(MEM_HEADS):
        cs = slice(i * hd, (i + 1) * hd)
        qn = _rms(q[:, cs], qw_ref[...]).astype(BF16)
        s = lax.dot_general(qn, k_ref[0, :, cs], (((1,), (1,)), ((), ())), preferred_element_type=F32)
        e = jnp.exp(s - jnp.max(s, axis=-1, keepdims=True))
        o = jnp.dot(e.astype(BF16), v_ref[0, :, cs], preferred_element_type=F32)
        heads.append((o / jnp.sum(e, axis=-1, keepdims=True)).astype(BF16))
    o = jnp.concatenate(heads, axis=1)
    out_ref[0] = x + jnp.dot(o, wo_ref[...], preferred_element_type=F32)


def _mem_attn(x, attn, ret, w_out, mk, mv, norm_w, w_mq, q_norm_w, w_mo):
    b, s, d = x.shape
    m = mk.shape[1]
    tm = MATMUL_TILE
    hd = d // MEM_HEADS
    qw = (q_norm_w * (hd ** -0.5)).reshape(1, hd)
    tok = lambda w: pl.BlockSpec((1, tm, w), lambda bb, i: (bb, i, 0))
    const = lambda r, c: pl.BlockSpec((r, c), lambda bb, i: (0, 0))
    mem = pl.BlockSpec((1, m, d), lambda bb, i: (bb, 0, 0))
    return pl.pallas_call(
        _mem_attn_body,
        grid=(b, s // tm),
        in_specs=[tok(d), tok(ATTN_WIDTH), tok(RET_V_WIDTH), const(ATTN_WIDTH + RET_V_WIDTH, d),
                  const(1, d), const(d, d), const(1, hd), mem, mem, const(d, d)],
        out_specs=tok(d),
        out_shape=jax.ShapeDtypeStruct((b, s, d), F32),
        compiler_params=_cparams(("parallel", "parallel"), VMEM_LIMIT),
        name="mem_attn",
    )(x, attn.reshape(b, s, ATTN_WIDTH), ret.reshape(b, s, RET_V_WIDTH), w_out.astype(BF16),
      norm_w.reshape(1, d), w_mq.astype(BF16), qw, mk, mv, w_mo.astype(BF16))


def _router_body(x_ref, nw_ref, w_ref, h_out, gate_out, meta_out, cnt_out, carry, below):
    tm = x_ref.shape[0]
    i = pl.program_id(0)

    sub = below.shape[0]

    @pl.when(i == 0)
    def _():
        carry[...] = jnp.zeros_like(carry)
        rr = lax.broadcasted_iota(I32, (sub, sub), 0)
        cc = lax.broadcasted_iota(I32, (sub, sub), 1)
        below[...] = (cc < rr).astype(F32).astype(BF16)

    base = carry[...]
    for part in range(tm // sub):
        rows = slice(part * sub, (part + 1) * sub)
        base = _route_rows(x_ref[rows, :], nw_ref, w_ref, below, base,
                           h_out.at[pl.ds(part * sub * ROW_SLABS, sub * ROW_SLABS)],
                           gate_out.at[rows], meta_out.at[rows])
    carry[...] = base
    cnt_out[...] = base


def _route_rows(x, nw_ref, w_ref, below, base, h_out, gate_out, meta_out):
    tm = x.shape[0]
    h = _rms(x, nw_ref[...])
    _store_row_tiles(h_out, h)
    hi, lo = _split_hi_lo(h)
    both = jnp.dot(hi, w_ref[...], preferred_element_type=F32)
    logits = (both[:, :LANES] + both[:, LANES:]
              + jnp.dot(lo, w_ref[:, :LANES], preferred_element_type=F32))
    lane = lax.broadcasted_iota(I32, (tm, LANES), 1)
    lane_f = lane.astype(F32)

    def argmax_lanes(vals):
        top = jnp.max(vals, axis=-1, keepdims=True)
        idx = jnp.min(jnp.where(vals == top, lane_f, float(LANES)), axis=-1, keepdims=True)
        return top, idx

    is_group = lane < N_GROUPS
    g_top, grp = argmax_lanes(jnp.where(is_group, logits, -jnp.inf))
    g_gate = 1.0 / jnp.sum(jnp.where(is_group, jnp.exp(logits - g_top), 0.0), axis=-1, keepdims=True)
    first = N_GROUPS + EXPERTS_PER_GROUP * grp
    in_group = (lane_f >= first) & (lane_f < first + EXPERTS_PER_GROUP)
    el = jnp.where(in_group, logits, -jnp.inf)
    t1, i1 = argmax_lanes(el)
    t2, i2 = argmax_lanes(jnp.where(lane_f == i1, -jnp.inf, el))
    z = jnp.exp(t2 - t1)
    g1 = g_gate / (1.0 + z)
    g2 = g_gate * z / (1.0 + z)
    e1 = i1 - N_GROUPS
    e2 = i2 - N_GROUPS

    oh1 = (lane_f == e1)
    oh2 = (lane_f == e2)
    oh1b = oh1.astype(F32).astype(BF16)
    oh2b = oh2.astype(F32).astype(BF16)
    pre = jnp.dot(below[...], jnp.concatenate([oh1b, oh2b], axis=1), preferred_element_type=F32)
    pre1, pre2 = pre[:, :LANES], pre[:, LANES:]
    cnt1 = jnp.sum(oh1.astype(F32), axis=0, keepdims=True)
    cnt2 = jnp.sum(oh2.astype(F32), axis=0, keepdims=True)
    r1 = jnp.sum(jnp.where(oh1, pre1 + base, 0.0), axis=-1, keepdims=True)
    r2 = jnp.sum(jnp.where(oh2, pre2 + base + cnt1, 0.0), axis=-1, keepdims=True)

    gate_out[...] = jnp.where(lane == 0, g1, jnp.where(lane == 1, g2, 0.0))
    meta = jnp.where(lane == 0, e1, jnp.where(lane == 1, e2, jnp.where(lane == 2, r1, jnp.where(lane == 3, r2, 0.0))))
    meta_out[...] = meta.astype(I32)
    return base + cnt1 + cnt2


def _router(x2d, norm_w, w_group, w_router):
    n, d = x2d.shape
    tm = ROUTER_TILE
    w_all = jnp.concatenate([w_group, w_router.transpose(1, 0, 2).reshape(d, N_EXPERTS)], axis=1)
    w_all = jnp.pad(w_all, ((0, 0), (0, LANES - w_all.shape[1])))
    whi = w_all.astype(BF16)
    wlo = (w_all - whi.astype(F32)).astype(BF16)
    w_split = jnp.concatenate([whi, wlo], axis=1)
    row = lambda w: pl.BlockSpec((tm, w), lambda i: (i, 0))
    const = lambda r, c: pl.BlockSpec((r, c), lambda i: (0, 0))
    return pl.pallas_call(
        _router_body,
        grid=(n // tm,),
        in_specs=[row(d), const(1, d), const(d, 2 * LANES)],
        out_specs=[pl.BlockSpec((tm * ROW_SLABS, LANES), lambda i: (i, 0)), row(LANES), row(LANES), const(1, LANES)],
        out_shape=[jax.ShapeDtypeStruct((n * ROW_SLABS, LANES), F32), jax.ShapeDtypeStruct((n, LANES), F32),
                   jax.ShapeDtypeStruct((n, LANES), I32), jax.ShapeDtypeStruct((1, LANES), F32)],
        scratch_shapes=[pltpu.VMEM((1, LANES), F32), pltpu.VMEM((ROUTER_SUBTILE, ROUTER_SUBTILE), BF16)],
        compiler_params=_cparams(("arbitrary",), VMEM_LIMIT),
        name="router",
    )(x2d, norm_w.reshape(1, d), w_split)


def _row_copy(src_ref, src_row, dst_ref, dst_row, sem):
    return pltpu.make_async_copy(_tile_rows(src_ref, src_row), _tile_rows(dst_ref, dst_row), sem)


def _slots_body(meta_ref, cnt_ref, dest_ref):
    tm = meta_ref.shape[0]
    blk = float(EXPERT_BLOCK)
    lane = lax.broadcasted_iota(I32, (tm, LANES), 1)
    lane_f = lane.astype(F32)
    blocks = jnp.floor((cnt_ref[...] + (blk - 1.0)) * (1.0 / blk))
    rr = lax.broadcasted_iota(I32, (LANES, LANES), 0)
    cc = lax.broadcasted_iota(I32, (LANES, LANES), 1)
    before = (rr < cc).astype(F32).astype(BF16)
    first_block = jnp.dot(jnp.broadcast_to(blocks, (8, LANES)).astype(BF16), before,
                          preferred_element_type=F32)[0:1]
    start = first_block * blk
    meta = meta_ref[...].astype(F32)

    def slot_of(k):
        e = jnp.sum(jnp.where(lane == k, meta, 0.0), axis=-1, keepdims=True)
        r = jnp.sum(jnp.where(lane == 2 + k, meta, 0.0), axis=-1, keepdims=True)
        return jnp.sum(jnp.where(lane_f == e, start, 0.0), axis=-1, keepdims=True) + r

    dest_ref[...] = jnp.where(lane == 0, slot_of(0), jnp.where(lane == 1, slot_of(1), 0.0)).astype(I32)


def _slots(meta, cnt):
    n = meta.shape[0]
    tm = min(SLOTS_TILE, n)
    row = pl.BlockSpec((tm, LANES), lambda i: (i, 0))
    return pl.pallas_call(
        _slots_body,
        grid=(n // tm,),
        in_specs=[row, pl.BlockSpec((1, LANES), lambda i: (0, 0))],
        out_specs=row,
        out_shape=jax.ShapeDtypeStruct((n, LANES), I32),
        compiler_params=_cparams(("parallel",), VMEM_LIMIT),
        name="slots",
    )(meta, cnt)


def _dispatch_body(cnt_ref, dest_ref, h_ref, xs_ref, zbuf, sems):
    i = pl.program_id(0)
    tt = h_ref.shape[0] // ROW_SLABS
    blk = EXPERT_BLOCK
    n_slots = xs_ref.shape[0] // ROW_SLABS

    def pad_copy(pend):
        return pltpu.make_async_copy(zbuf, _tile_rows(xs_ref, pend - blk, blk), sems.at[2])

    @pl.when(i == 0)
    def _():
        zbuf[...] = jnp.zeros_like(zbuf)

        def start(e, acc):
            pend = acc + ((cnt_ref[e] + (blk - 1)) // blk) * blk

            @pl.when(pend > acc)
            def _():
                pad_copy(pend).start()
            return pend

        used = lax.fori_loop(0, N_EXPERTS, start, jnp.int32(0))

        def tail_start(j, carry):
            pad_copy((j + 1) * blk).start()
            return carry

        lax.fori_loop(used // blk, n_slots // blk, tail_start, 0)

        def finish(e, acc):
            pend = acc + ((cnt_ref[e] + (blk - 1)) // blk) * blk

            @pl.when(pend > acc)
            def _():
                pad_copy(pend).wait()
            return pend

        lax.fori_loop(0, N_EXPERTS, finish, jnp.int32(0))

        def tail_wait(j, carry):
            pad_copy((j + 1) * blk).wait()
            return carry

        lax.fori_loop(used // blk, n_slots // blk, tail_wait, 0)

    def tokens(g, carry):
        t0 = pl.multiple_of(g * ROW_DMA_UNROLL, ROW_DMA_UNROLL)
        for u in range(ROW_DMA_UNROLL):
            for k in range(2):
                _row_copy(h_ref, t0 + u, xs_ref, dest_ref[2 * (t0 + u) + k], sems.at[k]).start(priority=k)
        return carry

    lax.fori_loop(0, tt // ROW_DMA_UNROLL, tokens, 0)
    for k in range(2):
        pltpu.make_async_copy(h_ref, _tile_rows(xs_ref, 0, tt), sems.at[k]).wait()


def _dispatch(counts, dest, h3, n_slots):
    n = h3.shape[0] // ROW_SLABS
    tt = DISPATCH_TILE
    grid_spec = pltpu.PrefetchScalarGridSpec(
        num_scalar_prefetch=1,
        grid=(n // tt,),
        in_specs=[pl.BlockSpec((2 * tt,), lambda i, cnt: (i,), memory_space=pltpu.SMEM),
                  pl.BlockSpec((tt * ROW_SLABS, LANES), lambda i, cnt: (i, 0))],
        out_specs=pl.BlockSpec(memory_space=pl.ANY),
        scratch_shapes=[pltpu.VMEM((EXPERT_BLOCK * ROW_SLABS, LANES), F32), pltpu.SemaphoreType.DMA((3,))],
    )
    return pl.pallas_call(
        _dispatch_body,
        grid_spec=grid_spec,
        out_shape=jax.ShapeDtypeStruct((n_slots * ROW_SLABS, LANES), F32),
        compiler_params=_cparams(("arbitrary",), VMEM_LIMIT),
        name="dispatch",
    )(counts, dest, h3)


def _experts_body(be_ref, nu_ref, xs_ref, w1_ref, w3_ref, w2_ref, ys_ref, w1b, w3b, w2b):
    i = pl.program_id(0)
    n_used = nu_ref[0]

    @pl.when(i < n_used)
    def _():
        changed = (i == 0) | (be_ref[i] != be_ref[jnp.maximum(i - 1, 0)])

        @pl.when(changed)
        def _():
            w1b[...] = w1_ref[0].astype(BF16)
            w3b[...] = w3_ref[0].astype(BF16)
            w2b[...] = w2_ref[0].astype(BF16)

        x = _load_row_tiles(xs_ref).astype(BF16)
        chunks = [slice(c0, c0 + EXPERT_FF_CHUNK) for c0 in range(0, w1b.shape[1], EXPERT_FF_CHUNK)]
        gates = [(jnp.dot(x, w1b[:, cs], preferred_element_type=F32),
                  jnp.dot(x, w3b[:, cs], preferred_element_type=F32)) for cs in chunks]
        y = None
        for cs, (a, u) in zip(chunks, gates):
            mid = (a * jax.nn.sigmoid(a) * u).astype(BF16)
            part = jnp.dot(mid, w2b[cs, :], preferred_element_type=F32)
            y = part if y is None else y + part
        _store_row_tiles(ys_ref, y)

    @pl.when(i >= n_used)
    def _():
        ys_ref[...] = jnp.zeros_like(ys_ref)


def _experts(block_expert, n_used, xs, w1, w3, w2):
    _, d, ff = w1.shape
    assert d == ROW_SLABS * LANES
    blk = EXPERT_BLOCK
    n_blocks = xs.shape[0] // (blk * ROW_SLABS)

    def slot_map(i, be, nu):
        return (jnp.minimum(i, nu[0] - 1), 0)

    def w_map(i, be, nu):
        return (be[jnp.minimum(i, nu[0] - 1)], 0, 0)

    grid_spec = pltpu.PrefetchScalarGridSpec(
        num_scalar_prefetch=2,
        grid=(n_blocks,),
        in_specs=[pl.BlockSpec((blk * ROW_SLABS, LANES), slot_map),
                  pl.BlockSpec((1, d, ff), w_map), pl.BlockSpec((1, d, ff), w_map),
                  pl.BlockSpec((1, ff, d), w_map)],
        out_specs=pl.BlockSpec((blk * ROW_SLABS, LANES), lambda i, be, nu: (i, 0)),
        scratch_shapes=[pltpu.VMEM((d, ff), BF16), pltpu.VMEM((d, ff), BF16), pltpu.VMEM((ff, d), BF16)],
    )
    return pl.pallas_call(
        _experts_body,
        grid_spec=grid_spec,
        out_shape=jax.ShapeDtypeStruct(xs.shape, F32),
        compiler_params=_cparams(("arbitrary",), VMEM_LIMIT),
        name="experts",
    )(block_expert, n_used, xs, w1, w3, w2)


def _combine_body(dest_ref, dest_next_ref, gate_ref, x_ref, ys_ref, out_ref, buf, sems):
    i = pl.program_id(0)
    n_steps = pl.num_programs(0)
    tt = x_ref.shape[0]

    def issue(dref, slot):
        def tokens(g, carry):
            t0 = pl.multiple_of(g * ROW_DMA_UNROLL, ROW_DMA_UNROLL)
            for u in range(ROW_DMA_UNROLL):
                for k in range(2):
                    _row_copy(ys_ref, dref[2 * (t0 + u) + k], buf.at[slot, k], t0 + u,
                              sems.at[slot, k]).start(priority=k)
            return carry

        lax.fori_loop(0, tt // ROW_DMA_UNROLL, tokens, 0)

    @pl.when(i == 0)
    def _():
        issue(dest_ref, 0)

    @pl.when(i + 1 < n_steps)
    def _():
        issue(dest_next_ref, (i + 1) % 2)

    slot = i % 2
    for k in range(2):
        pltpu.make_async_copy(_tile_rows(ys_ref, 0, tt), buf.at[slot, k], sems.at[slot, k]).wait()
    g = gate_ref[...]
    out_ref[...] = (x_ref[...] + g[:, 0:1] * _load_row_tiles(buf.at[slot, 0])
                    + g[:, 1:2] * _load_row_tiles(buf.at[slot, 1]))


def _combine(dest, gates, x2d, ys):
    n, d = x2d.shape
    tt = COMBINE_TILE
    row = lambda w: pl.BlockSpec((tt, w), lambda i: (i, 0))
    last = n // tt - 1
    return pl.pallas_call(
        _combine_body,
        grid=(n // tt,),
        in_specs=[pl.BlockSpec((2 * tt,), lambda i: (i,), memory_space=pltpu.SMEM),
                  pl.BlockSpec((2 * tt,), lambda i: (jnp.minimum(i + 1, last),), memory_space=pltpu.SMEM),
                  row(LANES), row(d), pl.BlockSpec(memory_space=pl.ANY)],
        out_specs=row(d),
        out_shape=jax.ShapeDtypeStruct((n, d), F32),
        scratch_shapes=[pltpu.VMEM((2, 2, tt * ROW_SLABS, LANES), F32), pltpu.SemaphoreType.DMA((2, 2))],
        compiler_params=_cparams(("arbitrary",), VMEM_LIMIT),
        name="combine",
    )(dest, dest, gates, x2d, ys)


def _moe(x2d, norm_w, w_group, w_router, w1, w3, w2):
    n, d = x2d.shape
    blk = EXPERT_BLOCK
    h3, gates, meta, cnt = _router(x2d, norm_w, w_group, w_router)
    counts = cnt[0, :N_EXPERTS].astype(I32)
    dest = _slots(meta, cnt)[:, 0:2].reshape(-1)
    n_slots = 2 * n + N_EXPERTS * blk
    n_blocks = n_slots // blk
    pends = jnp.cumsum((counts + blk - 1) // blk * blk)
    block_start = jnp.arange(n_blocks, dtype=I32) * blk
    block_expert = jnp.minimum(jnp.sum((pends[None, :] <= block_start[:, None]).astype(I32), axis=1),
                               N_EXPERTS - 1)
    n_used = (pends[-1:] // blk).astype(I32)
    xs = _dispatch(counts, dest, h3, n_slots)
    ys = _experts(block_expert, n_used, xs, w1, w3, w2)
    return _combine(dest, gates, x2d, ys)


def _layer(x, mem, norm_mix_w, w_in, attn_q_norm_w, attn_k_norm_w, ret_decay_f, ret_decay_b, ret_gn_w, w_out,
           norm_mem_w, norm_memkv_w, w_mq, w_mkv, mem_q_norm_w, mem_k_norm_w, w_mo,
           norm_moe_w, w_group, w_router, w_exp_gate, w_exp_up, w_exp_down):
    b, s, d = x.shape
    x2d = x.reshape(b * s, d)
    q, k, v, q_pl, k_pl, v_pl, rq, rk, rv, rg = _in_proj(x2d, norm_mix_w, w_in, attn_q_norm_w, attn_k_norm_w)
    attn = _attention(q, k, v, q_pl, k_pl, v_pl, b, s)
    ret = _retention(rq, rk, rv, rg, ret_decay_f, ret_decay_b, ret_gn_w, b, s)
    mk, mv = _mem_kv(mem, norm_memkv_w, w_mkv, mem_k_norm_w)
    x2 = _mem_attn(x, attn, ret, w_out, mk, mv, norm_mem_w, w_mq, mem_q_norm_w, w_mo)
    x3 = _moe(x2.reshape(b * s, d), norm_moe_w, w_group, w_router, w_exp_gate, w_exp_up, w_exp_down)
    return x3.reshape(b, s, d)


def kernel(x, mem, norm_mix_w, w_in, attn_q_norm_w, attn_k_norm_w, ret_decay_f, ret_decay_b, ret_gn_w, w_out,
           norm_mem_w, norm_memkv_w, w_mq, w_mkv, mem_q_norm_w, mem_k_norm_w, w_mo, norm_moe_w, w_group,
           w_router, w_exp_gate, w_exp_up, w_exp_down):
    depth = norm_mix_w.shape[0]
    for l in range(depth):
        x = _layer(x, mem, norm_mix_w[l], w_in[l], attn_q_norm_w[l], attn_k_norm_w[l], ret_decay_f[l],
                   ret_decay_b[l], ret_gn_w[l], w_out[l], norm_mem_w[l], norm_memkv_w[l], w_mq[l], w_mkv[l],
                   mem_q_norm_w[l], mem_k_norm_w[l], w_mo[l], norm_moe_w[l], w_group[l], w_router[l],
                   w_exp_gate[l], w_exp_up[l], w_exp_down[l])
    return x
```

```python
import functools

import numpy as np
import jax
import jax.numpy as jnp
from jax import lax
from jax.experimental import pallas as pl
from jax.experimental.pallas import tpu as pltpu

F32 = jnp.float32
BF16 = jnp.bfloat16
I32 = jnp.int32

NORM_EPS = 1e-6
GN_EPS = 1e-5
NEG_INF = -1e30

ATTN_HEADS = 8
ATTN_HEAD_DIM = 64
ATTN_WIDTH = ATTN_HEADS * ATTN_HEAD_DIM
ATTN_HALF = 64
PLANES = 16
ATTN_TILE = PLANES * ATTN_HALF
RET_HEADS = 4
RET_QK_DIM = 64
RET_V_DIM = 128
RET_QK_WIDTH = RET_HEADS * RET_QK_DIM
RET_V_WIDTH = RET_HEADS * RET_V_DIM
MEM_HEADS = 4
N_GROUPS = 4
EXPERTS_PER_GROUP = 8
N_EXPERTS = N_GROUPS * EXPERTS_PER_GROUP

LANES = 128
ROUTER_SUBTILE = 512
MATMUL_TILE = 1024
ATTN_BLOCKS_PER_STEP = 4
LOG2_E = 1.4426950408889634
RET_CHUNK = 256
EXPERT_BLOCK = 512
EXPERT_FF_CHUNK = 256
DISPATCH_TILE = 1024
COMBINE_TILE = 256
ROW_DMA_UNROLL = 16
SLOTS_TILE = 4096
VMEM_LIMIT = 56 * 1024 * 1024


def _cparams(sem, vmem=None):
    return pltpu.CompilerParams(dimension_semantics=sem, vmem_limit_bytes=vmem)


def _split_hi_lo(x):
    hi = x.astype(BF16)
    lo = (x - hi.astype(F32)).astype(BF16)
    return hi, lo


def _rms(x, w):
    ms = jnp.mean(x * x, axis=-1, keepdims=True)
    return x * lax.rsqrt(ms + NORM_EPS) * w


ROW_SLABS = 8


def _store_row_tiles(ref, val):
    rows = val.shape[0]
    for j in range(ROW_SLABS):
        ref[pl.ds(j, rows, stride=ROW_SLABS), :] = val[:, j * LANES:(j + 1) * LANES]


def _load_row_tiles(ref):
    rows = ref.shape[0] // ROW_SLABS
    return jnp.concatenate([ref[pl.ds(j, rows, stride=ROW_SLABS), :] for j in range(ROW_SLABS)], axis=1)


def _tile_rows(ref, row, count=1):
    return ref.at[pl.ds(pl.multiple_of(row * ROW_SLABS, ROW_SLABS), count * ROW_SLABS)]


def _in_proj_body(x_ref, nw_ref, w_ref, qw_ref, kw_ref, g_ref,
                  q_out, k_out, v_out, qp_out, kp_out, vp_out, rq_out, rk_out, rv_out, rg_out, slabs, quarters):
    tm = x_ref.shape[0]
    h = _rms(x_ref[...], nw_ref[...]).astype(BF16)

    def emit(val, nat_out, plane_out):
        nat_out[...] = val.astype(BF16)
        for j in range(ATTN_WIDTH // LANES):
            slabs[j] = val[:, j * LANES:(j + 1) * LANES]
        for j in range(ATTN_WIDTH // LANES):
            for r in range(4):
                quarters[j, r] = slabs[j, pl.ds(r, tm // 4, stride=4), :]
        for r in range(4):
            for q in range(4):
                for j in range(ATTN_WIDTH // LANES):
                    rows = quarters[j, r, pl.ds(q, tm // PLANES, stride=4), :]
                    plane_out[0, r + 4 * q, :, j * LANES:(j + 1) * LANES] = rows.astype(BF16)

    def proj(a, b):
        return jnp.dot(h, w_ref[:, a:b], preferred_element_type=F32)

    def head_norm(p, w):
        s = jnp.dot((p * p).astype(BF16), g_ref[...], preferred_element_type=F32)
        return p * lax.rsqrt(s * (1.0 / ATTN_HEAD_DIM) + NORM_EPS) * w

    a = ATTN_WIDTH
    emit(head_norm(proj(0, a), qw_ref[...]), q_out, qp_out)
    emit(head_norm(proj(a, 2 * a), kw_ref[...]), k_out, kp_out)
    emit(proj(2 * a, 3 * a), v_out, vp_out)
    c = 3 * a
    rq_out[...] = proj(c, c + RET_QK_WIDTH).astype(BF16)
    rk_out[...] = proj(c + RET_QK_WIDTH, c + 2 * RET_QK_WIDTH).astype(BF16)
    c += 2 * RET_QK_WIDTH
    rv_out[...] = proj(c, c + RET_V_WIDTH).astype(BF16)
    rg_out[...] = proj(c + RET_V_WIDTH, c + 2 * RET_V_WIDTH).astype(BF16)


def _in_proj(x2d, norm_w, w_in, q_norm_w, k_norm_w):
    n, d = x2d.shape
    cols = w_in.shape[1]
    tm = MATMUL_TILE
    qw = (jnp.tile(q_norm_w, ATTN_HEADS) * (ATTN_HEAD_DIM ** -0.5 * LOG2_E)).reshape(1, ATTN_WIDTH)
    kw = jnp.tile(k_norm_w, ATTN_HEADS).reshape(1, ATTN_WIDTH)
    head_of = np.arange(ATTN_WIDTH) // ATTN_HEAD_DIM
    gmat = jnp.asarray(head_of[:, None] == head_of[None, :], dtype=BF16)
    row = lambda w: pl.BlockSpec((tm, w), lambda i: (i, 0))
    full = lambda r, c: pl.BlockSpec((r, c), lambda i: (0, 0))
    per_tile = ATTN_TILE // tm
    plane = pl.BlockSpec((1, PLANES, tm // PLANES, ATTN_WIDTH), lambda i: (i // per_tile, 0, i % per_tile, 0))
    plane_shape = jax.ShapeDtypeStruct((n // ATTN_TILE, PLANES, ATTN_TILE // PLANES, ATTN_WIDTH), BF16)
    ret_widths = (RET_QK_WIDTH, RET_QK_WIDTH, RET_V_WIDTH, RET_V_WIDTH)
    return pl.pallas_call(
        _in_proj_body,
        grid=(n // tm,),
        in_specs=[row(d), full(1, d), full(d, cols), full(1, ATTN_WIDTH), full(1, ATTN_WIDTH),
                  full(ATTN_WIDTH, ATTN_WIDTH)],
        out_specs=[row(ATTN_WIDTH)] * 3 + [plane] * 3 + [row(w) for w in ret_widths],
        out_shape=([jax.ShapeDtypeStruct((n, ATTN_WIDTH), BF16)] * 3 + [plane_shape] * 3
                   + [jax.ShapeDtypeStruct((n, w), BF16) for w in ret_widths]),
        scratch_shapes=[pltpu.VMEM((ATTN_WIDTH // LANES, tm, LANES), F32),
                        pltpu.VMEM((ATTN_WIDTH // LANES, 4, tm // 4, LANES), F32)],
        compiler_params=_cparams(("parallel",), VMEM_LIMIT),
        name="in_proj",
    )(x2d, norm_w.reshape(1, d), w_in.astype(BF16), qw, kw, gmat)


def _attn_bias(dilation, interleave=1):
    half = ATTN_HALF
    idx = np.arange(half)
    sub = (idx % (half // interleave)) * interleave + idx // (half // interleave)
    qi = sub[:, None]
    kc = (np.arange(3)[:, None] * half + sub[None, :]).reshape(1, -1)
    seg = np.repeat(np.arange(3), half)[None, :]
    delta = kc - half - qi
    band = np.abs(delta) <= half
    slopes = np.exp2(-8.0 * np.arange(1, ATTN_HEADS + 1) / ATTN_HEADS)
    valid = [band & (seg >= 1), band, band & (seg <= 1)]
    out = np.empty((ATTN_HEADS // 2, 3, 2 * half, 3 * half), np.float32)
    for p in range(ATTN_HEADS // 2):
        for v in range(3):
            for s in range(2):
                b = -slopes[2 * p + s] * LOG2_E * (dilation * np.abs(delta)).astype(np.float32)
                out[p, v, s * half:(s + 1) * half] = np.where(valid[v], b, NEG_INF)
    return jnp.asarray(out.reshape(-1, 2 * half, 3 * half))


def _attn_scores(chains):
    lo_lanes = lax.broadcasted_iota(I32, (ATTN_HALF, LANES), 1) < ATTN_HEAD_DIM
    scores = []
    for q, k, v, bias in chains:
        zero = jnp.zeros_like(q)
        qs = jnp.concatenate([jnp.where(lo_lanes, q, zero), jnp.where(lo_lanes, zero, q)], axis=0)
        scores.append(lax.dot_general(qs, k, (((1,), (1,)), ((), ())), preferred_element_type=F32) + bias)
    return scores


def _attn_finish(chains, scores):
    half = ATTN_HALF
    lo_lanes = lax.broadcasted_iota(I32, (half, LANES), 1) < ATTN_HEAD_DIM
    ones = jnp.ones((3 * half, LANES), BF16)
    probs, maxes = [], []
    for s in scores:
        m = jnp.max(s, axis=-1, keepdims=True)
        probs.append(jnp.exp2(s - m).astype(BF16))
        maxes.append(m)
    results = []
    for (q, k, v, bias), e in zip(chains, probs):
        vext = jnp.concatenate([v, ones], axis=1)
        results.append(jnp.dot(e, vext, preferred_element_type=F32))
    outs = []
    for r, m in zip(results, maxes):
        acc = jnp.where(lo_lanes, r[:half, :LANES], r[half:, :LANES])
        denom = jnp.where(lo_lanes, r[:half, LANES:], r[half:, LANES:])
        mm = jnp.where(lo_lanes, m[:half], m[half:])
        outs.append((acc / denom, mm + jnp.log2(denom)))
    return outs


def _attend_groups(groups):
    pending = None
    for chains, consume in groups:
        scores = _attn_scores(chains)
        if pending is not None:
            p_chains, p_scores, p_consume = pending
            p_consume(_attn_finish(p_chains, p_scores))
        pending = (chains, scores, consume)
    p_chains, p_scores, p_consume = pending
    p_consume(_attn_finish(p_chains, p_scores))


def _merge(oa, la, ob, lb):
    m = jnp.maximum(la, lb)
    ea = jnp.exp2(la - m)
    eb = jnp.exp2(lb - m)
    den = ea + eb
    return (ea * oa + eb * ob) / den, m + jnp.log2(den)


def _attn_body(q_ref, kp_ref, km_ref, kn_ref, vp_ref, vm_ref, vn_ref,
               qpl_ref, kplp_ref, kplm_ref, kpln_ref, vplp_ref, vplm_ref, vpln_ref,
               b1_ref, b4_ref, b16_ref, o_ref, kcat, vcat, o_far, l_far):
    half = ATTN_HALF
    t = pl.program_id(1)
    n_tiles = pl.num_programs(1)
    tile = q_ref.shape[1]
    pairs = ATTN_HEADS // 2
    piece = half // 4

    kcat[0:half] = kp_ref[0]
    kcat[half:half + tile] = km_ref[0]
    kcat[half + tile:] = kn_ref[0]
    vcat[0:half] = vp_ref[0]
    vcat[half:half + tile] = vm_ref[0]
    vcat[half + tile:] = vn_ref[0]

    def edge(first, last):
        return jnp.where(first, 0, jnp.where(last, 2, 1))

    groups = []

    def far_groups(r):
        v16 = edge(t == 0, t == n_tiles - 1)
        for pr in range(pairs):
            cs = slice(pr * LANES, (pr + 1) * LANES)
            chains = []
            for m in range(4):
                c = r + 4 * m
                k3 = jnp.concatenate([kplp_ref[0, c, :, cs], kplm_ref[0, c, :, cs], kpln_ref[0, c, :, cs]], axis=0)
                v3 = jnp.concatenate([vplp_ref[0, c, :, cs], vplm_ref[0, c, :, cs], vpln_ref[0, c, :, cs]], axis=0)
                chains.append((qpl_ref[0, c, :, cs], k3, v3, b16_ref[pr * 3 + v16]))
            for nb in range(4):
                def rows(main, before, after, a0):
                    ref, lo = (before, a0 + half) if a0 < 0 else (after, a0 - half) if a0 >= half else (main, a0)
                    return [ref[0, r + 4 * m, lo:lo + piece, cs] for m in range(4)]
                a0 = nb * piece
                q4 = jnp.concatenate(rows(qpl_ref, None, None, a0), axis=0)
                k4 = jnp.concatenate(sum((rows(kplm_ref, kplp_ref, kpln_ref, a0 + d) for d in (-piece, 0, piece)), []),
                                     axis=0)
                v4 = jnp.concatenate(sum((rows(vplm_ref, vplp_ref, vpln_ref, a0 + d) for d in (-piece, 0, piece)), []),
                                     axis=0)
                v4e = edge((t == 0) & (nb == 0), (t == n_tiles - 1) & (nb == 3))
                chains.append((q4, k4, v4, b4_ref[pr * 3 + v4e]))

            def consume(res, r=r, pr=pr):
                for m in range(4):
                    o16, l16 = res[m]
                    o_rows, l_rows = [], []
                    for nb in range(4):
                        o4, l4 = res[4 + nb]
                        sl = slice(nb * piece, (nb + 1) * piece)
                        s4 = slice(m * piece, (m + 1) * piece)
                        om, lm = _merge(o16[sl], l16[sl], o4[s4], l4[s4])
                        o_rows.append(om)
                        l_rows.append(lm)
                    dst = pl.ds(r + 4 * m, half, stride=PLANES)
                    o_far[pr, dst, :] = jnp.concatenate(o_rows, axis=0)
                    l_far[pr, dst, :] = jnp.concatenate(l_rows, axis=0)

            groups.append((chains, consume))

    for r in range(4):
        far_groups(r)

    n_blocks = n_tiles * (tile // half)
    for it in range(tile // half // ATTN_BLOCKS_PER_STEP):
        chains, where = [], []
        for u in range(ATTN_BLOCKS_PER_STEP):
            jb = it * ATTN_BLOCKS_PER_STEP + u
            r0 = jb * half
            gb = t * (tile // half) + jb
            variant = edge(gb == 0, gb == n_blocks - 1)
            for pr in range(pairs):
                cs = slice(pr * LANES, (pr + 1) * LANES)
                chains.append((q_ref[0, pl.ds(r0, half), cs], kcat[pl.ds(r0, 3 * half), cs],
                               vcat[pl.ds(r0, 3 * half), cs], b1_ref[pr * 3 + variant]))
                where.append((r0, pr, cs))

        def consume(res, where=where):
            for (r0, pr, cs), (o1, l1) in zip(where, res):
                o, _ = _merge(o1, l1, o_far[pr, pl.ds(r0, half), :], l_far[pr, pl.ds(r0, half), :])
                o_ref[0, pl.ds(r0, half), cs] = o.astype(o_ref.dtype)

        groups.append((chains, consume))

    _attend_groups(groups)


def _attention(q, k, v, q_pl, k_pl, v_pl, b, s):
    w = ATTN_WIDTH
    half = ATTN_HALF
    tile = ATTN_TILE
    n_tiles = s // tile
    hb = tile // half
    last = s // half - 1
    main = pl.BlockSpec((1, tile, w), lambda bb, t: (bb, t, 0))
    prev = pl.BlockSpec((1, half, w), lambda bb, t: (bb, jnp.maximum(t * hb - 1, 0), 0))
    nxt = pl.BlockSpec((1, half, w), lambda bb, t: (bb, jnp.minimum((t + 1) * hb, last), 0))
    pshape = (1, PLANES, tile // PLANES, w)
    pl_main = pl.BlockSpec(pshape, lambda bb, t: (bb * n_tiles + t, 0, 0, 0))
    pl_prev = pl.BlockSpec(pshape, lambda bb, t: (bb * n_tiles + jnp.maximum(t - 1, 0), 0, 0, 0))
    pl_next = pl.BlockSpec(pshape, lambda bb, t: (bb * n_tiles + jnp.minimum(t + 1, n_tiles - 1), 0, 0, 0))
    biases = [_attn_bias(1), _attn_bias(4, interleave=4), _attn_bias(16)]
    bias_spec = pl.BlockSpec(biases[0].shape, lambda bb, t: (0, 0, 0))
    nat = lambda a: a.reshape(b, s, w)
    out = pl.pallas_call(
        _attn_body,
        grid=(b, n_tiles),
        in_specs=[main, prev, main, nxt, prev, main, nxt,
                  pl_main, pl_prev, pl_main, pl_next, pl_prev, pl_main, pl_next,
                  bias_spec, bias_spec, bias_spec],
        out_specs=main,
        out_shape=jax.ShapeDtypeStruct((b, s, w), BF16),
        scratch_shapes=[pltpu.VMEM((tile + 2 * half, w), BF16), pltpu.VMEM((tile + 2 * half, w), BF16),
                        pltpu.VMEM((ATTN_HEADS // 2, tile, LANES), F32),
                        pltpu.VMEM((ATTN_HEADS // 2, tile, LANES), F32)],
        compiler_params=_cparams(("parallel", "parallel"), VMEM_LIMIT),
        name="attention",
    )(nat(q), nat(k), nat(k), nat(k), nat(v), nat(v), nat(v),
      q_pl, k_pl, k_pl, k_pl, v_pl, v_pl, v_pl, *biases)
    return out.reshape(b * s, w)


def _ret_body(lg_ref, q_ref, k_ref, v_ref, g_ref, gnw_ref, out_ref,
              dmat, qdec, kdec, cdec, fstate, rstate, rall, *, chunk):
    c = chunk
    b = pl.program_id(0)
    ph = pl.program_id(1)
    n = pl.program_id(2)
    n_chunks = pl.num_programs(2)
    k_scale = RET_QK_DIM ** -0.5
    pairs = RET_HEADS // 2
    pw = 2 * RET_QK_DIM
    vw = 2 * RET_V_DIM
    first = lax.broadcasted_iota(I32, (c, pw), 1) < RET_QK_DIM
    diag = ((lax.broadcasted_iota(I32, (pw, vw), 0) < RET_QK_DIM)
            == (lax.broadcasted_iota(I32, (pw, vw), 1) < RET_V_DIM))

    @pl.when((b == 0) & (ph == 0) & (n == 0))
    def _init_tables():
        ii = lax.broadcasted_iota(I32, (c, c), 0)
        jj = lax.broadcasted_iota(I32, (c, c), 1)
        fwd = (ii - jj).astype(F32)
        ri = lax.broadcasted_iota(I32, (c, pw), 0).astype(F32)
        top = lax.broadcasted_iota(I32, (pw, vw), 0) < RET_QK_DIM
        for h in range(RET_HEADS):
            dmat[h] = jnp.where(ii >= jj, jnp.exp(lg_ref[0, h] * fwd), jnp.exp(-lg_ref[1, h] * fwd)) * k_scale
        for pr in range(pairs):
            lf = jnp.where(first, lg_ref[0, 2 * pr], lg_ref[0, 2 * pr + 1])
            lb = jnp.where(first, lg_ref[1, 2 * pr], lg_ref[1, 2 * pr + 1])
            qdec[0, pr] = jnp.exp(lf * (ri + 1.0))
            qdec[1, pr] = jnp.exp(lb * (c - ri))
            kdec[0, pr] = jnp.exp(lf * (c - 1.0 - ri)) * k_scale
            kdec[1, pr] = jnp.exp(lb * ri) * k_scale
            for direction in range(2):
                cdec[direction, pr] = jnp.exp(jnp.where(top, lg_ref[direction, 2 * pr],
                                                        lg_ref[direction, 2 * pr + 1]) * c)

    def kv_update(direction, pr):
        kp = k_ref[0, :, pr * pw:(pr + 1) * pw].astype(F32)
        ks = (kp * kdec[direction, pr]).astype(BF16)
        new = lax.dot_general(ks, v_ref[0, :, pr * vw:(pr + 1) * vw], (((0,), (0,)), ((), ())),
                              preferred_element_type=F32)
        return jnp.where(diag, new, 0.0)

    @pl.when(ph == 0)
    def _right_to_left():
        @pl.when(n == 0)
        def _():
            rstate[...] = jnp.zeros_like(rstate)

        ci = n_chunks - 1 - n
        for pr in range(pairs):
            st = rstate[pr]
            rall[ci, pr] = st
            rstate[pr] = st * cdec[1, pr] + kv_update(1, pr)

    @pl.when(ph == 1)
    def _left_to_right():
        @pl.when(n == 0)
        def _():
            fstate[...] = jnp.zeros_like(fstate)

        for pr in range(pairs):
            qp = q_ref[0, :, pr * pw:(pr + 1) * pw]
            kp = k_ref[0, :, pr * pw:(pr + 1) * pw]
            qf = qp.astype(F32)
            qcat = jnp.concatenate([(qf * qdec[0, pr]).astype(BF16), (qf * qdec[1, pr]).astype(BF16)], axis=1)
            st = fstate[pr]
            states = jnp.concatenate([st, rall[n, pr]], axis=0).astype(BF16)
            cross = jnp.dot(qcat, states, preferred_element_type=F32)
            fstate[pr] = st * cdec[0, pr] + kv_update(0, pr)
            zero = jnp.zeros_like(qp)
            for a in range(2):
                h = 2 * pr + a
                vs = slice(h * RET_V_DIM, (h + 1) * RET_V_DIM)
                qm = jnp.where(first, qp, zero) if a == 0 else jnp.where(first, zero, qp)
                s = lax.dot_general(qm, kp, (((1,), (1,)), ((), ())), preferred_element_type=F32) * dmat[h]
                y = jnp.dot(s.astype(BF16), v_ref[0, :, vs], preferred_element_type=F32)
                y = y + cross[:, a * RET_V_DIM:(a + 1) * RET_V_DIM]
                mu = jnp.mean(y, axis=-1, keepdims=True)
                yc = y - mu
                var = jnp.mean(yc * yc, axis=-1, keepdims=True)
                yn = yc * lax.rsqrt(var + GN_EPS) * gnw_ref[:, vs]
                gate = g_ref[0, :, vs].astype(F32)
                out_ref[0, :, vs] = (gate * jax.nn.sigmoid(gate) * yn).astype(out_ref.dtype)


def _retention(rq, rk, rv, rg, decay_f, decay_b, gn_w, b, s):
    c = RET_CHUNK
    nc = s // c
    lg = jnp.stack([jax.nn.log_sigmoid(decay_f.astype(F32)), jax.nn.log_sigmoid(decay_b.astype(F32))])
    qk_w, v_w = RET_QK_WIDTH, RET_V_WIDTH
    pairs, pair_qk, pair_v = RET_HEADS // 2, 2 * RET_QK_DIM, 2 * RET_V_DIM

    def both(bb, ph, n):
        return (bb, jnp.where(ph == 0, nc - 1 - n, n), 0)

    def fwd_only(bb, ph, n):
        return (bb, jnp.where(ph == 0, 0, n), 0)

    out = pl.pallas_call(
        functools.partial(_ret_body, chunk=c),
        grid=(b, 2, nc),
        in_specs=[pl.BlockSpec(memory_space=pltpu.SMEM),
                  pl.BlockSpec((1, c, qk_w), fwd_only),
                  pl.BlockSpec((1, c, qk_w), both),
                  pl.BlockSpec((1, c, v_w), both),
                  pl.BlockSpec((1, c, v_w), fwd_only),
                  pl.BlockSpec((1, v_w), lambda bb, ph, n: (0, 0))],
        out_specs=pl.BlockSpec((1, c, v_w), fwd_only),
        out_shape=jax.ShapeDtypeStruct((b, s, v_w), BF16),
        scratch_shapes=[pltpu.VMEM((RET_HEADS, c, c), F32),
                        pltpu.VMEM((2, pairs, c, pair_qk), F32),
                        pltpu.VMEM((2, pairs, c, pair_qk), F32),
                        pltpu.VMEM((2, pairs, pair_qk, pair_v), F32),
                        pltpu.VMEM((pairs, pair_qk, pair_v), F32),
                        pltpu.VMEM((pairs, pair_qk, pair_v), F32),
                        pltpu.VMEM((nc, pairs, pair_qk, pair_v), F32)],
        compiler_params=_cparams(("arbitrary", "arbitrary", "arbitrary"), VMEM_LIMIT),
        name="retention",
    )(lg, rq.reshape(b, s, qk_w), rk.reshape(b, s, qk_w), rv.reshape(b, s, v_w), rg.reshape(b, s, v_w),
      gn_w.reshape(1, v_w))
    return out.reshape(b * s, v_w)


def _mem_kv_body(mem_ref, nw_ref, w_ref, kw_ref, k_out, v_out):
    d = mem_ref.shape[-1]
    hd = d // MEM_HEADS
    h = _rms(mem_ref[0], nw_ref[...]).astype(BF16)
    kv = jnp.dot(h, w_ref[...], preferred_element_type=F32)
    for i in range(MEM_HEADS):
        k_out[0, :, i * hd:(i + 1) * hd] = _rms(kv[:, i * hd:(i + 1) * hd], kw_ref[...]).astype(BF16)
    v_out[0] = kv[:, d:].astype(BF16)


def _mem_kv(mem, norm_w, w_mkv, k_norm_w):
    b, m, d = mem.shape
    return pl.pallas_call(
        _mem_kv_body,
        grid=(b,),
        in_specs=[pl.BlockSpec((1, m, d), lambda i: (i, 0, 0)),
                  pl.BlockSpec((1, d), lambda i: (0, 0)),
                  pl.BlockSpec((d, 2 * d), lambda i: (0, 0)),
                  pl.BlockSpec((1, d // MEM_HEADS), lambda i: (0, 0))],
        out_specs=[pl.BlockSpec((1, m, d), lambda i: (i, 0, 0))] * 2,
        out_shape=[jax.ShapeDtypeStruct((b, m, d), BF16)] * 2,
        compiler_params=_cparams(("parallel",), VMEM_LIMIT),
        name="mem_kv",
    )(mem, norm_w.reshape(1, d), w_mkv.astype(BF16), k_norm_w.reshape(1, -1))


def _mem_attn_body(x_ref, attn_ref, ret_ref, wout_ref, nw_ref, wq_ref, qw_ref, k_ref, v_ref, wo_ref,
                   rnw_ref, rw_ref, out_ref, h_out, gate_out, meta_out, cnt_out, carry, below):
    d = x_ref.shape[-1]
    hd = d // MEM_HEADS
    tm = x_ref.shape[1]
    sub = below.shape[0]

    @pl.when((pl.program_id(0) == 0) & (pl.program_id(1) == 0))
    def _():
        carry[...] = jnp.zeros_like(carry)
        rr = lax.broadcasted_iota(I32, (sub, sub), 0)
        cc = lax.broadcasted_iota(I32, (sub, sub), 1)
        below[...] = (cc < rr).astype(F32).astype(BF16)

    x = (x_ref[0] + jnp.dot(attn_ref[0], wout_ref[:ATTN_WIDTH], preferred_element_type=F32)
         + jnp.dot(ret_ref[0], wout_ref[ATTN_WIDTH:], preferred_element_type=F32))
    h = _rms(x, nw_ref[...]).astype(BF16)
    q = jnp.dot(h, wq_ref[...], preferred_element_type=F32)
    heads = []
    for i in range(MEM_HEADS):
        cs = slice(i * hd, (i + 1) * hd)
        qn = _rms(q[:, cs], qw_ref[...]).astype(BF16)
        s = lax.dot_general(qn, k_ref[0, :, cs], (((1,), (1,)), ((), ())), preferred_element_type=F32)
        e = jnp.exp(s - jnp.max(s, axis=-1, keepdims=True))
        o = jnp.dot(e.astype(BF16), v_ref[0, :, cs], preferred_element_type=F32)
        heads.append((o / jnp.sum(e, axis=-1, keepdims=True)).astype(BF16))
    o = jnp.concatenate(heads, axis=1)
    x2 = x + jnp.dot(o, wo_ref[...], preferred_element_type=F32)
    out_ref[0] = x2

    base = carry[...]
    for part in range(tm // sub):
        rows = slice(part * sub, (part + 1) * sub)
        base = _route_rows(x2[rows], rnw_ref, rw_ref, below, base,
                           h_out.at[pl.ds(part * sub * ROW_SLABS, sub * ROW_SLABS)],
                           gate_out.at[rows], meta_out.at[rows])
    carry[...] = base
    cnt_out[...] = base


def _mem_attn_route(x, attn, ret, w_out, mk, mv, norm_w, w_mq, q_norm_w, w_mo, moe_norm_w, w_group, w_router):
    b, s, d = x.shape
    m = mk.shape[1]
    tm = MATMUL_TILE
    steps = s // tm
    hd = d // MEM_HEADS
    qw = (q_norm_w * (hd ** -0.5)).reshape(1, hd)
    w_all = jnp.concatenate([w_group, w_router.transpose(1, 0, 2).reshape(d, N_EXPERTS)], axis=1)
    w_all = jnp.pad(w_all, ((0, 0), (0, LANES - w_all.shape[1])))
    whi = w_all.astype(BF16)
    w_split = jnp.concatenate([whi, (w_all - whi.astype(F32)).astype(BF16)], axis=1)
    tok = lambda w: pl.BlockSpec((1, tm, w), lambda bb, i: (bb, i, 0))
    flat = lambda r, w: pl.BlockSpec((r, w), lambda bb, i: (bb * steps + i, 0))
    const = lambda r, c: pl.BlockSpec((r, c), lambda bb, i: (0, 0))
    mem = pl.BlockSpec((1, m, d), lambda bb, i: (bb, 0, 0))
    n = b * s
    return pl.pallas_call(
        _mem_attn_body,
        grid=(b, steps),
        in_specs=[tok(d), tok(ATTN_WIDTH), tok(RET_V_WIDTH), const(ATTN_WIDTH + RET_V_WIDTH, d),
                  const(1, d), const(d, d), const(1, hd), mem, mem, const(d, d),
                  const(1, d), const(d, 2 * LANES)],
        out_specs=[tok(d), flat(tm * ROW_SLABS, LANES), flat(tm, LANES), flat(tm, LANES), const(1, LANES)],
        out_shape=[jax.ShapeDtypeStruct((b, s, d), F32), jax.ShapeDtypeStruct((n * ROW_SLABS, LANES), F32),
                   jax.ShapeDtypeStruct((n, LANES), F32), jax.ShapeDtypeStruct((n, LANES), I32),
                   jax.ShapeDtypeStruct((1, LANES), F32)],
        scratch_shapes=[pltpu.VMEM((1, LANES), F32), pltpu.VMEM((ROUTER_SUBTILE, ROUTER_SUBTILE), BF16)],
        compiler_params=_cparams(("arbitrary", "arbitrary"), VMEM_LIMIT),
        name="mem_attn_route",
    )(x, attn.reshape(b, s, ATTN_WIDTH), ret.reshape(b, s, RET_V_WIDTH), w_out.astype(BF16),
      norm_w.reshape(1, d), w_mq.astype(BF16), qw, mk, mv, w_mo.astype(BF16),
      moe_norm_w.reshape(1, d), w_split)


def _route_rows(x, nw_ref, w_ref, below, base, h_out, gate_out, meta_out):
    tm = x.shape[0]
    h = _rms(x, nw_ref[...])
    _store_row_tiles(h_out, h)
    hi, lo = _split_hi_lo(h)
    both = jnp.dot(hi, w_ref[...], preferred_element_type=F32)
    logits = (both[:, :LANES] + both[:, LANES:]
              + jnp.dot(lo, w_ref[:, :LANES], preferred_element_type=F32))
    lane = lax.broadcasted_iota(I32, (tm, LANES), 1)
    lane_f = lane.astype(F32)

    def argmax_lanes(vals):
        top = jnp.max(vals, axis=-1, keepdims=True)
        idx = jnp.min(jnp.where(vals == top, lane_f, float(LANES)), axis=-1, keepdims=True)
        return top, idx

    is_group = lane < N_GROUPS
    g_top, grp = argmax_lanes(jnp.where(is_group, logits, -jnp.inf))
    g_gate = 1.0 / jnp.sum(jnp.where(is_group, jnp.exp(logits - g_top), 0.0), axis=-1, keepdims=True)
    first = N_GROUPS + EXPERTS_PER_GROUP * grp
    in_group = (lane_f >= first) & (lane_f < first + EXPERTS_PER_GROUP)
    el = jnp.where(in_group, logits, -jnp.inf)
    t1, i1 = argmax_lanes(el)
    t2, i2 = argmax_lanes(jnp.where(lane_f == i1, -jnp.inf, el))
    z = jnp.exp(t2 - t1)
    g1 = g_gate / (1.0 + z)
    g2 = g_gate * z / (1.0 + z)
    e1 = i1 - N_GROUPS
    e2 = i2 - N_GROUPS

    oh1 = (lane_f == e1)
    oh2 = (lane_f == e2)
    oh1b = oh1.astype(F32).astype(BF16)
    oh2b = oh2.astype(F32).astype(BF16)
    pre = jnp.dot(below[...], jnp.concatenate([oh1b, oh2b], axis=1), preferred_element_type=F32)
    pre1, pre2 = pre[:, :LANES], pre[:, LANES:]
    cnt1 = jnp.sum(oh1.astype(F32), axis=0, keepdims=True)
    cnt2 = jnp.sum(oh2.astype(F32), axis=0, keepdims=True)
    r1 = jnp.sum(jnp.where(oh1, pre1 + base, 0.0), axis=-1, keepdims=True)
    r2 = jnp.sum(jnp.where(oh2, pre2 + base + cnt1, 0.0), axis=-1, keepdims=True)

    gate_out[...] = jnp.where(lane == 0, g1, jnp.where(lane == 1, g2, 0.0))
    meta = jnp.where(lane == 0, e1, jnp.where(lane == 1, e2, jnp.where(lane == 2, r1, jnp.where(lane == 3, r2, 0.0))))
    meta_out[...] = meta.astype(I32)
    return base + cnt1 + cnt2


def _row_copy(src_ref, src_row, dst_ref, dst_row, sem):
    return pltpu.make_async_copy(_tile_rows(src_ref, src_row), _tile_rows(dst_ref, dst_row), sem)


def _slots_body(meta_ref, cnt_ref, dest_ref):
    tm = meta_ref.shape[0]
    blk = float(EXPERT_BLOCK)
    lane = lax.broadcasted_iota(I32, (tm, LANES), 1)
    lane_f = lane.astype(F32)
    blocks = jnp.floor((cnt_ref[...] + (blk - 1.0)) * (1.0 / blk))
    rr = lax.broadcasted_iota(I32, (LANES, LANES), 0)
    cc = lax.broadcasted_iota(I32, (LANES, LANES), 1)
    before = (rr < cc).astype(F32).astype(BF16)
    first_block = jnp.dot(jnp.broadcast_to(blocks, (8, LANES)).astype(BF16), before,
                          preferred_element_type=F32)[0:1]
    start = first_block * blk
    meta = meta_ref[...].astype(F32)

    def slot_of(k):
        e = jnp.sum(jnp.where(lane == k, meta, 0.0), axis=-1, keepdims=True)
        r = jnp.sum(jnp.where(lane == 2 + k, meta, 0.0), axis=-1, keepdims=True)
        return jnp.sum(jnp.where(lane_f == e, start, 0.0), axis=-1, keepdims=True) + r

    dest_ref[...] = jnp.where(lane == 0, slot_of(0), jnp.where(lane == 1, slot_of(1), 0.0)).astype(I32)


def _slots(meta, cnt):
    n = meta.shape[0]
    tm = min(SLOTS_TILE, n)
    row = pl.BlockSpec((tm, LANES), lambda i: (i, 0))
    return pl.pallas_call(
        _slots_body,
        grid=(n // tm,),
        in_specs=[row, pl.BlockSpec((1, LANES), lambda i: (0, 0))],
        out_specs=row,
        out_shape=jax.ShapeDtypeStruct((n, LANES), I32),
        compiler_params=_cparams(("parallel",), VMEM_LIMIT),
        name="slots",
    )(meta, cnt)


def _dispatch_body(cnt_ref, dest_ref, h_ref, xs_ref, zbuf, sems):
    i = pl.program_id(0)
    tt = h_ref.shape[0] // ROW_SLABS
    blk = EXPERT_BLOCK
    n_slots = xs_ref.shape[0] // ROW_SLABS

    def pad_copy(pend):
        return pltpu.make_async_copy(zbuf, _tile_rows(xs_ref, pend - blk, blk), sems.at[2])

    @pl.when(i == 0)
    def _():
        zbuf[...] = jnp.zeros_like(zbuf)

        def start(e, acc):
            pend = acc + ((cnt_ref[e] + (blk - 1)) // blk) * blk

            @pl.when(pend > acc)
            def _():
                pad_copy(pend).start()
            return pend

        used = lax.fori_loop(0, N_EXPERTS, start, jnp.int32(0))

        def tail_start(j, carry):
            pad_copy((j + 1) * blk).start()
            return carry

        lax.fori_loop(used // blk, n_slots // blk, tail_start, 0)

        def finish(e, acc):
            pend = acc + ((cnt_ref[e] + (blk - 1)) // blk) * blk

            @pl.when(pend > acc)
            def _():
                pad_copy(pend).wait()
            return pend

        lax.fori_loop(0, N_EXPERTS, finish, jnp.int32(0))

        def tail_wait(j, carry):
            pad_copy((j + 1) * blk).wait()
            return carry

        lax.fori_loop(used // blk, n_slots // blk, tail_wait, 0)

    def tokens(g, carry):
        t0 = pl.multiple_of(g * ROW_DMA_UNROLL, ROW_DMA_UNROLL)
        for u in range(ROW_DMA_UNROLL):
            for k in range(2):
                _row_copy(h_ref, t0 + u, xs_ref, dest_ref[2 * (t0 + u) + k], sems.at[k]).start(priority=k)
        return carry

    lax.fori_loop(0, tt // ROW_DMA_UNROLL, tokens, 0)
    for k in range(2):
        pltpu.make_async_copy(h_ref, _tile_rows(xs_ref, 0, tt), sems.at[k]).wait()


def _dispatch(counts, dest, h3, n_slots):
    n = h3.shape[0] // ROW_SLABS
    tt = DISPATCH_TILE
    grid_spec = pltpu.PrefetchScalarGridSpec(
        num_scalar_prefetch=1,
        grid=(n // tt,),
        in_specs=[pl.BlockSpec((2 * tt,), lambda i, cnt: (i,), memory_space=pltpu.SMEM),
                  pl.BlockSpec((tt * ROW_SLABS, LANES), lambda i, cnt: (i, 0))],
        out_specs=pl.BlockSpec(memory_space=pl.ANY),
        scratch_shapes=[pltpu.VMEM((EXPERT_BLOCK * ROW_SLABS, LANES), F32), pltpu.SemaphoreType.DMA((3,))],
    )
    return pl.pallas_call(
        _dispatch_body,
        grid_spec=grid_spec,
        out_shape=jax.ShapeDtypeStruct((n_slots * ROW_SLABS, LANES), F32),
        compiler_params=_cparams(("arbitrary",), VMEM_LIMIT),
        name="dispatch",
    )(counts, dest, h3)


def _experts_body(be_ref, nu_ref, xs_ref, w1_ref, w3_ref, w2_ref, ys_ref, w1b, w3b, w2b):
    i = pl.program_id(0)
    n_used = nu_ref[0]

    @pl.when(i < n_used)
    def _():
        changed = (i == 0) | (be_ref[i] != be_ref[jnp.maximum(i - 1, 0)])

        @pl.when(changed)
        def _():
            w1b[...] = w1_ref[0].astype(BF16)
            w3b[...] = w3_ref[0].astype(BF16)
            w2b[...] = w2_ref[0].astype(BF16)

        x = _load_row_tiles(xs_ref).astype(BF16)
        chunks = [slice(c0, c0 + EXPERT_FF_CHUNK) for c0 in range(0, w1b.shape[1], EXPERT_FF_CHUNK)]
        gates = [(jnp.dot(x, w1b[:, cs], preferred_element_type=F32),
                  jnp.dot(x, w3b[:, cs], preferred_element_type=F32)) for cs in chunks]
        y = None
        for cs, (a, u) in zip(chunks, gates):
            mid = (a * jax.nn.sigmoid(a) * u).astype(BF16)
            part = jnp.dot(mid, w2b[cs, :], preferred_element_type=F32)
            y = part if y is None else y + part
        _store_row_tiles(ys_ref, y)

    @pl.when(i >= n_used)
    def _():
        ys_ref[...] = jnp.zeros_like(ys_ref)


def _experts(block_expert, n_used, xs, w1, w3, w2):
    _, d, ff = w1.shape
    assert d == ROW_SLABS * LANES
    blk = EXPERT_BLOCK
    n_blocks = xs.shape[0] // (blk * ROW_SLABS)

    def slot_map(i, be, nu):
        return (jnp.minimum(i, nu[0] - 1), 0)

    def w_map(i, be, nu):
        return (be[jnp.minimum(i, nu[0] - 1)], 0, 0)

    grid_spec = pltpu.PrefetchScalarGridSpec(
        num_scalar_prefetch=2,
        grid=(n_blocks,),
        in_specs=[pl.BlockSpec((blk * ROW_SLABS, LANES), slot_map),
                  pl.BlockSpec((1, d, ff), w_map), pl.BlockSpec((1, d, ff), w_map),
                  pl.BlockSpec((1, ff, d), w_map)],
        out_specs=pl.BlockSpec((blk * ROW_SLABS, LANES), lambda i, be, nu: (i, 0)),
        scratch_shapes=[pltpu.VMEM((d, ff), BF16), pltpu.VMEM((d, ff), BF16), pltpu.VMEM((ff, d), BF16)],
    )
    return pl.pallas_call(
        _experts_body,
        grid_spec=grid_spec,
        out_shape=jax.ShapeDtypeStruct(xs.shape, F32),
        compiler_params=_cparams(("arbitrary",), VMEM_LIMIT),
        name="experts",
    )(block_expert, n_used, xs, w1, w3, w2)


def _combine_body(dest_ref, dest_next_ref, gate_ref, x_ref, ys_ref, out_ref, buf, sems):
    i = pl.program_id(0)
    n_steps = pl.num_programs(0)
    tt = x_ref.shape[0]

    def issue(dref, slot):
        def tokens(g, carry):
            t0 = pl.multiple_of(g * ROW_DMA_UNROLL, ROW_DMA_UNROLL)
            for u in range(ROW_DMA_UNROLL):
                for k in range(2):
                    _row_copy(ys_ref, dref[2 * (t0 + u) + k], buf.at[slot, k], t0 + u,
                              sems.at[slot, k]).start(priority=k)
            return carry

        lax.fori_loop(0, tt // ROW_DMA_UNROLL, tokens, 0)

    @pl.when(i == 0)
    def _():
        issue(dest_ref, 0)

    @pl.when(i + 1 < n_steps)
    def _():
        issue(dest_next_ref, (i + 1) % 2)

    slot = i % 2
    for k in range(2):
        pltpu.make_async_copy(_tile_rows(ys_ref, 0, tt), buf.at[slot, k], sems.at[slot, k]).wait()
    g = gate_ref[...]
    out_ref[...] = (x_ref[...] + g[:, 0:1] * _load_row_tiles(buf.at[slot, 0])
                    + g[:, 1:2] * _load_row_tiles(buf.at[slot, 1]))


def _combine(dest, gates, x2d, ys):
    n, d = x2d.shape
    tt = COMBINE_TILE
    row = lambda w: pl.BlockSpec((tt, w), lambda i: (i, 0))
    last = n // tt - 1
    return pl.pallas_call(
        _combine_body,
        grid=(n // tt,),
        in_specs=[pl.BlockSpec((2 * tt,), lambda i: (i,), memory_space=pltpu.SMEM),
                  pl.BlockSpec((2 * tt,), lambda i: (jnp.minimum(i + 1, last),), memory_space=pltpu.SMEM),
                  row(LANES), row(d), pl.BlockSpec(memory_space=pl.ANY)],
        out_specs=row(d),
        out_shape=jax.ShapeDtypeStruct((n, d), F32),
        scratch_shapes=[pltpu.VMEM((2, 2, tt * ROW_SLABS, LANES), F32), pltpu.SemaphoreType.DMA((2, 2))],
        compiler_params=_cparams(("arbitrary",), VMEM_LIMIT),
        name="combine",
    )(dest, dest, gates, x2d, ys)


def _moe(x2d, h3, gates, meta, cnt, w1, w3, w2):
    n, d = x2d.shape
    blk = EXPERT_BLOCK
    counts = cnt[0, :N_EXPERTS].astype(I32)
    dest = _slots(meta, cnt)[:, 0:2].reshape(-1)
    n_slots = 2 * n + N_EXPERTS * blk
    n_blocks = n_slots // blk
    pends = jnp.cumsum((counts + blk - 1) // blk * blk)
    block_start = jnp.arange(n_blocks, dtype=I32) * blk
    block_expert = jnp.minimum(jnp.sum((pends[None, :] <= block_start[:, None]).astype(I32), axis=1),
                               N_EXPERTS - 1)
    n_used = (pends[-1:] // blk).astype(I32)
    xs = _dispatch(counts, dest, h3, n_slots)
    ys = _experts(block_expert, n_used, xs, w1, w3, w2)
    return _combine(dest, gates, x2d, ys)


def _layer(x, mem, norm_mix_w, w_in, attn_q_norm_w, attn_k_norm_w, ret_decay_f, ret_decay_b, ret_gn_w, w_out,
           norm_mem_w, norm_memkv_w, w_mq, w_mkv, mem_q_norm_w, mem_k_norm_w, w_mo,
           norm_moe_w, w_group, w_router, w_exp_gate, w_exp_up, w_exp_down):
    b, s, d = x.shape
    x2d = x.reshape(b * s, d)
    q, k, v, q_pl, k_pl, v_pl, rq, rk, rv, rg = _in_proj(x2d, norm_mix_w, w_in, attn_q_norm_w, attn_k_norm_w)
    attn = _attention(q, k, v, q_pl, k_pl, v_pl, b, s)
    ret = _retention(rq, rk, rv, rg, ret_decay_f, ret_decay_b, ret_gn_w, b, s)
    mk, mv = _mem_kv(mem, norm_memkv_w, w_mkv, mem_k_norm_w)
    x2, h3, gates, meta, cnt = _mem_attn_route(x, attn, ret, w_out, mk, mv, norm_mem_w, w_mq, mem_q_norm_w, w_mo,
                                               norm_moe_w, w_group, w_router)
    x3 = _moe(x2.reshape(b * s, d), h3, gates, meta, cnt, w_exp_gate, w_exp_up, w_exp_down)
    return x3.reshape(b, s, d)


def kernel(x, mem, norm_mix_w, w_in, attn_q_norm_w, attn_k_norm_w, ret_decay_f, ret_decay_b, ret_gn_w, w_out,
           norm_mem_w, norm_memkv_w, w_mq, w_mkv, mem_q_norm_w, mem_k_norm_w, w_mo, norm_moe_w, w_group,
           w_router, w_exp_gate, w_exp_up, w_exp_down):
    depth = norm_mix_w.shape[0]
    for l in range(depth):
        x = _layer(x, mem, norm_mix_w[l], w_in[l], attn_q_norm_w[l], attn_k_norm_w[l], ret_decay_f[l],
                   ret_decay_b[l], ret_gn_w[l], w_out[l], norm_mem_w[l], norm_memkv_w[l], w_mq[l], w_mkv[l],
                   mem_q_norm_w[l], mem_k_norm_w[l], w_mo[l], norm_moe_w[l], w_group[l], w_router[l],
                   w_exp_gate[l], w_exp_up[l], w_exp_down[l])
    return x
```

```python
import functools

import numpy as np
import jax
import jax.numpy as jnp
from jax import lax
from jax.experimental import pallas as pl
from jax.experimental.pallas import tpu as pltpu

F32 = jnp.float32
BF16 = jnp.bfloat16
I32 = jnp.int32

NORM_EPS = 1e-6
GN_EPS = 1e-5
NEG_INF = -1e30

ATTN_HEADS = 8
ATTN_HEAD_DIM = 64
ATTN_WIDTH = ATTN_HEADS * ATTN_HEAD_DIM
ATTN_HALF = 64
PLANES = 16
ATTN_TILE = PLANES * ATTN_HALF
RET_HEADS = 4
RET_QK_DIM = 64
RET_V_DIM = 128
RET_QK_WIDTH = RET_HEADS * RET_QK_DIM
RET_V_WIDTH = RET_HEADS * RET_V_DIM
MEM_HEADS = 4
N_GROUPS = 4
EXPERTS_PER_GROUP = 8
N_EXPERTS = N_GROUPS * EXPERTS_PER_GROUP

LANES = 128
ROUTER_SUBTILE = 512
MATMUL_TILE = 1024
ATTN_BLOCKS_PER_STEP = 4
LOG2_E = 1.4426950408889634
RET_CHUNK = 256
RET_CHUNKS_PER_STEP = 4
EXPERT_BLOCK = 512
EXPERT_FF_CHUNK = 256
EXPERT_OUT_SLABS = 2
DISPATCH_TILE = 2048
COMBINE_TILE = 256
ROW_DMA_UNROLL = 16
SLOTS_TILE = 4096
VMEM_LIMIT = 56 * 1024 * 1024


def _cparams(sem, vmem=None):
    return pltpu.CompilerParams(dimension_semantics=sem, vmem_limit_bytes=vmem)


def _split_hi_lo(x):
    hi = x.astype(BF16)
    lo = (x - hi.astype(F32)).astype(BF16)
    return hi, lo


def _rms(x, w):
    ms = jnp.mean(x * x, axis=-1, keepdims=True)
    return x * lax.rsqrt(ms + NORM_EPS) * w


ROW_SLABS = 8


def _store_row_tiles(ref, val):
    rows = val.shape[0]
    for j in range(ROW_SLABS):
        ref[pl.ds(j, rows, stride=ROW_SLABS), :] = val[:, j * LANES:(j + 1) * LANES]


def _load_row_tiles(ref):
    rows = ref.shape[0] // ROW_SLABS
    return jnp.concatenate([ref[pl.ds(j, rows, stride=ROW_SLABS), :] for j in range(ROW_SLABS)], axis=1)


def _tile_rows(ref, row, count=1):
    return ref.at[pl.ds(pl.multiple_of(row * ROW_SLABS, ROW_SLABS), count * ROW_SLABS)]


def _in_proj_body(x_ref, nw_ref, w_ref, qw_ref, kw_ref, g_ref,
                  q_out, k_out, v_out, qp_out, kp_out, vp_out, rq_out, rk_out, rv_out, rg_out, slabs, quarters):
    tm = x_ref.shape[0]
    h = _rms(x_ref[...], nw_ref[...]).astype(BF16)

    def emit(val, nat_out, plane_out):
        nat_out[...] = val.astype(BF16)
        for j in range(ATTN_WIDTH // LANES):
            slabs[j] = val[:, j * LANES:(j + 1) * LANES]
        for j in range(ATTN_WIDTH // LANES):
            for r in range(4):
                quarters[j, r] = slabs[j, pl.ds(r, tm // 4, stride=4), :]
        for r in range(4):
            for q in range(4):
                for j in range(ATTN_WIDTH // LANES):
                    rows = quarters[j, r, pl.ds(q, tm // PLANES, stride=4), :]
                    plane_out[0, r + 4 * q, :, j * LANES:(j + 1) * LANES] = rows.astype(BF16)

    def proj(a, b):
        return jnp.dot(h, w_ref[:, a:b], preferred_element_type=F32)

    def head_norm(p, w):
        s = jnp.dot((p * p).astype(BF16), g_ref[...], preferred_element_type=F32)
        return p * lax.rsqrt(s * (1.0 / ATTN_HEAD_DIM) + NORM_EPS) * w

    a = ATTN_WIDTH
    emit(head_norm(proj(0, a), qw_ref[...]), q_out, qp_out)
    emit(head_norm(proj(a, 2 * a), kw_ref[...]), k_out, kp_out)
    emit(proj(2 * a, 3 * a), v_out, vp_out)
    c = 3 * a
    rq_out[...] = proj(c, c + RET_QK_WIDTH).astype(BF16)
    rk_out[...] = proj(c + RET_QK_WIDTH, c + 2 * RET_QK_WIDTH).astype(BF16)
    c += 2 * RET_QK_WIDTH
    rv_out[...] = proj(c, c + RET_V_WIDTH).astype(BF16)
    rg_out[...] = proj(c + RET_V_WIDTH, c + 2 * RET_V_WIDTH).astype(BF16)


def _in_proj(x2d, norm_w, w_in, q_norm_w, k_norm_w):
    n, d = x2d.shape
    cols = w_in.shape[1]
    tm = MATMUL_TILE
    qw = (jnp.tile(q_norm_w, ATTN_HEADS) * (ATTN_HEAD_DIM ** -0.5 * LOG2_E)).reshape(1, ATTN_WIDTH)
    kw = jnp.tile(k_norm_w, ATTN_HEADS).reshape(1, ATTN_WIDTH)
    head_of = np.arange(ATTN_WIDTH) // ATTN_HEAD_DIM
    gmat = jnp.asarray(head_of[:, None] == head_of[None, :], dtype=BF16)
    row = lambda w: pl.BlockSpec((tm, w), lambda i: (i, 0))
    full = lambda r, c: pl.BlockSpec((r, c), lambda i: (0, 0))
    per_tile = ATTN_TILE // tm
    plane = pl.BlockSpec((1, PLANES, tm // PLANES, ATTN_WIDTH), lambda i: (i // per_tile, 0, i % per_tile, 0))
    plane_shape = jax.ShapeDtypeStruct((n // ATTN_TILE, PLANES, ATTN_TILE // PLANES, ATTN_WIDTH), BF16)
    ret_widths = (RET_QK_WIDTH, RET_QK_WIDTH, RET_V_WIDTH, RET_V_WIDTH)
    return pl.pallas_call(
        _in_proj_body,
        grid=(n // tm,),
        in_specs=[row(d), full(1, d), full(d, cols), full(1, ATTN_WIDTH), full(1, ATTN_WIDTH),
                  full(ATTN_WIDTH, ATTN_WIDTH)],
        out_specs=[row(ATTN_WIDTH)] * 3 + [plane] * 3 + [row(w) for w in ret_widths],
        out_shape=([jax.ShapeDtypeStruct((n, ATTN_WIDTH), BF16)] * 3 + [plane_shape] * 3
                   + [jax.ShapeDtypeStruct((n, w), BF16) for w in ret_widths]),
        scratch_shapes=[pltpu.VMEM((ATTN_WIDTH // LANES, tm, LANES), F32),
                        pltpu.VMEM((ATTN_WIDTH // LANES, 4, tm // 4, LANES), F32)],
        compiler_params=_cparams(("parallel",), VMEM_LIMIT),
        name="in_proj",
    )(x2d, norm_w.reshape(1, d), w_in.astype(BF16), qw, kw, gmat)


def _attn_bias(dilation, interleave=1):
    half = ATTN_HALF
    idx = np.arange(half)
    sub = (idx % (half // interleave)) * interleave + idx // (half // interleave)
    qi = sub[:, None]
    kc = (np.arange(3)[:, None] * half + sub[None, :]).reshape(1, -1)
    seg = np.repeat(np.arange(3), half)[None, :]
    delta = kc - half - qi
    band = np.abs(delta) <= half
    slopes = np.exp2(-8.0 * np.arange(1, ATTN_HEADS + 1) / ATTN_HEADS)
    valid = [band & (seg >= 1), band, band & (seg <= 1)]
    out = np.empty((ATTN_HEADS // 2, 3, 2 * half, 3 * half), np.float32)
    for p in range(ATTN_HEADS // 2):
        for v in range(3):
            for s in range(2):
                b = -slopes[2 * p + s] * LOG2_E * (dilation * np.abs(delta)).astype(np.float32)
                out[p, v, s * half:(s + 1) * half] = np.where(valid[v], b, NEG_INF)
    return jnp.asarray(out.reshape(-1, 2 * half, 3 * half))


def _attn_scores(chains):
    lo_lanes = lax.broadcasted_iota(I32, (ATTN_HALF, LANES), 1) < ATTN_HEAD_DIM
    scores = []
    for q, k, v, bias in chains:
        zero = jnp.zeros_like(q)
        qs = jnp.concatenate([jnp.where(lo_lanes, q, zero), jnp.where(lo_lanes, zero, q)], axis=0)
        scores.append(lax.dot_general(qs, k, (((1,), (1,)), ((), ())), preferred_element_type=F32) + bias)
    return scores


def _attn_finish(chains, scores):
    half = ATTN_HALF
    lo_lanes = lax.broadcasted_iota(I32, (half, LANES), 1) < ATTN_HEAD_DIM
    ones = jnp.ones((3 * half, LANES), BF16)
    probs, maxes = [], []
    for s in scores:
        m = jnp.max(s, axis=-1, keepdims=True)
        probs.append(jnp.exp2(s - m).astype(BF16))
        maxes.append(m)
    results = []
    for (q, k, v, bias), e in zip(chains, probs):
        vext = jnp.concatenate([v, ones], axis=1)
        results.append(jnp.dot(e, vext, preferred_element_type=F32))
    outs = []
    for r, m in zip(results, maxes):
        acc = jnp.where(lo_lanes, r[:half, :LANES], r[half:, :LANES])
        denom = jnp.where(lo_lanes, r[:half, LANES:], r[half:, LANES:])
        mm = jnp.where(lo_lanes, m[:half], m[half:])
        outs.append((acc / denom, mm + jnp.log2(denom)))
    return outs


def _attend_groups(groups):
    pending = None
    for chains, consume in groups:
        scores = _attn_scores(chains)
        if pending is not None:
            p_chains, p_scores, p_consume = pending
            p_consume(_attn_finish(p_chains, p_scores))
        pending = (chains, scores, consume)
    p_chains, p_scores, p_consume = pending
    p_consume(_attn_finish(p_chains, p_scores))


def _merge(oa, la, ob, lb):
    m = jnp.maximum(la, lb)
    ea = jnp.exp2(la - m)
    eb = jnp.exp2(lb - m)
    den = ea + eb
    return (ea * oa + eb * ob) / den, m + jnp.log2(den)


def _attn_body(q_ref, kp_ref, km_ref, kn_ref, vp_ref, vm_ref, vn_ref,
               qpl_ref, kplp_ref, kplm_ref, kpln_ref, vplp_ref, vplm_ref, vpln_ref,
               b1_ref, b4_ref, b16_ref, o_ref, kcat, vcat, o_far, l_far):
    half = ATTN_HALF
    t = pl.program_id(1)
    n_tiles = pl.num_programs(1)
    tile = q_ref.shape[1]
    pairs = ATTN_HEADS // 2
    piece = half // 4

    kcat[0:half] = kp_ref[0]
    kcat[half:half + tile] = km_ref[0]
    kcat[half + tile:] = kn_ref[0]
    vcat[0:half] = vp_ref[0]
    vcat[half:half + tile] = vm_ref[0]
    vcat[half + tile:] = vn_ref[0]

    def edge(first, last):
        return jnp.where(first, 0, jnp.where(last, 2, 1))

    groups = []

    def far_groups(r):
        v16 = edge(t == 0, t == n_tiles - 1)
        for pr in range(pairs):
            cs = slice(pr * LANES, (pr + 1) * LANES)
            chains = []
            for m in range(4):
                c = r + 4 * m
                k3 = jnp.concatenate([kplp_ref[0, c, :, cs], kplm_ref[0, c, :, cs], kpln_ref[0, c, :, cs]], axis=0)
                v3 = jnp.concatenate([vplp_ref[0, c, :, cs], vplm_ref[0, c, :, cs], vpln_ref[0, c, :, cs]], axis=0)
                chains.append((qpl_ref[0, c, :, cs], k3, v3, b16_ref[pr * 3 + v16]))
            for nb in range(4):
                def rows(main, before, after, a0):
                    ref, lo = (before, a0 + half) if a0 < 0 else (after, a0 - half) if a0 >= half else (main, a0)
                    return [ref[0, r + 4 * m, lo:lo + piece, cs] for m in range(4)]
                a0 = nb * piece
                q4 = jnp.concatenate(rows(qpl_ref, None, None, a0), axis=0)
                k4 = jnp.concatenate(sum((rows(kplm_ref, kplp_ref, kpln_ref, a0 + d) for d in (-piece, 0, piece)), []),
                                     axis=0)
                v4 = jnp.concatenate(sum((rows(vplm_ref, vplp_ref, vpln_ref, a0 + d) for d in (-piece, 0, piece)), []),
                                     axis=0)
                v4e = edge((t == 0) & (nb == 0), (t == n_tiles - 1) & (nb == 3))
                chains.append((q4, k4, v4, b4_ref[pr * 3 + v4e]))

            def consume(res, r=r, pr=pr):
                for m in range(4):
                    o16, l16 = res[m]
                    o_rows, l_rows = [], []
                    for nb in range(4):
                        o4, l4 = res[4 + nb]
                        sl = slice(nb * piece, (nb + 1) * piece)
                        s4 = slice(m * piece, (m + 1) * piece)
                        om, lm = _merge(o16[sl], l16[sl], o4[s4], l4[s4])
                        o_rows.append(om)
                        l_rows.append(lm)
                    dst = pl.ds(r + 4 * m, half, stride=PLANES)
                    o_far[pr, dst, :] = jnp.concatenate(o_rows, axis=0)
                    l_far[pr, dst, :] = jnp.concatenate(l_rows, axis=0)

            groups.append((chains, consume))

    for r in range(4):
        far_groups(r)

    n_blocks = n_tiles * (tile // half)
    for it in range(tile // half // ATTN_BLOCKS_PER_STEP):
        chains, where = [], []
        for u in range(ATTN_BLOCKS_PER_STEP):
            jb = it * ATTN_BLOCKS_PER_STEP + u
            r0 = jb * half
            gb = t * (tile // half) + jb
            variant = edge(gb == 0, gb == n_blocks - 1)
            for pr in range(pairs):
                cs = slice(pr * LANES, (pr + 1) * LANES)
                chains.append((q_ref[0, pl.ds(r0, half), cs], kcat[pl.ds(r0, 3 * half), cs],
                               vcat[pl.ds(r0, 3 * half), cs], b1_ref[pr * 3 + variant]))
                where.append((r0, pr, cs))

        def consume(res, where=where):
            for (r0, pr, cs), (o1, l1) in zip(where, res):
                o, _ = _merge(o1, l1, o_far[pr, pl.ds(r0, half), :], l_far[pr, pl.ds(r0, half), :])
                o_ref[0, pl.ds(r0, half), cs] = o.astype(o_ref.dtype)

        groups.append((chains, consume))

    _attend_groups(groups)


def _attention(q, k, v, q_pl, k_pl, v_pl, b, s):
    w = ATTN_WIDTH
    half = ATTN_HALF
    tile = ATTN_TILE
    n_tiles = s // tile
    hb = tile // half
    last = s // half - 1
    main = pl.BlockSpec((1, tile, w), lambda bb, t: (bb, t, 0))
    prev = pl.BlockSpec((1, half, w), lambda bb, t: (bb, jnp.maximum(t * hb - 1, 0), 0))
    nxt = pl.BlockSpec((1, half, w), lambda bb, t: (bb, jnp.minimum((t + 1) * hb, last), 0))
    pshape = (1, PLANES, tile // PLANES, w)
    pl_main = pl.BlockSpec(pshape, lambda bb, t: (bb * n_tiles + t, 0, 0, 0))
    pl_prev = pl.BlockSpec(pshape, lambda bb, t: (bb * n_tiles + jnp.maximum(t - 1, 0), 0, 0, 0))
    pl_next = pl.BlockSpec(pshape, lambda bb, t: (bb * n_tiles + jnp.minimum(t + 1, n_tiles - 1), 0, 0, 0))
    biases = [_attn_bias(1), _attn_bias(4, interleave=4), _attn_bias(16)]
    bias_spec = pl.BlockSpec(biases[0].shape, lambda bb, t: (0, 0, 0))
    nat = lambda a: a.reshape(b, s, w)
    out = pl.pallas_call(
        _attn_body,
        grid=(b, n_tiles),
        in_specs=[main, prev, main, nxt, prev, main, nxt,
                  pl_main, pl_prev, pl_main, pl_next, pl_prev, pl_main, pl_next,
                  bias_spec, bias_spec, bias_spec],
        out_specs=main,
        out_shape=jax.ShapeDtypeStruct((b, s, w), BF16),
        scratch_shapes=[pltpu.VMEM((tile + 2 * half, w), BF16), pltpu.VMEM((tile + 2 * half, w), BF16),
                        pltpu.VMEM((ATTN_HEADS // 2, tile, LANES), F32),
                        pltpu.VMEM((ATTN_HEADS // 2, tile, LANES), F32)],
        compiler_params=_cparams(("parallel", "parallel"), VMEM_LIMIT),
        name="attention",
    )(nat(q), nat(k), nat(k), nat(k), nat(v), nat(v), nat(v),
      q_pl, k_pl, k_pl, k_pl, v_pl, v_pl, v_pl, *biases)
    return out.reshape(b * s, w)


def _ret_body(lg_ref, q_ref, k_ref, v_ref, g_ref, gnw_ref, out_ref,
              dmat, qdec, kdec, cdec, fstate, rstate, rall, *, chunk):
    c = chunk
    b = pl.program_id(0)
    ph = pl.program_id(1)
    n = pl.program_id(2)
    n_chunks = pl.num_programs(2)
    k_scale = RET_QK_DIM ** -0.5
    pairs = RET_HEADS // 2
    pw = 2 * RET_QK_DIM
    vw = 2 * RET_V_DIM
    first = lax.broadcasted_iota(I32, (c, pw), 1) < RET_QK_DIM
    diag = ((lax.broadcasted_iota(I32, (pw, vw), 0) < RET_QK_DIM)
            == (lax.broadcasted_iota(I32, (pw, vw), 1) < RET_V_DIM))

    @pl.when((b == 0) & (ph == 0) & (n == 0))
    def _init_tables():
        ii = lax.broadcasted_iota(I32, (c, c), 0)
        jj = lax.broadcasted_iota(I32, (c, c), 1)
        fwd = (ii - jj).astype(F32)
        ri = lax.broadcasted_iota(I32, (c, pw), 0).astype(F32)
        top = lax.broadcasted_iota(I32, (pw, vw), 0) < RET_QK_DIM
        for h in range(RET_HEADS):
            dmat[h] = jnp.where(ii >= jj, jnp.exp(lg_ref[0, h] * fwd), jnp.exp(-lg_ref[1, h] * fwd)) * k_scale
        for pr in range(pairs):
            lf = jnp.where(first, lg_ref[0, 2 * pr], lg_ref[0, 2 * pr + 1])
            lb = jnp.where(first, lg_ref[1, 2 * pr], lg_ref[1, 2 * pr + 1])
            qdec[0, pr] = jnp.exp(lf * (ri + 1.0))
            qdec[1, pr] = jnp.exp(lb * (c - ri))
            kdec[0, pr] = jnp.exp(lf * (c - 1.0 - ri)) * k_scale
            kdec[1, pr] = jnp.exp(lb * ri) * k_scale
            for direction in range(2):
                cdec[direction, pr] = jnp.exp(jnp.where(top, lg_ref[direction, 2 * pr],
                                                        lg_ref[direction, 2 * pr + 1]) * c)

    per_step = q_ref.shape[1] // c

    def kv_update(direction, pr, rows):
        kp = k_ref[0, rows, pr * pw:(pr + 1) * pw].astype(F32)
        ks = (kp * kdec[direction, pr]).astype(BF16)
        new = lax.dot_general(ks, v_ref[0, rows, pr * vw:(pr + 1) * vw], (((0,), (0,)), ((), ())),
                              preferred_element_type=F32)
        return jnp.where(diag, new, 0.0)

    @pl.when(ph == 0)
    def _right_to_left():
        @pl.when(n == 0)
        def _():
            rstate[...] = jnp.zeros_like(rstate)

        for sub in reversed(range(per_step)):
            rows = slice(sub * c, (sub + 1) * c)
            ci = (n_chunks - 1 - n) * per_step + sub
            for pr in range(pairs):
                st = rstate[pr]
                rall[ci, pr] = st
                rstate[pr] = st * cdec[1, pr] + kv_update(1, pr, rows)

    @pl.when(ph == 1)
    def _left_to_right():
        @pl.when(n == 0)
        def _():
            fstate[...] = jnp.zeros_like(fstate)

        for sub in range(per_step):
            rows = slice(sub * c, (sub + 1) * c)
            for pr in range(pairs):
                qp = q_ref[0, rows, pr * pw:(pr + 1) * pw]
                kp = k_ref[0, rows, pr * pw:(pr + 1) * pw]
                qf = qp.astype(F32)
                qcat = jnp.concatenate([(qf * qdec[0, pr]).astype(BF16), (qf * qdec[1, pr]).astype(BF16)], axis=1)
                st = fstate[pr]
                states = jnp.concatenate([st, rall[n * per_step + sub, pr]], axis=0).astype(BF16)
                cross = jnp.dot(qcat, states, preferred_element_type=F32)
                fstate[pr] = st * cdec[0, pr] + kv_update(0, pr, rows)
                zero = jnp.zeros_like(qp)
                for a in range(2):
                    h = 2 * pr + a
                    vs = slice(h * RET_V_DIM, (h + 1) * RET_V_DIM)
                    qm = jnp.where(first, qp, zero) if a == 0 else jnp.where(first, zero, qp)
                    s = lax.dot_general(qm, kp, (((1,), (1,)), ((), ())), preferred_element_type=F32) * dmat[h]
                    y = jnp.dot(s.astype(BF16), v_ref[0, rows, vs], preferred_element_type=F32)
                    y = y + cross[:, a * RET_V_DIM:(a + 1) * RET_V_DIM]
                    mu = jnp.mean(y, axis=-1, keepdims=True)
                    yc = y - mu
                    var = jnp.mean(yc * yc, axis=-1, keepdims=True)
                    yn = yc * lax.rsqrt(var + GN_EPS) * gnw_ref[:, vs]
                    gate = g_ref[0, rows, vs].astype(F32)
                    out_ref[0, rows, vs] = (gate * jax.nn.sigmoid(gate) * yn).astype(out_ref.dtype)


def _retention(rq, rk, rv, rg, decay_f, decay_b, gn_w, b, s):
    c = RET_CHUNK
    rows = RET_CHUNK * RET_CHUNKS_PER_STEP
    nb = s // rows
    lg = jnp.stack([jax.nn.log_sigmoid(decay_f.astype(F32)), jax.nn.log_sigmoid(decay_b.astype(F32))])
    qk_w, v_w = RET_QK_WIDTH, RET_V_WIDTH
    pairs, pair_qk, pair_v = RET_HEADS // 2, 2 * RET_QK_DIM, 2 * RET_V_DIM

    def both(bb, ph, n):
        return (bb, jnp.where(ph == 0, nb - 1 - n, n), 0)

    def fwd_only(bb, ph, n):
        return (bb, jnp.where(ph == 0, 0, n), 0)

    out = pl.pallas_call(
        functools.partial(_ret_body, chunk=c),
        grid=(b, 2, nb),
        in_specs=[pl.BlockSpec(memory_space=pltpu.SMEM),
                  pl.BlockSpec((1, rows, qk_w), fwd_only),
                  pl.BlockSpec((1, rows, qk_w), both),
                  pl.BlockSpec((1, rows, v_w), both),
                  pl.BlockSpec((1, rows, v_w), fwd_only),
                  pl.BlockSpec((1, v_w), lambda bb, ph, n: (0, 0))],
        out_specs=pl.BlockSpec((1, rows, v_w), fwd_only),
        out_shape=jax.ShapeDtypeStruct((b, s, v_w), BF16),
        scratch_shapes=[pltpu.VMEM((RET_HEADS, c, c), F32),
                        pltpu.VMEM((2, pairs, c, pair_qk), F32),
                        pltpu.VMEM((2, pairs, c, pair_qk), F32),
                        pltpu.VMEM((2, pairs, pair_qk, pair_v), F32),
                        pltpu.VMEM((pairs, pair_qk, pair_v), F32),
                        pltpu.VMEM((pairs, pair_qk, pair_v), F32),
                        pltpu.VMEM((s // c, pairs, pair_qk, pair_v), F32)],
        compiler_params=_cparams(("arbitrary", "arbitrary", "arbitrary"), VMEM_LIMIT),
        name="retention",
    )(lg, rq.reshape(b, s, qk_w), rk.reshape(b, s, qk_w), rv.reshape(b, s, v_w), rg.reshape(b, s, v_w),
      gn_w.reshape(1, v_w))
    return out.reshape(b * s, v_w)


def _mem_kv_body(mem_ref, nw_ref, w_ref, kw_ref, k_out, v_out):
    d = mem_ref.shape[-1]
    hd = d // MEM_HEADS
    h = _rms(mem_ref[0], nw_ref[...]).astype(BF16)
    kv = jnp.dot(h, w_ref[...], preferred_element_type=F32)
    for i in range(MEM_HEADS):
        k_out[0, :, i * hd:(i + 1) * hd] = _rms(kv[:, i * hd:(i + 1) * hd], kw_ref[...]).astype(BF16)
    v_out[0] = kv[:, d:].astype(BF16)


def _mem_kv(mem, norm_w, w_mkv, k_norm_w):
    b, m, d = mem.shape
    return pl.pallas_call(
        _mem_kv_body,
        grid=(b,),
        in_specs=[pl.BlockSpec((1, m, d), lambda i: (i, 0, 0)),
                  pl.BlockSpec((1, d), lambda i: (0, 0)),
                  pl.BlockSpec((d, 2 * d), lambda i: (0, 0)),
                  pl.BlockSpec((1, d // MEM_HEADS), lambda i: (0, 0))],
        out_specs=[pl.BlockSpec((1, m, d), lambda i: (i, 0, 0))] * 2,
        out_shape=[jax.ShapeDtypeStruct((b, m, d), BF16)] * 2,
        compiler_params=_cparams(("parallel",), VMEM_LIMIT),
        name="mem_kv",
    )(mem, norm_w.reshape(1, d), w_mkv.astype(BF16), k_norm_w.reshape(1, -1))


def _mem_attn_body(x_ref, attn_ref, ret_ref, wout_ref, nw_ref, wq_ref, qw_ref, k_ref, v_ref, wo_ref,
                   rnw_ref, rw_ref, out_ref, h_out, gate_out, meta_out, cnt_out, carry, below):
    d = x_ref.shape[-1]
    hd = d // MEM_HEADS
    tm = x_ref.shape[1]
    sub = below.shape[0]

    @pl.when((pl.program_id(0) == 0) & (pl.program_id(1) == 0))
    def _():
        carry[...] = jnp.zeros_like(carry)
        rr = lax.broadcasted_iota(I32, (sub, sub), 0)
        cc = lax.broadcasted_iota(I32, (sub, sub), 1)
        below[...] = (cc < rr).astype(F32).astype(BF16)

    x = (x_ref[0] + jnp.dot(attn_ref[0], wout_ref[:ATTN_WIDTH], preferred_element_type=F32)
         + jnp.dot(ret_ref[0], wout_ref[ATTN_WIDTH:], preferred_element_type=F32))
    h = _rms(x, nw_ref[...]).astype(BF16)
    q = jnp.dot(h, wq_ref[...], preferred_element_type=F32)
    heads = []
    for i in range(MEM_HEADS):
        cs = slice(i * hd, (i + 1) * hd)
        qn = _rms(q[:, cs], qw_ref[...]).astype(BF16)
        s = lax.dot_general(qn, k_ref[0, :, cs], (((1,), (1,)), ((), ())), preferred_element_type=F32)
        e = jnp.exp(s - jnp.max(s, axis=-1, keepdims=True))
        o = jnp.dot(e.astype(BF16), v_ref[0, :, cs], preferred_element_type=F32)
        heads.append((o / jnp.sum(e, axis=-1, keepdims=True)).astype(BF16))
    o = jnp.concatenate(heads, axis=1)
    x2 = x + jnp.dot(o, wo_ref[...], preferred_element_type=F32)
    out_ref[0] = x2

    base = carry[...]
    for part in range(tm // sub):
        rows = slice(part * sub, (part + 1) * sub)
        base = _route_rows(x2[rows], rnw_ref, rw_ref, below, base,
                           h_out.at[pl.ds(part * sub * ROW_SLABS, sub * ROW_SLABS)],
                           gate_out.at[rows], meta_out.at[rows])
    carry[...] = base
    cnt_out[...] = base


def _mem_attn_route(x, attn, ret, w_out, mk, mv, norm_w, w_mq, q_norm_w, w_mo, moe_norm_w, w_group, w_router):
    b, s, d = x.shape
    m = mk.shape[1]
    tm = MATMUL_TILE
    steps = s // tm
    hd = d // MEM_HEADS
    qw = (q_norm_w * (hd ** -0.5)).reshape(1, hd)
    w_all = jnp.concatenate([w_group, w_router.transpose(1, 0, 2).reshape(d, N_EXPERTS)], axis=1)
    w_all = jnp.pad(w_all, ((0, 0), (0, LANES - w_all.shape[1])))
    whi = w_all.astype(BF16)
    w_split = jnp.concatenate([whi, (w_all - whi.astype(F32)).astype(BF16)], axis=1)
    tok = lambda w: pl.BlockSpec((1, tm, w), lambda bb, i: (bb, i, 0))
    flat = lambda r, w: pl.BlockSpec((r, w), lambda bb, i: (bb * steps + i, 0))
    const = lambda r, c: pl.BlockSpec((r, c), lambda bb, i: (0, 0))
    mem = pl.BlockSpec((1, m, d), lambda bb, i: (bb, 0, 0))
    n = b * s
    return pl.pallas_call(
        _mem_attn_body,
        grid=(b, steps),
        in_specs=[tok(d), tok(ATTN_WIDTH), tok(RET_V_WIDTH), const(ATTN_WIDTH + RET_V_WIDTH, d),
                  const(1, d), const(d, d), const(1, hd), mem, mem, const(d, d),
                  const(1, d), const(d, 2 * LANES)],
        out_specs=[tok(d), flat(tm * ROW_SLABS, LANES), flat(tm, LANES), flat(tm, LANES), const(1, LANES)],
        out_shape=[jax.ShapeDtypeStruct((b, s, d), F32), jax.ShapeDtypeStruct((n * ROW_SLABS, LANES), F32),
                   jax.ShapeDtypeStruct((n, LANES), F32), jax.ShapeDtypeStruct((n, LANES), I32),
                   jax.ShapeDtypeStruct((1, LANES), F32)],
        scratch_shapes=[pltpu.VMEM((1, LANES), F32), pltpu.VMEM((ROUTER_SUBTILE, ROUTER_SUBTILE), BF16)],
        compiler_params=_cparams(("arbitrary", "arbitrary"), VMEM_LIMIT),
        name="mem_attn_route",
    )(x, attn.reshape(b, s, ATTN_WIDTH), ret.reshape(b, s, RET_V_WIDTH), w_out.astype(BF16),
      norm_w.reshape(1, d), w_mq.astype(BF16), qw, mk, mv, w_mo.astype(BF16),
      moe_norm_w.reshape(1, d), w_split)


def _route_rows(x, nw_ref, w_ref, below, base, h_out, gate_out, meta_out):
    tm = x.shape[0]
    h = _rms(x, nw_ref[...])
    _store_row_tiles(h_out, h)
    hi, lo = _split_hi_lo(h)
    both = jnp.dot(hi, w_ref[...], preferred_element_type=F32)
    logits = (both[:, :LANES] + both[:, LANES:]
              + jnp.dot(lo, w_ref[:, :LANES], preferred_element_type=F32))
    lane = lax.broadcasted_iota(I32, (tm, LANES), 1)
    lane_f = lane.astype(F32)

    def argmax_lanes(vals):
        top = jnp.max(vals, axis=-1, keepdims=True)
        idx = jnp.min(jnp.where(vals == top, lane_f, float(LANES)), axis=-1, keepdims=True)
        return top, idx

    is_group = lane < N_GROUPS
    g_top, grp = argmax_lanes(jnp.where(is_group, logits, -jnp.inf))
    g_gate = 1.0 / jnp.sum(jnp.where(is_group, jnp.exp(logits - g_top), 0.0), axis=-1, keepdims=True)
    first = N_GROUPS + EXPERTS_PER_GROUP * grp
    in_group = (lane_f >= first) & (lane_f < first + EXPERTS_PER_GROUP)
    el = jnp.where(in_group, logits, -jnp.inf)
    t1, i1 = argmax_lanes(el)
    t2, i2 = argmax_lanes(jnp.where(lane_f == i1, -jnp.inf, el))
    z = jnp.exp(t2 - t1)
    g1 = g_gate / (1.0 + z)
    g2 = g_gate * z / (1.0 + z)
    e1 = i1 - N_GROUPS
    e2 = i2 - N_GROUPS

    oh1 = (lane_f == e1)
    oh2 = (lane_f == e2)
    oh1b = oh1.astype(F32).astype(BF16)
    oh2b = oh2.astype(F32).astype(BF16)
    pre = jnp.dot(below[...], jnp.concatenate([oh1b, oh2b], axis=1), preferred_element_type=F32)
    pre1, pre2 = pre[:, :LANES], pre[:, LANES:]
    cnt1 = jnp.sum(oh1.astype(F32), axis=0, keepdims=True)
    cnt2 = jnp.sum(oh2.astype(F32), axis=0, keepdims=True)
    r1 = jnp.sum(jnp.where(oh1, pre1 + base, 0.0), axis=-1, keepdims=True)
    r2 = jnp.sum(jnp.where(oh2, pre2 + base + cnt1, 0.0), axis=-1, keepdims=True)

    gate_out[...] = jnp.where(lane == 0, g1, jnp.where(lane == 1, g2, 0.0))
    meta = jnp.where(lane == 0, e1, jnp.where(lane == 1, e2, jnp.where(lane == 2, r1, jnp.where(lane == 3, r2, 0.0))))
    meta_out[...] = meta.astype(I32)
    return base + cnt1 + cnt2


def _row_copy(src_ref, src_row, dst_ref, dst_row, sem):
    return pltpu.make_async_copy(_tile_rows(src_ref, src_row), _tile_rows(dst_ref, dst_row), sem)


def _slots_body(meta_ref, cnt_ref, dest_ref):
    tm = meta_ref.shape[0]
    blk = float(EXPERT_BLOCK)
    lane = lax.broadcasted_iota(I32, (tm, LANES), 1)
    lane_f = lane.astype(F32)
    blocks = jnp.floor((cnt_ref[...] + (blk - 1.0)) * (1.0 / blk))
    rr = lax.broadcasted_iota(I32, (LANES, LANES), 0)
    cc = lax.broadcasted_iota(I32, (LANES, LANES), 1)
    before = (rr < cc).astype(F32).astype(BF16)
    first_block = jnp.dot(jnp.broadcast_to(blocks, (8, LANES)).astype(BF16), before,
                          preferred_element_type=F32)[0:1]
    start = first_block * blk
    meta = meta_ref[...].astype(F32)

    def slot_of(k):
        e = jnp.sum(jnp.where(lane == k, meta, 0.0), axis=-1, keepdims=True)
        r = jnp.sum(jnp.where(lane == 2 + k, meta, 0.0), axis=-1, keepdims=True)
        return jnp.sum(jnp.where(lane_f == e, start, 0.0), axis=-1, keepdims=True) + r

    dest_ref[...] = jnp.where(lane == 0, slot_of(0), jnp.where(lane == 1, slot_of(1), 0.0)).astype(I32)


def _slots(meta, cnt):
    n = meta.shape[0]
    tm = min(SLOTS_TILE, n)
    row = pl.BlockSpec((tm, LANES), lambda i: (i, 0))
    return pl.pallas_call(
        _slots_body,
        grid=(n // tm,),
        in_specs=[row, pl.BlockSpec((1, LANES), lambda i: (0, 0))],
        out_specs=row,
        out_shape=jax.ShapeDtypeStruct((n, LANES), I32),
        compiler_params=_cparams(("parallel",), VMEM_LIMIT),
        name="slots",
    )(meta, cnt)


def _dispatch_body(cnt_ref, dest_ref, h_ref, xs_ref, zbuf, sems):
    i = pl.program_id(0)
    tt = h_ref.shape[0] // ROW_SLABS
    blk = EXPERT_BLOCK
    n_slots = xs_ref.shape[0] // ROW_SLABS

    def pad_copy(pend):
        return pltpu.make_async_copy(zbuf, _tile_rows(xs_ref, pend - blk, blk), sems.at[2])

    @pl.when(i == 0)
    def _():
        zbuf[...] = jnp.zeros_like(zbuf)

        def start(e, acc):
            pend = acc + ((cnt_ref[e] + (blk - 1)) // blk) * blk

            @pl.when(pend > acc)
            def _():
                pad_copy(pend).start()
            return pend

        used = lax.fori_loop(0, N_EXPERTS, start, jnp.int32(0))

        def tail_start(j, carry):
            pad_copy((j + 1) * blk).start()
            return carry

        lax.fori_loop(used // blk, n_slots // blk, tail_start, 0)

        def finish(e, acc):
            pend = acc + ((cnt_ref[e] + (blk - 1)) // blk) * blk

            @pl.when(pend > acc)
            def _():
                pad_copy(pend).wait()
            return pend

        lax.fori_loop(0, N_EXPERTS, finish, jnp.int32(0))

        def tail_wait(j, carry):
            pad_copy((j + 1) * blk).wait()
            return carry

        lax.fori_loop(used // blk, n_slots // blk, tail_wait, 0)

    def tokens(g, carry):
        t0 = pl.multiple_of(g * ROW_DMA_UNROLL, ROW_DMA_UNROLL)
        for u in range(ROW_DMA_UNROLL):
            for k in range(2):
                _row_copy(h_ref, t0 + u, xs_ref, dest_ref[2 * (t0 + u) + k], sems.at[k]).start(priority=k)
        return carry

    lax.fori_loop(0, tt // ROW_DMA_UNROLL, tokens, 0)
    for k in range(2):
        pltpu.make_async_copy(h_ref, _tile_rows(xs_ref, 0, tt), sems.at[k]).wait()


def _dispatch(counts, dest, h3, n_slots):
    n = h3.shape[0] // ROW_SLABS
    tt = DISPATCH_TILE
    grid_spec = pltpu.PrefetchScalarGridSpec(
        num_scalar_prefetch=1,
        grid=(n // tt,),
        in_specs=[pl.BlockSpec((2 * tt,), lambda i, cnt: (i,), memory_space=pltpu.SMEM),
                  pl.BlockSpec((tt * ROW_SLABS, LANES), lambda i, cnt: (i, 0))],
        out_specs=pl.BlockSpec(memory_space=pl.ANY),
        scratch_shapes=[pltpu.VMEM((EXPERT_BLOCK * ROW_SLABS, LANES), F32), pltpu.SemaphoreType.DMA((3,))],
    )
    return pl.pallas_call(
        _dispatch_body,
        grid_spec=grid_spec,
        out_shape=jax.ShapeDtypeStruct((n_slots * ROW_SLABS, LANES), F32),
        compiler_params=_cparams(("arbitrary",), VMEM_LIMIT),
        name="dispatch",
    )(counts, dest, h3)


def _experts_body(be_ref, nu_ref, xs_ref, w1_ref, w3_ref, w2_ref, ys_ref, w1b, w3b, w2b):
    i = pl.program_id(0)
    n_used = nu_ref[0]

    @pl.when(i < n_used)
    def _():
        changed = (i == 0) | (be_ref[i] != be_ref[jnp.maximum(i - 1, 0)])

        @pl.when(changed)
        def _():
            w1b[...] = w1_ref[0].astype(BF16)
            w3b[...] = w3_ref[0].astype(BF16)
            w2b[...] = w2_ref[0].astype(BF16)

        x = _load_row_tiles(xs_ref).astype(BF16)
        chunks = [slice(c0, c0 + EXPERT_FF_CHUNK) for c0 in range(0, w1b.shape[1], EXPERT_FF_CHUNK)]
        gates = [(jnp.dot(x, w1b[:, cs], preferred_element_type=F32),
                  jnp.dot(x, w3b[:, cs], preferred_element_type=F32)) for cs in chunks]
        mid = jnp.concatenate([(a * jax.nn.sigmoid(a) * u).astype(BF16) for a, u in gates], axis=1)
        rows = mid.shape[0]
        for j0 in range(0, ROW_SLABS, EXPERT_OUT_SLABS):
            cols = slice(j0 * LANES, (j0 + EXPERT_OUT_SLABS) * LANES)
            y = jnp.dot(mid, w2b[:, cols], preferred_element_type=F32)
            for j in range(EXPERT_OUT_SLABS):
                ys_ref[pl.ds(j0 + j, rows, stride=ROW_SLABS), :] = y[:, j * LANES:(j + 1) * LANES]

    @pl.when(i >= n_used)
    def _():
        ys_ref[...] = jnp.zeros_like(ys_ref)


def _experts(block_expert, n_used, xs, w1, w3, w2):
    _, d, ff = w1.shape
    assert d == ROW_SLABS * LANES
    blk = EXPERT_BLOCK
    n_blocks = xs.shape[0] // (blk * ROW_SLABS)

    def slot_map(i, be, nu):
        return (jnp.minimum(i, nu[0] - 1), 0)

    def w_map(i, be, nu):
        return (be[jnp.minimum(i, nu[0] - 1)], 0, 0)

    grid_spec = pltpu.PrefetchScalarGridSpec(
        num_scalar_prefetch=2,
        grid=(n_blocks,),
        in_specs=[pl.BlockSpec((blk * ROW_SLABS, LANES), slot_map),
                  pl.BlockSpec((1, d, ff), w_map), pl.BlockSpec((1, d, ff), w_map),
                  pl.BlockSpec((1, ff, d), w_map)],
        out_specs=pl.BlockSpec((blk * ROW_SLABS, LANES), lambda i, be, nu: (i, 0)),
        scratch_shapes=[pltpu.VMEM((d, ff), BF16), pltpu.VMEM((d, ff), BF16), pltpu.VMEM((ff, d), BF16)],
    )
    return pl.pallas_call(
        _experts_body,
        grid_spec=grid_spec,
        out_shape=jax.ShapeDtypeStruct(xs.shape, F32),
        compiler_params=_cparams(("arbitrary",), VMEM_LIMIT),
        name="experts",
    )(block_expert, n_used, xs, w1, w3, w2)


def _combine_body(dest_ref, dest_next_ref, gate_ref, x_ref, ys_ref, out_ref, buf, sems):
    i = pl.program_id(0)
    n_steps = pl.num_programs(0)
    tt = x_ref.shape[0]

    def issue(dref, slot):
        def tokens(g, carry):
            t0 = pl.multiple_of(g * ROW_DMA_UNROLL, ROW_DMA_UNROLL)
            for u in range(ROW_DMA_UNROLL):
                for k in range(2):
                    _row_copy(ys_ref, dref[2 * (t0 + u) + k], buf.at[slot, k], t0 + u,
                              sems.at[slot, k]).start(priority=k)
            return carry

        lax.fori_loop(0, tt // ROW_DMA_UNROLL, tokens, 0)

    @pl.when(i == 0)
    def _():
        issue(dest_ref, 0)

    @pl.when(i + 1 < n_steps)
    def _():
        issue(dest_next_ref, (i + 1) % 2)

    slot = i % 2
    for k in range(2):
        pltpu.make_async_copy(_tile_rows(ys_ref, 0, tt), buf.at[slot, k], sems.at[slot, k]).wait()
    g = gate_ref[...]
    out_ref[...] = (x_ref[...] + g[:, 0:1] * _load_row_tiles(buf.at[slot, 0])
                    + g[:, 1:2] * _load_row_tiles(buf.at[slot, 1]))


def _combine(dest, gates, x2d, ys):
    n, d = x2d.shape
    tt = COMBINE_TILE
    row = lambda w: pl.BlockSpec((tt, w), lambda i: (i, 0))
    last = n // tt - 1
    return pl.pallas_call(
        _combine_body,
        grid=(n // tt,),
        in_specs=[pl.BlockSpec((2 * tt,), lambda i: (i,), memory_space=pltpu.SMEM),
                  pl.BlockSpec((2 * tt,), lambda i: (jnp.minimum(i + 1, last),), memory_space=pltpu.SMEM),
                  row(LANES), row(d), pl.BlockSpec(memory_space=pl.ANY)],
        out_specs=row(d),
        out_shape=jax.ShapeDtypeStruct((n, d), F32),
        scratch_shapes=[pltpu.VMEM((2, 2, tt * ROW_SLABS, LANES), F32), pltpu.SemaphoreType.DMA((2, 2))],
        compiler_params=_cparams(("arbitrary",), VMEM_LIMIT),
        name="combine",
    )(dest, dest, gates, x2d, ys)


def _moe(x2d, h3, gates, meta, cnt, w1, w3, w2):
    n, d = x2d.shape
    blk = EXPERT_BLOCK
    counts = cnt[0, :N_EXPERTS].astype(I32)
    dest = _slots(meta, cnt)[:, 0:2].reshape(-1)
    n_slots = 2 * n + N_EXPERTS * blk
    n_blocks = n_slots // blk
    pends = jnp.cumsum((counts + blk - 1) // blk * blk)
    block_start = jnp.arange(n_blocks, dtype=I32) * blk
    block_expert = jnp.minimum(jnp.sum((pends[None, :] <= block_start[:, None]).astype(I32), axis=1),
                               N_EXPERTS - 1)
    n_used = (pends[-1:] // blk).astype(I32)
    xs = _dispatch(counts, dest, h3, n_slots)
    ys = _experts(block_expert, n_used, xs, w1, w3, w2)
    return _combine(dest, gates, x2d, ys)


def _layer(x, mem, norm_mix_w, w_in, attn_q_norm_w, attn_k_norm_w, ret_decay_f, ret_decay_b, ret_gn_w, w_out,
           norm_mem_w, norm_memkv_w, w_mq, w_mkv, mem_q_norm_w, mem_k_norm_w, w_mo,
           norm_moe_w, w_group, w_router, w_exp_gate, w_exp_up, w_exp_down):
    b, s, d = x.shape
    x2d = x.reshape(b * s, d)
    q, k, v, q_pl, k_pl, v_pl, rq, rk, rv, rg = _in_proj(x2d, norm_mix_w, w_in, attn_q_norm_w, attn_k_norm_w)
    attn = _attention(q, k, v, q_pl, k_pl, v_pl, b, s)
    ret = _retention(rq, rk, rv, rg, ret_decay_f, ret_decay_b, ret_gn_w, b, s)
    mk, mv = _mem_kv(mem, norm_memkv_w, w_mkv, mem_k_norm_w)
    x2, h3, gates, meta, cnt = _mem_attn_route(x, attn, ret, w_out, mk, mv, norm_mem_w, w_mq, mem_q_norm_w, w_mo,
                                               norm_moe_w, w_group, w_router)
    x3 = _moe(x2.reshape(b * s, d), h3, gates, meta, cnt, w_exp_gate, w_exp_up, w_exp_down)
    return x3.reshape(b, s, d)


def kernel(x, mem, norm_mix_w, w_in, attn_q_norm_w, attn_k_norm_w, ret_decay_f, ret_decay_b, ret_gn_w, w_out,
           norm_mem_w, norm_memkv_w, w_mq, w_mkv, mem_q_norm_w, mem_k_norm_w, w_mo, norm_moe_w, w_group,
           w_router, w_exp_gate, w_exp_up, w_exp_down):
    depth = norm_mix_w.shape[0]
    for l in range(depth):
        x = _layer(x, mem, norm_mix_w[l], w_in[l], attn_q_norm_w[l], attn_k_norm_w[l], ret_decay_f[l],
                   ret_decay_b[l], ret_gn_w[l], w_out[l], norm_mem_w[l], norm_memkv_w[l], w_mq[l], w_mkv[l],
                   mem_q_norm_w[l], mem_k_norm_w[l], w_mo[l], norm_moe_w[l], w_group[l], w_router[l],
                   w_exp_gate[l], w_exp_up[l], w_exp_down[l])
    return x
```

```python
import functools

import numpy as np
import jax
import jax.numpy as jnp
from jax import lax
from jax.experimental import pallas as pl
from jax.experimental.pallas import tpu as pltpu

F32 = jnp.float32
BF16 = jnp.bfloat16
I32 = jnp.int32

NORM_EPS = 1e-6
GN_EPS = 1e-5
NEG_INF = -1e30

ATTN_HEADS = 8
ATTN_HEAD_DIM = 64
ATTN_WIDTH = ATTN_HEADS * ATTN_HEAD_DIM
ATTN_HALF = 64
PLANES = 16
ATTN_TILE = PLANES * ATTN_HALF
RET_HEADS = 4
RET_QK_DIM = 64
RET_V_DIM = 128
RET_QK_WIDTH = RET_HEADS * RET_QK_DIM
RET_V_WIDTH = RET_HEADS * RET_V_DIM
MEM_HEADS = 4
N_GROUPS = 4
EXPERTS_PER_GROUP = 8
N_EXPERTS = N_GROUPS * EXPERTS_PER_GROUP

LANES = 128
ROUTER_SUBTILE = 512
MATMUL_TILE = 1024
ATTN_BLOCKS_PER_STEP = 4
LOG2_E = 1.4426950408889634
RET_CHUNK = 256
RET_CHUNKS_PER_STEP = 8
EXPERT_BLOCK = 512
EXPERT_FF_CHUNK = 256
EXPERT_OUT_SLABS = 2
DISPATCH_TILE = 2048
COMBINE_TILE = 256
ROW_DMA_UNROLL = 16
SLOTS_TILE = 4096
VMEM_LIMIT = 56 * 1024 * 1024


def _cparams(sem, vmem=None):
    return pltpu.CompilerParams(dimension_semantics=sem, vmem_limit_bytes=vmem)


def _split_hi_lo(x):
    hi = x.astype(BF16)
    lo = (x - hi.astype(F32)).astype(BF16)
    return hi, lo


def _rms(x, w):
    ms = jnp.mean(x * x, axis=-1, keepdims=True)
    return x * lax.rsqrt(ms + NORM_EPS) * w


ROW_SLABS = 8


def _store_row_tiles(ref, val):
    rows = val.shape[0]
    for j in range(ROW_SLABS):
        ref[pl.ds(j, rows, stride=ROW_SLABS), :] = val[:, j * LANES:(j + 1) * LANES]


def _load_row_tiles(ref):
    rows = ref.shape[0] // ROW_SLABS
    return jnp.concatenate([ref[pl.ds(j, rows, stride=ROW_SLABS), :] for j in range(ROW_SLABS)], axis=1)


def _tile_rows(ref, row, count=1):
    return ref.at[pl.ds(pl.multiple_of(row * ROW_SLABS, ROW_SLABS), count * ROW_SLABS)]


def _in_proj_body(x_ref, nw_ref, w_ref, qw_ref, kw_ref, g_ref,
                  q_out, k_out, v_out, qp_out, kp_out, vp_out, rq_out, rk_out, rv_out, rg_out, slabs, quarters):
    tm = x_ref.shape[0]
    h = _rms(x_ref[...], nw_ref[...]).astype(BF16)

    def emit(val, nat_out, plane_out):
        nat_out[...] = val.astype(BF16)
        for j in range(ATTN_WIDTH // LANES):
            slabs[j] = val[:, j * LANES:(j + 1) * LANES]
        for j in range(ATTN_WIDTH // LANES):
            for r in range(4):
                quarters[j, r] = slabs[j, pl.ds(r, tm // 4, stride=4), :]
        for r in range(4):
            for q in range(4):
                for j in range(ATTN_WIDTH // LANES):
                    rows = quarters[j, r, pl.ds(q, tm // PLANES, stride=4), :]
                    plane_out[0, r + 4 * q, :, j * LANES:(j + 1) * LANES] = rows.astype(BF16)

    def proj(a, b):
        return jnp.dot(h, w_ref[:, a:b], preferred_element_type=F32)

    def head_norm(p, w):
        s = jnp.dot((p * p).astype(BF16), g_ref[...], preferred_element_type=F32)
        return p * lax.rsqrt(s * (1.0 / ATTN_HEAD_DIM) + NORM_EPS) * w

    a = ATTN_WIDTH
    emit(head_norm(proj(0, a), qw_ref[...]), q_out, qp_out)
    emit(head_norm(proj(a, 2 * a), kw_ref[...]), k_out, kp_out)
    emit(proj(2 * a, 3 * a), v_out, vp_out)
    c = 3 * a
    rq_out[...] = proj(c, c + RET_QK_WIDTH).astype(BF16)
    rk_out[...] = proj(c + RET_QK_WIDTH, c + 2 * RET_QK_WIDTH).astype(BF16)
    c += 2 * RET_QK_WIDTH
    rv_out[...] = proj(c, c + RET_V_WIDTH).astype(BF16)
    rg_out[...] = proj(c + RET_V_WIDTH, c + 2 * RET_V_WIDTH).astype(BF16)


def _in_proj(x2d, norm_w, w_in, q_norm_w, k_norm_w):
    n, d = x2d.shape
    cols = w_in.shape[1]
    tm = MATMUL_TILE
    qw = (jnp.tile(q_norm_w, ATTN_HEADS) * (ATTN_HEAD_DIM ** -0.5 * LOG2_E)).reshape(1, ATTN_WIDTH)
    kw = jnp.tile(k_norm_w, ATTN_HEADS).reshape(1, ATTN_WIDTH)
    head_of = np.arange(ATTN_WIDTH) // ATTN_HEAD_DIM
    gmat = jnp.asarray(head_of[:, None] == head_of[None, :], dtype=BF16)
    row = lambda w: pl.BlockSpec((tm, w), lambda i: (i, 0))
    full = lambda r, c: pl.BlockSpec((r, c), lambda i: (0, 0))
    per_tile = ATTN_TILE // tm
    plane = pl.BlockSpec((1, PLANES, tm // PLANES, ATTN_WIDTH), lambda i: (i // per_tile, 0, i % per_tile, 0))
    plane_shape = jax.ShapeDtypeStruct((n // ATTN_TILE, PLANES, ATTN_TILE // PLANES, ATTN_WIDTH), BF16)
    ret_widths = (RET_QK_WIDTH, RET_QK_WIDTH, RET_V_WIDTH, RET_V_WIDTH)
    return pl.pallas_call(
        _in_proj_body,
        grid=(n // tm,),
        in_specs=[row(d), full(1, d), full(d, cols), full(1, ATTN_WIDTH), full(1, ATTN_WIDTH),
                  full(ATTN_WIDTH, ATTN_WIDTH)],
        out_specs=[row(ATTN_WIDTH)] * 3 + [plane] * 3 + [row(w) for w in ret_widths],
        out_shape=([jax.ShapeDtypeStruct((n, ATTN_WIDTH), BF16)] * 3 + [plane_shape] * 3
                   + [jax.ShapeDtypeStruct((n, w), BF16) for w in ret_widths]),
        scratch_shapes=[pltpu.VMEM((ATTN_WIDTH // LANES, tm, LANES), F32),
                        pltpu.VMEM((ATTN_WIDTH // LANES, 4, tm // 4, LANES), F32)],
        compiler_params=_cparams(("parallel",), VMEM_LIMIT),
        name="in_proj",
    )(x2d, norm_w.reshape(1, d), w_in.astype(BF16), qw, kw, gmat)


def _attn_bias(dilation, interleave=1):
    half = ATTN_HALF
    idx = np.arange(half)
    sub = (idx % (half // interleave)) * interleave + idx // (half // interleave)
    qi = sub[:, None]
    kc = (np.arange(3)[:, None] * half + sub[None, :]).reshape(1, -1)
    seg = np.repeat(np.arange(3), half)[None, :]
    delta = kc - half - qi
    band = np.abs(delta) <= half
    slopes = np.exp2(-8.0 * np.arange(1, ATTN_HEADS + 1) / ATTN_HEADS)
    valid = [band & (seg >= 1), band, band & (seg <= 1)]
    out = np.empty((ATTN_HEADS // 2, 3, 2 * half, 3 * half), np.float32)
    for p in range(ATTN_HEADS // 2):
        for v in range(3):
            for s in range(2):
                b = -slopes[2 * p + s] * LOG2_E * (dilation * np.abs(delta)).astype(np.float32)
                out[p, v, s * half:(s + 1) * half] = np.where(valid[v], b, NEG_INF)
    return jnp.asarray(out.reshape(-1, 2 * half, 3 * half))


def _attn_scores(chains):
    lo_lanes = lax.broadcasted_iota(I32, (ATTN_HALF, LANES), 1) < ATTN_HEAD_DIM
    scores = []
    for q, k, v, bias in chains:
        zero = jnp.zeros_like(q)
        qs = jnp.concatenate([jnp.where(lo_lanes, q, zero), jnp.where(lo_lanes, zero, q)], axis=0)
        scores.append(lax.dot_general(qs, k, (((1,), (1,)), ((), ())), preferred_element_type=F32) + bias)
    return scores


def _attn_finish(chains, scores):
    half = ATTN_HALF
    lo_lanes = lax.broadcasted_iota(I32, (half, LANES), 1) < ATTN_HEAD_DIM
    ones = jnp.ones((3 * half, LANES), BF16)
    probs, maxes = [], []
    for s in scores:
        m = jnp.max(s, axis=-1, keepdims=True)
        probs.append(jnp.exp2(s - m).astype(BF16))
        maxes.append(m)
    results = []
    for (q, k, v, bias), e in zip(chains, probs):
        vext = jnp.concatenate([v, ones], axis=1)
        results.append(jnp.dot(e, vext, preferred_element_type=F32))
    outs = []
    for r, m in zip(results, maxes):
        acc = jnp.where(lo_lanes, r[:half, :LANES], r[half:, :LANES])
        denom = jnp.where(lo_lanes, r[:half, LANES:], r[half:, LANES:])
        mm = jnp.where(lo_lanes, m[:half], m[half:])
        outs.append((acc / denom, mm + jnp.log2(denom)))
    return outs


def _attend_groups(groups):
    pending = None
    for chains, consume in groups:
        scores = _attn_scores(chains)
        if pending is not None:
            p_chains, p_scores, p_consume = pending
            p_consume(_attn_finish(p_chains, p_scores))
        pending = (chains, scores, consume)
    p_chains, p_scores, p_consume = pending
    p_consume(_attn_finish(p_chains, p_scores))


def _merge(oa, la, ob, lb):
    m = jnp.maximum(la, lb)
    ea = jnp.exp2(la - m)
    eb = jnp.exp2(lb - m)
    den = ea + eb
    return (ea * oa + eb * ob) / den, m + jnp.log2(den)


def _attn_body(q_ref, kp_ref, km_ref, kn_ref, vp_ref, vm_ref, vn_ref,
               qpl_ref, kplp_ref, kplm_ref, kpln_ref, vplp_ref, vplm_ref, vpln_ref,
               b1_ref, b4_ref, b16_ref, o_ref, kcat, vcat, o_far, l_far):
    half = ATTN_HALF
    t = pl.program_id(1)
    n_tiles = pl.num_programs(1)
    tile = q_ref.shape[1]
    pairs = ATTN_HEADS // 2
    piece = half // 4

    kcat[0:half] = kp_ref[0]
    kcat[half:half + tile] = km_ref[0]
    kcat[half + tile:] = kn_ref[0]
    vcat[0:half] = vp_ref[0]
    vcat[half:half + tile] = vm_ref[0]
    vcat[half + tile:] = vn_ref[0]

    def edge(first, last):
        return jnp.where(first, 0, jnp.where(last, 2, 1))

    groups = []

    def far_groups(r):
        v16 = edge(t == 0, t == n_tiles - 1)
        for pr in range(pairs):
            cs = slice(pr * LANES, (pr + 1) * LANES)
            chains = []
            for m in range(4):
                c = r + 4 * m
                k3 = jnp.concatenate([kplp_ref[0, c, :, cs], kplm_ref[0, c, :, cs], kpln_ref[0, c, :, cs]], axis=0)
                v3 = jnp.concatenate([vplp_ref[0, c, :, cs], vplm_ref[0, c, :, cs], vpln_ref[0, c, :, cs]], axis=0)
                chains.append((qpl_ref[0, c, :, cs], k3, v3, b16_ref[pr * 3 + v16]))
            for nb in range(4):
                def rows(main, before, after, a0):
                    ref, lo = (before, a0 + half) if a0 < 0 else (after, a0 - half) if a0 >= half else (main, a0)
                    return [ref[0, r + 4 * m, lo:lo + piece, cs] for m in range(4)]
                a0 = nb * piece
                q4 = jnp.concatenate(rows(qpl_ref, None, None, a0), axis=0)
                k4 = jnp.concatenate(sum((rows(kplm_ref, kplp_ref, kpln_ref, a0 + d) for d in (-piece, 0, piece)), []),
                                     axis=0)
                v4 = jnp.concatenate(sum((rows(vplm_ref, vplp_ref, vpln_ref, a0 + d) for d in (-piece, 0, piece)), []),
                                     axis=0)
                v4e = edge((t == 0) & (nb == 0), (t == n_tiles - 1) & (nb == 3))
                chains.append((q4, k4, v4, b4_ref[pr * 3 + v4e]))

            def consume(res, r=r, pr=pr):
                for m in range(4):
                    o16, l16 = res[m]
                    o_rows, l_rows = [], []
                    for nb in range(4):
                        o4, l4 = res[4 + nb]
                        sl = slice(nb * piece, (nb + 1) * piece)
                        s4 = slice(m * piece, (m + 1) * piece)
                        om, lm = _merge(o16[sl], l16[sl], o4[s4], l4[s4])
                        o_rows.append(om)
                        l_rows.append(lm)
                    dst = pl.ds(r + 4 * m, half, stride=PLANES)
                    o_far[pr, dst, :] = jnp.concatenate(o_rows, axis=0)
                    l_far[pr, dst, :] = jnp.concatenate(l_rows, axis=0)

            groups.append((chains, consume))

    for r in range(4):
        far_groups(r)

    n_blocks = n_tiles * (tile // half)
    for it in range(tile // half // ATTN_BLOCKS_PER_STEP):
        chains, where = [], []
        for u in range(ATTN_BLOCKS_PER_STEP):
            jb = it * ATTN_BLOCKS_PER_STEP + u
            r0 = jb * half
            gb = t * (tile // half) + jb
            variant = edge(gb == 0, gb == n_blocks - 1)
            for pr in range(pairs):
                cs = slice(pr * LANES, (pr + 1) * LANES)
                chains.append((q_ref[0, pl.ds(r0, half), cs], kcat[pl.ds(r0, 3 * half), cs],
                               vcat[pl.ds(r0, 3 * half), cs], b1_ref[pr * 3 + variant]))
                where.append((r0, pr, cs))

        def consume(res, where=where):
            for (r0, pr, cs), (o1, l1) in zip(where, res):
                o, _ = _merge(o1, l1, o_far[pr, pl.ds(r0, half), :], l_far[pr, pl.ds(r0, half), :])
                o_ref[0, pl.ds(r0, half), cs] = o.astype(o_ref.dtype)

        groups.append((chains, consume))

    _attend_groups(groups)


def _attention(q, k, v, q_pl, k_pl, v_pl, b, s):
    w = ATTN_WIDTH
    half = ATTN_HALF
    tile = ATTN_TILE
    n_tiles = s // tile
    hb = tile // half
    last = s // half - 1
    main = pl.BlockSpec((1, tile, w), lambda bb, t: (bb, t, 0))
    prev = pl.BlockSpec((1, half, w), lambda bb, t: (bb, jnp.maximum(t * hb - 1, 0), 0))
    nxt = pl.BlockSpec((1, half, w), lambda bb, t: (bb, jnp.minimum((t + 1) * hb, last), 0))
    pshape = (1, PLANES, tile // PLANES, w)
    pl_main = pl.BlockSpec(pshape, lambda bb, t: (bb * n_tiles + t, 0, 0, 0))
    pl_prev = pl.BlockSpec(pshape, lambda bb, t: (bb * n_tiles + jnp.maximum(t - 1, 0), 0, 0, 0))
    pl_next = pl.BlockSpec(pshape, lambda bb, t: (bb * n_tiles + jnp.minimum(t + 1, n_tiles - 1), 0, 0, 0))
    biases = [_attn_bias(1), _attn_bias(4, interleave=4), _attn_bias(16)]
    bias_spec = pl.BlockSpec(biases[0].shape, lambda bb, t: (0, 0, 0))
    nat = lambda a: a.reshape(b, s, w)
    out = pl.pallas_call(
        _attn_body,
        grid=(b, n_tiles),
        in_specs=[main, prev, main, nxt, prev, main, nxt,
                  pl_main, pl_prev, pl_main, pl_next, pl_prev, pl_main, pl_next,
                  bias_spec, bias_spec, bias_spec],
        out_specs=main,
        out_shape=jax.ShapeDtypeStruct((b, s, w), BF16),
        scratch_shapes=[pltpu.VMEM((tile + 2 * half, w), BF16), pltpu.VMEM((tile + 2 * half, w), BF16),
                        pltpu.VMEM((ATTN_HEADS // 2, tile, LANES), F32),
                        pltpu.VMEM((ATTN_HEADS // 2, tile, LANES), F32)],
        compiler_params=_cparams(("parallel", "parallel"), VMEM_LIMIT),
        name="attention",
    )(nat(q), nat(k), nat(k), nat(k), nat(v), nat(v), nat(v),
      q_pl, k_pl, k_pl, k_pl, v_pl, v_pl, v_pl, *biases)
    return out.reshape(b * s, w)


def _ret_body(lg_ref, q_ref, k_ref, v_ref, g_ref, gnw_ref, out_ref,
              dmat, qdec, kdec, cdec, fstate, rstate, rall, *, chunk):
    c = chunk
    b = pl.program_id(0)
    ph = pl.program_id(1)
    n = pl.program_id(2)
    n_chunks = pl.num_programs(2)
    k_scale = RET_QK_DIM ** -0.5
    pairs = RET_HEADS // 2
    pw = 2 * RET_QK_DIM
    vw = 2 * RET_V_DIM
    first = lax.broadcasted_iota(I32, (c, pw), 1) < RET_QK_DIM
    diag = ((lax.broadcasted_iota(I32, (pw, vw), 0) < RET_QK_DIM)
            == (lax.broadcasted_iota(I32, (pw, vw), 1) < RET_V_DIM))

    @pl.when((b == 0) & (ph == 0) & (n == 0))
    def _init_tables():
        ii = lax.broadcasted_iota(I32, (c, c), 0)
        jj = lax.broadcasted_iota(I32, (c, c), 1)
        fwd = (ii - jj).astype(F32)
        ri = lax.broadcasted_iota(I32, (c, pw), 0).astype(F32)
        top = lax.broadcasted_iota(I32, (pw, vw), 0) < RET_QK_DIM
        for h in range(RET_HEADS):
            dmat[h] = jnp.where(ii >= jj, jnp.exp(lg_ref[0, h] * fwd), jnp.exp(-lg_ref[1, h] * fwd)) * k_scale
        for pr in range(pairs):
            lf = jnp.where(first, lg_ref[0, 2 * pr], lg_ref[0, 2 * pr + 1])
            lb = jnp.where(first, lg_ref[1, 2 * pr], lg_ref[1, 2 * pr + 1])
            qdec[0, pr] = jnp.exp(lf * (ri + 1.0))
            qdec[1, pr] = jnp.exp(lb * (c - ri))
            kdec[0, pr] = jnp.exp(lf * (c - 1.0 - ri)) * k_scale
            kdec[1, pr] = jnp.exp(lb * ri) * k_scale
            for direction in range(2):
                cdec[direction, pr] = jnp.exp(jnp.where(top, lg_ref[direction, 2 * pr],
                                                        lg_ref[direction, 2 * pr + 1]) * c)

    per_step = q_ref.shape[1] // c

    def kv_update(direction, pr, rows):
        kp = k_ref[0, rows, pr * pw:(pr + 1) * pw].astype(F32)
        ks = (kp * kdec[direction, pr]).astype(BF16)
        new = lax.dot_general(ks, v_ref[0, rows, pr * vw:(pr + 1) * vw], (((0,), (0,)), ((), ())),
                              preferred_element_type=F32)
        return jnp.where(diag, new, 0.0)

    @pl.when(ph == 0)
    def _right_to_left():
        @pl.when(n == 0)
        def _():
            rstate[...] = jnp.zeros_like(rstate)

        for sub in reversed(range(per_step)):
            rows = slice(sub * c, (sub + 1) * c)
            ci = (n_chunks - 1 - n) * per_step + sub
            for pr in range(pairs):
                st = rstate[pr]
                rall[ci, pr] = st
                rstate[pr] = st * cdec[1, pr] + kv_update(1, pr, rows)

    @pl.when(ph == 1)
    def _left_to_right():
        @pl.when(n == 0)
        def _():
            fstate[...] = jnp.zeros_like(fstate)

        for sub in range(per_step):
            rows = slice(sub * c, (sub + 1) * c)
            for pr in range(pairs):
                qp = q_ref[0, rows, pr * pw:(pr + 1) * pw]
                kp = k_ref[0, rows, pr * pw:(pr + 1) * pw]
                qf = qp.astype(F32)
                qcat = jnp.concatenate([(qf * qdec[0, pr]).astype(BF16), (qf * qdec[1, pr]).astype(BF16)], axis=1)
                st = fstate[pr]
                states = jnp.concatenate([st, rall[n * per_step + sub, pr]], axis=0).astype(BF16)
                cross = jnp.dot(qcat, states, preferred_element_type=F32)
                fstate[pr] = st * cdec[0, pr] + kv_update(0, pr, rows)
                zero = jnp.zeros_like(qp)
                for a in range(2):
                    h = 2 * pr + a
                    vs = slice(h * RET_V_DIM, (h + 1) * RET_V_DIM)
                    qm = jnp.where(first, qp, zero) if a == 0 else jnp.where(first, zero, qp)
                    s = lax.dot_general(qm, kp, (((1,), (1,)), ((), ())), preferred_element_type=F32) * dmat[h]
                    y = jnp.dot(s.astype(BF16), v_ref[0, rows, vs], preferred_element_type=F32)
                    y = y + cross[:, a * RET_V_DIM:(a + 1) * RET_V_DIM]
                    mu = jnp.mean(y, axis=-1, keepdims=True)
                    yc = y - mu
                    var = jnp.mean(yc * yc, axis=-1, keepdims=True)
                    yn = yc * lax.rsqrt(var + GN_EPS) * gnw_ref[:, vs]
                    gate = g_ref[0, rows, vs].astype(F32)
                    out_ref[0, rows, vs] = (gate * jax.nn.sigmoid(gate) * yn).astype(out_ref.dtype)


def _retention(rq, rk, rv, rg, decay_f, decay_b, gn_w, b, s):
    c = RET_CHUNK
    rows = RET_CHUNK * RET_CHUNKS_PER_STEP
    nb = s // rows
    lg = jnp.stack([jax.nn.log_sigmoid(decay_f.astype(F32)), jax.nn.log_sigmoid(decay_b.astype(F32))])
    qk_w, v_w = RET_QK_WIDTH, RET_V_WIDTH
    pairs, pair_qk, pair_v = RET_HEADS // 2, 2 * RET_QK_DIM, 2 * RET_V_DIM

    def both(bb, ph, n):
        return (bb, jnp.where(ph == 0, nb - 1 - n, n), 0)

    def fwd_only(bb, ph, n):
        return (bb, jnp.where(ph == 0, 0, n), 0)

    out = pl.pallas_call(
        functools.partial(_ret_body, chunk=c),
        grid=(b, 2, nb),
        in_specs=[pl.BlockSpec(memory_space=pltpu.SMEM),
                  pl.BlockSpec((1, rows, qk_w), fwd_only),
                  pl.BlockSpec((1, rows, qk_w), both),
                  pl.BlockSpec((1, rows, v_w), both),
                  pl.BlockSpec((1, rows, v_w), fwd_only),
                  pl.BlockSpec((1, v_w), lambda bb, ph, n: (0, 0))],
        out_specs=pl.BlockSpec((1, rows, v_w), fwd_only),
        out_shape=jax.ShapeDtypeStruct((b, s, v_w), BF16),
        scratch_shapes=[pltpu.VMEM((RET_HEADS, c, c), F32),
                        pltpu.VMEM((2, pairs, c, pair_qk), F32),
                        pltpu.VMEM((2, pairs, c, pair_qk), F32),
                        pltpu.VMEM((2, pairs, pair_qk, pair_v), F32),
                        pltpu.VMEM((pairs, pair_qk, pair_v), F32),
                        pltpu.VMEM((pairs, pair_qk, pair_v), F32),
                        pltpu.VMEM((s // c, pairs, pair_qk, pair_v), F32)],
        compiler_params=_cparams(("arbitrary", "arbitrary", "arbitrary"), VMEM_LIMIT),
        name="retention",
    )(lg, rq.reshape(b, s, qk_w), rk.reshape(b, s, qk_w), rv.reshape(b, s, v_w), rg.reshape(b, s, v_w),
      gn_w.reshape(1, v_w))
    return out.reshape(b * s, v_w)


def _mem_kv_body(mem_ref, nw_ref, w_ref, kw_ref, k_out, v_out):
    d = mem_ref.shape[-1]
    hd = d // MEM_HEADS
    h = _rms(mem_ref[0], nw_ref[...]).astype(BF16)
    kv = jnp.dot(h, w_ref[...], preferred_element_type=F32)
    for i in range(MEM_HEADS):
        k_out[0, :, i * hd:(i + 1) * hd] = _rms(kv[:, i * hd:(i + 1) * hd], kw_ref[...]).astype(BF16)
    v_out[0] = kv[:, d:].astype(BF16)


def _mem_kv(mem, norm_w, w_mkv, k_norm_w):
    b, m, d = mem.shape
    return pl.pallas_call(
        _mem_kv_body,
        grid=(b,),
        in_specs=[pl.BlockSpec((1, m, d), lambda i: (i, 0, 0)),
                  pl.BlockSpec((1, d), lambda i: (0, 0)),
                  pl.BlockSpec((d, 2 * d), lambda i: (0, 0)),
                  pl.BlockSpec((1, d // MEM_HEADS), lambda i: (0, 0))],
        out_specs=[pl.BlockSpec((1, m, d), lambda i: (i, 0, 0))] * 2,
        out_shape=[jax.ShapeDtypeStruct((b, m, d), BF16)] * 2,
        compiler_params=_cparams(("parallel",), VMEM_LIMIT),
        name="mem_kv",
    )(mem, norm_w.reshape(1, d), w_mkv.astype(BF16), k_norm_w.reshape(1, -1))


def _mem_attn_body(x_ref, attn_ref, ret_ref, wout_ref, nw_ref, wq_ref, qw_ref, k_ref, v_ref, wo_ref,
                   rnw_ref, rw_ref, out_ref, h_out, gate_out, meta_out, cnt_out, carry, below):
    d = x_ref.shape[-1]
    hd = d // MEM_HEADS
    tm = x_ref.shape[1]
    sub = below.shape[0]

    @pl.when((pl.program_id(0) == 0) & (pl.program_id(1) == 0))
    def _():
        carry[...] = jnp.zeros_like(carry)
        rr = lax.broadcasted_iota(I32, (sub, sub), 0)
        cc = lax.broadcasted_iota(I32, (sub, sub), 1)
        below[...] = (cc < rr).astype(F32).astype(BF16)

    x = (x_ref[0] + jnp.dot(attn_ref[0], wout_ref[:ATTN_WIDTH], preferred_element_type=F32)
         + jnp.dot(ret_ref[0], wout_ref[ATTN_WIDTH:], preferred_element_type=F32))
    h = _rms(x, nw_ref[...]).astype(BF16)
    q = jnp.dot(h, wq_ref[...], preferred_element_type=F32)
    heads = []
    for i in range(MEM_HEADS):
        cs = slice(i * hd, (i + 1) * hd)
        qn = _rms(q[:, cs], qw_ref[...]).astype(BF16)
        s = lax.dot_general(qn, k_ref[0, :, cs], (((1,), (1,)), ((), ())), preferred_element_type=F32)
        e = jnp.exp(s - jnp.max(s, axis=-1, keepdims=True))
        o = jnp.dot(e.astype(BF16), v_ref[0, :, cs], preferred_element_type=F32)
        heads.append((o / jnp.sum(e, axis=-1, keepdims=True)).astype(BF16))
    o = jnp.concatenate(heads, axis=1)
    x2 = x + jnp.dot(o, wo_ref[...], preferred_element_type=F32)
    out_ref[0] = x2

    base = carry[...]
    for part in range(tm // sub):
        rows = slice(part * sub, (part + 1) * sub)
        base = _route_rows(x2[rows], rnw_ref, rw_ref, below, base,
                           h_out.at[pl.ds(part * sub * ROW_SLABS, sub * ROW_SLABS)],
                           gate_out.at[rows], meta_out.at[rows])
    carry[...] = base
    cnt_out[...] = base


def _mem_attn_route(x, attn, ret, w_out, mk, mv, norm_w, w_mq, q_norm_w, w_mo, moe_norm_w, w_group, w_router):
    b, s, d = x.shape
    m = mk.shape[1]
    tm = MATMUL_TILE
    steps = s // tm
    hd = d // MEM_HEADS
    qw = (q_norm_w * (hd ** -0.5)).reshape(1, hd)
    w_all = jnp.concatenate([w_group, w_router.transpose(1, 0, 2).reshape(d, N_EXPERTS)], axis=1)
    w_all = jnp.pad(w_all, ((0, 0), (0, LANES - w_all.shape[1])))
    whi = w_all.astype(BF16)
    w_split = jnp.concatenate([whi, (w_all - whi.astype(F32)).astype(BF16)], axis=1)
    tok = lambda w: pl.BlockSpec((1, tm, w), lambda bb, i: (bb, i, 0))
    flat = lambda r, w: pl.BlockSpec((r, w), lambda bb, i: (bb * steps + i, 0))
    const = lambda r, c: pl.BlockSpec((r, c), lambda bb, i: (0, 0))
    mem = pl.BlockSpec((1, m, d), lambda bb, i: (bb, 0, 0))
    n = b * s
    return pl.pallas_call(
        _mem_attn_body,
        grid=(b, steps),
        in_specs=[tok(d), tok(ATTN_WIDTH), tok(RET_V_WIDTH), const(ATTN_WIDTH + RET_V_WIDTH, d),
                  const(1, d), const(d, d), const(1, hd), mem, mem, const(d, d),
                  const(1, d), const(d, 2 * LANES)],
        out_specs=[tok(d), flat(tm * ROW_SLABS, LANES), flat(tm, LANES), flat(tm, LANES), const(1, LANES)],
        out_shape=[jax.ShapeDtypeStruct((b, s, d), F32), jax.ShapeDtypeStruct((n * ROW_SLABS, LANES), F32),
                   jax.ShapeDtypeStruct((n, LANES), F32), jax.ShapeDtypeStruct((n, LANES), I32),
                   jax.ShapeDtypeStruct((1, LANES), F32)],
        scratch_shapes=[pltpu.VMEM((1, LANES), F32), pltpu.VMEM((ROUTER_SUBTILE, ROUTER_SUBTILE), BF16)],
        compiler_params=_cparams(("arbitrary", "arbitrary"), VMEM_LIMIT),
        name="mem_attn_route",
    )(x, attn.reshape(b, s, ATTN_WIDTH), ret.reshape(b, s, RET_V_WIDTH), w_out.astype(BF16),
      norm_w.reshape(1, d), w_mq.astype(BF16), qw, mk, mv, w_mo.astype(BF16),
      moe_norm_w.reshape(1, d), w_split)


def _route_rows(x, nw_ref, w_ref, below, base, h_out, gate_out, meta_out):
    tm = x.shape[0]
    h = _rms(x, nw_ref[...])
    _store_row_tiles(h_out, h)
    hi, lo = _split_hi_lo(h)
    both = jnp.dot(hi, w_ref[...], preferred_element_type=F32)
    logits = (both[:, :LANES] + both[:, LANES:]
              + jnp.dot(lo, w_ref[:, :LANES], preferred_element_type=F32))
    lane = lax.broadcasted_iota(I32, (tm, LANES), 1)
    lane_f = lane.astype(F32)

    def argmax_lanes(vals):
        top = jnp.max(vals, axis=-1, keepdims=True)
        idx = jnp.min(jnp.where(vals == top, lane_f, float(LANES)), axis=-1, keepdims=True)
        return top, idx

    is_group = lane < N_GROUPS
    g_top, grp = argmax_lanes(jnp.where(is_group, logits, -jnp.inf))
    g_gate = 1.0 / jnp.sum(jnp.where(is_group, jnp.exp(logits - g_top), 0.0), axis=-1, keepdims=True)
    first = N_GROUPS + EXPERTS_PER_GROUP * grp
    in_group = (lane_f >= first) & (lane_f < first + EXPERTS_PER_GROUP)
    el = jnp.where(in_group, logits, -jnp.inf)
    t1, i1 = argmax_lanes(el)
    t2, i2 = argmax_lanes(jnp.where(lane_f == i1, -jnp.inf, el))
    z = jnp.exp(t2 - t1)
    g1 = g_gate / (1.0 + z)
    g2 = g_gate * z / (1.0 + z)
    e1 = i1 - N_GROUPS
    e2 = i2 - N_GROUPS

    oh1 = (lane_f == e1)
    oh2 = (lane_f == e2)
    oh1b = oh1.astype(F32).astype(BF16)
    oh2b = oh2.astype(F32).astype(BF16)
    pre = jnp.dot(below[...], jnp.concatenate([oh1b, oh2b], axis=1), preferred_element_type=F32)
    pre1, pre2 = pre[:, :LANES], pre[:, LANES:]
    cnt1 = jnp.sum(oh1.astype(F32), axis=0, keepdims=True)
    cnt2 = jnp.sum(oh2.astype(F32), axis=0, keepdims=True)
    r1 = jnp.sum(jnp.where(oh1, pre1 + base, 0.0), axis=-1, keepdims=True)
    r2 = jnp.sum(jnp.where(oh2, pre2 + base + cnt1, 0.0), axis=-1, keepdims=True)

    gate_out[...] = jnp.where(lane == 0, g1, jnp.where(lane == 1, g2, 0.0))
    meta = jnp.where(lane == 0, e1, jnp.where(lane == 1, e2, jnp.where(lane == 2, r1, jnp.where(lane == 3, r2, 0.0))))
    meta_out[...] = meta.astype(I32)
    return base + cnt1 + cnt2


def _row_copy(src_ref, src_row, dst_ref, dst_row, sem):
    return pltpu.make_async_copy(_tile_rows(src_ref, src_row), _tile_rows(dst_ref, dst_row), sem)


def _slots_body(meta_ref, cnt_ref, dest_ref):
    tm = meta_ref.shape[0]
    blk = float(EXPERT_BLOCK)
    lane = lax.broadcasted_iota(I32, (tm, LANES), 1)
    lane_f = lane.astype(F32)
    blocks = jnp.floor((cnt_ref[...] + (blk - 1.0)) * (1.0 / blk))
    rr = lax.broadcasted_iota(I32, (LANES, LANES), 0)
    cc = lax.broadcasted_iota(I32, (LANES, LANES), 1)
    before = (rr < cc).astype(F32).astype(BF16)
    first_block = jnp.dot(jnp.broadcast_to(blocks, (8, LANES)).astype(BF16), before,
                          preferred_element_type=F32)[0:1]
    start = first_block * blk
    meta = meta_ref[...].astype(F32)

    def slot_of(k):
        e = jnp.sum(jnp.where(lane == k, meta, 0.0), axis=-1, keepdims=True)
        r = jnp.sum(jnp.where(lane == 2 + k, meta, 0.0), axis=-1, keepdims=True)
        return jnp.sum(jnp.where(lane_f == e, start, 0.0), axis=-1, keepdims=True) + r

    dest_ref[...] = jnp.where(lane == 0, slot_of(0), jnp.where(lane == 1, slot_of(1), 0.0)).astype(I32)


def _slots(meta, cnt):
    n = meta.shape[0]
    tm = min(SLOTS_TILE, n)
    row = pl.BlockSpec((tm, LANES), lambda i: (i, 0))
    return pl.pallas_call(
        _slots_body,
        grid=(n // tm,),
        in_specs=[row, pl.BlockSpec((1, LANES), lambda i: (0, 0))],
        out_specs=row,
        out_shape=jax.ShapeDtypeStruct((n, LANES), I32),
        compiler_params=_cparams(("parallel",), VMEM_LIMIT),
        name="slots",
    )(meta, cnt)


def _dispatch_body(cnt_ref, dest_ref, h_ref, xs_ref, zbuf, sems):
    i = pl.program_id(0)
    tt = h_ref.shape[0] // ROW_SLABS
    blk = EXPERT_BLOCK
    n_slots = xs_ref.shape[0] // ROW_SLABS

    def pad_copy(pend):
        return pltpu.make_async_copy(zbuf, _tile_rows(xs_ref, pend - blk, blk), sems.at[2])

    @pl.when(i == 0)
    def _():
        zbuf[...] = jnp.zeros_like(zbuf)

        def start(e, acc):
            pend = acc + ((cnt_ref[e] + (blk - 1)) // blk) * blk

            @pl.when(pend > acc)
            def _():
                pad_copy(pend).start()
            return pend

        used = lax.fori_loop(0, N_EXPERTS, start, jnp.int32(0))

        def tail_start(j, carry):
            pad_copy((j + 1) * blk).start()
            return carry

        lax.fori_loop(used // blk, n_slots // blk, tail_start, 0)

        def finish(e, acc):
            pend = acc + ((cnt_ref[e] + (blk - 1)) // blk) * blk

            @pl.when(pend > acc)
            def _():
                pad_copy(pend).wait()
            return pend

        lax.fori_loop(0, N_EXPERTS, finish, jnp.int32(0))

        def tail_wait(j, carry):
            pad_copy((j + 1) * blk).wait()
            return carry

        lax.fori_loop(used // blk, n_slots // blk, tail_wait, 0)

    def tokens(g, carry):
        t0 = pl.multiple_of(g * ROW_DMA_UNROLL, ROW_DMA_UNROLL)
        for u in range(ROW_DMA_UNROLL):
            for k in range(2):
                _row_copy(h_ref, t0 + u, xs_ref, dest_ref[2 * (t0 + u) + k], sems.at[k]).start(priority=k)
        return carry

    lax.fori_loop(0, tt // ROW_DMA_UNROLL, tokens, 0)
    for k in range(2):
        pltpu.make_async_copy(h_ref, _tile_rows(xs_ref, 0, tt), sems.at[k]).wait()


def _dispatch(counts, dest, h3, n_slots):
    n = h3.shape[0] // ROW_SLABS
    tt = DISPATCH_TILE
    grid_spec = pltpu.PrefetchScalarGridSpec(
        num_scalar_prefetch=1,
        grid=(n // tt,),
        in_specs=[pl.BlockSpec((2 * tt,), lambda i, cnt: (i,), memory_space=pltpu.SMEM),
                  pl.BlockSpec((tt * ROW_SLABS, LANES), lambda i, cnt: (i, 0))],
        out_specs=pl.BlockSpec(memory_space=pl.ANY),
        scratch_shapes=[pltpu.VMEM((EXPERT_BLOCK * ROW_SLABS, LANES), F32), pltpu.SemaphoreType.DMA((3,))],
    )
    return pl.pallas_call(
        _dispatch_body,
        grid_spec=grid_spec,
        out_shape=jax.ShapeDtypeStruct((n_slots * ROW_SLABS, LANES), F32),
        compiler_params=_cparams(("arbitrary",), VMEM_LIMIT),
        name="dispatch",
    )(counts, dest, h3)


def _experts_body(be_ref, nu_ref, xs_ref, w1_ref, w3_ref, w2_ref, ys_ref, w1b, w3b, w2b):
    i = pl.program_id(0)
    n_used = nu_ref[0]

    @pl.when(i < n_used)
    def _():
        changed = (i == 0) | (be_ref[i] != be_ref[jnp.maximum(i - 1, 0)])

        @pl.when(changed)
        def _():
            w1b[...] = w1_ref[0].astype(BF16)
            w3b[...] = w3_ref[0].astype(BF16)
            w2b[...] = w2_ref[0].astype(BF16)

        x = _load_row_tiles(xs_ref).astype(BF16)
        chunks = [slice(c0, c0 + EXPERT_FF_CHUNK) for c0 in range(0, w1b.shape[1], EXPERT_FF_CHUNK)]
        gates = [(jnp.dot(x, w1b[:, cs], preferred_element_type=F32),
                  jnp.dot(x, w3b[:, cs], preferred_element_type=F32)) for cs in chunks]
        mid = jnp.concatenate([(a * jax.nn.sigmoid(a) * u).astype(BF16) for a, u in gates], axis=1)
        rows = mid.shape[0]
        for j0 in range(0, ROW_SLABS, EXPERT_OUT_SLABS):
            cols = slice(j0 * LANES, (j0 + EXPERT_OUT_SLABS) * LANES)
            y = jnp.dot(mid, w2b[:, cols], preferred_element_type=F32)
            for j in range(EXPERT_OUT_SLABS):
                ys_ref[pl.ds(j0 + j, rows, stride=ROW_SLABS), :] = y[:, j * LANES:(j + 1) * LANES]

    @pl.when(i >= n_used)
    def _():
        ys_ref[...] = jnp.zeros_like(ys_ref)


def _experts(block_expert, n_used, xs, w1, w3, w2):
    _, d, ff = w1.shape
    assert d == ROW_SLABS * LANES
    blk = EXPERT_BLOCK
    n_blocks = xs.shape[0] // (blk * ROW_SLABS)

    def slot_map(i, be, nu):
        return (jnp.minimum(i, nu[0] - 1), 0)

    def w_map(i, be, nu):
        return (be[jnp.minimum(i, nu[0] - 1)], 0, 0)

    grid_spec = pltpu.PrefetchScalarGridSpec(
        num_scalar_prefetch=2,
        grid=(n_blocks,),
        in_specs=[pl.BlockSpec((blk * ROW_SLABS, LANES), slot_map),
                  pl.BlockSpec((1, d, ff), w_map), pl.BlockSpec((1, d, ff), w_map),
                  pl.BlockSpec((1, ff, d), w_map)],
        out_specs=pl.BlockSpec((blk * ROW_SLABS, LANES), lambda i, be, nu: (i, 0)),
        scratch_shapes=[pltpu.VMEM((d, ff), BF16), pltpu.VMEM((d, ff), BF16), pltpu.VMEM((ff, d), BF16)],
    )
    return pl.pallas_call(
        _experts_body,
        grid_spec=grid_spec,
        out_shape=jax.ShapeDtypeStruct(xs.shape, F32),
        compiler_params=_cparams(("arbitrary",), VMEM_LIMIT),
        name="experts",
    )(block_expert, n_used, xs, w1, w3, w2)


def _combine_body(dest_ref, dest_next_ref, gate_ref, x_ref, ys_ref, out_ref, buf, sems):
    i = pl.program_id(0)
    n_steps = pl.num_programs(0)
    tt = x_ref.shape[0]

    def issue(dref, slot):
        def tokens(g, carry):
            t0 = pl.multiple_of(g * ROW_DMA_UNROLL, ROW_DMA_UNROLL)
            for u in range(ROW_DMA_UNROLL):
                for k in range(2):
                    _row_copy(ys_ref, dref[2 * (t0 + u) + k], buf.at[slot, k], t0 + u,
                              sems.at[slot, k]).start(priority=k)
            return carry

        lax.fori_loop(0, tt // ROW_DMA_UNROLL, tokens, 0)

    @pl.when(i == 0)
    def _():
        issue(dest_ref, 0)

    @pl.when(i + 1 < n_steps)
    def _():
        issue(dest_next_ref, (i + 1) % 2)

    slot = i % 2
    for k in range(2):
        pltpu.make_async_copy(_tile_rows(ys_ref, 0, tt), buf.at[slot, k], sems.at[slot, k]).wait()
    g = gate_ref[...]
    out_ref[...] = (x_ref[...] + g[:, 0:1] * _load_row_tiles(buf.at[slot, 0])
                    + g[:, 1:2] * _load_row_tiles(buf.at[slot, 1]))


def _combine(dest, gates, x2d, ys):
    n, d = x2d.shape
    tt = COMBINE_TILE
    row = lambda w: pl.BlockSpec((tt, w), lambda i: (i, 0))
    last = n // tt - 1
    return pl.pallas_call(
        _combine_body,
        grid=(n // tt,),
        in_specs=[pl.BlockSpec((2 * tt,), lambda i: (i,), memory_space=pltpu.SMEM),
                  pl.BlockSpec((2 * tt,), lambda i: (jnp.minimum(i + 1, last),), memory_space=pltpu.SMEM),
                  row(LANES), row(d), pl.BlockSpec(memory_space=pl.ANY)],
        out_specs=row(d),
        out_shape=jax.ShapeDtypeStruct((n, d), F32),
        scratch_shapes=[pltpu.VMEM((2, 2, tt * ROW_SLABS, LANES), F32), pltpu.SemaphoreType.DMA((2, 2))],
        compiler_params=_cparams(("arbitrary",), VMEM_LIMIT),
        name="combine",
    )(dest, dest, gates, x2d, ys)


def _moe(x2d, h3, gates, meta, cnt, w1, w3, w2):
    n, d = x2d.shape
    blk = EXPERT_BLOCK
    counts = cnt[0, :N_EXPERTS].astype(I32)
    dest = _slots(meta, cnt)[:, 0:2].reshape(-1)
    n_slots = 2 * n + N_EXPERTS * blk
    n_blocks = n_slots // blk
    pends = jnp.cumsum((counts + blk - 1) // blk * blk)
    block_start = jnp.arange(n_blocks, dtype=I32) * blk
    block_expert = jnp.minimum(jnp.sum((pends[None, :] <= block_start[:, None]).astype(I32), axis=1),
                               N_EXPERTS - 1)
    n_used = (pends[-1:] // blk).astype(I32)
    xs = _dispatch(counts, dest, h3, n_slots)
    ys = _experts(block_expert, n_used, xs, w1, w3, w2)
    return _combine(dest, gates, x2d, ys)


def _layer(x, mem, norm_mix_w, w_in, attn_q_norm_w, attn_k_norm_w, ret_decay_f, ret_decay_b, ret_gn_w, w_out,
           norm_mem_w, norm_memkv_w, w_mq, w_mkv, mem_q_norm_w, mem_k_norm_w, w_mo,
           norm_moe_w, w_group, w_router, w_exp_gate, w_exp_up, w_exp_down):
    b, s, d = x.shape
    x2d = x.reshape(b * s, d)
    q, k, v, q_pl, k_pl, v_pl, rq, rk, rv, rg = _in_proj(x2d, norm_mix_w, w_in, attn_q_norm_w, attn_k_norm_w)
    attn = _attention(q, k, v, q_pl, k_pl, v_pl, b, s)
    ret = _retention(rq, rk, rv, rg, ret_decay_f, ret_decay_b, ret_gn_w, b, s)
    mk, mv = _mem_kv(mem, norm_memkv_w, w_mkv, mem_k_norm_w)
    x2, h3, gates, meta, cnt = _mem_attn_route(x, attn, ret, w_out, mk, mv, norm_mem_w, w_mq, mem_q_norm_w, w_mo,
                                               norm_moe_w, w_group, w_router)
    x3 = _moe(x2.reshape(b * s, d), h3, gates, meta, cnt, w_exp_gate, w_exp_up, w_exp_down)
    return x3.reshape(b, s, d)


def kernel(x, mem, norm_mix_w, w_in, attn_q_norm_w, attn_k_norm_w, ret_decay_f, ret_decay_b, ret_gn_w, w_out,
           norm_mem_w, norm_memkv_w, w_mq, w_mkv, mem_q_norm_w, mem_k_norm_w, w_mo, norm_moe_w, w_group,
           w_router, w_exp_gate, w_exp_up, w_exp_down):
    depth = norm_mix_w.shape[0]
    for l in range(depth):
        x = _layer(x, mem, norm_mix_w[l], w_in[l], attn_q_norm_w[l], attn_k_norm_w[l], ret_decay_f[l],
                   ret_decay_b[l], ret_gn_w[l], w_out[l], norm_mem_w[l], norm_memkv_w[l], w_mq[l], w_mkv[l],
                   mem_q_norm_w[l], mem_k_norm_w[l], w_mo[l], norm_moe_w[l], w_group[l], w_router[l],
                   w_exp_gate[l], w_exp_up[l], w_exp_down[l])
    return x
```

```python
import functools

import numpy as np
import jax
import jax.numpy as jnp
from jax import lax
from jax.experimental import pallas as pl
from jax.experimental.pallas import tpu as pltpu

F32 = jnp.float32
BF16 = jnp.bfloat16
I32 = jnp.int32

NORM_EPS = 1e-6
GN_EPS = 1e-5
NEG_INF = -1e30

ATTN_HEADS = 8
ATTN_HEAD_DIM = 64
ATTN_WIDTH = ATTN_HEADS * ATTN_HEAD_DIM
ATTN_HALF = 64
PLANES = 16
ATTN_TILE = PLANES * ATTN_HALF
RET_HEADS = 4
RET_QK_DIM = 64
RET_V_DIM = 128
RET_QK_WIDTH = RET_HEADS * RET_QK_DIM
RET_V_WIDTH = RET_HEADS * RET_V_DIM
MEM_HEADS = 4
N_GROUPS = 4
EXPERTS_PER_GROUP = 8
N_EXPERTS = N_GROUPS * EXPERTS_PER_GROUP

LANES = 128
ROUTER_SUBTILE = 512
MATMUL_TILE = 1024
ATTN_BLOCKS_PER_STEP = 4
LOG2_E = 1.4426950408889634
RET_CHUNK = 256
RET_CHUNKS_PER_STEP = 8
EXPERT_BLOCK = 512
EXPERT_FF_CHUNK = 256
EXPERT_OUT_SLABS = 2
DISPATCH_TILE = 2048
COMBINE_TILE = 256
ROW_DMA_UNROLL = 16
SLOTS_TILE = 4096
VMEM_LIMIT = 56 * 1024 * 1024


def _cparams(sem, vmem=None):
    return pltpu.CompilerParams(dimension_semantics=sem, vmem_limit_bytes=vmem)


def _split_hi_lo(x):
    hi = x.astype(BF16)
    lo = (x - hi.astype(F32)).astype(BF16)
    return hi, lo


def _rms(x, w):
    ms = jnp.mean(x * x, axis=-1, keepdims=True)
    return x * lax.rsqrt(ms + NORM_EPS) * w


ROW_SLABS = 8


def _store_row_tiles(ref, val):
    rows = val.shape[0]
    for j in range(ROW_SLABS):
        ref[pl.ds(j, rows, stride=ROW_SLABS), :] = val[:, j * LANES:(j + 1) * LANES]


def _load_row_tiles(ref):
    rows = ref.shape[0] // ROW_SLABS
    return jnp.concatenate([ref[pl.ds(j, rows, stride=ROW_SLABS), :] for j in range(ROW_SLABS)], axis=1)


def _tile_rows(ref, row, count=1):
    return ref.at[pl.ds(pl.multiple_of(row * ROW_SLABS, ROW_SLABS), count * ROW_SLABS)]


def _in_proj_body(x_ref, nw_ref, w_ref, qw_ref, kw_ref, g_ref,
                  q_out, k_out, v_out, qp_out, kp_out, vp_out, rq_out, rk_out, rv_out, rg_out, slabs, quarters):
    tm = x_ref.shape[0]
    h = _rms(x_ref[...], nw_ref[...]).astype(BF16)

    def emit(val, nat_out, plane_out):
        nat_out[...] = val.astype(BF16)
        for j in range(ATTN_WIDTH // LANES):
            slabs[j] = val[:, j * LANES:(j + 1) * LANES]
        for j in range(ATTN_WIDTH // LANES):
            for r in range(4):
                quarters[j, r] = slabs[j, pl.ds(r, tm // 4, stride=4), :]
        for r in range(4):
            for q in range(4):
                for j in range(ATTN_WIDTH // LANES):
                    rows = quarters[j, r, pl.ds(q, tm // PLANES, stride=4), :]
                    plane_out[0, r + 4 * q, :, j * LANES:(j + 1) * LANES] = rows.astype(BF16)

    def proj(a, b):
        return jnp.dot(h, w_ref[:, a:b], preferred_element_type=F32)

    def head_norm(p, w):
        s = jnp.dot((p * p).astype(BF16), g_ref[...], preferred_element_type=F32)
        return p * lax.rsqrt(s * (1.0 / ATTN_HEAD_DIM) + NORM_EPS) * w

    a = ATTN_WIDTH
    emit(head_norm(proj(0, a), qw_ref[...]), q_out, qp_out)
    emit(head_norm(proj(a, 2 * a), kw_ref[...]), k_out, kp_out)
    emit(proj(2 * a, 3 * a), v_out, vp_out)
    c = 3 * a
    rq_out[...] = proj(c, c + RET_QK_WIDTH).astype(BF16)
    rk_out[...] = proj(c + RET_QK_WIDTH, c + 2 * RET_QK_WIDTH).astype(BF16)
    c += 2 * RET_QK_WIDTH
    rv_out[...] = proj(c, c + RET_V_WIDTH).astype(BF16)
    rg_out[...] = proj(c + RET_V_WIDTH, c + 2 * RET_V_WIDTH).astype(BF16)


def _in_proj(x2d, norm_w, w_in, q_norm_w, k_norm_w):
    n, d = x2d.shape
    cols = w_in.shape[1]
    tm = MATMUL_TILE
    qw = (jnp.tile(q_norm_w, ATTN_HEADS) * (ATTN_HEAD_DIM ** -0.5 * LOG2_E)).reshape(1, ATTN_WIDTH)
    kw = jnp.tile(k_norm_w, ATTN_HEADS).reshape(1, ATTN_WIDTH)
    head_of = np.arange(ATTN_WIDTH) // ATTN_HEAD_DIM
    gmat = jnp.asarray(head_of[:, None] == head_of[None, :], dtype=BF16)
    row = lambda w: pl.BlockSpec((tm, w), lambda i: (i, 0))
    full = lambda r, c: pl.BlockSpec((r, c), lambda i: (0, 0))
    per_tile = ATTN_TILE // tm
    plane = pl.BlockSpec((1, PLANES, tm // PLANES, ATTN_WIDTH), lambda i: (i // per_tile, 0, i % per_tile, 0))
    plane_shape = jax.ShapeDtypeStruct((n // ATTN_TILE, PLANES, ATTN_TILE // PLANES, ATTN_WIDTH), BF16)
    ret_widths = (RET_QK_WIDTH, RET_QK_WIDTH, RET_V_WIDTH, RET_V_WIDTH)
    return pl.pallas_call(
        _in_proj_body,
        grid=(n // tm,),
        in_specs=[row(d), full(1, d), full(d, cols), full(1, ATTN_WIDTH), full(1, ATTN_WIDTH),
                  full(ATTN_WIDTH, ATTN_WIDTH)],
        out_specs=[row(ATTN_WIDTH)] * 3 + [plane] * 3 + [row(w) for w in ret_widths],
        out_shape=([jax.ShapeDtypeStruct((n, ATTN_WIDTH), BF16)] * 3 + [plane_shape] * 3
                   + [jax.ShapeDtypeStruct((n, w), BF16) for w in ret_widths]),
        scratch_shapes=[pltpu.VMEM((ATTN_WIDTH // LANES, tm, LANES), F32),
                        pltpu.VMEM((ATTN_WIDTH // LANES, 4, tm // 4, LANES), F32)],
        compiler_params=_cparams(("parallel",), VMEM_LIMIT),
        name="in_proj",
    )(x2d, norm_w.reshape(1, d), w_in.astype(BF16), qw, kw, gmat)


def _attn_bias(dilation, interleave=1):
    half = ATTN_HALF
    idx = np.arange(half)
    sub = (idx % (half // interleave)) * interleave + idx // (half // interleave)
    qi = sub[:, None]
    kc = (np.arange(3)[:, None] * half + sub[None, :]).reshape(1, -1)
    seg = np.repeat(np.arange(3), half)[None, :]
    delta = kc - half - qi
    band = np.abs(delta) <= half
    slopes = np.exp2(-8.0 * np.arange(1, ATTN_HEADS + 1) / ATTN_HEADS)
    valid = [band & (seg >= 1), band, band & (seg <= 1)]
    out = np.empty((ATTN_HEADS // 2, 3, 2 * half, 3 * half), np.float32)
    for p in range(ATTN_HEADS // 2):
        for v in range(3):
            for s in range(2):
                b = -slopes[2 * p + s] * LOG2_E * (dilation * np.abs(delta)).astype(np.float32)
                out[p, v, s * half:(s + 1) * half] = np.where(valid[v], b, NEG_INF)
    return jnp.asarray(out.reshape(-1, 2 * half, 3 * half))


def _attn_scores(chains):
    lo_lanes = lax.broadcasted_iota(I32, (ATTN_HALF, LANES), 1) < ATTN_HEAD_DIM
    scores = []
    for q, k, v, bias in chains:
        zero = jnp.zeros_like(q)
        qs = jnp.concatenate([jnp.where(lo_lanes, q, zero), jnp.where(lo_lanes, zero, q)], axis=0)
        scores.append(lax.dot_general(qs, k, (((1,), (1,)), ((), ())), preferred_element_type=F32) + bias)
    return scores


def _attn_finish(chains, scores):
    half = ATTN_HALF
    lo_lanes = lax.broadcasted_iota(I32, (half, LANES), 1) < ATTN_HEAD_DIM
    ones = jnp.ones((3 * half, LANES), BF16)
    probs, maxes = [], []
    for s in scores:
        m = jnp.max(s, axis=-1, keepdims=True)
        probs.append(jnp.exp2(s - m).astype(BF16))
        maxes.append(m)
    results = []
    for (q, k, v, bias), e in zip(chains, probs):
        vext = jnp.concatenate([v, ones], axis=1)
        results.append(jnp.dot(e, vext, preferred_element_type=F32))
    outs = []
    for r, m in zip(results, maxes):
        acc = jnp.where(lo_lanes, r[:half, :LANES], r[half:, :LANES])
        denom = jnp.where(lo_lanes, r[:half, LANES:], r[half:, LANES:])
        mm = jnp.where(lo_lanes, m[:half], m[half:])
        outs.append((acc / denom, mm + jnp.log2(denom)))
    return outs


def _attend_groups(groups):
    pending = None
    for chains, consume in groups:
        scores = _attn_scores(chains)
        if pending is not None:
            p_chains, p_scores, p_consume = pending
            p_consume(_attn_finish(p_chains, p_scores))
        pending = (chains, scores, consume)
    p_chains, p_scores, p_consume = pending
    p_consume(_attn_finish(p_chains, p_scores))


def _merge(oa, la, ob, lb):
    m = jnp.maximum(la, lb)
    ea = jnp.exp2(la - m)
    eb = jnp.exp2(lb - m)
    den = ea + eb
    return (ea * oa + eb * ob) / den, m + jnp.log2(den)


def _attn_body(q_ref, kp_ref, km_ref, kn_ref, vp_ref, vm_ref, vn_ref,
               qpl_ref, kplp_ref, kplm_ref, kpln_ref, vplp_ref, vplm_ref, vpln_ref,
               b1_ref, b4_ref, b16_ref, o_ref, kcat, vcat, o_far, l_far):
    half = ATTN_HALF
    t = pl.program_id(1)
    n_tiles = pl.num_programs(1)
    tile = q_ref.shape[1]
    pairs = ATTN_HEADS // 2
    piece = half // 4

    kcat[0:half] = kp_ref[0]
    kcat[half:half + tile] = km_ref[0]
    kcat[half + tile:] = kn_ref[0]
    vcat[0:half] = vp_ref[0]
    vcat[half:half + tile] = vm_ref[0]
    vcat[half + tile:] = vn_ref[0]

    def edge(first, last):
        return jnp.where(first, 0, jnp.where(last, 2, 1))

    groups = []

    def far_groups(r):
        v16 = edge(t == 0, t == n_tiles - 1)
        for pr in range(pairs):
            cs = slice(pr * LANES, (pr + 1) * LANES)
            chains = []
            for m in range(4):
                c = r + 4 * m
                k3 = jnp.concatenate([kplp_ref[0, c, :, cs], kplm_ref[0, c, :, cs], kpln_ref[0, c, :, cs]], axis=0)
                v3 = jnp.concatenate([vplp_ref[0, c, :, cs], vplm_ref[0, c, :, cs], vpln_ref[0, c, :, cs]], axis=0)
                chains.append((qpl_ref[0, c, :, cs], k3, v3, b16_ref[pr * 3 + v16]))
            for nb in range(4):
                def rows(main, before, after, a0):
                    ref, lo = (before, a0 + half) if a0 < 0 else (after, a0 - half) if a0 >= half else (main, a0)
                    return [ref[0, r + 4 * m, lo:lo + piece, cs] for m in range(4)]
                a0 = nb * piece
                q4 = jnp.concatenate(rows(qpl_ref, None, None, a0), axis=0)
                k4 = jnp.concatenate(sum((rows(kplm_ref, kplp_ref, kpln_ref, a0 + d) for d in (-piece, 0, piece)), []),
                                     axis=0)
                v4 = jnp.concatenate(sum((rows(vplm_ref, vplp_ref, vpln_ref, a0 + d) for d in (-piece, 0, piece)), []),
                                     axis=0)
                v4e = edge((t == 0) & (nb == 0), (t == n_tiles - 1) & (nb == 3))
                chains.append((q4, k4, v4, b4_ref[pr * 3 + v4e]))

            def consume(res, r=r, pr=pr):
                for m in range(4):
                    o16, l16 = res[m]
                    o_rows, l_rows = [], []
                    for nb in range(4):
                        o4, l4 = res[4 + nb]
                        sl = slice(nb * piece, (nb + 1) * piece)
                        s4 = slice(m * piece, (m + 1) * piece)
                        om, lm = _merge(o16[sl], l16[sl], o4[s4], l4[s4])
                        o_rows.append(om)
                        l_rows.append(lm)
                    dst = pl.ds(r + 4 * m, half, stride=PLANES)
                    o_far[pr, dst, :] = jnp.concatenate(o_rows, axis=0)
                    l_far[pr, dst, :] = jnp.concatenate(l_rows, axis=0)

            groups.append((chains, consume))

    for r in range(4):
        far_groups(r)

    n_blocks = n_tiles * (tile // half)
    for it in range(tile // half // ATTN_BLOCKS_PER_STEP):
        chains, where = [], []
        for u in range(ATTN_BLOCKS_PER_STEP):
            jb = it * ATTN_BLOCKS_PER_STEP + u
            r0 = jb * half
            gb = t * (tile // half) + jb
            variant = edge(gb == 0, gb == n_blocks - 1)
            for pr in range(pairs):
                cs = slice(pr * LANES, (pr + 1) * LANES)
                chains.append((q_ref[0, pl.ds(r0, half), cs], kcat[pl.ds(r0, 3 * half), cs],
                               vcat[pl.ds(r0, 3 * half), cs], b1_ref[pr * 3 + variant]))
                where.append((r0, pr, cs))

        def consume(res, where=where):
            for (r0, pr, cs), (o1, l1) in zip(where, res):
                o, _ = _merge(o1, l1, o_far[pr, pl.ds(r0, half), :], l_far[pr, pl.ds(r0, half), :])
                o_ref[0, pl.ds(r0, half), cs] = o.astype(o_ref.dtype)

        groups.append((chains, consume))

    _attend_groups(groups)


def _attention(q, k, v, q_pl, k_pl, v_pl, b, s):
    w = ATTN_WIDTH
    half = ATTN_HALF
    tile = ATTN_TILE
    n_tiles = s // tile
    hb = tile // half
    last = s // half - 1
    main = pl.BlockSpec((1, tile, w), lambda bb, t: (bb, t, 0))
    prev = pl.BlockSpec((1, half, w), lambda bb, t: (bb, jnp.maximum(t * hb - 1, 0), 0))
    nxt = pl.BlockSpec((1, half, w), lambda bb, t: (bb, jnp.minimum((t + 1) * hb, last), 0))
    pshape = (1, PLANES, tile // PLANES, w)
    pl_main = pl.BlockSpec(pshape, lambda bb, t: (bb * n_tiles + t, 0, 0, 0))
    pl_prev = pl.BlockSpec(pshape, lambda bb, t: (bb * n_tiles + jnp.maximum(t - 1, 0), 0, 0, 0))
    pl_next = pl.BlockSpec(pshape, lambda bb, t: (bb * n_tiles + jnp.minimum(t + 1, n_tiles - 1), 0, 0, 0))
    biases = [_attn_bias(1), _attn_bias(4, interleave=4), _attn_bias(16)]
    bias_spec = pl.BlockSpec(biases[0].shape, lambda bb, t: (0, 0, 0))
    nat = lambda a: a.reshape(b, s, w)
    out = pl.pallas_call(
        _attn_body,
        grid=(b, n_tiles),
        in_specs=[main, prev, main, nxt, prev, main, nxt,
                  pl_main, pl_prev, pl_main, pl_next, pl_prev, pl_main, pl_next,
                  bias_spec, bias_spec, bias_spec],
        out_specs=main,
        out_shape=jax.ShapeDtypeStruct((b, s, w), BF16),
        scratch_shapes=[pltpu.VMEM((tile + 2 * half, w), BF16), pltpu.VMEM((tile + 2 * half, w), BF16),
                        pltpu.VMEM((ATTN_HEADS // 2, tile, LANES), F32),
                        pltpu.VMEM((ATTN_HEADS // 2, tile, LANES), F32)],
        compiler_params=_cparams(("parallel", "parallel"), VMEM_LIMIT),
        name="attention",
    )(nat(q), nat(k), nat(k), nat(k), nat(v), nat(v), nat(v),
      q_pl, k_pl, k_pl, k_pl, v_pl, v_pl, v_pl, *biases)
    return out.reshape(b * s, w)


def _ret_body(lg_ref, q_ref, k_ref, v_ref, g_ref, gnw_ref, out_ref,
              dmat, qdec, kdec, cdec, fstate, rstate, rall, *, chunk):
    c = chunk
    b = pl.program_id(0)
    ph = pl.program_id(1)
    n = pl.program_id(2)
    n_chunks = pl.num_programs(2)
    k_scale = RET_QK_DIM ** -0.5
    pairs = RET_HEADS // 2
    pw = 2 * RET_QK_DIM
    vw = 2 * RET_V_DIM
    first = lax.broadcasted_iota(I32, (c, pw), 1) < RET_QK_DIM
    diag = ((lax.broadcasted_iota(I32, (pw, vw), 0) < RET_QK_DIM)
            == (lax.broadcasted_iota(I32, (pw, vw), 1) < RET_V_DIM))

    @pl.when((b == 0) & (ph == 0) & (n == 0))
    def _init_tables():
        ii = lax.broadcasted_iota(I32, (c, c), 0)
        jj = lax.broadcasted_iota(I32, (c, c), 1)
        fwd = (ii - jj).astype(F32)
        ri = lax.broadcasted_iota(I32, (c, pw), 0).astype(F32)
        top = lax.broadcasted_iota(I32, (pw, vw), 0) < RET_QK_DIM
        for h in range(RET_HEADS):
            dmat[h] = jnp.where(ii >= jj, jnp.exp(lg_ref[0, h] * fwd), jnp.exp(-lg_ref[1, h] * fwd)) * k_scale
        for pr in range(pairs):
            lf = jnp.where(first, lg_ref[0, 2 * pr], lg_ref[0, 2 * pr + 1])
            lb = jnp.where(first, lg_ref[1, 2 * pr], lg_ref[1, 2 * pr + 1])
            qdec[0, pr] = jnp.exp(lf * (ri + 1.0))
            qdec[1, pr] = jnp.exp(lb * (c - ri))
            kdec[0, pr] = jnp.exp(lf * (c - 1.0 - ri)) * k_scale
            kdec[1, pr] = jnp.exp(lb * ri) * k_scale
            for direction in range(2):
                cdec[direction, pr] = jnp.exp(jnp.where(top, lg_ref[direction, 2 * pr],
                                                        lg_ref[direction, 2 * pr + 1]) * c)

    per_step = q_ref.shape[1] // c

    def kv_update(direction, pr, rows):
        kp = k_ref[0, rows, pr * pw:(pr + 1) * pw].astype(F32)
        ks = (kp * kdec[direction, pr]).astype(BF16)
        new = lax.dot_general(ks, v_ref[0, rows, pr * vw:(pr + 1) * vw], (((0,), (0,)), ((), ())),
                              preferred_element_type=F32)
        return jnp.where(diag, new, 0.0)

    @pl.when(ph == 0)
    def _right_to_left():
        @pl.when(n == 0)
        def _():
            rstate[...] = jnp.zeros_like(rstate)

        for sub in reversed(range(per_step)):
            rows = slice(sub * c, (sub + 1) * c)
            ci = (n_chunks - 1 - n) * per_step + sub
            for pr in range(pairs):
                st = rstate[pr]
                rall[ci, pr] = st
                rstate[pr] = st * cdec[1, pr] + kv_update(1, pr, rows)

    @pl.when(ph == 1)
    def _left_to_right():
        @pl.when(n == 0)
        def _():
            fstate[...] = jnp.zeros_like(fstate)

        for sub in range(per_step):
            rows = slice(sub * c, (sub + 1) * c)
            for pr in range(pairs):
                qp = q_ref[0, rows, pr * pw:(pr + 1) * pw]
                kp = k_ref[0, rows, pr * pw:(pr + 1) * pw]
                qf = qp.astype(F32)
                qcat = jnp.concatenate([(qf * qdec[0, pr]).astype(BF16), (qf * qdec[1, pr]).astype(BF16)], axis=1)
                st = fstate[pr]
                states = jnp.concatenate([st, rall[n * per_step + sub, pr]], axis=0).astype(BF16)
                cross = jnp.dot(qcat, states, preferred_element_type=F32)
                fstate[pr] = st * cdec[0, pr] + kv_update(0, pr, rows)
                zero = jnp.zeros_like(qp)
                for a in range(2):
                    h = 2 * pr + a
                    vs = slice(h * RET_V_DIM, (h + 1) * RET_V_DIM)
                    qm = jnp.where(first, qp, zero) if a == 0 else jnp.where(first, zero, qp)
                    s = lax.dot_general(qm, kp, (((1,), (1,)), ((), ())), preferred_element_type=F32) * dmat[h]
                    y = jnp.dot(s.astype(BF16), v_ref[0, rows, vs], preferred_element_type=F32)
                    y = y + cross[:, a * RET_V_DIM:(a + 1) * RET_V_DIM]
                    mu = jnp.mean(y, axis=-1, keepdims=True)
                    yc = y - mu
                    var = jnp.mean(yc * yc, axis=-1, keepdims=True)
                    yn = yc * lax.rsqrt(var + GN_EPS) * gnw_ref[:, vs]
                    gate = g_ref[0, rows, vs].astype(F32)
                    out_ref[0, rows, vs] = (gate * jax.nn.sigmoid(gate) * yn).astype(out_ref.dtype)


def _retention(rq, rk, rv, rg, decay_f, decay_b, gn_w, b, s):
    c = RET_CHUNK
    rows = RET_CHUNK * RET_CHUNKS_PER_STEP
    nb = s // rows
    lg = jnp.stack([jax.nn.log_sigmoid(decay_f.astype(F32)), jax.nn.log_sigmoid(decay_b.astype(F32))])
    qk_w, v_w = RET_QK_WIDTH, RET_V_WIDTH
    pairs, pair_qk, pair_v = RET_HEADS // 2, 2 * RET_QK_DIM, 2 * RET_V_DIM

    def both(bb, ph, n):
        return (bb, jnp.where(ph == 0, nb - 1 - n, n), 0)

    def fwd_only(bb, ph, n):
        return (bb, jnp.where(ph == 0, 0, n), 0)

    out = pl.pallas_call(
        functools.partial(_ret_body, chunk=c),
        grid=(b, 2, nb),
        in_specs=[pl.BlockSpec(memory_space=pltpu.SMEM),
                  pl.BlockSpec((1, rows, qk_w), fwd_only),
                  pl.BlockSpec((1, rows, qk_w), both),
                  pl.BlockSpec((1, rows, v_w), both),
                  pl.BlockSpec((1, rows, v_w), fwd_only),
                  pl.BlockSpec((1, v_w), lambda bb, ph, n: (0, 0))],
        out_specs=pl.BlockSpec((1, rows, v_w), fwd_only),
        out_shape=jax.ShapeDtypeStruct((b, s, v_w), BF16),
        scratch_shapes=[pltpu.VMEM((RET_HEADS, c, c), F32),
                        pltpu.VMEM((2, pairs, c, pair_qk), F32),
                        pltpu.VMEM((2, pairs, c, pair_qk), F32),
                        pltpu.VMEM((2, pairs, pair_qk, pair_v), F32),
                        pltpu.VMEM((pairs, pair_qk, pair_v), F32),
                        pltpu.VMEM((pairs, pair_qk, pair_v), F32),
                        pltpu.VMEM((s // c, pairs, pair_qk, pair_v), F32)],
        compiler_params=_cparams(("arbitrary", "arbitrary", "arbitrary"), VMEM_LIMIT),
        name="retention",
    )(lg, rq.reshape(b, s, qk_w), rk.reshape(b, s, qk_w), rv.reshape(b, s, v_w), rg.reshape(b, s, v_w),
      gn_w.reshape(1, v_w))
    return out.reshape(b * s, v_w)


def _mem_kv_body(mem_ref, nw_ref, w_ref, kw_ref, k_out, v_out):
    d = mem_ref.shape[-1]
    hd = d // MEM_HEADS
    h = _rms(mem_ref[0], nw_ref[...]).astype(BF16)
    kv = jnp.dot(h, w_ref[...], preferred_element_type=F32)
    for i in range(MEM_HEADS):
        k_out[0, :, i * hd:(i + 1) * hd] = _rms(kv[:, i * hd:(i + 1) * hd], kw_ref[...]).astype(BF16)
    v_out[0] = kv[:, d:].astype(BF16)


def _mem_kv(mem, norm_w, w_mkv, k_norm_w):
    b, m, d = mem.shape
    return pl.pallas_call(
        _mem_kv_body,
        grid=(b,),
        in_specs=[pl.BlockSpec((1, m, d), lambda i: (i, 0, 0)),
                  pl.BlockSpec((1, d), lambda i: (0, 0)),
                  pl.BlockSpec((d, 2 * d), lambda i: (0, 0)),
                  pl.BlockSpec((1, d // MEM_HEADS), lambda i: (0, 0))],
        out_specs=[pl.BlockSpec((1, m, d), lambda i: (i, 0, 0))] * 2,
        out_shape=[jax.ShapeDtypeStruct((b, m, d), BF16)] * 2,
        compiler_params=_cparams(("parallel",), VMEM_LIMIT),
        name="mem_kv",
    )(mem, norm_w.reshape(1, d), w_mkv.astype(BF16), k_norm_w.reshape(1, -1))


def _mem_attn_body(x_ref, attn_ref, ret_ref, wout_ref, nw_ref, wq_ref, qw_ref, k_ref, v_ref, wo_ref,
                   rnw_ref, rw_ref, out_ref, h_out, gate_out, meta_out, cnt_out, carry, below):
    d = x_ref.shape[-1]
    hd = d // MEM_HEADS
    tm = x_ref.shape[1]
    sub = below.shape[0]

    @pl.when((pl.program_id(0) == 0) & (pl.program_id(1) == 0))
    def _():
        carry[...] = jnp.zeros_like(carry)
        rr = lax.broadcasted_iota(I32, (sub, sub), 0)
        cc = lax.broadcasted_iota(I32, (sub, sub), 1)
        below[...] = (cc < rr).astype(F32).astype(BF16)

    x = (x_ref[0] + jnp.dot(attn_ref[0], wout_ref[:ATTN_WIDTH], preferred_element_type=F32)
         + jnp.dot(ret_ref[0], wout_ref[ATTN_WIDTH:], preferred_element_type=F32))
    h = _rms(x, nw_ref[...]).astype(BF16)
    q = jnp.dot(h, wq_ref[...], preferred_element_type=F32)
    heads = []
    for i in range(MEM_HEADS):
        cs = slice(i * hd, (i + 1) * hd)
        qn = _rms(q[:, cs], qw_ref[...]).astype(BF16)
        s = lax.dot_general(qn, k_ref[0, :, cs], (((1,), (1,)), ((), ())), preferred_element_type=F32)
        e = jnp.exp(s - jnp.max(s, axis=-1, keepdims=True))
        o = jnp.dot(e.astype(BF16), v_ref[0, :, cs], preferred_element_type=F32)
        heads.append((o / jnp.sum(e, axis=-1, keepdims=True)).astype(BF16))
    o = jnp.concatenate(heads, axis=1)
    x2 = x + jnp.dot(o, wo_ref[...], preferred_element_type=F32)
    out_ref[0] = x2

    base = carry[...]
    for part in range(tm // sub):
        rows = slice(part * sub, (part + 1) * sub)
        base = _route_rows(x2[rows], rnw_ref, rw_ref, below, base,
                           h_out.at[pl.ds(part * sub * ROW_SLABS, sub * ROW_SLABS)],
                           gate_out.at[rows], meta_out.at[rows])
    carry[...] = base
    cnt_out[...] = base


def _mem_attn_route(x, attn, ret, w_out, mk, mv, norm_w, w_mq, q_norm_w, w_mo, moe_norm_w, w_group, w_router):
    b, s, d = x.shape
    m = mk.shape[1]
    tm = MATMUL_TILE
    steps = s // tm
    hd = d // MEM_HEADS
    qw = (q_norm_w * (hd ** -0.5)).reshape(1, hd)
    w_all = jnp.concatenate([w_group, w_router.transpose(1, 0, 2).reshape(d, N_EXPERTS)], axis=1)
    w_all = jnp.pad(w_all, ((0, 0), (0, LANES - w_all.shape[1])))
    whi = w_all.astype(BF16)
    w_split = jnp.concatenate([whi, (w_all - whi.astype(F32)).astype(BF16)], axis=1)
    tok = lambda w: pl.BlockSpec((1, tm, w), lambda bb, i: (bb, i, 0))
    flat = lambda r, w: pl.BlockSpec((r, w), lambda bb, i: (bb * steps + i, 0))
    const = lambda r, c: pl.BlockSpec((r, c), lambda bb, i: (0, 0))
    mem = pl.BlockSpec((1, m, d), lambda bb, i: (bb, 0, 0))
    n = b * s
    return pl.pallas_call(
        _mem_attn_body,
        grid=(b, steps),
        in_specs=[tok(d), tok(ATTN_WIDTH), tok(RET_V_WIDTH), const(ATTN_WIDTH + RET_V_WIDTH, d),
                  const(1, d), const(d, d), const(1, hd), mem, mem, const(d, d),
                  const(1, d), const(d, 2 * LANES)],
        out_specs=[tok(d), flat(tm * ROW_SLABS, LANES), flat(tm, LANES), flat(tm, LANES), const(1, LANES)],
        out_shape=[jax.ShapeDtypeStruct((b, s, d), F32), jax.ShapeDtypeStruct((n * ROW_SLABS, LANES), F32),
                   jax.ShapeDtypeStruct((n, LANES), F32), jax.ShapeDtypeStruct((n, LANES), F32),
                   jax.ShapeDtypeStruct((1, LANES), F32)],
        scratch_shapes=[pltpu.VMEM((1, LANES), F32), pltpu.VMEM((ROUTER_SUBTILE, ROUTER_SUBTILE), BF16)],
        compiler_params=_cparams(("arbitrary", "arbitrary"), VMEM_LIMIT),
        name="mem_attn_route",
    )(x, attn.reshape(b, s, ATTN_WIDTH), ret.reshape(b, s, RET_V_WIDTH), w_out.astype(BF16),
      norm_w.reshape(1, d), w_mq.astype(BF16), qw, mk, mv, w_mo.astype(BF16),
      moe_norm_w.reshape(1, d), w_split)


def _route_rows(x, nw_ref, w_ref, below, base, h_out, gate_out, meta_out):
    tm = x.shape[0]
    h = _rms(x, nw_ref[...])
    _store_row_tiles(h_out, h)
    hi, lo = _split_hi_lo(h)
    both = jnp.dot(hi, w_ref[...], preferred_element_type=F32)
    logits = (both[:, :LANES] + both[:, LANES:]
              + jnp.dot(lo, w_ref[:, :LANES], preferred_element_type=F32))
    lane = lax.broadcasted_iota(I32, (tm, LANES), 1)
    lane_f = lane.astype(F32)

    def argmax_lanes(vals):
        top = jnp.max(vals, axis=-1, keepdims=True)
        idx = jnp.min(jnp.where(vals == top, lane_f, float(LANES)), axis=-1, keepdims=True)
        return top, idx

    is_group = lane < N_GROUPS
    g_top, grp = argmax_lanes(jnp.where(is_group, logits, -jnp.inf))
    g_gate = 1.0 / jnp.sum(jnp.where(is_group, jnp.exp(logits - g_top), 0.0), axis=-1, keepdims=True)
    first = N_GROUPS + EXPERTS_PER_GROUP * grp
    in_group = (lane_f >= first) & (lane_f < first + EXPERTS_PER_GROUP)
    el = jnp.where(in_group, logits, -jnp.inf)
    t1, i1 = argmax_lanes(el)
    t2, i2 = argmax_lanes(jnp.where(lane_f == i1, -jnp.inf, el))
    z = jnp.exp(t2 - t1)
    g1 = g_gate / (1.0 + z)
    g2 = g_gate * z / (1.0 + z)
    e1 = i1 - N_GROUPS
    e2 = i2 - N_GROUPS

    oh1 = (lane_f == e1)
    oh2 = (lane_f == e2)
    oh1b = oh1.astype(F32).astype(BF16)
    oh2b = oh2.astype(F32).astype(BF16)
    pre = jnp.dot(below[...], jnp.concatenate([oh1b, oh2b], axis=1), preferred_element_type=F32)
    pre1, pre2 = pre[:, :LANES], pre[:, LANES:]
    cnt1 = jnp.sum(oh1.astype(F32), axis=0, keepdims=True)
    cnt2 = jnp.sum(oh2.astype(F32), axis=0, keepdims=True)
    r1 = jnp.sum(jnp.where(oh1, pre1 + base, 0.0), axis=-1, keepdims=True)
    r2 = jnp.sum(jnp.where(oh2, pre2 + base + cnt1, 0.0), axis=-1, keepdims=True)

    gate_out[...] = jnp.where(lane == 0, g1, jnp.where(lane == 1, g2, 0.0))
    meta_out[...] = jnp.where(oh1, r1 + 1.0, jnp.where(oh2, -(r2 + 1.0), 0.0))
    return base + cnt1 + cnt2


def _row_copy(src_ref, src_row, dst_ref, dst_row, sem):
    return pltpu.make_async_copy(_tile_rows(src_ref, src_row), _tile_rows(dst_ref, dst_row), sem)


def _slots_body(meta_ref, cnt_ref, dest_ref):
    tm = meta_ref.shape[0]
    blk = float(EXPERT_BLOCK)
    lane = lax.broadcasted_iota(I32, (tm, LANES), 1)
    blocks = jnp.floor((cnt_ref[...] + (blk - 1.0)) * (1.0 / blk))
    rr = lax.broadcasted_iota(I32, (LANES, LANES), 0)
    cc = lax.broadcasted_iota(I32, (LANES, LANES), 1)
    before = (rr < cc).astype(F32).astype(BF16)
    first_block = jnp.dot(jnp.broadcast_to(blocks, (8, LANES)).astype(BF16), before,
                          preferred_element_type=F32)[0:1]
    start = first_block * blk
    meta = meta_ref[...]
    slot0 = jnp.sum(jnp.where(meta > 0.0, start + meta - 1.0, 0.0), axis=-1, keepdims=True)
    slot1 = jnp.sum(jnp.where(meta < 0.0, start - meta - 1.0, 0.0), axis=-1, keepdims=True)
    dest_ref[...] = jnp.where(lane == 0, slot0, jnp.where(lane == 1, slot1, 0.0)).astype(I32)


def _slots(meta, cnt):
    n = meta.shape[0]
    tm = min(SLOTS_TILE, n)
    row = pl.BlockSpec((tm, LANES), lambda i: (i, 0))
    return pl.pallas_call(
        _slots_body,
        grid=(n // tm,),
        in_specs=[row, pl.BlockSpec((1, LANES), lambda i: (0, 0))],
        out_specs=row,
        out_shape=jax.ShapeDtypeStruct((n, LANES), I32),
        compiler_params=_cparams(("parallel",), VMEM_LIMIT),
        name="slots",
    )(meta, cnt)


def _dispatch_body(cnt_ref, dest_ref, h_ref, xs_ref, zbuf, sems):
    i = pl.program_id(0)
    tt = h_ref.shape[0] // ROW_SLABS
    blk = EXPERT_BLOCK
    n_slots = xs_ref.shape[0] // ROW_SLABS

    def pad_copy(pend):
        return pltpu.make_async_copy(zbuf, _tile_rows(xs_ref, pend - blk, blk), sems.at[2])

    @pl.when(i == 0)
    def _():
        zbuf[...] = jnp.zeros_like(zbuf)

        def start(e, acc):
            pend = acc + ((cnt_ref[e] + (blk - 1)) // blk) * blk

            @pl.when(pend > acc)
            def _():
                pad_copy(pend).start()
            return pend

        used = lax.fori_loop(0, N_EXPERTS, start, jnp.int32(0))

        def tail_start(j, carry):
            pad_copy((j + 1) * blk).start()
            return carry

        lax.fori_loop(used // blk, n_slots // blk, tail_start, 0)

        def finish(e, acc):
            pend = acc + ((cnt_ref[e] + (blk - 1)) // blk) * blk

            @pl.when(pend > acc)
            def _():
                pad_copy(pend).wait()
            return pend

        lax.fori_loop(0, N_EXPERTS, finish, jnp.int32(0))

        def tail_wait(j, carry):
            pad_copy((j + 1) * blk).wait()
            return carry

        lax.fori_loop(used // blk, n_slots // blk, tail_wait, 0)

    def tokens(g, carry):
        t0 = pl.multiple_of(g * ROW_DMA_UNROLL, ROW_DMA_UNROLL)
        for u in range(ROW_DMA_UNROLL):
            for k in range(2):
                _row_copy(h_ref, t0 + u, xs_ref, dest_ref[2 * (t0 + u) + k], sems.at[k]).start(priority=k)
        return carry

    lax.fori_loop(0, tt // ROW_DMA_UNROLL, tokens, 0)
    for k in range(2):
        pltpu.make_async_copy(h_ref, _tile_rows(xs_ref, 0, tt), sems.at[k]).wait()


def _dispatch(counts, dest, h3, n_slots):
    n = h3.shape[0] // ROW_SLABS
    tt = DISPATCH_TILE
    grid_spec = pltpu.PrefetchScalarGridSpec(
        num_scalar_prefetch=1,
        grid=(n // tt,),
        in_specs=[pl.BlockSpec((2 * tt,), lambda i, cnt: (i,), memory_space=pltpu.SMEM),
                  pl.BlockSpec((tt * ROW_SLABS, LANES), lambda i, cnt: (i, 0))],
        out_specs=pl.BlockSpec(memory_space=pl.ANY),
        scratch_shapes=[pltpu.VMEM((EXPERT_BLOCK * ROW_SLABS, LANES), F32), pltpu.SemaphoreType.DMA((3,))],
    )
    return pl.pallas_call(
        _dispatch_body,
        grid_spec=grid_spec,
        out_shape=jax.ShapeDtypeStruct((n_slots * ROW_SLABS, LANES), F32),
        compiler_params=_cparams(("arbitrary",), VMEM_LIMIT),
        name="dispatch",
    )(counts, dest, h3)


def _experts_body(be_ref, nu_ref, xs_ref, w1_ref, w3_ref, w2_ref, ys_ref, w1b, w3b, w2b):
    i = pl.program_id(0)
    n_used = nu_ref[0]

    @pl.when(i < n_used)
    def _():
        changed = (i == 0) | (be_ref[i] != be_ref[jnp.maximum(i - 1, 0)])

        @pl.when(changed)
        def _():
            w1b[...] = w1_ref[0].astype(BF16)
            w3b[...] = w3_ref[0].astype(BF16)
            w2b[...] = w2_ref[0].astype(BF16)

        x = _load_row_tiles(xs_ref).astype(BF16)
        chunks = [slice(c0, c0 + EXPERT_FF_CHUNK) for c0 in range(0, w1b.shape[1], EXPERT_FF_CHUNK)]
        gates = [(jnp.dot(x, w1b[:, cs], preferred_element_type=F32),
                  jnp.dot(x, w3b[:, cs], preferred_element_type=F32)) for cs in chunks]
        mid = jnp.concatenate([(a * jax.nn.sigmoid(a) * u).astype(BF16) for a, u in gates], axis=1)
        rows = mid.shape[0]
        for j0 in range(0, ROW_SLABS, EXPERT_OUT_SLABS):
            cols = slice(j0 * LANES, (j0 + EXPERT_OUT_SLABS) * LANES)
            y = jnp.dot(mid, w2b[:, cols], preferred_element_type=F32)
            for j in range(EXPERT_OUT_SLABS):
                ys_ref[pl.ds(j0 + j, rows, stride=ROW_SLABS), :] = y[:, j * LANES:(j + 1) * LANES]

    @pl.when(i >= n_used)
    def _():
        ys_ref[...] = jnp.zeros_like(ys_ref)


def _experts(block_expert, n_used, xs, w1, w3, w2):
    _, d, ff = w1.shape
    assert d == ROW_SLABS * LANES
    blk = EXPERT_BLOCK
    n_blocks = xs.shape[0] // (blk * ROW_SLABS)

    def slot_map(i, be, nu):
        return (jnp.minimum(i, nu[0] - 1), 0)

    def w_map(i, be, nu):
        return (be[jnp.minimum(i, nu[0] - 1)], 0, 0)

    grid_spec = pltpu.PrefetchScalarGridSpec(
        num_scalar_prefetch=2,
        grid=(n_blocks,),
        in_specs=[pl.BlockSpec((blk * ROW_SLABS, LANES), slot_map),
                  pl.BlockSpec((1, d, ff), w_map), pl.BlockSpec((1, d, ff), w_map),
                  pl.BlockSpec((1, ff, d), w_map)],
        out_specs=pl.BlockSpec((blk * ROW_SLABS, LANES), lambda i, be, nu: (i, 0)),
        scratch_shapes=[pltpu.VMEM((d, ff), BF16), pltpu.VMEM((d, ff), BF16), pltpu.VMEM((ff, d), BF16)],
    )
    return pl.pallas_call(
        _experts_body,
        grid_spec=grid_spec,
        out_shape=jax.ShapeDtypeStruct(xs.shape, F32),
        compiler_params=_cparams(("arbitrary",), VMEM_LIMIT),
        name="experts",
    )(block_expert, n_used, xs, w1, w3, w2)


def _combine_body(dest_ref, dest_next_ref, gate_ref, x_ref, ys_ref, out_ref, buf, sems):
    i = pl.program_id(0)
    n_steps = pl.num_programs(0)
    tt = x_ref.shape[0]

    def issue(dref, slot):
        def tokens(g, carry):
            t0 = pl.multiple_of(g * ROW_DMA_UNROLL, ROW_DMA_UNROLL)
            for u in range(ROW_DMA_UNROLL):
                for k in range(2):
                    _row_copy(ys_ref, dref[2 * (t0 + u) + k], buf.at[slot, k], t0 + u,
                              sems.at[slot, k]).start(priority=k)
            return carry

        lax.fori_loop(0, tt // ROW_DMA_UNROLL, tokens, 0)

    @pl.when(i == 0)
    def _():
        issue(dest_ref, 0)

    @pl.when(i + 1 < n_steps)
    def _():
        issue(dest_next_ref, (i + 1) % 2)

    slot = i % 2
    for k in range(2):
        pltpu.make_async_copy(_tile_rows(ys_ref, 0, tt), buf.at[slot, k], sems.at[slot, k]).wait()
    g = gate_ref[...]
    out_ref[...] = (x_ref[...] + g[:, 0:1] * _load_row_tiles(buf.at[slot, 0])
                    + g[:, 1:2] * _load_row_tiles(buf.at[slot, 1]))


def _combine(dest, gates, x2d, ys):
    n, d = x2d.shape
    tt = COMBINE_TILE
    row = lambda w: pl.BlockSpec((tt, w), lambda i: (i, 0))
    last = n // tt - 1
    return pl.pallas_call(
        _combine_body,
        grid=(n // tt,),
        in_specs=[pl.BlockSpec((2 * tt,), lambda i: (i,), memory_space=pltpu.SMEM),
                  pl.BlockSpec((2 * tt,), lambda i: (jnp.minimum(i + 1, last),), memory_space=pltpu.SMEM),
                  row(LANES), row(d), pl.BlockSpec(memory_space=pl.ANY)],
        out_specs=row(d),
        out_shape=jax.ShapeDtypeStruct((n, d), F32),
        scratch_shapes=[pltpu.VMEM((2, 2, tt * ROW_SLABS, LANES), F32), pltpu.SemaphoreType.DMA((2, 2))],
        compiler_params=_cparams(("arbitrary",), VMEM_LIMIT),
        name="combine",
    )(dest, dest, gates, x2d, ys)


def _moe(x2d, h3, gates, meta, cnt, w1, w3, w2):
    n, d = x2d.shape
    blk = EXPERT_BLOCK
    counts = cnt[0, :N_EXPERTS].astype(I32)
    dest = _slots(meta, cnt)[:, 0:2].reshape(-1)
    n_slots = 2 * n + N_EXPERTS * blk
    n_blocks = n_slots // blk
    pends = jnp.cumsum((counts + blk - 1) // blk * blk)
    block_start = jnp.arange(n_blocks, dtype=I32) * blk
    block_expert = jnp.minimum(jnp.sum((pends[None, :] <= block_start[:, None]).astype(I32), axis=1),
                               N_EXPERTS - 1)
    n_used = (pends[-1:] // blk).astype(I32)
    xs = _dispatch(counts, dest, h3, n_slots)
    ys = _experts(block_expert, n_used, xs, w1, w3, w2)
    return _combine(dest, gates, x2d, ys)


def _layer(x, mem, norm_mix_w, w_in, attn_q_norm_w, attn_k_norm_w, ret_decay_f, ret_decay_b, ret_gn_w, w_out,
           norm_mem_w, norm_memkv_w, w_mq, w_mkv, mem_q_norm_w, mem_k_norm_w, w_mo,
           norm_moe_w, w_group, w_router, w_exp_gate, w_exp_up, w_exp_down):
    b, s, d = x.shape
    x2d = x.reshape(b * s, d)
    q, k, v, q_pl, k_pl, v_pl, rq, rk, rv, rg = _in_proj(x2d, norm_mix_w, w_in, attn_q_norm_w, attn_k_norm_w)
    attn = _attention(q, k, v, q_pl, k_pl, v_pl, b, s)
    ret = _retention(rq, rk, rv, rg, ret_decay_f, ret_decay_b, ret_gn_w, b, s)
    mk, mv = _mem_kv(mem, norm_memkv_w, w_mkv, mem_k_norm_w)
    x2, h3, gates, meta, cnt = _mem_attn_route(x, attn, ret, w_out, mk, mv, norm_mem_w, w_mq, mem_q_norm_w, w_mo,
                                               norm_moe_w, w_group, w_router)
    x3 = _moe(x2.reshape(b * s, d), h3, gates, meta, cnt, w_exp_gate, w_exp_up, w_exp_down)
    return x3.reshape(b, s, d)


def kernel(x, mem, norm_mix_w, w_in, attn_q_norm_w, attn_k_norm_w, ret_decay_f, ret_decay_b, ret_gn_w, w_out,
           norm_mem_w, norm_memkv_w, w_mq, w_mkv, mem_q_norm_w, mem_k_norm_w, w_mo, norm_moe_w, w_group,
           w_router, w_exp_gate, w_exp_up, w_exp_down):
    depth = norm_mix_w.shape[0]
    for l in range(depth):
        x = _layer(x, mem, norm_mix_w[l], w_in[l], attn_q_norm_w[l], attn_k_norm_w[l], ret_decay_f[l],
                   ret_decay_b[l], ret_gn_w[l], w_out[l], norm_mem_w[l], norm_memkv_w[l], w_mq[l], w_mkv[l],
                   mem_q_norm_w[l], mem_k_norm_w[l], w_mo[l], norm_moe_w[l], w_group[l], w_router[l],
                   w_exp_gate[l], w_exp_up[l], w_exp_down[l])
    return x
```

```python
import functools

import numpy as np
import jax
import jax.numpy as jnp
from jax import lax
from jax.experimental import pallas as pl
from jax.experimental.pallas import tpu as pltpu

F32 = jnp.float32
BF16 = jnp.bfloat16
I32 = jnp.int32

NORM_EPS = 1e-6
GN_EPS = 1e-5
NEG_INF = -1e30

ATTN_HEADS = 8
ATTN_HEAD_DIM = 64
ATTN_WIDTH = ATTN_HEADS * ATTN_HEAD_DIM
ATTN_HALF = 64
PLANES = 16
ATTN_TILE = PLANES * ATTN_HALF
RET_HEADS = 4
RET_QK_DIM = 64
RET_V_DIM = 128
RET_QK_WIDTH = RET_HEADS * RET_QK_DIM
RET_V_WIDTH = RET_HEADS * RET_V_DIM
MEM_HEADS = 4
N_GROUPS = 4
EXPERTS_PER_GROUP = 8
N_EXPERTS = N_GROUPS * EXPERTS_PER_GROUP

LANES = 128
ROUTER_SUBTILE = 512
MATMUL_TILE = 1024
ATTN_BLOCKS_PER_STEP = 4
LOG2_E = 1.4426950408889634
RET_CHUNK = 256
RET_CHUNKS_PER_STEP = 8
EXPERT_BLOCK = 512
EXPERT_FF_CHUNK = 256
EXPERT_OUT_SLABS = 2
DISPATCH_TILE = 2048
COMBINE_TILE = 256
ROW_DMA_UNROLL = 16
SLOTS_TILE = 4096
VMEM_LIMIT = 56 * 1024 * 1024


def _cparams(sem, vmem=None):
    return pltpu.CompilerParams(dimension_semantics=sem, vmem_limit_bytes=vmem)


def _split_hi_lo(x):
    hi = x.astype(BF16)
    lo = (x - hi.astype(F32)).astype(BF16)
    return hi, lo


def _rms(x, w):
    ms = jnp.mean(x * x, axis=-1, keepdims=True)
    return x * lax.rsqrt(ms + NORM_EPS) * w


ROW_SLABS = 8


def _store_row_tiles(ref, val):
    rows = val.shape[0]
    for j in range(ROW_SLABS):
        ref[pl.ds(j, rows, stride=ROW_SLABS), :] = val[:, j * LANES:(j + 1) * LANES]


def _load_row_tiles(ref):
    rows = ref.shape[0] // ROW_SLABS
    return jnp.concatenate([ref[pl.ds(j, rows, stride=ROW_SLABS), :] for j in range(ROW_SLABS)], axis=1)


def _tile_rows(ref, row, count=1):
    return ref.at[pl.ds(pl.multiple_of(row * ROW_SLABS, ROW_SLABS), count * ROW_SLABS)]


def _in_proj_body(x_ref, nw_ref, w_ref, qw_ref, kw_ref, g_ref,
                  q_out, k_out, v_out, qp_out, kp_out, vp_out, rq_out, rk_out, rv_out, rg_out, slabs, quarters):
    tm = x_ref.shape[0]
    h = _rms(x_ref[...], nw_ref[...]).astype(BF16)

    def emit(val, nat_out, plane_out):
        nat_out[...] = val.astype(BF16)
        for j in range(ATTN_WIDTH // LANES):
            slabs[j] = val[:, j * LANES:(j + 1) * LANES]
        for j in range(ATTN_WIDTH // LANES):
            for r in range(4):
                quarters[j, r] = slabs[j, pl.ds(r, tm // 4, stride=4), :]
        for r in range(4):
            for q in range(4):
                for j in range(ATTN_WIDTH // LANES):
                    rows = quarters[j, r, pl.ds(q, tm // PLANES, stride=4), :]
                    plane_out[0, r + 4 * q, :, j * LANES:(j + 1) * LANES] = rows.astype(BF16)

    def proj(a, b):
        return jnp.dot(h, w_ref[:, a:b], preferred_element_type=F32)

    def head_norm(p, w):
        s = jnp.dot((p * p).astype(BF16), g_ref[...], preferred_element_type=F32)
        return p * lax.rsqrt(s * (1.0 / ATTN_HEAD_DIM) + NORM_EPS) * w

    a = ATTN_WIDTH
    emit(head_norm(proj(0, a), qw_ref[...]), q_out, qp_out)
    emit(head_norm(proj(a, 2 * a), kw_ref[...]), k_out, kp_out)
    emit(proj(2 * a, 3 * a), v_out, vp_out)
    c = 3 * a
    rq_out[...] = proj(c, c + RET_QK_WIDTH).astype(BF16)
    rk_out[...] = proj(c + RET_QK_WIDTH, c + 2 * RET_QK_WIDTH).astype(BF16)
    c += 2 * RET_QK_WIDTH
    rv_out[...] = proj(c, c + RET_V_WIDTH).astype(BF16)
    rg_out[...] = proj(c + RET_V_WIDTH, c + 2 * RET_V_WIDTH).astype(BF16)


def _in_proj(x2d, norm_w, w_in, q_norm_w, k_norm_w):
    n, d = x2d.shape
    cols = w_in.shape[1]
    tm = MATMUL_TILE
    qw = (jnp.tile(q_norm_w, ATTN_HEADS) * (ATTN_HEAD_DIM ** -0.5 * LOG2_E)).reshape(1, ATTN_WIDTH)
    kw = jnp.tile(k_norm_w, ATTN_HEADS).reshape(1, ATTN_WIDTH)
    head_of = np.arange(ATTN_WIDTH) // ATTN_HEAD_DIM
    gmat = jnp.asarray(head_of[:, None] == head_of[None, :], dtype=BF16)
    row = lambda w: pl.BlockSpec((tm, w), lambda i: (i, 0))
    full = lambda r, c: pl.BlockSpec((r, c), lambda i: (0, 0))
    per_tile = ATTN_TILE // tm
    plane = pl.BlockSpec((1, PLANES, tm // PLANES, ATTN_WIDTH), lambda i: (i // per_tile, 0, i % per_tile, 0))
    plane_shape = jax.ShapeDtypeStruct((n // ATTN_TILE, PLANES, ATTN_TILE // PLANES, ATTN_WIDTH), BF16)
    ret_widths = (RET_QK_WIDTH, RET_QK_WIDTH, RET_V_WIDTH, RET_V_WIDTH)
    return pl.pallas_call(
        _in_proj_body,
        grid=(n // tm,),
        in_specs=[row(d), full(1, d), full(d, cols), full(1, ATTN_WIDTH), full(1, ATTN_WIDTH),
                  full(ATTN_WIDTH, ATTN_WIDTH)],
        out_specs=[row(ATTN_WIDTH)] * 3 + [plane] * 3 + [row(w) for w in ret_widths],
        out_shape=([jax.ShapeDtypeStruct((n, ATTN_WIDTH), BF16)] * 3 + [plane_shape] * 3
                   + [jax.ShapeDtypeStruct((n, w), BF16) for w in ret_widths]),
        scratch_shapes=[pltpu.VMEM((ATTN_WIDTH // LANES, tm, LANES), F32),
                        pltpu.VMEM((ATTN_WIDTH // LANES, 4, tm // 4, LANES), F32)],
        compiler_params=_cparams(("parallel",), VMEM_LIMIT),
        name="in_proj",
    )(x2d, norm_w.reshape(1, d), w_in.astype(BF16), qw, kw, gmat)


def _attn_bias(dilation, interleave=1):
    half = ATTN_HALF
    idx = np.arange(half)
    sub = (idx % (half // interleave)) * interleave + idx // (half // interleave)
    qi = sub[:, None]
    kc = (np.arange(3)[:, None] * half + sub[None, :]).reshape(1, -1)
    seg = np.repeat(np.arange(3), half)[None, :]
    delta = kc - half - qi
    band = np.abs(delta) <= half
    slopes = np.exp2(-8.0 * np.arange(1, ATTN_HEADS + 1) / ATTN_HEADS)
    valid = [band & (seg >= 1), band, band & (seg <= 1)]
    out = np.empty((ATTN_HEADS // 2, 3, 2 * half, 3 * half), np.float32)
    for p in range(ATTN_HEADS // 2):
        for v in range(3):
            for s in range(2):
                b = -slopes[2 * p + s] * LOG2_E * (dilation * np.abs(delta)).astype(np.float32)
                out[p, v, s * half:(s + 1) * half] = np.where(valid[v], b, NEG_INF)
    return jnp.asarray(out.reshape(-1, 2 * half, 3 * half))


def _attn_scores(chains):
    lo_lanes = lax.broadcasted_iota(I32, (ATTN_HALF, LANES), 1) < ATTN_HEAD_DIM
    scores = []
    for q, k, v, bias in chains:
        zero = jnp.zeros_like(q)
        qs = jnp.concatenate([jnp.where(lo_lanes, q, zero), jnp.where(lo_lanes, zero, q)], axis=0)
        scores.append(lax.dot_general(qs, k, (((1,), (1,)), ((), ())), preferred_element_type=F32) + bias)
    return scores


def _attn_finish(chains, scores):
    half = ATTN_HALF
    lo_lanes = lax.broadcasted_iota(I32, (half, LANES), 1) < ATTN_HEAD_DIM
    ones = jnp.ones((3 * half, LANES), BF16)
    probs, maxes = [], []
    for s in scores:
        m = jnp.max(s, axis=-1, keepdims=True)
        probs.append(jnp.exp2(s - m).astype(BF16))
        maxes.append(m)
    results = []
    for (q, k, v, bias), e in zip(chains, probs):
        vext = jnp.concatenate([v, ones], axis=1)
        results.append(jnp.dot(e, vext, preferred_element_type=F32))
    outs = []
    for r, m in zip(results, maxes):
        acc = jnp.where(lo_lanes, r[:half, :LANES], r[half:, :LANES])
        denom = jnp.where(lo_lanes, r[:half, LANES:], r[half:, LANES:])
        mm = jnp.where(lo_lanes, m[:half], m[half:])
        outs.append((acc / denom, mm + jnp.log2(denom)))
    return outs


def _attend_groups(groups):
    pending = None
    for chains, consume in groups:
        scores = _attn_scores(chains)
        if pending is not None:
            p_chains, p_scores, p_consume = pending
            p_consume(_attn_finish(p_chains, p_scores))
        pending = (chains, scores, consume)
    p_chains, p_scores, p_consume = pending
    p_consume(_attn_finish(p_chains, p_scores))


def _merge(oa, la, ob, lb):
    m = jnp.maximum(la, lb)
    ea = jnp.exp2(la - m)
    eb = jnp.exp2(lb - m)
    den = ea + eb
    return (ea * oa + eb * ob) / den, m + jnp.log2(den)


def _attn_body(q_ref, kp_ref, km_ref, kn_ref, vp_ref, vm_ref, vn_ref,
               qpl_ref, kplp_ref, kplm_ref, kpln_ref, vplp_ref, vplm_ref, vpln_ref,
               b1_ref, b4_ref, b16_ref, o_ref, kcat, vcat, o_far, l_far):
    half = ATTN_HALF
    t = pl.program_id(1)
    n_tiles = pl.num_programs(1)
    tile = q_ref.shape[1]
    pairs = ATTN_HEADS // 2
    piece = half // 4

    kcat[0:half] = kp_ref[0]
    kcat[half:half + tile] = km_ref[0]
    kcat[half + tile:] = kn_ref[0]
    vcat[0:half] = vp_ref[0]
    vcat[half:half + tile] = vm_ref[0]
    vcat[half + tile:] = vn_ref[0]

    def edge(first, last):
        return jnp.where(first, 0, jnp.where(last, 2, 1))

    groups = []

    def far_groups(r):
        v16 = edge(t == 0, t == n_tiles - 1)
        for pr in range(pairs):
            cs = slice(pr * LANES, (pr + 1) * LANES)
            chains = []
            for m in range(4):
                c = r + 4 * m
                k3 = jnp.concatenate([kplp_ref[0, c, :, cs], kplm_ref[0, c, :, cs], kpln_ref[0, c, :, cs]], axis=0)
                v3 = jnp.concatenate([vplp_ref[0, c, :, cs], vplm_ref[0, c, :, cs], vpln_ref[0, c, :, cs]], axis=0)
                chains.append((qpl_ref[0, c, :, cs], k3, v3, b16_ref[pr * 3 + v16]))
            for nb in range(4):
                def rows(main, before, after, a0):
                    ref, lo = (before, a0 + half) if a0 < 0 else (after, a0 - half) if a0 >= half else (main, a0)
                    return [ref[0, r + 4 * m, lo:lo + piece, cs] for m in range(4)]
                a0 = nb * piece
                q4 = jnp.concatenate(rows(qpl_ref, None, None, a0), axis=0)
                k4 = jnp.concatenate(sum((rows(kplm_ref, kplp_ref, kpln_ref, a0 + d) for d in (-piece, 0, piece)), []),
                                     axis=0)
                v4 = jnp.concatenate(sum((rows(vplm_ref, vplp_ref, vpln_ref, a0 + d) for d in (-piece, 0, piece)), []),
                                     axis=0)
                v4e = edge((t == 0) & (nb == 0), (t == n_tiles - 1) & (nb == 3))
                chains.append((q4, k4, v4, b4_ref[pr * 3 + v4e]))

            def consume(res, r=r, pr=pr):
                for m in range(4):
                    o16, l16 = res[m]
                    o_rows, l_rows = [], []
                    for nb in range(4):
                        o4, l4 = res[4 + nb]
                        sl = slice(nb * piece, (nb + 1) * piece)
                        s4 = slice(m * piece, (m + 1) * piece)
                        om, lm = _merge(o16[sl], l16[sl], o4[s4], l4[s4])
                        o_rows.append(om)
                        l_rows.append(lm)
                    dst = pl.ds(r + 4 * m, half, stride=PLANES)
                    o_far[pr, dst, :] = jnp.concatenate(o_rows, axis=0)
                    l_far[pr, dst, :] = jnp.concatenate(l_rows, axis=0)

            groups.append((chains, consume))

    for r in range(4):
        far_groups(r)

    n_blocks = n_tiles * (tile // half)
    for it in range(tile // half // ATTN_BLOCKS_PER_STEP):
        chains, where = [], []
        for u in range(ATTN_BLOCKS_PER_STEP):
            jb = it * ATTN_BLOCKS_PER_STEP + u
            r0 = jb * half
            gb = t * (tile // half) + jb
            variant = edge(gb == 0, gb == n_blocks - 1)
            for pr in range(pairs):
                cs = slice(pr * LANES, (pr + 1) * LANES)
                chains.append((q_ref[0, pl.ds(r0, half), cs], kcat[pl.ds(r0, 3 * half), cs],
                               vcat[pl.ds(r0, 3 * half), cs], b1_ref[pr * 3 + variant]))
                where.append((r0, pr, cs))

        def consume(res, where=where):
            for (r0, pr, cs), (o1, l1) in zip(where, res):
                o, _ = _merge(o1, l1, o_far[pr, pl.ds(r0, half), :], l_far[pr, pl.ds(r0, half), :])
                o_ref[0, pl.ds(r0, half), cs] = o.astype(o_ref.dtype)

        groups.append((chains, consume))

    _attend_groups(groups)


def _attention(q, k, v, q_pl, k_pl, v_pl, b, s):
    w = ATTN_WIDTH
    half = ATTN_HALF
    tile = ATTN_TILE
    n_tiles = s // tile
    hb = tile // half
    last = s // half - 1
    main = pl.BlockSpec((1, tile, w), lambda bb, t: (bb, t, 0))
    prev = pl.BlockSpec((1, half, w), lambda bb, t: (bb, jnp.maximum(t * hb - 1, 0), 0))
    nxt = pl.BlockSpec((1, half, w), lambda bb, t: (bb, jnp.minimum((t + 1) * hb, last), 0))
    pshape = (1, PLANES, tile // PLANES, w)
    pl_main = pl.BlockSpec(pshape, lambda bb, t: (bb * n_tiles + t, 0, 0, 0))
    pl_prev = pl.BlockSpec(pshape, lambda bb, t: (bb * n_tiles + jnp.maximum(t - 1, 0), 0, 0, 0))
    pl_next = pl.BlockSpec(pshape, lambda bb, t: (bb * n_tiles + jnp.minimum(t + 1, n_tiles - 1), 0, 0, 0))
    biases = [_attn_bias(1), _attn_bias(4, interleave=4), _attn_bias(16)]
    bias_spec = pl.BlockSpec(biases[0].shape, lambda bb, t: (0, 0, 0))
    nat = lambda a: a.reshape(b, s, w)
    out = pl.pallas_call(
        _attn_body,
        grid=(b, n_tiles),
        in_specs=[main, prev, main, nxt, prev, main, nxt,
                  pl_main, pl_prev, pl_main, pl_next, pl_prev, pl_main, pl_next,
                  bias_spec, bias_spec, bias_spec],
        out_specs=main,
        out_shape=jax.ShapeDtypeStruct((b, s, w), BF16),
        scratch_shapes=[pltpu.VMEM((tile + 2 * half, w), BF16), pltpu.VMEM((tile + 2 * half, w), BF16),
                        pltpu.VMEM((ATTN_HEADS // 2, tile, LANES), F32),
                        pltpu.VMEM((ATTN_HEADS // 2, tile, LANES), F32)],
        compiler_params=_cparams(("parallel", "parallel"), VMEM_LIMIT),
        name="attention",
    )(nat(q), nat(k), nat(k), nat(k), nat(v), nat(v), nat(v),
      q_pl, k_pl, k_pl, k_pl, v_pl, v_pl, v_pl, *biases)
    return out.reshape(b * s, w)


def _ret_body(lg_ref, q_ref, k_ref, v_ref, g_ref, gnw_ref, out_ref,
              dmat, qdec, kdec, cdec, fstate, rstate, rall, *, chunk):
    c = chunk
    b = pl.program_id(0)
    ph = pl.program_id(1)
    n = pl.program_id(2)
    n_chunks = pl.num_programs(2)
    k_scale = RET_QK_DIM ** -0.5
    pairs = RET_HEADS // 2
    pw = 2 * RET_QK_DIM
    vw = 2 * RET_V_DIM
    first = lax.broadcasted_iota(I32, (c, pw), 1) < RET_QK_DIM
    diag = ((lax.broadcasted_iota(I32, (pw, vw), 0) < RET_QK_DIM)
            == (lax.broadcasted_iota(I32, (pw, vw), 1) < RET_V_DIM))

    @pl.when((b == 0) & (ph == 0) & (n == 0))
    def _init_tables():
        ii = lax.broadcasted_iota(I32, (c, c), 0)
        jj = lax.broadcasted_iota(I32, (c, c), 1)
        fwd = (ii - jj).astype(F32)
        ri = lax.broadcasted_iota(I32, (c, pw), 0).astype(F32)
        top = lax.broadcasted_iota(I32, (pw, vw), 0) < RET_QK_DIM
        for h in range(RET_HEADS):
            dmat[h] = jnp.where(ii >= jj, jnp.exp(lg_ref[0, h] * fwd), jnp.exp(-lg_ref[1, h] * fwd)) * k_scale
        for pr in range(pairs):
            lf = jnp.where(first, lg_ref[0, 2 * pr], lg_ref[0, 2 * pr + 1])
            lb = jnp.where(first, lg_ref[1, 2 * pr], lg_ref[1, 2 * pr + 1])
            qdec[0, pr] = jnp.exp(lf * (ri + 1.0))
            qdec[1, pr] = jnp.exp(lb * (c - ri))
            kdec[0, pr] = jnp.exp(lf * (c - 1.0 - ri)) * k_scale
            kdec[1, pr] = jnp.exp(lb * ri) * k_scale
            for direction in range(2):
                cdec[direction, pr] = jnp.exp(jnp.where(top, lg_ref[direction, 2 * pr],
                                                        lg_ref[direction, 2 * pr + 1]) * c)

    per_step = q_ref.shape[1] // c

    def kv_update(direction, pr, rows):
        kp = k_ref[0, rows, pr * pw:(pr + 1) * pw].astype(F32)
        ks = (kp * kdec[direction, pr]).astype(BF16)
        new = lax.dot_general(ks, v_ref[0, rows, pr * vw:(pr + 1) * vw], (((0,), (0,)), ((), ())),
                              preferred_element_type=F32)
        return jnp.where(diag, new, 0.0)

    @pl.when(ph == 0)
    def _right_to_left():
        @pl.when(n == 0)
        def _():
            rstate[...] = jnp.zeros_like(rstate)

        for sub in reversed(range(per_step)):
            rows = slice(sub * c, (sub + 1) * c)
            ci = (n_chunks - 1 - n) * per_step + sub
            for pr in range(pairs):
                st = rstate[pr]
                rall[ci, pr] = st
                rstate[pr] = st * cdec[1, pr] + kv_update(1, pr, rows)

    @pl.when(ph == 1)
    def _left_to_right():
        @pl.when(n == 0)
        def _():
            fstate[...] = jnp.zeros_like(fstate)

        for sub in range(per_step):
            rows = slice(sub * c, (sub + 1) * c)
            for pr in range(pairs):
                qp = q_ref[0, rows, pr * pw:(pr + 1) * pw]
                kp = k_ref[0, rows, pr * pw:(pr + 1) * pw]
                qf = qp.astype(F32)
                qcat = jnp.concatenate([(qf * qdec[0, pr]).astype(BF16), (qf * qdec[1, pr]).astype(BF16)], axis=1)
                st = fstate[pr]
                states = jnp.concatenate([st, rall[n * per_step + sub, pr]], axis=0).astype(BF16)
                cross = jnp.dot(qcat, states, preferred_element_type=F32)
                fstate[pr] = st * cdec[0, pr] + kv_update(0, pr, rows)
                zero = jnp.zeros_like(qp)
                for a in range(2):
                    h = 2 * pr + a
                    vs = slice(h * RET_V_DIM, (h + 1) * RET_V_DIM)
                    qm = jnp.where(first, qp, zero) if a == 0 else jnp.where(first, zero, qp)
                    s = lax.dot_general(qm, kp, (((1,), (1,)), ((), ())), preferred_element_type=F32) * dmat[h]
                    y = jnp.dot(s.astype(BF16), v_ref[0, rows, vs], preferred_element_type=F32)
                    y = y + cross[:, a * RET_V_DIM:(a + 1) * RET_V_DIM]
                    mu = jnp.mean(y, axis=-1, keepdims=True)
                    yc = y - mu
                    var = jnp.mean(yc * yc, axis=-1, keepdims=True)
                    yn = yc * lax.rsqrt(var + GN_EPS) * gnw_ref[:, vs]
                    gate = g_ref[0, rows, vs].astype(F32)
                    out_ref[0, rows, vs] = (gate * jax.nn.sigmoid(gate) * yn).astype(out_ref.dtype)


def _retention(rq, rk, rv, rg, decay_f, decay_b, gn_w, b, s):
    c = RET_CHUNK
    rows = RET_CHUNK * RET_CHUNKS_PER_STEP
    nb = s // rows
    lg = jnp.stack([jax.nn.log_sigmoid(decay_f.astype(F32)), jax.nn.log_sigmoid(decay_b.astype(F32))])
    qk_w, v_w = RET_QK_WIDTH, RET_V_WIDTH
    pairs, pair_qk, pair_v = RET_HEADS // 2, 2 * RET_QK_DIM, 2 * RET_V_DIM

    def both(bb, ph, n):
        return (bb, jnp.where(ph == 0, nb - 1 - n, n), 0)

    def fwd_only(bb, ph, n):
        return (bb, jnp.where(ph == 0, 0, n), 0)

    out = pl.pallas_call(
        functools.partial(_ret_body, chunk=c),
        grid=(b, 2, nb),
        in_specs=[pl.BlockSpec(memory_space=pltpu.SMEM),
                  pl.BlockSpec((1, rows, qk_w), fwd_only),
                  pl.BlockSpec((1, rows, qk_w), both),
                  pl.BlockSpec((1, rows, v_w), both),
                  pl.BlockSpec((1, rows, v_w), fwd_only),
                  pl.BlockSpec((1, v_w), lambda bb, ph, n: (0, 0))],
        out_specs=pl.BlockSpec((1, rows, v_w), fwd_only),
        out_shape=jax.ShapeDtypeStruct((b, s, v_w), BF16),
        scratch_shapes=[pltpu.VMEM((RET_HEADS, c, c), F32),
                        pltpu.VMEM((2, pairs, c, pair_qk), F32),
                        pltpu.VMEM((2, pairs, c, pair_qk), F32),
                        pltpu.VMEM((2, pairs, pair_qk, pair_v), F32),
                        pltpu.VMEM((pairs, pair_qk, pair_v), F32),
                        pltpu.VMEM((pairs, pair_qk, pair_v), F32),
                        pltpu.VMEM((s // c, pairs, pair_qk, pair_v), F32)],
        compiler_params=_cparams(("arbitrary", "arbitrary", "arbitrary"), VMEM_LIMIT),
        name="retention",
    )(lg, rq.reshape(b, s, qk_w), rk.reshape(b, s, qk_w), rv.reshape(b, s, v_w), rg.reshape(b, s, v_w),
      gn_w.reshape(1, v_w))
    return out.reshape(b * s, v_w)


def _mem_kv_body(mem_ref, nw_ref, w_ref, kw_ref, k_out, v_out):
    d = mem_ref.shape[-1]
    hd = d // MEM_HEADS
    h = _rms(mem_ref[0], nw_ref[...]).astype(BF16)
    kv = jnp.dot(h, w_ref[...], preferred_element_type=F32)
    for i in range(MEM_HEADS):
        k_out[0, :, i * hd:(i + 1) * hd] = _rms(kv[:, i * hd:(i + 1) * hd], kw_ref[...]).astype(BF16)
    v_out[0] = kv[:, d:].astype(BF16)


def _mem_kv(mem, norm_w, w_mkv, k_norm_w):
    b, m, d = mem.shape
    return pl.pallas_call(
        _mem_kv_body,
        grid=(b,),
        in_specs=[pl.BlockSpec((1, m, d), lambda i: (i, 0, 0)),
                  pl.BlockSpec((1, d), lambda i: (0, 0)),
                  pl.BlockSpec((d, 2 * d), lambda i: (0, 0)),
                  pl.BlockSpec((1, d // MEM_HEADS), lambda i: (0, 0))],
        out_specs=[pl.BlockSpec((1, m, d), lambda i: (i, 0, 0))] * 2,
        out_shape=[jax.ShapeDtypeStruct((b, m, d), BF16)] * 2,
        compiler_params=_cparams(("parallel",), VMEM_LIMIT),
        name="mem_kv",
    )(mem, norm_w.reshape(1, d), w_mkv.astype(BF16), k_norm_w.reshape(1, -1))


def _mem_attn_body(x_ref, attn_ref, ret_ref, wout_ref, nw_ref, wq_ref, qw_ref, k_ref, v_ref, wo_ref,
                   rnw_ref, rw_ref, out_ref, h_out, gate_out, meta_out, cnt_out, carry, below):
    d = x_ref.shape[-1]
    hd = d // MEM_HEADS
    tm = x_ref.shape[1]
    sub = below.shape[0]

    @pl.when((pl.program_id(0) == 0) & (pl.program_id(1) == 0))
    def _():
        carry[...] = jnp.zeros_like(carry)
        rr = lax.broadcasted_iota(I32, (sub, sub), 0)
        cc = lax.broadcasted_iota(I32, (sub, sub), 1)
        below[...] = (cc < rr).astype(F32).astype(BF16)

    x = (x_ref[0] + jnp.dot(attn_ref[0], wout_ref[:ATTN_WIDTH], preferred_element_type=F32)
         + jnp.dot(ret_ref[0], wout_ref[ATTN_WIDTH:], preferred_element_type=F32))
    h = _rms(x, nw_ref[...]).astype(BF16)
    q = jnp.dot(h, wq_ref[...], preferred_element_type=F32)
    heads = []
    for i in range(MEM_HEADS):
        cs = slice(i * hd, (i + 1) * hd)
        qn = _rms(q[:, cs], qw_ref[...]).astype(BF16)
        s = lax.dot_general(qn, k_ref[0, :, cs], (((1,), (1,)), ((), ())), preferred_element_type=F32)
        e = jnp.exp(s - jnp.max(s, axis=-1, keepdims=True))
        o = jnp.dot(e.astype(BF16), v_ref[0, :, cs], preferred_element_type=F32)
        heads.append((o / jnp.sum(e, axis=-1, keepdims=True)).astype(BF16))
    o = jnp.concatenate(heads, axis=1)
    x2 = x + jnp.dot(o, wo_ref[...], preferred_element_type=F32)
    out_ref[0] = x2

    base = carry[...]
    for part in range(tm // sub):
        rows = slice(part * sub, (part + 1) * sub)
        base = _route_rows(x2[rows], rnw_ref, rw_ref, below, base,
                           h_out.at[pl.ds(part * sub * ROW_SLABS, sub * ROW_SLABS)],
                           gate_out.at[rows], meta_out.at[rows])
    carry[...] = base
    cnt_out[...] = base


def _mem_attn_route(x, attn, ret, w_out, mk, mv, norm_w, w_mq, q_norm_w, w_mo, moe_norm_w, w_group, w_router):
    b, s, d = x.shape
    m = mk.shape[1]
    tm = MATMUL_TILE
    steps = s // tm
    hd = d // MEM_HEADS
    qw = (q_norm_w * (hd ** -0.5)).reshape(1, hd)
    w_all = jnp.concatenate([w_group, w_router.transpose(1, 0, 2).reshape(d, N_EXPERTS)], axis=1)
    w_all = jnp.pad(w_all, ((0, 0), (0, LANES - w_all.shape[1])))
    whi = w_all.astype(BF16)
    w_split = jnp.concatenate([whi, (w_all - whi.astype(F32)).astype(BF16)], axis=1)
    tok = lambda w: pl.BlockSpec((1, tm, w), lambda bb, i: (bb, i, 0))
    flat = lambda r, w: pl.BlockSpec((r, w), lambda bb, i: (bb * steps + i, 0))
    const = lambda r, c: pl.BlockSpec((r, c), lambda bb, i: (0, 0))
    mem = pl.BlockSpec((1, m, d), lambda bb, i: (bb, 0, 0))
    n = b * s
    return pl.pallas_call(
        _mem_attn_body,
        grid=(b, steps),
        in_specs=[tok(d), tok(ATTN_WIDTH), tok(RET_V_WIDTH), const(ATTN_WIDTH + RET_V_WIDTH, d),
                  const(1, d), const(d, d), const(1, hd), mem, mem, const(d, d),
                  const(1, d), const(d, 2 * LANES)],
        out_specs=[tok(d), flat(tm * ROW_SLABS, LANES), flat(tm, LANES), flat(tm, LANES), const(1, LANES)],
        out_shape=[jax.ShapeDtypeStruct((b, s, d), F32), jax.ShapeDtypeStruct((n * ROW_SLABS, LANES), F32),
                   jax.ShapeDtypeStruct((n, LANES), F32), jax.ShapeDtypeStruct((n, LANES), F32),
                   jax.ShapeDtypeStruct((1, LANES), F32)],
        scratch_shapes=[pltpu.VMEM((1, LANES), F32), pltpu.VMEM((ROUTER_SUBTILE, ROUTER_SUBTILE), BF16)],
        compiler_params=_cparams(("arbitrary", "arbitrary"), VMEM_LIMIT),
        name="mem_attn_route",
    )(x, attn.reshape(b, s, ATTN_WIDTH), ret.reshape(b, s, RET_V_WIDTH), w_out.astype(BF16),
      norm_w.reshape(1, d), w_mq.astype(BF16), qw, mk, mv, w_mo.astype(BF16),
      moe_norm_w.reshape(1, d), w_split)


def _route_rows(x, nw_ref, w_ref, below, base, h_out, gate_out, meta_out):
    tm = x.shape[0]
    h = _rms(x, nw_ref[...])
    _store_row_tiles(h_out, h)
    hi, lo = _split_hi_lo(h)
    both = jnp.dot(hi, w_ref[...], preferred_element_type=F32)
    logits = (both[:, :LANES] + both[:, LANES:]
              + jnp.dot(lo, w_ref[:, :LANES], preferred_element_type=F32))
    lane = lax.broadcasted_iota(I32, (tm, LANES), 1)
    lane_f = lane.astype(F32)

    def argmax_lanes(vals):
        top = jnp.max(vals, axis=-1, keepdims=True)
        idx = jnp.min(jnp.where(vals == top, lane_f, float(LANES)), axis=-1, keepdims=True)
        return top, idx

    is_group = lane < N_GROUPS
    g_top, grp = argmax_lanes(jnp.where(is_group, logits, -jnp.inf))
    g_gate = 1.0 / jnp.sum(jnp.where(is_group, jnp.exp(logits - g_top), 0.0), axis=-1, keepdims=True)
    first = N_GROUPS + EXPERTS_PER_GROUP * grp
    in_group = (lane_f >= first) & (lane_f < first + EXPERTS_PER_GROUP)
    el = jnp.where(in_group, logits, -jnp.inf)
    t1, i1 = argmax_lanes(el)
    t2, i2 = argmax_lanes(jnp.where(lane_f == i1, -jnp.inf, el))
    z = jnp.exp(t2 - t1)
    g1 = g_gate / (1.0 + z)
    g2 = g_gate * z / (1.0 + z)
    e1 = i1 - N_GROUPS
    e2 = i2 - N_GROUPS

    oh1 = (lane_f == e1)
    oh2 = (lane_f == e2)
    oh1b = oh1.astype(F32).astype(BF16)
    oh2b = oh2.astype(F32).astype(BF16)
    pre = jnp.dot(below[...], jnp.concatenate([oh1b, oh2b], axis=1), preferred_element_type=F32)
    pre1, pre2 = pre[:, :LANES], pre[:, LANES:]
    cnt1 = jnp.sum(oh1.astype(F32), axis=0, keepdims=True)
    cnt2 = jnp.sum(oh2.astype(F32), axis=0, keepdims=True)
    r1 = jnp.sum(jnp.where(oh1, pre1 + base, 0.0), axis=-1, keepdims=True)
    r2 = jnp.sum(jnp.where(oh2, pre2 + base + cnt1, 0.0), axis=-1, keepdims=True)

    gate_out[...] = jnp.where(lane == 0, g1, jnp.where(lane == 1, g2, 0.0))
    meta_out[...] = jnp.where(oh1, r1 + 1.0, jnp.where(oh2, -(r2 + 1.0), 0.0))
    return base + cnt1 + cnt2


def _row_copy(src_ref, src_row, dst_ref, dst_row, sem):
    return pltpu.make_async_copy(_tile_rows(src_ref, src_row), _tile_rows(dst_ref, dst_row), sem)


def _slots_body(meta_ref, cnt_ref, dest_ref):
    tm = meta_ref.shape[0]
    blk = float(EXPERT_BLOCK)
    lane = lax.broadcasted_iota(I32, (tm, LANES), 1)
    blocks = jnp.floor((cnt_ref[...] + (blk - 1.0)) * (1.0 / blk))
    rr = lax.broadcasted_iota(I32, (LANES, LANES), 0)
    cc = lax.broadcasted_iota(I32, (LANES, LANES), 1)
    before = (rr < cc).astype(F32).astype(BF16)
    first_block = jnp.dot(jnp.broadcast_to(blocks, (8, LANES)).astype(BF16), before,
                          preferred_element_type=F32)[0:1]
    start = first_block * blk
    meta = meta_ref[...]
    slot0 = jnp.sum(jnp.where(meta > 0.0, start + meta - 1.0, 0.0), axis=-1, keepdims=True)
    slot1 = jnp.sum(jnp.where(meta < 0.0, start - meta - 1.0, 0.0), axis=-1, keepdims=True)
    dest_ref[...] = jnp.where(lane == 0, slot0, jnp.where(lane == 1, slot1, 0.0)).astype(I32)


def _slots(meta, cnt):
    n = meta.shape[0]
    tm = min(SLOTS_TILE, n)
    row = pl.BlockSpec((tm, LANES), lambda i: (i, 0))
    return pl.pallas_call(
        _slots_body,
        grid=(n // tm,),
        in_specs=[row, pl.BlockSpec((1, LANES), lambda i: (0, 0))],
        out_specs=row,
        out_shape=jax.ShapeDtypeStruct((n, LANES), I32),
        compiler_params=_cparams(("parallel",), VMEM_LIMIT),
        name="slots",
    )(meta, cnt)


def _dispatch_body(cnt_ref, dest_ref, h_ref, xs_ref, zbuf, sems):
    i = pl.program_id(0)
    tt = h_ref.shape[0] // ROW_SLABS
    blk = EXPERT_BLOCK
    n_slots = xs_ref.shape[0] // ROW_SLABS

    def pad_copy(pend, sem=2):
        return pltpu.make_async_copy(zbuf, _tile_rows(xs_ref, pend - blk, blk), sems.at[sem])

    def each_expert(fn):
        def body(e, acc):
            pend = acc + ((cnt_ref[e] + (blk - 1)) // blk) * blk

            @pl.when(pend > acc)
            def _():
                fn(pend)
            return pend

        return lax.fori_loop(0, N_EXPERTS, body, jnp.int32(0))

    def each_tail_block(used, fn):
        def body(j, carry):
            fn((j + 1) * blk)
            return carry

        lax.fori_loop(used // blk, n_slots // blk, body, 0)

    @pl.when(i == 0)
    def _():
        zbuf[...] = jnp.zeros_like(zbuf)
        used = each_expert(lambda pend: pad_copy(pend).start())
        each_tail_block(used, lambda pend: pad_copy(pend, 3).start())
        each_expert(lambda pend: pad_copy(pend).wait())

    def tokens(g, carry):
        t0 = pl.multiple_of(g * ROW_DMA_UNROLL, ROW_DMA_UNROLL)
        for u in range(ROW_DMA_UNROLL):
            for k in range(2):
                _row_copy(h_ref, t0 + u, xs_ref, dest_ref[2 * (t0 + u) + k], sems.at[k]).start(priority=k)
        return carry

    lax.fori_loop(0, tt // ROW_DMA_UNROLL, tokens, 0)
    for k in range(2):
        pltpu.make_async_copy(h_ref, _tile_rows(xs_ref, 0, tt), sems.at[k]).wait()

    @pl.when(i == pl.num_programs(0) - 1)
    def _():
        used = each_expert(lambda pend: None)
        each_tail_block(used, lambda pend: pad_copy(pend, 3).wait())


def _dispatch(counts, dest, h3, n_slots):
    n = h3.shape[0] // ROW_SLABS
    tt = DISPATCH_TILE
    grid_spec = pltpu.PrefetchScalarGridSpec(
        num_scalar_prefetch=1,
        grid=(n // tt,),
        in_specs=[pl.BlockSpec((2 * tt,), lambda i, cnt: (i,), memory_space=pltpu.SMEM),
                  pl.BlockSpec((tt * ROW_SLABS, LANES), lambda i, cnt: (i, 0))],
        out_specs=pl.BlockSpec(memory_space=pl.ANY),
        scratch_shapes=[pltpu.VMEM((EXPERT_BLOCK * ROW_SLABS, LANES), F32), pltpu.SemaphoreType.DMA((4,))],
    )
    return pl.pallas_call(
        _dispatch_body,
        grid_spec=grid_spec,
        out_shape=jax.ShapeDtypeStruct((n_slots * ROW_SLABS, LANES), F32),
        compiler_params=_cparams(("arbitrary",), VMEM_LIMIT),
        name="dispatch",
    )(counts, dest, h3)


def _experts_body(be_ref, nu_ref, xs_ref, w1_ref, w3_ref, w2_ref, ys_ref, w1b, w3b, w2b):
    i = pl.program_id(0)
    n_used = nu_ref[0]

    @pl.when(i < n_used)
    def _():
        changed = (i == 0) | (be_ref[i] != be_ref[jnp.maximum(i - 1, 0)])

        @pl.when(changed)
        def _():
            w1b[...] = w1_ref[0].astype(BF16)
            w3b[...] = w3_ref[0].astype(BF16)
            w2b[...] = w2_ref[0].astype(BF16)

        x = _load_row_tiles(xs_ref).astype(BF16)
        chunks = [slice(c0, c0 + EXPERT_FF_CHUNK) for c0 in range(0, w1b.shape[1], EXPERT_FF_CHUNK)]
        gates = [(jnp.dot(x, w1b[:, cs], preferred_element_type=F32),
                  jnp.dot(x, w3b[:, cs], preferred_element_type=F32)) for cs in chunks]
        mid = jnp.concatenate([(a * jax.nn.sigmoid(a) * u).astype(BF16) for a, u in gates], axis=1)
        rows = mid.shape[0]
        for j0 in range(0, ROW_SLABS, EXPERT_OUT_SLABS):
            cols = slice(j0 * LANES, (j0 + EXPERT_OUT_SLABS) * LANES)
            y = jnp.dot(mid, w2b[:, cols], preferred_element_type=F32)
            for j in range(EXPERT_OUT_SLABS):
                ys_ref[pl.ds(j0 + j, rows, stride=ROW_SLABS), :] = y[:, j * LANES:(j + 1) * LANES]

    @pl.when(i >= n_used)
    def _():
        ys_ref[...] = jnp.zeros_like(ys_ref)


def _experts(block_expert, n_used, xs, w1, w3, w2):
    _, d, ff = w1.shape
    assert d == ROW_SLABS * LANES
    blk = EXPERT_BLOCK
    n_blocks = xs.shape[0] // (blk * ROW_SLABS)

    def slot_map(i, be, nu):
        return (jnp.minimum(i, nu[0] - 1), 0)

    def w_map(i, be, nu):
        return (be[jnp.minimum(i, nu[0] - 1)], 0, 0)

    grid_spec = pltpu.PrefetchScalarGridSpec(
        num_scalar_prefetch=2,
        grid=(n_blocks,),
        in_specs=[pl.BlockSpec((blk * ROW_SLABS, LANES), slot_map),
                  pl.BlockSpec((1, d, ff), w_map), pl.BlockSpec((1, d, ff), w_map),
                  pl.BlockSpec((1, ff, d), w_map)],
        out_specs=pl.BlockSpec((blk * ROW_SLABS, LANES), lambda i, be, nu: (i, 0)),
        scratch_shapes=[pltpu.VMEM((d, ff), BF16), pltpu.VMEM((d, ff), BF16), pltpu.VMEM((ff, d), BF16)],
    )
    return pl.pallas_call(
        _experts_body,
        grid_spec=grid_spec,
        out_shape=jax.ShapeDtypeStruct(xs.shape, F32),
        compiler_params=_cparams(("arbitrary",), VMEM_LIMIT),
        name="experts",
    )(block_expert, n_used, xs, w1, w3, w2)


def _combine_body(dest_ref, dest_next_ref, gate_ref, x_ref, ys_ref, out_ref, buf, sems):
    i = pl.program_id(0)
    n_steps = pl.num_programs(0)
    tt = x_ref.shape[0]

    def issue(dref, slot):
        def tokens(g, carry):
            t0 = pl.multiple_of(g * ROW_DMA_UNROLL, ROW_DMA_UNROLL)
            for u in range(ROW_DMA_UNROLL):
                for k in range(2):
                    _row_copy(ys_ref, dref[2 * (t0 + u) + k], buf.at[slot, k], t0 + u,
                              sems.at[slot, k]).start(priority=k)
            return carry

        lax.fori_loop(0, tt // ROW_DMA_UNROLL, tokens, 0)

    @pl.when(i == 0)
    def _():
        issue(dest_ref, 0)

    @pl.when(i + 1 < n_steps)
    def _():
        issue(dest_next_ref, (i + 1) % 2)

    slot = i % 2
    for k in range(2):
        pltpu.make_async_copy(_tile_rows(ys_ref, 0, tt), buf.at[slot, k], sems.at[slot, k]).wait()
    g = gate_ref[...]
    out_ref[...] = (x_ref[...] + g[:, 0:1] * _load_row_tiles(buf.at[slot, 0])
                    + g[:, 1:2] * _load_row_tiles(buf.at[slot, 1]))


def _combine(dest, gates, x2d, ys):
    n, d = x2d.shape
    tt = COMBINE_TILE
    row = lambda w: pl.BlockSpec((tt, w), lambda i: (i, 0))
    last = n // tt - 1
    return pl.pallas_call(
        _combine_body,
        grid=(n // tt,),
        in_specs=[pl.BlockSpec((2 * tt,), lambda i: (i,), memory_space=pltpu.SMEM),
                  pl.BlockSpec((2 * tt,), lambda i: (jnp.minimum(i + 1, last),), memory_space=pltpu.SMEM),
                  row(LANES), row(d), pl.BlockSpec(memory_space=pl.ANY)],
        out_specs=row(d),
        out_shape=jax.ShapeDtypeStruct((n, d), F32),
        scratch_shapes=[pltpu.VMEM((2, 2, tt * ROW_SLABS, LANES), F32), pltpu.SemaphoreType.DMA((2, 2))],
        compiler_params=_cparams(("arbitrary",), VMEM_LIMIT),
        name="combine",
    )(dest, dest, gates, x2d, ys)


def _moe(x2d, h3, gates, meta, cnt, w1, w3, w2):
    n, d = x2d.shape
    blk = EXPERT_BLOCK
    counts = cnt[0, :N_EXPERTS].astype(I32)
    dest = _slots(meta, cnt)[:, 0:2].reshape(-1)
    n_slots = 2 * n + N_EXPERTS * blk
    n_blocks = n_slots // blk
    pends = jnp.cumsum((counts + blk - 1) // blk * blk)
    block_start = jnp.arange(n_blocks, dtype=I32) * blk
    block_expert = jnp.minimum(jnp.sum((pends[None, :] <= block_start[:, None]).astype(I32), axis=1),
                               N_EXPERTS - 1)
    n_used = (pends[-1:] // blk).astype(I32)
    xs = _dispatch(counts, dest, h3, n_slots)
    ys = _experts(block_expert, n_used, xs, w1, w3, w2)
    return _combine(dest, gates, x2d, ys)


def _layer(x, mem, norm_mix_w, w_in, attn_q_norm_w, attn_k_norm_w, ret_decay_f, ret_decay_b, ret_gn_w, w_out,
           norm_mem_w, norm_memkv_w, w_mq, w_mkv, mem_q_norm_w, mem_k_norm_w, w_mo,
           norm_moe_w, w_group, w_router, w_exp_gate, w_exp_up, w_exp_down):
    b, s, d = x.shape
    x2d = x.reshape(b * s, d)
    q, k, v, q_pl, k_pl, v_pl, rq, rk, rv, rg = _in_proj(x2d, norm_mix_w, w_in, attn_q_norm_w, attn_k_norm_w)
    attn = _attention(q, k, v, q_pl, k_pl, v_pl, b, s)
    ret = _retention(rq, rk, rv, rg, ret_decay_f, ret_decay_b, ret_gn_w, b, s)
    mk, mv = _mem_kv(mem, norm_memkv_w, w_mkv, mem_k_norm_w)
    x2, h3, gates, meta, cnt = _mem_attn_route(x, attn, ret, w_out, mk, mv, norm_mem_w, w_mq, mem_q_norm_w, w_mo,
                                               norm_moe_w, w_group, w_router)
    x3 = _moe(x2.reshape(b * s, d), h3, gates, meta, cnt, w_exp_gate, w_exp_up, w_exp_down)
    return x3.reshape(b, s, d)


def kernel(x, mem, norm_mix_w, w_in, attn_q_norm_w, attn_k_norm_w, ret_decay_f, ret_decay_b, ret_gn_w, w_out,
           norm_mem_w, norm_memkv_w, w_mq, w_mkv, mem_q_norm_w, mem_k_norm_w, w_mo, norm_moe_w, w_group,
           w_router, w_exp_gate, w_exp_up, w_exp_down):
    depth = norm_mix_w.shape[0]
    for l in range(depth):
        x = _layer(x, mem, norm_mix_w[l], w_in[l], attn_q_norm_w[l], attn_k_norm_w[l], ret_decay_f[l],
                   ret_decay_b[l], ret_gn_w[l], w_out[l], norm_mem_w[l], norm_memkv_w[l], w_mq[l], w_mkv[l],
                   mem_q_norm_w[l], mem_k_norm_w[l], w_mo[l], norm_moe_w[l], w_group[l], w_router[l],
                   w_exp_gate[l], w_exp_up[l], w_exp_down[l])
    return x
```

```python
import functools

import numpy as np
import jax
import jax.numpy as jnp
from jax import lax
from jax.experimental import pallas as pl
from jax.experimental.pallas import tpu as pltpu

F32 = jnp.float32
BF16 = jnp.bfloat16
I32 = jnp.int32

NORM_EPS = 1e-6
GN_EPS = 1e-5
NEG_INF = -1e30

ATTN_HEADS = 8
ATTN_HEAD_DIM = 64
ATTN_WIDTH = ATTN_HEADS * ATTN_HEAD_DIM
ATTN_HALF = 64
PLANES = 16
ATTN_TILE = PLANES * ATTN_HALF
RET_HEADS = 4
RET_QK_DIM = 64
RET_V_DIM = 128
RET_QK_WIDTH = RET_HEADS * RET_QK_DIM
RET_V_WIDTH = RET_HEADS * RET_V_DIM
MEM_HEADS = 4
N_GROUPS = 4
EXPERTS_PER_GROUP = 8
N_EXPERTS = N_GROUPS * EXPERTS_PER_GROUP

LANES = 128
ROUTER_SUBTILE = 512
MATMUL_TILE = 1024
ATTN_BLOCKS_PER_STEP = 4
LOG2_E = 1.4426950408889634
RET_CHUNK = 256
RET_CHUNKS_PER_STEP = 8
EXPERT_BLOCK = 512
EXPERT_FF_CHUNK = 256
EXPERT_OUT_SLABS = 2
DISPATCH_TILE = 4096
COMBINE_TILE = 256
ROW_DMA_UNROLL = 16
SLOTS_TILE = 4096
VMEM_LIMIT = 56 * 1024 * 1024


def _cparams(sem, vmem=None):
    return pltpu.CompilerParams(dimension_semantics=sem, vmem_limit_bytes=vmem)


def _split_hi_lo(x):
    hi = x.astype(BF16)
    lo = (x - hi.astype(F32)).astype(BF16)
    return hi, lo


def _rms(x, w):
    ms = jnp.mean(x * x, axis=-1, keepdims=True)
    return x * lax.rsqrt(ms + NORM_EPS) * w


ROW_SLABS = 8


def _store_row_tiles(ref, val):
    rows = val.shape[0]
    for j in range(ROW_SLABS):
        ref[pl.ds(j, rows, stride=ROW_SLABS), :] = val[:, j * LANES:(j + 1) * LANES]


def _load_row_tiles(ref):
    rows = ref.shape[0] // ROW_SLABS
    return jnp.concatenate([ref[pl.ds(j, rows, stride=ROW_SLABS), :] for j in range(ROW_SLABS)], axis=1)


def _tile_rows(ref, row, count=1):
    return ref.at[pl.ds(pl.multiple_of(row * ROW_SLABS, ROW_SLABS), count * ROW_SLABS)]


def _in_proj_body(x_ref, nw_ref, w_ref, qw_ref, kw_ref, g_ref,
                  q_out, k_out, v_out, qp_out, kp_out, vp_out, rq_out, rk_out, rv_out, rg_out, slabs, quarters):
    tm = x_ref.shape[0]
    h = _rms(x_ref[...], nw_ref[...]).astype(BF16)

    def emit(val, nat_out, plane_out):
        nat_out[...] = val.astype(BF16)
        for j in range(ATTN_WIDTH // LANES):
            slabs[j] = val[:, j * LANES:(j + 1) * LANES]
        for j in range(ATTN_WIDTH // LANES):
            for r in range(4):
                quarters[j, r] = slabs[j, pl.ds(r, tm // 4, stride=4), :]
        for r in range(4):
            for q in range(4):
                for j in range(ATTN_WIDTH // LANES):
                    rows = quarters[j, r, pl.ds(q, tm // PLANES, stride=4), :]
                    plane_out[0, r + 4 * q, :, j * LANES:(j + 1) * LANES] = rows.astype(BF16)

    def proj(a, b):
        return jnp.dot(h, w_ref[:, a:b], preferred_element_type=F32)

    def head_norm(p, w):
        s = jnp.dot((p * p).astype(BF16), g_ref[...], preferred_element_type=F32)
        return p * lax.rsqrt(s * (1.0 / ATTN_HEAD_DIM) + NORM_EPS) * w

    a = ATTN_WIDTH
    emit(head_norm(proj(0, a), qw_ref[...]), q_out, qp_out)
    emit(head_norm(proj(a, 2 * a), kw_ref[...]), k_out, kp_out)
    emit(proj(2 * a, 3 * a), v_out, vp_out)
    c = 3 * a
    rq_out[...] = proj(c, c + RET_QK_WIDTH).astype(BF16)
    rk_out[...] = proj(c + RET_QK_WIDTH, c + 2 * RET_QK_WIDTH).astype(BF16)
    c += 2 * RET_QK_WIDTH
    rv_out[...] = proj(c, c + RET_V_WIDTH).astype(BF16)
    rg_out[...] = proj(c + RET_V_WIDTH, c + 2 * RET_V_WIDTH).astype(BF16)


def _in_proj(x2d, norm_w, w_in, q_norm_w, k_norm_w):
    n, d = x2d.shape
    cols = w_in.shape[1]
    tm = MATMUL_TILE
    qw = (jnp.tile(q_norm_w, ATTN_HEADS) * (ATTN_HEAD_DIM ** -0.5 * LOG2_E)).reshape(1, ATTN_WIDTH)
    kw = jnp.tile(k_norm_w, ATTN_HEADS).reshape(1, ATTN_WIDTH)
    head_of = np.arange(ATTN_WIDTH) // ATTN_HEAD_DIM
    gmat = jnp.asarray(head_of[:, None] == head_of[None, :], dtype=BF16)
    row = lambda w: pl.BlockSpec((tm, w), lambda i: (i, 0))
    full = lambda r, c: pl.BlockSpec((r, c), lambda i: (0, 0))
    per_tile = ATTN_TILE // tm
    plane = pl.BlockSpec((1, PLANES, tm // PLANES, ATTN_WIDTH), lambda i: (i // per_tile, 0, i % per_tile, 0))
    plane_shape = jax.ShapeDtypeStruct((n // ATTN_TILE, PLANES, ATTN_TILE // PLANES, ATTN_WIDTH), BF16)
    ret_widths = (RET_QK_WIDTH, RET_QK_WIDTH, RET_V_WIDTH, RET_V_WIDTH)
    return pl.pallas_call(
        _in_proj_body,
        grid=(n // tm,),
        in_specs=[row(d), full(1, d), full(d, cols), full(1, ATTN_WIDTH), full(1, ATTN_WIDTH),
                  full(ATTN_WIDTH, ATTN_WIDTH)],
        out_specs=[row(ATTN_WIDTH)] * 3 + [plane] * 3 + [row(w) for w in ret_widths],
        out_shape=([jax.ShapeDtypeStruct((n, ATTN_WIDTH), BF16)] * 3 + [plane_shape] * 3
                   + [jax.ShapeDtypeStruct((n, w), BF16) for w in ret_widths]),
        scratch_shapes=[pltpu.VMEM((ATTN_WIDTH // LANES, tm, LANES), F32),
                        pltpu.VMEM((ATTN_WIDTH // LANES, 4, tm // 4, LANES), F32)],
        compiler_params=_cparams(("parallel",), VMEM_LIMIT),
        name="in_proj",
    )(x2d, norm_w.reshape(1, d), w_in.astype(BF16), qw, kw, gmat)


def _attn_bias(dilation, interleave=1):
    half = ATTN_HALF
    idx = np.arange(half)
    sub = (idx % (half // interleave)) * interleave + idx // (half // interleave)
    qi = sub[:, None]
    kc = (np.arange(3)[:, None] * half + sub[None, :]).reshape(1, -1)
    seg = np.repeat(np.arange(3), half)[None, :]
    delta = kc - half - qi
    band = np.abs(delta) <= half
    slopes = np.exp2(-8.0 * np.arange(1, ATTN_HEADS + 1) / ATTN_HEADS)
    valid = [band & (seg >= 1), band, band & (seg <= 1)]
    out = np.empty((ATTN_HEADS // 2, 3, 2 * half, 3 * half), np.float32)
    for p in range(ATTN_HEADS // 2):
        for v in range(3):
            for s in range(2):
                b = -slopes[2 * p + s] * LOG2_E * (dilation * np.abs(delta)).astype(np.float32)
                out[p, v, s * half:(s + 1) * half] = np.where(valid[v], b, NEG_INF)
    return jnp.asarray(out.reshape(-1, 2 * half, 3 * half))


def _attn_scores(chains):
    lo_lanes = lax.broadcasted_iota(I32, (ATTN_HALF, LANES), 1) < ATTN_HEAD_DIM
    scores = []
    for q, k, v, bias in chains:
        zero = jnp.zeros_like(q)
        qs = jnp.concatenate([jnp.where(lo_lanes, q, zero), jnp.where(lo_lanes, zero, q)], axis=0)
        scores.append(lax.dot_general(qs, k, (((1,), (1,)), ((), ())), preferred_element_type=F32) + bias)
    return scores


def _attn_finish(chains, scores):
    half = ATTN_HALF
    lo_lanes = lax.broadcasted_iota(I32, (half, LANES), 1) < ATTN_HEAD_DIM
    ones = jnp.ones((3 * half, LANES), BF16)
    probs, maxes = [], []
    for s in scores:
        m = jnp.max(s, axis=-1, keepdims=True)
        probs.append(jnp.exp2(s - m).astype(BF16))
        maxes.append(m)
    results = []
    for (q, k, v, bias), e in zip(chains, probs):
        vext = jnp.concatenate([v, ones], axis=1)
        results.append(jnp.dot(e, vext, preferred_element_type=F32))
    outs = []
    for r, m in zip(results, maxes):
        acc = jnp.where(lo_lanes, r[:half, :LANES], r[half:, :LANES])
        denom = jnp.where(lo_lanes, r[:half, LANES:], r[half:, LANES:])
        mm = jnp.where(lo_lanes, m[:half], m[half:])
        outs.append((acc / denom, mm + jnp.log2(denom)))
    return outs


def _attend_groups(groups):
    pending = None
    for chains, consume in groups:
        scores = _attn_scores(chains)
        if pending is not None:
            p_chains, p_scores, p_consume = pending
            p_consume(_attn_finish(p_chains, p_scores))
        pending = (chains, scores, consume)
    p_chains, p_scores, p_consume = pending
    p_consume(_attn_finish(p_chains, p_scores))


def _merge(oa, la, ob, lb):
    m = jnp.maximum(la, lb)
    ea = jnp.exp2(la - m)
    eb = jnp.exp2(lb - m)
    den = ea + eb
    return (ea * oa + eb * ob) / den, m + jnp.log2(den)


def _attn_body(q_ref, kp_ref, km_ref, kn_ref, vp_ref, vm_ref, vn_ref,
               qpl_ref, kplp_ref, kplm_ref, kpln_ref, vplp_ref, vplm_ref, vpln_ref,
               b1_ref, b4_ref, b16_ref, o_ref, kcat, vcat, o_far, l_far):
    half = ATTN_HALF
    t = pl.program_id(1)
    n_tiles = pl.num_programs(1)
    tile = q_ref.shape[1]
    pairs = ATTN_HEADS // 2
    piece = half // 4

    kcat[0:half] = kp_ref[0]
    kcat[half:half + tile] = km_ref[0]
    kcat[half + tile:] = kn_ref[0]
    vcat[0:half] = vp_ref[0]
    vcat[half:half + tile] = vm_ref[0]
    vcat[half + tile:] = vn_ref[0]

    def edge(first, last):
        return jnp.where(first, 0, jnp.where(last, 2, 1))

    groups = []

    def far_groups(r):
        v16 = edge(t == 0, t == n_tiles - 1)
        for pr in range(pairs):
            cs = slice(pr * LANES, (pr + 1) * LANES)
            chains = []
            for m in range(4):
                c = r + 4 * m
                k3 = jnp.concatenate([kplp_ref[0, c, :, cs], kplm_ref[0, c, :, cs], kpln_ref[0, c, :, cs]], axis=0)
                v3 = jnp.concatenate([vplp_ref[0, c, :, cs], vplm_ref[0, c, :, cs], vpln_ref[0, c, :, cs]], axis=0)
                chains.append((qpl_ref[0, c, :, cs], k3, v3, b16_ref[pr * 3 + v16]))
            for nb in range(4):
                def rows(main, before, after, a0):
                    ref, lo = (before, a0 + half) if a0 < 0 else (after, a0 - half) if a0 >= half else (main, a0)
                    return [ref[0, r + 4 * m, lo:lo + piece, cs] for m in range(4)]
                a0 = nb * piece
                q4 = jnp.concatenate(rows(qpl_ref, None, None, a0), axis=0)
                k4 = jnp.concatenate(sum((rows(kplm_ref, kplp_ref, kpln_ref, a0 + d) for d in (-piece, 0, piece)), []),
                                     axis=0)
                v4 = jnp.concatenate(sum((rows(vplm_ref, vplp_ref, vpln_ref, a0 + d) for d in (-piece, 0, piece)), []),
                                     axis=0)
                v4e = edge((t == 0) & (nb == 0), (t == n_tiles - 1) & (nb == 3))
                chains.append((q4, k4, v4, b4_ref[pr * 3 + v4e]))

            def consume(res, r=r, pr=pr):
                for m in range(4):
                    o16, l16 = res[m]
                    o_rows, l_rows = [], []
                    for nb in range(4):
                        o4, l4 = res[4 + nb]
                        sl = slice(nb * piece, (nb + 1) * piece)
                        s4 = slice(m * piece, (m + 1) * piece)
                        om, lm = _merge(o16[sl], l16[sl], o4[s4], l4[s4])
                        o_rows.append(om)
                        l_rows.append(lm)
                    dst = pl.ds(r + 4 * m, half, stride=PLANES)
                    o_far[pr, dst, :] = jnp.concatenate(o_rows, axis=0)
                    l_far[pr, dst, :] = jnp.concatenate(l_rows, axis=0)

            groups.append((chains, consume))

    for r in range(4):
        far_groups(r)

    n_blocks = n_tiles * (tile // half)
    for it in range(tile // half // ATTN_BLOCKS_PER_STEP):
        chains, where = [], []
        for u in range(ATTN_BLOCKS_PER_STEP):
            jb = it * ATTN_BLOCKS_PER_STEP + u
            r0 = jb * half
            gb = t * (tile // half) + jb
            variant = edge(gb == 0, gb == n_blocks - 1)
            for pr in range(pairs):
                cs = slice(pr * LANES, (pr + 1) * LANES)
                chains.append((q_ref[0, pl.ds(r0, half), cs], kcat[pl.ds(r0, 3 * half), cs],
                               vcat[pl.ds(r0, 3 * half), cs], b1_ref[pr * 3 + variant]))
                where.append((r0, pr, cs))

        def consume(res, where=where):
            for (r0, pr, cs), (o1, l1) in zip(where, res):
                o, _ = _merge(o1, l1, o_far[pr, pl.ds(r0, half), :], l_far[pr, pl.ds(r0, half), :])
                o_ref[0, pl.ds(r0, half), cs] = o.astype(o_ref.dtype)

        groups.append((chains, consume))

    _attend_groups(groups)


def _attention(q, k, v, q_pl, k_pl, v_pl, b, s):
    w = ATTN_WIDTH
    half = ATTN_HALF
    tile = ATTN_TILE
    n_tiles = s // tile
    hb = tile // half
    last = s // half - 1
    main = pl.BlockSpec((1, tile, w), lambda bb, t: (bb, t, 0))
    prev = pl.BlockSpec((1, half, w), lambda bb, t: (bb, jnp.maximum(t * hb - 1, 0), 0))
    nxt = pl.BlockSpec((1, half, w), lambda bb, t: (bb, jnp.minimum((t + 1) * hb, last), 0))
    pshape = (1, PLANES, tile // PLANES, w)
    pl_main = pl.BlockSpec(pshape, lambda bb, t: (bb * n_tiles + t, 0, 0, 0))
    pl_prev = pl.BlockSpec(pshape, lambda bb, t: (bb * n_tiles + jnp.maximum(t - 1, 0), 0, 0, 0))
    pl_next = pl.BlockSpec(pshape, lambda bb, t: (bb * n_tiles + jnp.minimum(t + 1, n_tiles - 1), 0, 0, 0))
    biases = [_attn_bias(1), _attn_bias(4, interleave=4), _attn_bias(16)]
    bias_spec = pl.BlockSpec(biases[0].shape, lambda bb, t: (0, 0, 0))
    nat = lambda a: a.reshape(b, s, w)
    out = pl.pallas_call(
        _attn_body,
        grid=(b, n_tiles),
        in_specs=[main, prev, main, nxt, prev, main, nxt,
                  pl_main, pl_prev, pl_main, pl_next, pl_prev, pl_main, pl_next,
                  bias_spec, bias_spec, bias_spec],
        out_specs=main,
        out_shape=jax.ShapeDtypeStruct((b, s, w), BF16),
        scratch_shapes=[pltpu.VMEM((tile + 2 * half, w), BF16), pltpu.VMEM((tile + 2 * half, w), BF16),
                        pltpu.VMEM((ATTN_HEADS // 2, tile, LANES), F32),
                        pltpu.VMEM((ATTN_HEADS // 2, tile, LANES), F32)],
        compiler_params=_cparams(("parallel", "parallel"), VMEM_LIMIT),
        name="attention",
    )(nat(q), nat(k), nat(k), nat(k), nat(v), nat(v), nat(v),
      q_pl, k_pl, k_pl, k_pl, v_pl, v_pl, v_pl, *biases)
    return out.reshape(b * s, w)


def _ret_body(lg_ref, q_ref, k_ref, v_ref, g_ref, gnw_ref, out_ref,
              dmat, qdec, kdec, cdec, fstate, rstate, rall, *, chunk):
    c = chunk
    b = pl.program_id(0)
    ph = pl.program_id(1)
    n = pl.program_id(2)
    n_chunks = pl.num_programs(2)
    k_scale = RET_QK_DIM ** -0.5
    pairs = RET_HEADS // 2
    pw = 2 * RET_QK_DIM
    vw = 2 * RET_V_DIM
    first = lax.broadcasted_iota(I32, (c, pw), 1) < RET_QK_DIM
    diag = ((lax.broadcasted_iota(I32, (pw, vw), 0) < RET_QK_DIM)
            == (lax.broadcasted_iota(I32, (pw, vw), 1) < RET_V_DIM))

    @pl.when((b == 0) & (ph == 0) & (n == 0))
    def _init_tables():
        ii = lax.broadcasted_iota(I32, (c, c), 0)
        jj = lax.broadcasted_iota(I32, (c, c), 1)
        fwd = (ii - jj).astype(F32)
        ri = lax.broadcasted_iota(I32, (c, pw), 0).astype(F32)
        top = lax.broadcasted_iota(I32, (pw, vw), 0) < RET_QK_DIM
        for h in range(RET_HEADS):
            dmat[h] = jnp.where(ii >= jj, jnp.exp(lg_ref[0, h] * fwd), jnp.exp(-lg_ref[1, h] * fwd)) * k_scale
        for pr in range(pairs):
            lf = jnp.where(first, lg_ref[0, 2 * pr], lg_ref[0, 2 * pr + 1])
            lb = jnp.where(first, lg_ref[1, 2 * pr], lg_ref[1, 2 * pr + 1])
            qdec[0, pr] = jnp.exp(lf * (ri + 1.0))
            qdec[1, pr] = jnp.exp(lb * (c - ri))
            kdec[0, pr] = jnp.exp(lf * (c - 1.0 - ri)) * k_scale
            kdec[1, pr] = jnp.exp(lb * ri) * k_scale
            for direction in range(2):
                cdec[direction, pr] = jnp.exp(jnp.where(top, lg_ref[direction, 2 * pr],
                                                        lg_ref[direction, 2 * pr + 1]) * c)

    per_step = q_ref.shape[1] // c

    def kv_update(direction, pr, rows):
        kp = k_ref[0, rows, pr * pw:(pr + 1) * pw].astype(F32)
        ks = (kp * kdec[direction, pr]).astype(BF16)
        new = lax.dot_general(ks, v_ref[0, rows, pr * vw:(pr + 1) * vw], (((0,), (0,)), ((), ())),
                              preferred_element_type=F32)
        return jnp.where(diag, new, 0.0)

    @pl.when(ph == 0)
    def _right_to_left():
        @pl.when(n == 0)
        def _():
            rstate[...] = jnp.zeros_like(rstate)

        for sub in reversed(range(per_step)):
            rows = slice(sub * c, (sub + 1) * c)
            ci = (n_chunks - 1 - n) * per_step + sub
            for pr in range(pairs):
                st = rstate[pr]
                rall[ci, pr] = st
                rstate[pr] = st * cdec[1, pr] + kv_update(1, pr, rows)

    @pl.when(ph == 1)
    def _left_to_right():
        @pl.when(n == 0)
        def _():
            fstate[...] = jnp.zeros_like(fstate)

        for sub in range(per_step):
            rows = slice(sub * c, (sub + 1) * c)
            for pr in range(pairs):
                qp = q_ref[0, rows, pr * pw:(pr + 1) * pw]
                kp = k_ref[0, rows, pr * pw:(pr + 1) * pw]
                qf = qp.astype(F32)
                qcat = jnp.concatenate([(qf * qdec[0, pr]).astype(BF16), (qf * qdec[1, pr]).astype(BF16)], axis=1)
                st = fstate[pr]
                states = jnp.concatenate([st, rall[n * per_step + sub, pr]], axis=0).astype(BF16)
                cross = jnp.dot(qcat, states, preferred_element_type=F32)
                fstate[pr] = st * cdec[0, pr] + kv_update(0, pr, rows)
                zero = jnp.zeros_like(qp)
                for a in range(2):
                    h = 2 * pr + a
                    vs = slice(h * RET_V_DIM, (h + 1) * RET_V_DIM)
                    qm = jnp.where(first, qp, zero) if a == 0 else jnp.where(first, zero, qp)
                    s = lax.dot_general(qm, kp, (((1,), (1,)), ((), ())), preferred_element_type=F32) * dmat[h]
                    y = jnp.dot(s.astype(BF16), v_ref[0, rows, vs], preferred_element_type=F32)
                    y = y + cross[:, a * RET_V_DIM:(a + 1) * RET_V_DIM]
                    mu = jnp.mean(y, axis=-1, keepdims=True)
                    yc = y - mu
                    var = jnp.mean(yc * yc, axis=-1, keepdims=True)
                    yn = yc * lax.rsqrt(var + GN_EPS) * gnw_ref[:, vs]
                    gate = g_ref[0, rows, vs].astype(F32)
                    out_ref[0, rows, vs] = (gate * jax.nn.sigmoid(gate) * yn).astype(out_ref.dtype)


def _retention(rq, rk, rv, rg, decay_f, decay_b, gn_w, b, s):
    c = RET_CHUNK
    rows = RET_CHUNK * RET_CHUNKS_PER_STEP
    nb = s // rows
    lg = jnp.stack([jax.nn.log_sigmoid(decay_f.astype(F32)), jax.nn.log_sigmoid(decay_b.astype(F32))])
    qk_w, v_w = RET_QK_WIDTH, RET_V_WIDTH
    pairs, pair_qk, pair_v = RET_HEADS // 2, 2 * RET_QK_DIM, 2 * RET_V_DIM

    def both(bb, ph, n):
        return (bb, jnp.where(ph == 0, nb - 1 - n, n), 0)

    def fwd_only(bb, ph, n):
        return (bb, jnp.where(ph == 0, 0, n), 0)

    out = pl.pallas_call(
        functools.partial(_ret_body, chunk=c),
        grid=(b, 2, nb),
        in_specs=[pl.BlockSpec(memory_space=pltpu.SMEM),
                  pl.BlockSpec((1, rows, qk_w), fwd_only),
                  pl.BlockSpec((1, rows, qk_w), both),
                  pl.BlockSpec((1, rows, v_w), both),
                  pl.BlockSpec((1, rows, v_w), fwd_only),
                  pl.BlockSpec((1, v_w), lambda bb, ph, n: (0, 0))],
        out_specs=pl.BlockSpec((1, rows, v_w), fwd_only),
        out_shape=jax.ShapeDtypeStruct((b, s, v_w), BF16),
        scratch_shapes=[pltpu.VMEM((RET_HEADS, c, c), F32),
                        pltpu.VMEM((2, pairs, c, pair_qk), F32),
                        pltpu.VMEM((2, pairs, c, pair_qk), F32),
                        pltpu.VMEM((2, pairs, pair_qk, pair_v), F32),
                        pltpu.VMEM((pairs, pair_qk, pair_v), F32),
                        pltpu.VMEM((pairs, pair_qk, pair_v), F32),
                        pltpu.VMEM((s // c, pairs, pair_qk, pair_v), F32)],
        compiler_params=_cparams(("arbitrary", "arbitrary", "arbitrary"), VMEM_LIMIT),
        name="retention",
    )(lg, rq.reshape(b, s, qk_w), rk.reshape(b, s, qk_w), rv.reshape(b, s, v_w), rg.reshape(b, s, v_w),
      gn_w.reshape(1, v_w))
    return out.reshape(b * s, v_w)


def _mem_kv_body(mem_ref, nw_ref, w_ref, kw_ref, k_out, v_out):
    d = mem_ref.shape[-1]
    hd = d // MEM_HEADS
    h = _rms(mem_ref[0], nw_ref[...]).astype(BF16)
    kv = jnp.dot(h, w_ref[...], preferred_element_type=F32)
    for i in range(MEM_HEADS):
        k_out[0, :, i * hd:(i + 1) * hd] = _rms(kv[:, i * hd:(i + 1) * hd], kw_ref[...]).astype(BF16)
    v_out[0] = kv[:, d:].astype(BF16)


def _mem_kv(mem, norm_w, w_mkv, k_norm_w):
    b, m, d = mem.shape
    return pl.pallas_call(
        _mem_kv_body,
        grid=(b,),
        in_specs=[pl.BlockSpec((1, m, d), lambda i: (i, 0, 0)),
                  pl.BlockSpec((1, d), lambda i: (0, 0)),
                  pl.BlockSpec((d, 2 * d), lambda i: (0, 0)),
                  pl.BlockSpec((1, d // MEM_HEADS), lambda i: (0, 0))],
        out_specs=[pl.BlockSpec((1, m, d), lambda i: (i, 0, 0))] * 2,
        out_shape=[jax.ShapeDtypeStruct((b, m, d), BF16)] * 2,
        compiler_params=_cparams(("parallel",), VMEM_LIMIT),
        name="mem_kv",
    )(mem, norm_w.reshape(1, d), w_mkv.astype(BF16), k_norm_w.reshape(1, -1))


def _mem_attn_body(x_ref, attn_ref, ret_ref, wout_ref, nw_ref, wq_ref, qw_ref, k_ref, v_ref, wo_ref,
                   rnw_ref, rw_ref, out_ref, h_out, gate_out, meta_out, cnt_out, carry, below):
    d = x_ref.shape[-1]
    hd = d // MEM_HEADS
    tm = x_ref.shape[1]
    sub = below.shape[0]

    @pl.when((pl.program_id(0) == 0) & (pl.program_id(1) == 0))
    def _():
        carry[...] = jnp.zeros_like(carry)
        rr = lax.broadcasted_iota(I32, (sub, sub), 0)
        cc = lax.broadcasted_iota(I32, (sub, sub), 1)
        below[...] = (cc < rr).astype(F32).astype(BF16)

    x = (x_ref[0] + jnp.dot(attn_ref[0], wout_ref[:ATTN_WIDTH], preferred_element_type=F32)
         + jnp.dot(ret_ref[0], wout_ref[ATTN_WIDTH:], preferred_element_type=F32))
    h = _rms(x, nw_ref[...]).astype(BF16)
    q = jnp.dot(h, wq_ref[...], preferred_element_type=F32)
    heads = []
    for i in range(MEM_HEADS):
        cs = slice(i * hd, (i + 1) * hd)
        qn = _rms(q[:, cs], qw_ref[...]).astype(BF16)
        s = lax.dot_general(qn, k_ref[0, :, cs], (((1,), (1,)), ((), ())), preferred_element_type=F32)
        e = jnp.exp(s - jnp.max(s, axis=-1, keepdims=True))
        o = jnp.dot(e.astype(BF16), v_ref[0, :, cs], preferred_element_type=F32)
        heads.append((o / jnp.sum(e, axis=-1, keepdims=True)).astype(BF16))
    o = jnp.concatenate(heads, axis=1)
    x2 = x + jnp.dot(o, wo_ref[...], preferred_element_type=F32)
    out_ref[0] = x2

    base = carry[...]
    for part in range(tm // sub):
        rows = slice(part * sub, (part + 1) * sub)
        base = _route_rows(x2[rows], rnw_ref, rw_ref, below, base,
                           h_out.at[pl.ds(part * sub * ROW_SLABS, sub * ROW_SLABS)],
                           gate_out.at[rows], meta_out.at[rows])
    carry[...] = base
    cnt_out[...] = base


def _mem_attn_route(x, attn, ret, w_out, mk, mv, norm_w, w_mq, q_norm_w, w_mo, moe_norm_w, w_group, w_router):
    b, s, d = x.shape
    m = mk.shape[1]
    tm = MATMUL_TILE
    steps = s // tm
    hd = d // MEM_HEADS
    qw = (q_norm_w * (hd ** -0.5)).reshape(1, hd)
    w_all = jnp.concatenate([w_group, w_router.transpose(1, 0, 2).reshape(d, N_EXPERTS)], axis=1)
    w_all = jnp.pad(w_all, ((0, 0), (0, LANES - w_all.shape[1])))
    whi = w_all.astype(BF16)
    w_split = jnp.concatenate([whi, (w_all - whi.astype(F32)).astype(BF16)], axis=1)
    tok = lambda w: pl.BlockSpec((1, tm, w), lambda bb, i: (bb, i, 0))
    flat = lambda r, w: pl.BlockSpec((r, w), lambda bb, i: (bb * steps + i, 0))
    const = lambda r, c: pl.BlockSpec((r, c), lambda bb, i: (0, 0))
    mem = pl.BlockSpec((1, m, d), lambda bb, i: (bb, 0, 0))
    n = b * s
    return pl.pallas_call(
        _mem_attn_body,
        grid=(b, steps),
        in_specs=[tok(d), tok(ATTN_WIDTH), tok(RET_V_WIDTH), const(ATTN_WIDTH + RET_V_WIDTH, d),
                  const(1, d), const(d, d), const(1, hd), mem, mem, const(d, d),
                  const(1, d), const(d, 2 * LANES)],
        out_specs=[tok(d), flat(tm * ROW_SLABS, LANES), flat(tm, LANES), flat(tm, LANES), const(1, LANES)],
        out_shape=[jax.ShapeDtypeStruct((b, s, d), F32), jax.ShapeDtypeStruct((n * ROW_SLABS, LANES), F32),
                   jax.ShapeDtypeStruct((n, LANES), F32), jax.ShapeDtypeStruct((n, LANES), F32),
                   jax.ShapeDtypeStruct((1, LANES), F32)],
        scratch_shapes=[pltpu.VMEM((1, LANES), F32), pltpu.VMEM((ROUTER_SUBTILE, ROUTER_SUBTILE), BF16)],
        compiler_params=_cparams(("arbitrary", "arbitrary"), VMEM_LIMIT),
        name="mem_attn_route",
    )(x, attn.reshape(b, s, ATTN_WIDTH), ret.reshape(b, s, RET_V_WIDTH), w_out.astype(BF16),
      norm_w.reshape(1, d), w_mq.astype(BF16), qw, mk, mv, w_mo.astype(BF16),
      moe_norm_w.reshape(1, d), w_split)


def _route_rows(x, nw_ref, w_ref, below, base, h_out, gate_out, meta_out):
    tm = x.shape[0]
    h = _rms(x, nw_ref[...])
    _store_row_tiles(h_out, h)
    hi, lo = _split_hi_lo(h)
    both = jnp.dot(hi, w_ref[...], preferred_element_type=F32)
    logits = (both[:, :LANES] + both[:, LANES:]
              + jnp.dot(lo, w_ref[:, :LANES], preferred_element_type=F32))
    lane = lax.broadcasted_iota(I32, (tm, LANES), 1)
    lane_f = lane.astype(F32)

    def argmax_lanes(vals):
        top = jnp.max(vals, axis=-1, keepdims=True)
        idx = jnp.min(jnp.where(vals == top, lane_f, float(LANES)), axis=-1, keepdims=True)
        return top, idx

    is_group = lane < N_GROUPS
    g_top, grp = argmax_lanes(jnp.where(is_group, logits, -jnp.inf))
    g_gate = 1.0 / jnp.sum(jnp.where(is_group, jnp.exp(logits - g_top), 0.0), axis=-1, keepdims=True)
    first = N_GROUPS + EXPERTS_PER_GROUP * grp
    in_group = (lane_f >= first) & (lane_f < first + EXPERTS_PER_GROUP)
    el = jnp.where(in_group, logits, -jnp.inf)
    t1, i1 = argmax_lanes(el)
    t2, i2 = argmax_lanes(jnp.where(lane_f == i1, -jnp.inf, el))
    z = jnp.exp(t2 - t1)
    g1 = g_gate / (1.0 + z)
    g2 = g_gate * z / (1.0 + z)
    e1 = i1 - N_GROUPS
    e2 = i2 - N_GROUPS

    oh1 = (lane_f == e1)
    oh2 = (lane_f == e2)
    oh1b = oh1.astype(F32).astype(BF16)
    oh2b = oh2.astype(F32).astype(BF16)
    pre = jnp.dot(below[...], jnp.concatenate([oh1b, oh2b], axis=1), preferred_element_type=F32)
    pre1, pre2 = pre[:, :LANES], pre[:, LANES:]
    cnt1 = jnp.sum(oh1.astype(F32), axis=0, keepdims=True)
    cnt2 = jnp.sum(oh2.astype(F32), axis=0, keepdims=True)
    r1 = jnp.sum(jnp.where(oh1, pre1 + base, 0.0), axis=-1, keepdims=True)
    r2 = jnp.sum(jnp.where(oh2, pre2 + base + cnt1, 0.0), axis=-1, keepdims=True)

    gate_out[...] = jnp.where(lane == 0, g1, jnp.where(lane == 1, g2, 0.0))
    meta_out[...] = jnp.where(oh1, r1 + 1.0, jnp.where(oh2, -(r2 + 1.0), 0.0))
    return base + cnt1 + cnt2


def _row_copy(src_ref, src_row, dst_ref, dst_row, sem):
    return pltpu.make_async_copy(_tile_rows(src_ref, src_row), _tile_rows(dst_ref, dst_row), sem)


def _slots_body(meta_ref, cnt_ref, dest_ref):
    tm = meta_ref.shape[0]
    blk = float(EXPERT_BLOCK)
    lane = lax.broadcasted_iota(I32, (tm, LANES), 1)
    blocks = jnp.floor((cnt_ref[...] + (blk - 1.0)) * (1.0 / blk))
    rr = lax.broadcasted_iota(I32, (LANES, LANES), 0)
    cc = lax.broadcasted_iota(I32, (LANES, LANES), 1)
    before = (rr < cc).astype(F32).astype(BF16)
    first_block = jnp.dot(jnp.broadcast_to(blocks, (8, LANES)).astype(BF16), before,
                          preferred_element_type=F32)[0:1]
    start = first_block * blk
    meta = meta_ref[...]
    slot0 = jnp.sum(jnp.where(meta > 0.0, start + meta - 1.0, 0.0), axis=-1, keepdims=True)
    slot1 = jnp.sum(jnp.where(meta < 0.0, start - meta - 1.0, 0.0), axis=-1, keepdims=True)
    dest_ref[...] = jnp.where(lane == 0, slot0, jnp.where(lane == 1, slot1, 0.0)).astype(I32)


def _slots(meta, cnt):
    n = meta.shape[0]
    tm = min(SLOTS_TILE, n)
    row = pl.BlockSpec((tm, LANES), lambda i: (i, 0))
    return pl.pallas_call(
        _slots_body,
        grid=(n // tm,),
        in_specs=[row, pl.BlockSpec((1, LANES), lambda i: (0, 0))],
        out_specs=row,
        out_shape=jax.ShapeDtypeStruct((n, LANES), I32),
        compiler_params=_cparams(("parallel",), VMEM_LIMIT),
        name="slots",
    )(meta, cnt)


def _dispatch_body(cnt_ref, dest_ref, h_ref, xs_ref, zbuf, sems):
    i = pl.program_id(0)
    tt = h_ref.shape[0] // ROW_SLABS
    blk = EXPERT_BLOCK
    n_slots = xs_ref.shape[0] // ROW_SLABS

    def pad_copy(pend):
        return pltpu.make_async_copy(zbuf, _tile_rows(xs_ref, pend - blk, blk), sems.at[2])

    @pl.when(i == 0)
    def _():
        zbuf[...] = jnp.zeros_like(zbuf)

        def start(e, acc):
            pend = acc + ((cnt_ref[e] + (blk - 1)) // blk) * blk

            @pl.when(pend > acc)
            def _():
                pad_copy(pend).start()
            return pend

        used = lax.fori_loop(0, N_EXPERTS, start, jnp.int32(0))

        def tail_start(j, carry):
            pad_copy((j + 1) * blk).start()
            return carry

        lax.fori_loop(used // blk, n_slots // blk, tail_start, 0)

        def finish(e, acc):
            pend = acc + ((cnt_ref[e] + (blk - 1)) // blk) * blk

            @pl.when(pend > acc)
            def _():
                pad_copy(pend).wait()
            return pend

        lax.fori_loop(0, N_EXPERTS, finish, jnp.int32(0))

        def tail_wait(j, carry):
            pad_copy((j + 1) * blk).wait()
            return carry

        lax.fori_loop(used // blk, n_slots // blk, tail_wait, 0)

    def tokens(g, carry):
        t0 = pl.multiple_of(g * ROW_DMA_UNROLL, ROW_DMA_UNROLL)
        for u in range(ROW_DMA_UNROLL):
            for k in range(2):
                _row_copy(h_ref, t0 + u, xs_ref, dest_ref[2 * (t0 + u) + k], sems.at[k]).start(priority=k)
        return carry

    lax.fori_loop(0, tt // ROW_DMA_UNROLL, tokens, 0)
    for k in range(2):
        pltpu.make_async_copy(h_ref, _tile_rows(xs_ref, 0, tt), sems.at[k]).wait()


def _dispatch(counts, dest, h3, n_slots):
    n = h3.shape[0] // ROW_SLABS
    tt = DISPATCH_TILE
    grid_spec = pltpu.PrefetchScalarGridSpec(
        num_scalar_prefetch=1,
        grid=(n // tt,),
        in_specs=[pl.BlockSpec((2 * tt,), lambda i, cnt: (i,), memory_space=pltpu.SMEM),
                  pl.BlockSpec((tt * ROW_SLABS, LANES), lambda i, cnt: (i, 0))],
        out_specs=pl.BlockSpec(memory_space=pl.ANY),
        scratch_shapes=[pltpu.VMEM((EXPERT_BLOCK * ROW_SLABS, LANES), F32), pltpu.SemaphoreType.DMA((3,))],
    )
    return pl.pallas_call(
        _dispatch_body,
        grid_spec=grid_spec,
        out_shape=jax.ShapeDtypeStruct((n_slots * ROW_SLABS, LANES), F32),
        compiler_params=_cparams(("arbitrary",), VMEM_LIMIT),
        name="dispatch",
    )(counts, dest, h3)


def _experts_body(be_ref, nu_ref, xs_ref, w1_ref, w3_ref, w2_ref, ys_ref, w1b, w3b, w2b):
    i = pl.program_id(0)
    n_used = nu_ref[0]

    @pl.when(i < n_used)
    def _():
        changed = (i == 0) | (be_ref[i] != be_ref[jnp.maximum(i - 1, 0)])

        @pl.when(changed)
        def _():
            w1b[...] = w1_ref[0].astype(BF16)
            w3b[...] = w3_ref[0].astype(BF16)
            w2b[...] = w2_ref[0].astype(BF16)

        x = _load_row_tiles(xs_ref).astype(BF16)
        chunks = [slice(c0, c0 + EXPERT_FF_CHUNK) for c0 in range(0, w1b.shape[1], EXPERT_FF_CHUNK)]
        gates = [(jnp.dot(x, w1b[:, cs], preferred_element_type=F32),
                  jnp.dot(x, w3b[:, cs], preferred_element_type=F32)) for cs in chunks]
        mid = jnp.concatenate([(a * jax.nn.sigmoid(a) * u).astype(BF16) for a, u in gates], axis=1)
        rows = mid.shape[0]
        for j0 in range(0, ROW_SLABS, EXPERT_OUT_SLABS):
            cols = slice(j0 * LANES, (j0 + EXPERT_OUT_SLABS) * LANES)
            y = jnp.dot(mid, w2b[:, cols], preferred_element_type=F32)
            for j in range(EXPERT_OUT_SLABS):
                ys_ref[pl.ds(j0 + j, rows, stride=ROW_SLABS), :] = y[:, j * LANES:(j + 1) * LANES]

    @pl.when(i >= n_used)
    def _():
        ys_ref[...] = jnp.zeros_like(ys_ref)


def _experts(block_expert, n_used, xs, w1, w3, w2):
    _, d, ff = w1.shape
    assert d == ROW_SLABS * LANES
    blk = EXPERT_BLOCK
    n_blocks = xs.shape[0] // (blk * ROW_SLABS)

    def slot_map(i, be, nu):
        return (jnp.minimum(i, nu[0] - 1), 0)

    def w_map(i, be, nu):
        return (be[jnp.minimum(i, nu[0] - 1)], 0, 0)

    grid_spec = pltpu.PrefetchScalarGridSpec(
        num_scalar_prefetch=2,
        grid=(n_blocks,),
        in_specs=[pl.BlockSpec((blk * ROW_SLABS, LANES), slot_map),
                  pl.BlockSpec((1, d, ff), w_map), pl.BlockSpec((1, d, ff), w_map),
                  pl.BlockSpec((1, ff, d), w_map)],
        out_specs=pl.BlockSpec((blk * ROW_SLABS, LANES), lambda i, be, nu: (i, 0)),
        scratch_shapes=[pltpu.VMEM((d, ff), BF16), pltpu.VMEM((d, ff), BF16), pltpu.VMEM((ff, d), BF16)],
    )
    return pl.pallas_call(
        _experts_body,
        grid_spec=grid_spec,
        out_shape=jax.ShapeDtypeStruct(xs.shape, F32),
        compiler_params=_cparams(("arbitrary",), VMEM_LIMIT),
        name="experts",
    )(block_expert, n_used, xs, w1, w3, w2)


def _combine_body(dest_ref, dest_next_ref, gate_ref, x_ref, ys_ref, out_ref, buf, sems):
    i = pl.program_id(0)
    n_steps = pl.num_programs(0)
    tt = x_ref.shape[0]

    def issue(dref, slot):
        def tokens(g, carry):
            t0 = pl.multiple_of(g * ROW_DMA_UNROLL, ROW_DMA_UNROLL)
            for u in range(ROW_DMA_UNROLL):
                for k in range(2):
                    _row_copy(ys_ref, dref[2 * (t0 + u) + k], buf.at[slot, k], t0 + u,
                              sems.at[slot, k]).start(priority=k)
            return carry

        lax.fori_loop(0, tt // ROW_DMA_UNROLL, tokens, 0)

    @pl.when(i == 0)
    def _():
        issue(dest_ref, 0)

    @pl.when(i + 1 < n_steps)
    def _():
        issue(dest_next_ref, (i + 1) % 2)

    slot = i % 2
    for k in range(2):
        pltpu.make_async_copy(_tile_rows(ys_ref, 0, tt), buf.at[slot, k], sems.at[slot, k]).wait()
    g = gate_ref[...]
    out_ref[...] = (x_ref[...] + g[:, 0:1] * _load_row_tiles(buf.at[slot, 0])
                    + g[:, 1:2] * _load_row_tiles(buf.at[slot, 1]))


def _combine(dest, gates, x2d, ys):
    n, d = x2d.shape
    tt = COMBINE_TILE
    row = lambda w: pl.BlockSpec((tt, w), lambda i: (i, 0))
    last = n // tt - 1
    return pl.pallas_call(
        _combine_body,
        grid=(n // tt,),
        in_specs=[pl.BlockSpec((2 * tt,), lambda i: (i,), memory_space=pltpu.SMEM),
                  pl.BlockSpec((2 * tt,), lambda i: (jnp.minimum(i + 1, last),), memory_space=pltpu.SMEM),
                  row(LANES), row(d), pl.BlockSpec(memory_space=pl.ANY)],
        out_specs=row(d),
        out_shape=jax.ShapeDtypeStruct((n, d), F32),
        scratch_shapes=[pltpu.VMEM((2, 2, tt * ROW_SLABS, LANES), F32), pltpu.SemaphoreType.DMA((2, 2))],
        compiler_params=_cparams(("arbitrary",), VMEM_LIMIT),
        name="combine",
    )(dest, dest, gates, x2d, ys)


def _moe(x2d, h3, gates, meta, cnt, w1, w3, w2):
    n, d = x2d.shape
    blk = EXPERT_BLOCK
    counts = cnt[0, :N_EXPERTS].astype(I32)
    dest = _slots(meta, cnt)[:, 0:2].reshape(-1)
    n_slots = 2 * n + N_EXPERTS * blk
    n_blocks = n_slots // blk
    pends = jnp.cumsum((counts + blk - 1) // blk * blk)
    block_start = jnp.arange(n_blocks, dtype=I32) * blk
    block_expert = jnp.minimum(jnp.sum((pends[None, :] <= block_start[:, None]).astype(I32), axis=1),
                               N_EXPERTS - 1)
    n_used = (pends[-1:] // blk).astype(I32)
    xs = _dispatch(counts, dest, h3, n_slots)
    ys = _experts(block_expert, n_used, xs, w1, w3, w2)
    return _combine(dest, gates, x2d, ys)


def _layer(x, mem, norm_mix_w, w_in, attn_q_norm_w, attn_k_norm_w, ret_decay_f, ret_decay_b, ret_gn_w, w_out,
           norm_mem_w, norm_memkv_w, w_mq, w_mkv, mem_q_norm_w, mem_k_norm_w, w_mo,
           norm_moe_w, w_group, w_router, w_exp_gate, w_exp_up, w_exp_down):
    b, s, d = x.shape
    x2d = x.reshape(b * s, d)
    q, k, v, q_pl, k_pl, v_pl, rq, rk, rv, rg = _in_proj(x2d, norm_mix_w, w_in, attn_q_norm_w, attn_k_norm_w)
    attn = _attention(q, k, v, q_pl, k_pl, v_pl, b, s)
    ret = _retention(rq, rk, rv, rg, ret_decay_f, ret_decay_b, ret_gn_w, b, s)
    mk, mv = _mem_kv(mem, norm_memkv_w, w_mkv, mem_k_norm_w)
    x2, h3, gates, meta, cnt = _mem_attn_route(x, attn, ret, w_out, mk, mv, norm_mem_w, w_mq, mem_q_norm_w, w_mo,
                                               norm_moe_w, w_group, w_router)
    x3 = _moe(x2.reshape(b * s, d), h3, gates, meta, cnt, w_exp_gate, w_exp_up, w_exp_down)
    return x3.reshape(b, s, d)


def kernel(x, mem, norm_mix_w, w_in, attn_q_norm_w, attn_k_norm_w, ret_decay_f, ret_decay_b, ret_gn_w, w_out,
           norm_mem_w, norm_memkv_w, w_mq, w_mkv, mem_q_norm_w, mem_k_norm_w, w_mo, norm_moe_w, w_group,
           w_router, w_exp_gate, w_exp_up, w_exp_down):
    depth = norm_mix_w.shape[0]
    for l in range(depth):
        x = _layer(x, mem, norm_mix_w[l], w_in[l], attn_q_norm_w[l], attn_k_norm_w[l], ret_decay_f[l],
                   ret_decay_b[l], ret_gn_w[l], w_out[l], norm_mem_w[l], norm_memkv_w[l], w_mq[l], w_mkv[l],
                   mem_q_norm_w[l], mem_k_norm_w[l], w_mo[l], norm_moe_w[l], w_group[l], w_router[l],
                   w_exp_gate[l], w_exp_up[l], w_exp_down[l])
    return x
```
